```python
import math
import jax, jax.numpy as jnp
from jax import lax
import numpy as np


D_MODEL = 1024
BATCH = 16
SEQ = 4096
DEPTH = 4

N_META = 16
D_SSM = D_MODEL // 2
SSM_GROUP = 16
SSM_GROUPS = D_SSM // SSM_GROUP
SSM_STATE = 64
DT_MIN = 1e-3
DT_MAX = 1e-1
D_CONV = D_MODEL // 2
CONV_WIDTH = 31
N_HEADS = 8
HEAD_DIM = 64
D_ATTN = N_HEADS * HEAD_DIM
Q_BLOCK = 128
D_FF = 2816
N_BRANCH = 3
SPLITS = [D_SSM,
          D_SSM + 2 * D_CONV,
          D_SSM + 2 * D_CONV + D_ATTN,
          D_SSM + 2 * D_CONV + 2 * D_ATTN,
          D_SSM + 2 * D_CONV + 3 * D_ATTN]
D_IN = D_SSM + 2 * D_CONV + 3 * D_ATTN + N_BRANCH * D_MODEL
RMS_EPS = 1e-6
LN_EPS = 1e-5
F32 = jnp.float32

kernel_name = 'hybrid_s5_conformer_stickbreaking_block'


def rms_norm(x, g):
    xf = x.astype(F32)
    y = xf * lax.rsqrt(jnp.mean(xf * xf, axis=-1, keepdims=True) + RMS_EPS)
    return (y * g.astype(F32)).astype(x.dtype)


def layer_norm(x, g, b):
    xf = x.astype(F32)
    mu = jnp.mean(xf, axis=-1, keepdims=True)
    xc = xf - mu
    y = xc * lax.rsqrt(jnp.mean(xc * xc, axis=-1, keepdims=True) + LN_EPS)
    return (y * g.astype(F32) + b.astype(F32)).astype(x.dtype)


def swiglu_ffn(x, w13, w2):
    a, b = jnp.split(x @ w13, 2, axis=-1)
    return (jax.nn.silu(a) * b) @ w2


def _complex_affine_combine(e1, e2):
    a1r, a1i, b1r, b1i = e1
    a2r, a2i, b2r, b2i = e2
    ar = a2r * a1r - a2i * a1i
    ai = a2r * a1i + a2i * a1r
    br = a2r * b1r - a2i * b1i + b2r
    bi = a2r * b1i + a2i * b1r + b2i
    return (ar, ai, br, bi)


def s5_mixer(u, lam_re, lam_im, log_dt, b_re, b_im, c_re, c_im, d_skip, w_glu):
    Bsz, L, _ = u.shape
    uf = u.astype(F32).reshape(Bsz, L, SSM_GROUPS, SSM_GROUP)
    dt = jnp.exp(log_dt.astype(F32))[:, None]
    lr = lam_re.astype(F32)
    li = lam_im.astype(F32)
    mag = jnp.exp(lr * dt)
    ab_re = mag * jnp.cos(li * dt)
    ab_im = mag * jnp.sin(li * dt)
    den = lr * lr + li * li
    nr = ab_re - 1.0
    ni = ab_im
    coef_re = (nr * lr + ni * li) / den
    coef_im = (ni * lr - nr * li) / den
    br = b_re.astype(F32)
    bi = b_im.astype(F32)
    bb_re = coef_re[..., None] * br - coef_im[..., None] * bi
    bb_im = coef_re[..., None] * bi + coef_im[..., None] * br
    bu_re = jnp.einsum('gnc,blgc->lbgn', bb_re, uf)
    bu_im = jnp.einsum('gnc,blgc->lbgn', bb_im, uf)
    a_re = jnp.broadcast_to(ab_re, (L, 1, SSM_GROUPS, SSM_STATE))
    a_im = jnp.broadcast_to(ab_im, (L, 1, SSM_GROUPS, SSM_STATE))
    _, _, s_re, s_im = lax.associative_scan(_complex_affine_combine,
                                            (a_re, a_im, bu_re, bu_im), axis=0)
    y = (jnp.einsum('gcn,lbgn->blgc', c_re.astype(F32), s_re)
         - jnp.einsum('gcn,lbgn->blgc', c_im.astype(F32), s_im))
    y = y.reshape(Bsz, L, D_SSM) + d_skip.astype(F32) * u.astype(F32)
    y = jax.nn.gelu(y).astype(u.dtype)
    a, g = jnp.split(y @ w_glu, 2, axis=-1)
    return a * jax.nn.sigmoid(g)


def conformer_conv(xc, conv_w, conv_b, ln_g, ln_b, w_pw):
    a, g = jnp.split(xc, 2, axis=-1)
    h = a * jax.nn.sigmoid(g)
    h = lax.conv_general_dilated(
        h, conv_w[:, None, :].astype(h.dtype), window_strides=(1,),
        padding=((CONV_WIDTH - 1, 0),),
        dimension_numbers=('NWC', 'WIO', 'NWC'),
        feature_group_count=D_CONV) + conv_b
    h = jax.nn.silu(layer_norm(h, ln_g, ln_b))
    return h @ w_pw


def stick_breaking_attention(q, k, v, w_o):
    Bsz, L = q.shape[0], q.shape[1]
    scale = 1.0 / math.sqrt(HEAD_DIM)
    qf = q.astype(F32).transpose(0, 2, 1, 3)
    kf = k.astype(F32).transpose(0, 2, 1, 3)
    vf = v.astype(F32).transpose(0, 2, 1, 3)
    key_pos = jnp.arange(L)

    def attend(q_blk, q_pos):
        z = jnp.einsum('bhqd,bhkd->bhqk', q_blk, kf) * scale
        mask = key_pos[None, :] < q_pos[:, None]
        log_keep = jnp.where(mask, jax.nn.log_sigmoid(-z), 0.0)
        later = lax.cumsum(log_keep, axis=3, reverse=True) - log_keep
        w = jnp.where(mask, jnp.exp(jax.nn.log_sigmoid(z) + later), 0.0)
        return jnp.einsum('bhqk,bhkd->bhqd', w, vf)

    meta_out = attend(qf[:, :, :N_META], jnp.arange(N_META))
    n_blk = (L - N_META) // Q_BLOCK
    q_real = qf[:, :, N_META:].reshape(Bsz, N_HEADS, n_blk, Q_BLOCK, HEAD_DIM)
    q_real = q_real.transpose(2, 0, 1, 3, 4)
    pos = N_META + jnp.arange(n_blk * Q_BLOCK).reshape(n_blk, Q_BLOCK)
    real_out = lax.map(lambda a: attend(a[0], a[1]), (q_real, pos))
    real_out = real_out.transpose(1, 2, 0, 3, 4).reshape(Bsz, N_HEADS, n_blk * Q_BLOCK, HEAD_DIM)
    o = jnp.concatenate([meta_out, real_out], axis=2)
    o = o.transpose(0, 2, 1, 3).reshape(Bsz, L, D_ATTN).astype(q.dtype)
    return o @ w_o


def hybrid_mixer(xn, w_in, lam_re, lam_im, log_dt, b_re, b_im, c_re, c_im, d_skip, w_glu,
                 conv_w, conv_b, conv_ln_g, conv_ln_b, conv_w_out, attn_w_o, w_out):
    Bsz, L, _ = xn.shape
    proj = xn @ w_in
    u, xc, q, k, v, gates = jnp.split(proj, SPLITS, axis=-1)
    o_ssm = s5_mixer(u, lam_re, lam_im, log_dt, b_re, b_im, c_re, c_im, d_skip, w_glu)
    o_conv = conformer_conv(xc, conv_w, conv_b, conv_ln_g, conv_ln_b, conv_w_out)
    o_attn = stick_breaking_attention(q.reshape(Bsz, L, N_HEADS, HEAD_DIM),
                                      k.reshape(Bsz, L, N_HEADS, HEAD_DIM),
                                      v.reshape(Bsz, L, N_HEADS, HEAD_DIM), attn_w_o)
    g = jax.nn.sigmoid(gates.astype(F32)).reshape(Bsz, L, N_BRANCH, D_MODEL)
    merged = (g[:, :, 0] * o_ssm.astype(F32) + g[:, :, 1] * o_conv.astype(F32)
              + g[:, :, 2] * o_attn.astype(F32))
    return merged.astype(xn.dtype) @ w_out


def _fwd_setup_inputs(seed: int = 0) -> dict:
    key = jax.random.key(seed)
    ks = jax.random.split(key, 32)

    def nrm(k, shape, scale):
        return scale * jax.random.normal(k, shape, F32)

    G, N = SSM_GROUPS, SSM_STATE
    lam_im_init = math.pi * jnp.arange(N, dtype=F32)
    return {
        'x': nrm(ks[0], (BATCH, SEQ, D_MODEL), 1.0),
        'meta_tokens': nrm(ks[1], (N_META, D_MODEL), 1.0),
        'ffn1_norm': 1.0 + nrm(ks[2], (DEPTH, D_MODEL), 0.05),
        'ffn1_w13': nrm(ks[3], (DEPTH, D_MODEL, 2 * D_FF), D_MODEL ** -0.5),
        'ffn1_w2': nrm(ks[4], (DEPTH, D_FF, D_MODEL), D_FF ** -0.5),
        'mix_norm': 1.0 + nrm(ks[5], (DEPTH, D_MODEL), 0.05),
        'w_in': nrm(ks[6], (DEPTH, D_MODEL, D_IN), D_MODEL ** -0.5),
        'ssm_lam_re': -0.5 + nrm(ks[7], (DEPTH, G, N), 0.01),
        'ssm_lam_im': lam_im_init + nrm(ks[8], (DEPTH, G, N), 0.01),
        'ssm_log_dt': jax.random.uniform(ks[9], (DEPTH, G), F32,
                                         math.log(DT_MIN), math.log(DT_MAX)),
        'ssm_b_re': nrm(ks[10], (DEPTH, G, N, SSM_GROUP), (2 * SSM_GROUP) ** -0.5),
        'ssm_b_im': nrm(ks[11], (DEPTH, G, N, SSM_GROUP), (2 * SSM_GROUP) ** -0.5),
        'ssm_c_re': nrm(ks[12], (DEPTH, G, SSM_GROUP, N), (2 * N) ** -0.5),
        'ssm_c_im': nrm(ks[13], (DEPTH, G, SSM_GROUP, N), (2 * N) ** -0.5),
        'ssm_d': nrm(ks[14], (DEPTH, D_SSM), 1.0),
        'ssm_w_glu': nrm(ks[15], (DEPTH, D_SSM, 2 * D_MODEL), D_SSM ** -0.5),
        'conv_w': nrm(ks[16], (DEPTH, CONV_WIDTH, D_CONV), CONV_WIDTH ** -0.5),
        'conv_b': nrm(ks[17], (DEPTH, D_CONV), 0.01),
        'conv_ln_g': 1.0 + nrm(ks[18], (DEPTH, D_CONV), 0.05),
        'conv_ln_b': nrm(ks[19], (DEPTH, D_CONV), 0.01),
        'conv_w_out': nrm(ks[20], (DEPTH, D_CONV, D_MODEL), D_CONV ** -0.5),
        'attn_w_o': nrm(ks[21], (DEPTH, D_ATTN, D_MODEL), D_ATTN ** -0.5),
        'w_out': nrm(ks[22], (DEPTH, D_MODEL, D_MODEL), D_MODEL ** -0.5),
        'ffn2_norm': 1.0 + nrm(ks[23], (DEPTH, D_MODEL), 0.05),
        'ffn2_w13': nrm(ks[24], (DEPTH, D_MODEL, 2 * D_FF), D_MODEL ** -0.5),
        'ffn2_w2': nrm(ks[25], (DEPTH, D_FF, D_MODEL), D_FF ** -0.5),
        'final_norm': 1.0 + nrm(ks[26], (D_MODEL,), 0.05),
    }


def _fwd_reference(x, meta_tokens, ffn1_norm, ffn1_w13, ffn1_w2, mix_norm, w_in,
              ssm_lam_re, ssm_lam_im, ssm_log_dt, ssm_b_re, ssm_b_im, ssm_c_re, ssm_c_im,
              ssm_d, ssm_w_glu, conv_w, conv_b, conv_ln_g, conv_ln_b, conv_w_out,
              attn_w_o, w_out, ffn2_norm, ffn2_w13, ffn2_w2, final_norm):
    Bsz = x.shape[0]
    meta = jnp.broadcast_to(meta_tokens[None].astype(x.dtype), (Bsz, N_META, D_MODEL))
    h = jnp.concatenate([meta, x], axis=1)
    for i in range(DEPTH):
        h = h + 0.5 * swiglu_ffn(rms_norm(h, ffn1_norm[i]), ffn1_w13[i], ffn1_w2[i])
        h = h + hybrid_mixer(rms_norm(h, mix_norm[i]), w_in[i],
                             ssm_lam_re[i], ssm_lam_im[i], ssm_log_dt[i],
                             ssm_b_re[i], ssm_b_im[i], ssm_c_re[i], ssm_c_im[i],
                             ssm_d[i], ssm_w_glu[i],
                             conv_w[i], conv_b[i], conv_ln_g[i], conv_ln_b[i], conv_w_out[i],
                             attn_w_o[i], w_out[i])
        h = h + 0.5 * swiglu_ffn(rms_norm(h, ffn2_norm[i]), ffn2_w13[i], ffn2_w2[i])
    return rms_norm(h, final_norm)[:, N_META:]


import jax as _jax
import jax.numpy as _jnp

TWIN_FORMAT = 'train_step'
FWD_PARAMS = ['x', 'meta_tokens', 'ffn1_norm', 'ffn1_w13', 'ffn1_w2', 'mix_norm', 'w_in', 'ssm_lam_re', 'ssm_lam_im', 'ssm_log_dt', 'ssm_b_re', 'ssm_b_im', 'ssm_c_re', 'ssm_c_im', 'ssm_d', 'ssm_w_glu', 'conv_w', 'conv_b', 'conv_ln_g', 'conv_ln_b', 'conv_w_out', 'attn_w_o', 'w_out', 'ffn2_norm', 'ffn2_w13', 'ffn2_w2', 'final_norm']
TWIN_WEIGHTS = ['meta_tokens', 'ffn1_norm', 'ffn1_w13', 'ffn1_w2', 'mix_norm', 'w_in', 'ssm_lam_re', 'ssm_lam_im', 'ssm_log_dt', 'ssm_b_re', 'ssm_b_im', 'ssm_c_re', 'ssm_c_im', 'ssm_d', 'ssm_w_glu', 'conv_w', 'conv_b', 'conv_ln_g', 'conv_ln_b', 'conv_w_out', 'attn_w_o', 'w_out', 'ffn2_norm', 'ffn2_w13', 'ffn2_w2', 'final_norm']
TWIN_DIFF_INPUT = 'x'
TWIN_INPUTS = ['x', 'meta_tokens', 'ffn1_norm', 'ffn1_w13', 'ffn1_w2', 'mix_norm', 'w_in', 'ssm_lam_re', 'ssm_lam_im', 'ssm_log_dt', 'ssm_b_re', 'ssm_b_im', 'ssm_c_re', 'ssm_c_im', 'ssm_d', 'ssm_w_glu', 'conv_w', 'conv_b', 'conv_ln_g', 'conv_ln_b', 'conv_w_out', 'attn_w_o', 'w_out', 'ffn2_norm', 'ffn2_w13', 'ffn2_w2', 'final_norm', 'loss_target', 'm_meta_tokens', 'm_ffn1_norm', 'm_ffn1_w13', 'm_ffn1_w2', 'm_mix_norm', 'm_w_in', 'm_ssm_lam_re', 'm_ssm_lam_im', 'm_ssm_log_dt', 'm_ssm_b_re', 'm_ssm_b_im', 'm_ssm_c_re', 'm_ssm_c_im', 'm_ssm_d', 'm_ssm_w_glu', 'm_conv_w', 'm_conv_b', 'm_conv_ln_g', 'm_conv_ln_b', 'm_conv_w_out', 'm_attn_w_o', 'm_w_out', 'm_ffn2_norm', 'm_ffn2_w13', 'm_ffn2_w2', 'm_final_norm', 'v_meta_tokens', 'v_ffn1_norm', 'v_ffn1_w13', 'v_ffn1_w2', 'v_mix_norm', 'v_w_in', 'v_ssm_lam_re', 'v_ssm_lam_im', 'v_ssm_log_dt', 'v_ssm_b_re', 'v_ssm_b_im', 'v_ssm_c_re', 'v_ssm_c_im', 'v_ssm_d', 'v_ssm_w_glu', 'v_conv_w', 'v_conv_b', 'v_conv_ln_g', 'v_conv_ln_b', 'v_conv_w_out', 'v_attn_w_o', 'v_w_out', 'v_ffn2_norm', 'v_ffn2_w13', 'v_ffn2_w2', 'v_final_norm']
TWIN_OUTPUTS = ['loss', 'grad_x', 'grad_meta_tokens', 'grad_ffn1_norm', 'grad_ffn1_w13', 'grad_ffn1_w2', 'grad_mix_norm', 'grad_w_in', 'grad_ssm_lam_re', 'grad_ssm_lam_im', 'grad_ssm_log_dt', 'grad_ssm_b_re', 'grad_ssm_b_im', 'grad_ssm_c_re', 'grad_ssm_c_im', 'grad_ssm_d', 'grad_ssm_w_glu', 'grad_conv_w', 'grad_conv_b', 'grad_conv_ln_g', 'grad_conv_ln_b', 'grad_conv_w_out', 'grad_attn_w_o', 'grad_w_out', 'grad_ffn2_norm', 'grad_ffn2_w13', 'grad_ffn2_w2', 'grad_final_norm', 'delta_meta_tokens', 'delta_ffn1_norm', 'delta_ffn1_w13', 'delta_ffn1_w2', 'delta_mix_norm', 'delta_w_in', 'delta_ssm_lam_re', 'delta_ssm_lam_im', 'delta_ssm_log_dt', 'delta_ssm_b_re', 'delta_ssm_b_im', 'delta_ssm_c_re', 'delta_ssm_c_im', 'delta_ssm_d', 'delta_ssm_w_glu', 'delta_conv_w', 'delta_conv_b', 'delta_conv_ln_g', 'delta_conv_ln_b', 'delta_conv_w_out', 'delta_attn_w_o', 'delta_w_out', 'delta_ffn2_norm', 'delta_ffn2_w13', 'delta_ffn2_w2', 'delta_final_norm', 'new_m_meta_tokens', 'new_m_ffn1_norm', 'new_m_ffn1_w13', 'new_m_ffn1_w2', 'new_m_mix_norm', 'new_m_w_in', 'new_m_ssm_lam_re', 'new_m_ssm_lam_im', 'new_m_ssm_log_dt', 'new_m_ssm_b_re', 'new_m_ssm_b_im', 'new_m_ssm_c_re', 'new_m_ssm_c_im', 'new_m_ssm_d', 'new_m_ssm_w_glu', 'new_m_conv_w', 'new_m_conv_b', 'new_m_conv_ln_g', 'new_m_conv_ln_b', 'new_m_conv_w_out', 'new_m_attn_w_o', 'new_m_w_out', 'new_m_ffn2_norm', 'new_m_ffn2_w13', 'new_m_ffn2_w2', 'new_m_final_norm', 'new_v_meta_tokens', 'new_v_ffn1_norm', 'new_v_ffn1_w13', 'new_v_ffn1_w2', 'new_v_mix_norm', 'new_v_w_in', 'new_v_ssm_lam_re', 'new_v_ssm_lam_im', 'new_v_ssm_log_dt', 'new_v_ssm_b_re', 'new_v_ssm_b_im', 'new_v_ssm_c_re', 'new_v_ssm_c_im', 'new_v_ssm_d', 'new_v_ssm_w_glu', 'new_v_conv_w', 'new_v_conv_b', 'new_v_conv_ln_g', 'new_v_conv_ln_b', 'new_v_conv_w_out', 'new_v_attn_w_o', 'new_v_w_out', 'new_v_ffn2_norm', 'new_v_ffn2_w13', 'new_v_ffn2_w2', 'new_v_final_norm']
TWIN_LEAF_KINDS = {'loss': 'loss', 'grad_x': 'grad_x', 'grad_meta_tokens': 'grad_w', 'grad_ffn1_norm': 'grad_w', 'grad_ffn1_w13': 'grad_w', 'grad_ffn1_w2': 'grad_w', 'grad_mix_norm': 'grad_w', 'grad_w_in': 'grad_w', 'grad_ssm_lam_re': 'grad_w', 'grad_ssm_lam_im': 'grad_w', 'grad_ssm_log_dt': 'grad_w', 'grad_ssm_b_re': 'grad_w', 'grad_ssm_b_im': 'grad_w', 'grad_ssm_c_re': 'grad_w', 'grad_ssm_c_im': 'grad_w', 'grad_ssm_d': 'grad_w', 'grad_ssm_w_glu': 'grad_w', 'grad_conv_w': 'grad_w', 'grad_conv_b': 'grad_w', 'grad_conv_ln_g': 'grad_w', 'grad_conv_ln_b': 'grad_w', 'grad_conv_w_out': 'grad_w', 'grad_attn_w_o': 'grad_w', 'grad_w_out': 'grad_w', 'grad_ffn2_norm': 'grad_w', 'grad_ffn2_w13': 'grad_w', 'grad_ffn2_w2': 'grad_w', 'grad_final_norm': 'grad_w', 'delta_meta_tokens': 'delta_w', 'delta_ffn1_norm': 'delta_w', 'delta_ffn1_w13': 'delta_w', 'delta_ffn1_w2': 'delta_w', 'delta_mix_norm': 'delta_w', 'delta_w_in': 'delta_w', 'delta_ssm_lam_re': 'delta_w', 'delta_ssm_lam_im': 'delta_w', 'delta_ssm_log_dt': 'delta_w', 'delta_ssm_b_re': 'delta_w', 'delta_ssm_b_im': 'delta_w', 'delta_ssm_c_re': 'delta_w', 'delta_ssm_c_im': 'delta_w', 'delta_ssm_d': 'delta_w', 'delta_ssm_w_glu': 'delta_w', 'delta_conv_w': 'delta_w', 'delta_conv_b': 'delta_w', 'delta_conv_ln_g': 'delta_w', 'delta_conv_ln_b': 'delta_w', 'delta_conv_w_out': 'delta_w', 'delta_attn_w_o': 'delta_w', 'delta_w_out': 'delta_w', 'delta_ffn2_norm': 'delta_w', 'delta_ffn2_w13': 'delta_w', 'delta_ffn2_w2': 'delta_w', 'delta_final_norm': 'delta_w', 'new_m_meta_tokens': 'new_m', 'new_m_ffn1_norm': 'new_m', 'new_m_ffn1_w13': 'new_m', 'new_m_ffn1_w2': 'new_m', 'new_m_mix_norm': 'new_m', 'new_m_w_in': 'new_m', 'new_m_ssm_lam_re': 'new_m', 'new_m_ssm_lam_im': 'new_m', 'new_m_ssm_log_dt': 'new_m', 'new_m_ssm_b_re': 'new_m', 'new_m_ssm_b_im': 'new_m', 'new_m_ssm_c_re': 'new_m', 'new_m_ssm_c_im': 'new_m', 'new_m_ssm_d': 'new_m', 'new_m_ssm_w_glu': 'new_m', 'new_m_conv_w': 'new_m', 'new_m_conv_b': 'new_m', 'new_m_conv_ln_g': 'new_m', 'new_m_conv_ln_b': 'new_m', 'new_m_conv_w_out': 'new_m', 'new_m_attn_w_o': 'new_m', 'new_m_w_out': 'new_m', 'new_m_ffn2_norm': 'new_m', 'new_m_ffn2_w13': 'new_m', 'new_m_ffn2_w2': 'new_m', 'new_m_final_norm': 'new_m', 'new_v_meta_tokens': 'new_v', 'new_v_ffn1_norm': 'new_v', 'new_v_ffn1_w13': 'new_v', 'new_v_ffn1_w2': 'new_v', 'new_v_mix_norm': 'new_v', 'new_v_w_in': 'new_v', 'new_v_ssm_lam_re': 'new_v', 'new_v_ssm_lam_im': 'new_v', 'new_v_ssm_log_dt': 'new_v', 'new_v_ssm_b_re': 'new_v', 'new_v_ssm_b_im': 'new_v', 'new_v_ssm_c_re': 'new_v', 'new_v_ssm_c_im': 'new_v', 'new_v_ssm_d': 'new_v', 'new_v_ssm_w_glu': 'new_v', 'new_v_conv_w': 'new_v', 'new_v_conv_b': 'new_v', 'new_v_conv_ln_g': 'new_v', 'new_v_conv_ln_b': 'new_v', 'new_v_conv_w_out': 'new_v', 'new_v_attn_w_o': 'new_v', 'new_v_w_out': 'new_v', 'new_v_ffn2_norm': 'new_v', 'new_v_ffn2_w13': 'new_v', 'new_v_ffn2_w2': 'new_v', 'new_v_final_norm': 'new_v'}


def _forward(args):
    return _fwd_reference(*[args[k] for k in FWD_PARAMS])


def _output_shape():
    out = _jax.eval_shape(lambda: _forward(_fwd_setup_inputs(0)))
    return out.shape, out.dtype

N_MICROBATCH = 1
ADAM_LR = 0.001
ADAM_B1 = 0.9
ADAM_B2 = 0.999
ADAM_EPS = 1e-08
ADAM_WD = 0.01
ADAM_STEP = 10
PER_EXAMPLE_BATCH_AXIS = {'x': 0, 'loss_target': 0}
SHARED_INPUTS = []
_WEIGHT_DTYPES = {'meta_tokens': _jnp.float32, 'ffn1_norm': _jnp.float32, 'ffn1_w13': _jnp.float32, 'ffn1_w2': _jnp.float32, 'mix_norm': _jnp.float32, 'w_in': _jnp.float32, 'ssm_lam_re': _jnp.float32, 'ssm_lam_im': _jnp.float32, 'ssm_log_dt': _jnp.float32, 'ssm_b_re': _jnp.float32, 'ssm_b_im': _jnp.float32, 'ssm_c_re': _jnp.float32, 'ssm_c_im': _jnp.float32, 'ssm_d': _jnp.float32, 'ssm_w_glu': _jnp.float32, 'conv_w': _jnp.float32, 'conv_b': _jnp.float32, 'conv_ln_g': _jnp.float32, 'conv_ln_b': _jnp.float32, 'conv_w_out': _jnp.float32, 'attn_w_o': _jnp.float32, 'w_out': _jnp.float32, 'ffn2_norm': _jnp.float32, 'ffn2_w13': _jnp.float32, 'ffn2_w2': _jnp.float32, 'final_norm': _jnp.float32}
MOMENT_SCALE = {'meta_tokens': 1.106648e-02, 'ffn1_norm': 1.046517e-01, 'ffn1_w13': 4.417766e-02, 'ffn1_w2': 7.204142e-02, 'mix_norm': 1.376060e-01, 'w_in': 5.539416e-02, 'ssm_lam_re': 3.473176e-03, 'ssm_lam_im': 3.536796e-03, 'ssm_log_dt': 3.231407e+00, 'ssm_b_re': 2.170440e-03, 'ssm_b_im': 2.173857e-03, 'ssm_c_re': 4.213957e-03, 'ssm_c_im': 4.391092e-03, 'ssm_d': 7.806033e-02, 'ssm_w_glu': 3.562970e-02, 'conv_w': 1.033017e-01, 'conv_b': 2.731337e-01, 'conv_ln_g': 1.360300e-01, 'conv_ln_b': 1.496512e-01, 'conv_w_out': 7.674685e-02, 'attn_w_o': 7.831963e-02, 'w_out': 1.205682e-01, 'ffn2_norm': 9.324229e-02, 'ffn2_w13': 3.712920e-02, 'ffn2_w2': 6.074991e-02, 'final_norm': 6.413687e+01}


def _to_microbatches(a, axis):
    t = _jnp.moveaxis(a, axis, 0)
    t = t.reshape((N_MICROBATCH, t.shape[0] // N_MICROBATCH) + t.shape[1:])
    return _jnp.moveaxis(t, 1, axis + 1)


def setup_inputs(seed: int = 0) -> dict:
    inp = _fwd_setup_inputs(seed)
    key = _jax.random.fold_in(_jax.random.key(seed), 7919)
    shape, _ = _output_shape()
    out = dict(inp)
    out["loss_target"] = _jax.random.normal(_jax.random.fold_in(key, 0), shape, _jnp.float32)
    for i, name in enumerate(TWIN_WEIGHTS):
        w = inp[name].astype(_jnp.float32)
        if MOMENT_SCALE is None:
            s = _jnp.sqrt(_jnp.mean(_jnp.square(w)) + 1e-30)
        else:
            s = MOMENT_SCALE[name]
        km, kv = _jax.random.split(_jax.random.fold_in(key, i + 1))
        out[name] = w
        out["m_" + name] = s * _jax.random.normal(km, w.shape, _jnp.float32)
        out["v_" + name] = (s * s) * _jax.random.uniform(kv, w.shape, _jnp.float32, 0.5, 1.5)
    if N_MICROBATCH > 1:
        for name, axis in PER_EXAMPLE_BATCH_AXIS.items():
            out[name] = _to_microbatches(out[name], axis)
    return {'x': out['x'], 'meta_tokens': out['meta_tokens'], 'ffn1_norm': out['ffn1_norm'], 'ffn1_w13': out['ffn1_w13'], 'ffn1_w2': out['ffn1_w2'], 'mix_norm': out['mix_norm'], 'w_in': out['w_in'], 'ssm_lam_re': out['ssm_lam_re'], 'ssm_lam_im': out['ssm_lam_im'], 'ssm_log_dt': out['ssm_log_dt'], 'ssm_b_re': out['ssm_b_re'], 'ssm_b_im': out['ssm_b_im'], 'ssm_c_re': out['ssm_c_re'], 'ssm_c_im': out['ssm_c_im'], 'ssm_d': out['ssm_d'], 'ssm_w_glu': out['ssm_w_glu'], 'conv_w': out['conv_w'], 'conv_b': out['conv_b'], 'conv_ln_g': out['conv_ln_g'], 'conv_ln_b': out['conv_ln_b'], 'conv_w_out': out['conv_w_out'], 'attn_w_o': out['attn_w_o'], 'w_out': out['w_out'], 'ffn2_norm': out['ffn2_norm'], 'ffn2_w13': out['ffn2_w13'], 'ffn2_w2': out['ffn2_w2'], 'final_norm': out['final_norm'], 'loss_target': out['loss_target'], 'm_meta_tokens': out['m_meta_tokens'], 'm_ffn1_norm': out['m_ffn1_norm'], 'm_ffn1_w13': out['m_ffn1_w13'], 'm_ffn1_w2': out['m_ffn1_w2'], 'm_mix_norm': out['m_mix_norm'], 'm_w_in': out['m_w_in'], 'm_ssm_lam_re': out['m_ssm_lam_re'], 'm_ssm_lam_im': out['m_ssm_lam_im'], 'm_ssm_log_dt': out['m_ssm_log_dt'], 'm_ssm_b_re': out['m_ssm_b_re'], 'm_ssm_b_im': out['m_ssm_b_im'], 'm_ssm_c_re': out['m_ssm_c_re'], 'm_ssm_c_im': out['m_ssm_c_im'], 'm_ssm_d': out['m_ssm_d'], 'm_ssm_w_glu': out['m_ssm_w_glu'], 'm_conv_w': out['m_conv_w'], 'm_conv_b': out['m_conv_b'], 'm_conv_ln_g': out['m_conv_ln_g'], 'm_conv_ln_b': out['m_conv_ln_b'], 'm_conv_w_out': out['m_conv_w_out'], 'm_attn_w_o': out['m_attn_w_o'], 'm_w_out': out['m_w_out'], 'm_ffn2_norm': out['m_ffn2_norm'], 'm_ffn2_w13': out['m_ffn2_w13'], 'm_ffn2_w2': out['m_ffn2_w2'], 'm_final_norm': out['m_final_norm'], 'v_meta_tokens': out['v_meta_tokens'], 'v_ffn1_norm': out['v_ffn1_norm'], 'v_ffn1_w13': out['v_ffn1_w13'], 'v_ffn1_w2': out['v_ffn1_w2'], 'v_mix_norm': out['v_mix_norm'], 'v_w_in': out['v_w_in'], 'v_ssm_lam_re': out['v_ssm_lam_re'], 'v_ssm_lam_im': out['v_ssm_lam_im'], 'v_ssm_log_dt': out['v_ssm_log_dt'], 'v_ssm_b_re': out['v_ssm_b_re'], 'v_ssm_b_im': out['v_ssm_b_im'], 'v_ssm_c_re': out['v_ssm_c_re'], 'v_ssm_c_im': out['v_ssm_c_im'], 'v_ssm_d': out['v_ssm_d'], 'v_ssm_w_glu': out['v_ssm_w_glu'], 'v_conv_w': out['v_conv_w'], 'v_conv_b': out['v_conv_b'], 'v_conv_ln_g': out['v_conv_ln_g'], 'v_conv_ln_b': out['v_conv_ln_b'], 'v_conv_w_out': out['v_conv_w_out'], 'v_attn_w_o': out['v_attn_w_o'], 'v_w_out': out['v_w_out'], 'v_ffn2_norm': out['v_ffn2_norm'], 'v_ffn2_w13': out['v_ffn2_w13'], 'v_ffn2_w2': out['v_ffn2_w2'], 'v_final_norm': out['v_final_norm']}


def _loss(weights, diff, rest, loss_target):
    with _jax.named_scope("forward"):
        args = {**rest, TWIN_DIFF_INPUT: diff, **{k: w.astype(_WEIGHT_DTYPES[k]) for k, w in weights.items()}}
        y = _forward(args)
    with _jax.named_scope("loss_head"):
        err = _jnp.square(y.astype(_jnp.float32) - loss_target)
        return 0.5 * _jnp.sum(_jnp.mean(err, axis=-1)) if err.ndim else 0.5 * err


def _adamw(w, g, m, v):
    m = ADAM_B1 * m + (1.0 - ADAM_B1) * g
    v = ADAM_B2 * v + (1.0 - ADAM_B2) * _jnp.square(g)
    m_hat = m / (1.0 - ADAM_B1 ** ADAM_STEP)
    v_hat = v / (1.0 - ADAM_B2 ** ADAM_STEP)
    delta = -ADAM_LR * (m_hat / (_jnp.sqrt(v_hat) + ADAM_EPS) + ADAM_WD * w)
    return delta, m, v


def reference(x, meta_tokens, ffn1_norm, ffn1_w13, ffn1_w2, mix_norm, w_in, ssm_lam_re, ssm_lam_im, ssm_log_dt, ssm_b_re, ssm_b_im, ssm_c_re, ssm_c_im, ssm_d, ssm_w_glu, conv_w, conv_b, conv_ln_g, conv_ln_b, conv_w_out, attn_w_o, w_out, ffn2_norm, ffn2_w13, ffn2_w2, final_norm, loss_target, m_meta_tokens, m_ffn1_norm, m_ffn1_w13, m_ffn1_w2, m_mix_norm, m_w_in, m_ssm_lam_re, m_ssm_lam_im, m_ssm_log_dt, m_ssm_b_re, m_ssm_b_im, m_ssm_c_re, m_ssm_c_im, m_ssm_d, m_ssm_w_glu, m_conv_w, m_conv_b, m_conv_ln_g, m_conv_ln_b, m_conv_w_out, m_attn_w_o, m_w_out, m_ffn2_norm, m_ffn2_w13, m_ffn2_w2, m_final_norm, v_meta_tokens, v_ffn1_norm, v_ffn1_w13, v_ffn1_w2, v_mix_norm, v_w_in, v_ssm_lam_re, v_ssm_lam_im, v_ssm_log_dt, v_ssm_b_re, v_ssm_b_im, v_ssm_c_re, v_ssm_c_im, v_ssm_d, v_ssm_w_glu, v_conv_w, v_conv_b, v_conv_ln_g, v_conv_ln_b, v_conv_w_out, v_attn_w_o, v_w_out, v_ffn2_norm, v_ffn2_w13, v_ffn2_w2, v_final_norm):
    given = dict(x=x, meta_tokens=meta_tokens, ffn1_norm=ffn1_norm, ffn1_w13=ffn1_w13, ffn1_w2=ffn1_w2, mix_norm=mix_norm, w_in=w_in, ssm_lam_re=ssm_lam_re, ssm_lam_im=ssm_lam_im, ssm_log_dt=ssm_log_dt, ssm_b_re=ssm_b_re, ssm_b_im=ssm_b_im, ssm_c_re=ssm_c_re, ssm_c_im=ssm_c_im, ssm_d=ssm_d, ssm_w_glu=ssm_w_glu, conv_w=conv_w, conv_b=conv_b, conv_ln_g=conv_ln_g, conv_ln_b=conv_ln_b, conv_w_out=conv_w_out, attn_w_o=attn_w_o, w_out=w_out, ffn2_norm=ffn2_norm, ffn2_w13=ffn2_w13, ffn2_w2=ffn2_w2, final_norm=final_norm, loss_target=loss_target, m_meta_tokens=m_meta_tokens, m_ffn1_norm=m_ffn1_norm, m_ffn1_w13=m_ffn1_w13, m_ffn1_w2=m_ffn1_w2, m_mix_norm=m_mix_norm, m_w_in=m_w_in, m_ssm_lam_re=m_ssm_lam_re, m_ssm_lam_im=m_ssm_lam_im, m_ssm_log_dt=m_ssm_log_dt, m_ssm_b_re=m_ssm_b_re, m_ssm_b_im=m_ssm_b_im, m_ssm_c_re=m_ssm_c_re, m_ssm_c_im=m_ssm_c_im, m_ssm_d=m_ssm_d, m_ssm_w_glu=m_ssm_w_glu, m_conv_w=m_conv_w, m_conv_b=m_conv_b, m_conv_ln_g=m_conv_ln_g, m_conv_ln_b=m_conv_ln_b, m_conv_w_out=m_conv_w_out, m_attn_w_o=m_attn_w_o, m_w_out=m_w_out, m_ffn2_norm=m_ffn2_norm, m_ffn2_w13=m_ffn2_w13, m_ffn2_w2=m_ffn2_w2, m_final_norm=m_final_norm, v_meta_tokens=v_meta_tokens, v_ffn1_norm=v_ffn1_norm, v_ffn1_w13=v_ffn1_w13, v_ffn1_w2=v_ffn1_w2, v_mix_norm=v_mix_norm, v_w_in=v_w_in, v_ssm_lam_re=v_ssm_lam_re, v_ssm_lam_im=v_ssm_lam_im, v_ssm_log_dt=v_ssm_log_dt, v_ssm_b_re=v_ssm_b_re, v_ssm_b_im=v_ssm_b_im, v_ssm_c_re=v_ssm_c_re, v_ssm_c_im=v_ssm_c_im, v_ssm_d=v_ssm_d, v_ssm_w_glu=v_ssm_w_glu, v_conv_w=v_conv_w, v_conv_b=v_conv_b, v_conv_ln_g=v_conv_ln_g, v_conv_ln_b=v_conv_ln_b, v_conv_w_out=v_conv_w_out, v_attn_w_o=v_attn_w_o, v_w_out=v_w_out, v_ffn2_norm=v_ffn2_norm, v_ffn2_w13=v_ffn2_w13, v_ffn2_w2=v_ffn2_w2, v_final_norm=v_final_norm)
    weights = {n: given[n] for n in TWIN_WEIGHTS}
    shared = {n: given[n] for n in SHARED_INPUTS}
    per_example = {n: given[n] for n in ['x']}
    grad_fn = _jax.value_and_grad(_loss, argnums=(0, 1))

    def one_microbatch(ex, loss_target):
        ex = dict(ex)
        diff = ex.pop(TWIN_DIFF_INPUT)
        return grad_fn(weights, diff, {**shared, **ex}, loss_target)

    if N_MICROBATCH == 1:
        loss, (grad_w, grad_x) = one_microbatch(per_example, given["loss_target"])
    else:
        def body(carry, xs):
            loss_sum, grad_sum = carry
            l_k, (gw_k, gx_k) = one_microbatch(xs[0], xs[1])
            with _jax.named_scope("update"):
                return (loss_sum + l_k, _jax.tree.map(_jnp.add, grad_sum, gw_k)), gx_k

        init = (_jnp.zeros((), _jnp.float32), _jax.tree.map(_jnp.zeros_like, weights))
        (loss, grad_w), grad_x = _jax.lax.scan(body, init, (per_example, given["loss_target"]))
    with _jax.named_scope("update"):
        delta_w, new_m, new_v = {}, {}, {}
        for n in TWIN_WEIGHTS:
            delta_w[n], new_m[n], new_v[n] = _adamw(weights[n], grad_w[n], given["m_" + n], given["v_" + n])
    return (loss, grad_x, *[grad_w[n] for n in TWIN_WEIGHTS], *[delta_w[n] for n in TWIN_WEIGHTS],
            *[new_m[n] for n in TWIN_WEIGHTS], *[new_v[n] for n in TWIN_WEIGHTS])
```

```python
import functools
import math
from typing import NamedTuple

import jax
import jax.numpy as jnp
from jax import lax
from jax.experimental import pallas as pl
from jax.experimental.pallas import tpu as pltpu

F32 = jnp.float32
BF16 = jnp.bfloat16
RMS_EPS = 1e-6
LN_EPS = 1e-5
ADAM_LR = 0.001
ADAM_B1 = 0.9
ADAM_B2 = 0.999
ADAM_EPS = 1e-08
ADAM_WD = 0.01
ADAM_STEP = 10
V7X_LANES = 128
V7X_VMEM_LIMIT = 52 * 1024 * 1024
Q_BLOCK = 128
SSM_CHUNK = 128
SSM_SB = 128
CONV_HALO = 32

WEIGHT_NAMES = ['meta_tokens', 'ffn1_norm', 'ffn1_w13', 'ffn1_w2', 'mix_norm', 'w_in', 'ssm_lam_re', 'ssm_lam_im',
                'ssm_log_dt', 'ssm_b_re', 'ssm_b_im', 'ssm_c_re', 'ssm_c_im', 'ssm_d', 'ssm_w_glu', 'conv_w', 'conv_b',
                'conv_ln_g', 'conv_ln_b', 'conv_w_out', 'attn_w_o', 'w_out', 'ffn2_norm', 'ffn2_w13', 'ffn2_w2',
                'final_norm']
REPLICATED = ['ffn1_norm', 'mix_norm', 'ssm_lam_re', 'ssm_lam_im', 'ssm_log_dt', 'ssm_b_re', 'ssm_b_im', 'ssm_c_re',
              'ssm_c_im', 'ssm_d', 'conv_b', 'conv_ln_g', 'conv_ln_b', 'ffn2_norm', 'final_norm']


class Cfg(NamedTuple):
    d_model: int
    seq: int
    depth: int
    d_ff: int
    b_loc: int
    ndev: int = 8
    n_meta: int = 16
    n_heads: int = 8
    head_dim: int = 64
    ssm_group: int = 16
    ssm_state: int = 64
    conv_width: int = 31

    @property
    def d_ssm(self): return self.d_model // 2
    @property
    def groups(self): return self.d_ssm // self.ssm_group
    @property
    def n_state(self): return self.groups * self.ssm_state
    @property
    def d_conv(self): return self.d_model // 2
    @property
    def d_attn(self): return self.n_heads * self.head_dim
    @property
    def off_xa(self): return self.d_ssm
    @property
    def off_xg(self): return self.d_ssm + self.d_conv
    @property
    def off_q(self): return self.d_ssm + 2 * self.d_conv
    @property
    def off_k(self): return self.off_q + self.d_attn
    @property
    def off_v(self): return self.off_k + self.d_attn
    @property
    def off_gate(self): return self.off_v + self.d_attn
    @property
    def d_in(self): return self.off_gate + 3 * self.d_model
    @property
    def pad(self): return Q_BLOCK - self.n_meta
    @property
    def lp(self): return Q_BLOCK + self.seq
    @property
    def tokens(self): return self.b_loc * self.lp


FULL = Cfg(d_model=1024, seq=4096, depth=4, d_ff=2816, b_loc=2)


def _tile(n, pref, align):
    for t in range(min(pref, n), 0, -1):
        if n % t == 0 and t % align == 0:
            return t
    return n


def _cp(sem):
    return pltpu.CompilerParams(dimension_semantics=sem, vmem_limit_bytes=V7X_VMEM_LIMIT)


def _sds(shape, dtype):
    return jax.ShapeDtypeStruct(shape, dtype)


def _sigmoid(x):
    return jax.nn.sigmoid(x)


def _dot(a, b, ca, cb):
    return lax.dot_general(a, b, (((ca,), (cb,)), ((), ())), preferred_element_type=F32)


def rms_fwd(h, g, name):
    t, d = h.shape
    tm = _tile(t, 1056, 16)

    def body(h_ref, g_ref, o_ref):
        x = h_ref[...]
        r = lax.rsqrt(jnp.mean(x * x, axis=-1, keepdims=True) + RMS_EPS)
        o_ref[...] = (x * r * g_ref[...]).astype(BF16)

    return pl.pallas_call(
        body, name=name, grid=(t // tm,),
        in_specs=[pl.BlockSpec((tm, d), lambda i: (i, 0)), pl.BlockSpec((1, d), lambda i: (0, 0))],
        out_specs=pl.BlockSpec((tm, d), lambda i: (i, 0)), out_shape=_sds((t, d), BF16),
        compiler_params=_cp(("parallel",)))(h, g.reshape(1, d))


def rms_bwd(dxn, h, g, dh_res, name):
    t, d = h.shape
    tm = _tile(t, 528, 16)

    def body(dxn_ref, h_ref, g_ref, r_ref, dh_ref, dg_ref):
        x = h_ref[...]
        r = lax.rsqrt(jnp.mean(x * x, axis=-1, keepdims=True) + RMS_EPS)
        xhat = x * r
        dy = dxn_ref[...]
        dyg = dy * g_ref[...]
        dx = r * (dyg - xhat * jnp.mean(dyg * xhat, axis=-1, keepdims=True))
        dh_ref[...] = r_ref[...] + dx

        @pl.when(pl.program_id(0) == 0)
        def _():
            dg_ref[...] = jnp.zeros_like(dg_ref)

        dg_ref[...] += jnp.sum(dy * xhat, axis=0, keepdims=True)

    row = pl.BlockSpec((tm, d), lambda i: (i, 0))
    vec = pl.BlockSpec((1, d), lambda i: (0, 0))
    return pl.pallas_call(
        body, name=name, grid=(t // tm,), in_specs=[row, row, vec, row], out_specs=[row, vec],
        out_shape=[_sds((t, d), F32), _sds((1, d), F32)],
        compiler_params=_cp(("arbitrary",)))(dxn, h, g.reshape(1, d), dh_res)


def mm(pairs, *, nt, n, tm, tn, out_dtype, name, res=None, res_scale=1.0):
    m = pairs[0][0].shape[0]
    np_ = len(pairs)

    def body(*refs):
        o_ref = refs[-1]
        acc = None
        for p in range(np_):
            a = refs[2 * p][...].astype(BF16)
            b = refs[2 * p + 1][...].astype(BF16)
            d = _dot(a, b, 1, 1 if nt else 0)
            acc = d if acc is None else acc + d
        if res is not None:
            acc = refs[2 * np_][...] + res_scale * acc
        o_ref[...] = acc.astype(out_dtype)

    in_specs, args = [], []
    for a, b, kblk in pairs:
        k = a.shape[1]
        in_specs.append(pl.BlockSpec((tm, k), lambda i, j: (i, 0)))
        if nt:
            nb = n // tn
            in_specs.append(pl.BlockSpec((tn, k), functools.partial(lambda i, j, o: (j + o, 0), o=kblk * nb)))
        else:
            in_specs.append(pl.BlockSpec((k, tn), functools.partial(lambda i, j, o: (o, j), o=kblk)))
        args += [a, b]
    if res is not None:
        in_specs.append(pl.BlockSpec((tm, tn), lambda i, j: (i, j)))
        args.append(res)
    return pl.pallas_call(
        body, name=name, grid=(m // tm, n // tn), in_specs=in_specs,
        out_specs=pl.BlockSpec((tm, tn), lambda i, j: (i, j)), out_shape=_sds((m, n), out_dtype),
        compiler_params=_cp(("parallel", "arbitrary")))(*args)


def mm_tn(a, b, *, name, scale=1.0):
    t, m = a.shape
    n = b.shape[1]
    tm = _tile(m, 1536, 128)
    tn = _tile(n, 1024, 128)
    tk = _tile(t, 1056, 16)
    nk = t // tk

    def body(a_ref, b_ref, o_ref):
        k = pl.program_id(2)

        @pl.when(k == 0)
        def _():
            o_ref[...] = jnp.zeros_like(o_ref)

        o_ref[...] += _dot(a_ref[...].astype(BF16), b_ref[...].astype(BF16), 0, 0)
        if scale != 1.0:
            @pl.when(k == nk - 1)
            def _():
                o_ref[...] = o_ref[...] * scale

    return pl.pallas_call(
        body, name=name, grid=(m // tm, n // tn, nk),
        in_specs=[pl.BlockSpec((tk, tm), lambda i, j, k: (k, i)), pl.BlockSpec((tk, tn), lambda i, j, k: (k, j))],
        out_specs=pl.BlockSpec((tm, tn), lambda i, j, k: (i, j)), out_shape=_sds((m, n), F32),
        compiler_params=_cp(("parallel", "parallel", "arbitrary")))(a, b)


def dual_mm_act(x, wt, *, kind, name):
    t, k = x.shape
    hdim = wt.shape[0] // 2
    tm = _tile(t, 1056, 16)
    tn = _tile(hdim, 256, 128)
    nb = hdim // tn
    act_dtype = BF16 if kind == 'swiglu' else F32

    def body(x_ref, wa_ref, wb_ref, a_ref, b_ref, act_ref):
        xv = x_ref[...]
        a = _dot(xv, wa_ref[...], 1, 1)
        b = _dot(xv, wb_ref[...], 1, 1)
        a_ref[...] = a
        b_ref[...] = b
        if kind == 'swiglu':
            act_ref[...] = (a * _sigmoid(a) * b).astype(act_dtype)
        else:
            act_ref[...] = (a * _sigmoid(b)).astype(act_dtype)

    ob = pl.BlockSpec((tm, tn), lambda i, j: (i, j))
    return pl.pallas_call(
        body, name=name, grid=(t // tm, nb),
        in_specs=[pl.BlockSpec((tm, k), lambda i, j: (i, 0)), pl.BlockSpec((tn, k), lambda i, j: (j, 0)),
                  pl.BlockSpec((tn, k), lambda i, j: (j + nb, 0))],
        out_specs=[ob, ob, ob], out_shape=[_sds((t, hdim), F32), _sds((t, hdim), F32), _sds((t, hdim), act_dtype)],
        compiler_params=_cp(("parallel", "arbitrary")))(x, wt, wt)


def ffn_down_bwd(dh, w2, a, b, name):
    t, d = dh.shape
    hdim = w2.shape[0]
    tm = _tile(t, 1056, 16)
    tn = _tile(hdim, 256, 128)

    def body(dh_ref, w_ref, a_ref, b_ref, da_ref, db_ref):
        dact = 0.5 * _dot(dh_ref[...].astype(BF16), w_ref[...], 1, 1)
        av = a_ref[...]
        sg = _sigmoid(av)
        da_ref[...] = (dact * b_ref[...] * sg * (1.0 + av * (1.0 - sg))).astype(BF16)
        db_ref[...] = (dact * av * sg).astype(BF16)

    ob = pl.BlockSpec((tm, tn), lambda i, j: (i, j))
    return pl.pallas_call(
        body, name=name, grid=(t // tm, hdim // tn),
        in_specs=[pl.BlockSpec((tm, d), lambda i, j: (i, 0)), pl.BlockSpec((tn, d), lambda i, j: (j, 0)), ob, ob],
        out_specs=[ob, ob], out_shape=[_sds((t, hdim), BF16), _sds((t, hdim), BF16)],
        compiler_params=_cp(("parallel", "arbitrary")))(dh, w2, a, b)


def final_fwd_bwd(cfg, h, gf, target):
    bsz, lp, d = h.shape
    nq = lp // Q_BLOCK

    def body(h_ref, g_ref, t_ref, dh_ref, dg_ref, loss_ref):
        b, j = pl.program_id(0), pl.program_id(1)
        x = h_ref[0]
        r = lax.rsqrt(jnp.mean(x * x, axis=-1, keepdims=True) + RMS_EPS)
        xhat = x * r
        gv = g_ref[...]
        diff = jnp.where(j > 0, xhat * gv - t_ref[0], 0.0)
        dy = diff * (1.0 / d)
        dyg = dy * gv
        dh_ref[0] = r * (dyg - xhat * jnp.mean(dyg * xhat, axis=-1, keepdims=True))

        @pl.when((b == 0) & (j == 0))
        def _():
            dg_ref[...] = jnp.zeros_like(dg_ref)
            loss_ref[...] = jnp.zeros_like(loss_ref)

        dg_ref[...] += jnp.sum(dy * xhat, axis=0, keepdims=True)
        loss_ref[...] += jnp.sum(jnp.sum(diff * diff, axis=1, keepdims=True), axis=0, keepdims=True)

    blk = pl.BlockSpec((1, Q_BLOCK, d), lambda b, j: (b, j, 0))
    return pl.pallas_call(
        body, name="final_loss", grid=(bsz, nq),
        in_specs=[blk, pl.BlockSpec((1, d), lambda b, j: (0, 0)),
                  pl.BlockSpec((1, Q_BLOCK, d), lambda b, j: (b, jnp.maximum(j - 1, 0), 0))],
        out_specs=[blk, pl.BlockSpec((1, d), lambda b, j: (0, 0)), pl.BlockSpec((1, 1), lambda b, j: (0, 0))],
        out_shape=[_sds((bsz, lp, d), F32), _sds((1, d), F32), _sds((1, 1), F32)],
        compiler_params=_cp(("arbitrary", "arbitrary")))(h, gf.reshape(1, d), target)


def _row_valid(cfg, tm, i):
    pos = (i * tm) % cfg.lp + lax.broadcasted_iota(jnp.int32, (tm, 1), 0)
    return pos >= cfg.pad


def merge_fwd(cfg, proj, o_ssm, o_conv, o_attn):
    t, d = o_ssm.shape
    tc = math.gcd(math.gcd(d, cfg.off_gate), 512)
    tm = _tile(cfg.lp, 528, 16)
    g0 = cfg.off_gate // tc
    nc = d // tc

    def body(g0_ref, g1_ref, g2_ref, s_ref, c_ref, a_ref, o_ref):
        valid = _row_valid(cfg, tm, pl.program_id(0))
        m = (_sigmoid(g0_ref[...]) * s_ref[...] + _sigmoid(g1_ref[...]) * c_ref[...]
             + _sigmoid(g2_ref[...]) * a_ref[...])
        o_ref[...] = jnp.where(valid, m, 0.0).astype(BF16)

    gate = [pl.BlockSpec((tm, tc), functools.partial(lambda i, j, o: (i, o + j), o=g0 + k * nc)) for k in range(3)]
    blk = pl.BlockSpec((tm, tc), lambda i, j: (i, j))
    return pl.pallas_call(
        body, name="merge_fwd", grid=(t // tm, nc), in_specs=gate + [blk, blk, blk], out_specs=blk,
        out_shape=_sds((t, d), BF16), compiler_params=_cp(("parallel", "parallel")))(
            proj, proj, proj, o_ssm, o_conv, o_attn)


def merge_bwd(cfg, dmerged, proj, ga, gg, o_conv, o_attn):
    t, d = dmerged.shape
    tc = math.gcd(math.gcd(d, cfg.off_gate), 512)
    tm = _tile(cfg.lp, 528, 16)
    g0 = cfg.off_gate // tc
    nc = d // tc

    def body(dm_ref, g0_ref, g1_ref, g2_ref, ga_ref, gg_ref, c_ref, a_ref,
             dg0_ref, dg1_ref, dg2_ref, dga_ref, dgg_ref, dc_ref, da_ref):
        dm = dm_ref[...]
        s0, s1, s2 = _sigmoid(g0_ref[...]), _sigmoid(g1_ref[...]), _sigmoid(g2_ref[...])
        sg = _sigmoid(gg_ref[...])
        gav = ga_ref[...]
        o_s = gav * sg
        dg0_ref[...] = (dm * o_s * s0 * (1.0 - s0)).astype(BF16)
        dg1_ref[...] = (dm * c_ref[...] * s1 * (1.0 - s1)).astype(BF16)
        dg2_ref[...] = (dm * a_ref[...] * s2 * (1.0 - s2)).astype(BF16)
        dos = dm * s0
        dga_ref[...] = (dos * sg).astype(BF16)
        dgg_ref[...] = (dos * gav * sg * (1.0 - sg)).astype(BF16)
        dc_ref[...] = (dm * s1).astype(BF16)
        da_ref[...] = (dm * s2).astype(BF16)

    gate = [pl.BlockSpec((tm, tc), functools.partial(lambda i, j, o: (i, o + j), o=g0 + k * nc)) for k in range(3)]
    blk = pl.BlockSpec((tm, tc), lambda i, j: (i, j))
    out = _sds((t, d), BF16)
    return pl.pallas_call(
        body, name="merge_bwd", grid=(t // tm, nc), in_specs=[blk] + gate + [blk, blk, blk, blk],
        out_specs=[blk] * 7, out_shape=[out] * 7, compiler_params=_cp(("parallel", "parallel")))(
            dmerged, proj, proj, proj, ga, gg, o_conv, o_attn)


def _conv_tile(cfg):
    return _tile(cfg.lp, 384, CONV_HALO)


def _shift_up(x, off, rows):
    if off == 0:
        return x[:rows]
    return pltpu.roll(x, x.shape[0] - off, 0)[:rows]


def conv_fwd(cfg, proj, w, bias, ln_g, ln_b):
    t = proj.shape[0]
    wc = cfg.d_conv
    tt = _conv_tile(cfg)
    nc = cfg.lp // tt
    kw = cfg.conv_width
    first = CONV_HALO - (kw - 1)

    def body(a_ref, g_ref, ap_ref, gp_ref, w_ref, b_ref, lg_ref, lb_ref, cpre_ref, hn_ref):
        c = pl.program_id(1)
        hc = a_ref[...] * _sigmoid(g_ref[...])
        hp = jnp.where(c > 0, ap_ref[...] * _sigmoid(gp_ref[...]), 0.0)
        hcat = jnp.concatenate([hp, hc], axis=0)
        acc = jnp.zeros((tt, wc), F32) + b_ref[...]
        for k in range(kw):
            acc = acc + w_ref[k:k + 1, :] * _shift_up(hcat, first + k, tt)
        cpre_ref[...] = acc
        mu = jnp.mean(acc, axis=-1, keepdims=True)
        xc = acc - mu
        y = xc * lax.rsqrt(jnp.mean(xc * xc, axis=-1, keepdims=True) + LN_EPS) * lg_ref[...] + lb_ref[...]
        hn_ref[...] = (y * _sigmoid(y)).astype(BF16)

    ca, cg = cfg.off_xa // wc, cfg.off_xg // wc
    hb = tt // CONV_HALO
    lph = cfg.lp // CONV_HALO

    def halo(col):
        return pl.BlockSpec((CONV_HALO, wc), lambda b, c: (jnp.maximum(b * lph + c * hb - 1, 0), col))

    vec = pl.BlockSpec((1, wc), lambda b, c: (0, 0))
    blk = pl.BlockSpec((tt, wc), lambda b, c: (b * nc + c, 0))
    return pl.pallas_call(
        body, name="conv_fwd", grid=(cfg.b_loc, nc),
        in_specs=[pl.BlockSpec((tt, wc), lambda b, c: (b * nc + c, ca)),
                  pl.BlockSpec((tt, wc), lambda b, c: (b * nc + c, cg)), halo(ca), halo(cg),
                  pl.BlockSpec((kw, wc), lambda b, c: (0, 0)), vec, vec, vec],
        out_specs=[blk, blk], out_shape=[_sds((t, wc), F32), _sds((t, wc), BF16)],
        compiler_params=_cp(("parallel", "arbitrary")))(
            proj, proj, proj, proj, w, bias.reshape(1, wc), ln_g.reshape(1, wc), ln_b.reshape(1, wc))


def conv_bwd(cfg, dhn, cpre, proj, w, ln_g, ln_b):
    t = proj.shape[0]
    wc = cfg.d_conv
    tt = _conv_tile(cfg)
    nc = cfg.lp // tt
    kw = cfg.conv_width
    first = CONV_HALO - (kw - 1)

    def body(dhn_ref, dhn_nx_ref, cp_ref, cp_nx_ref, a_ref, g_ref, ap_ref, gp_ref, w_ref, lg_ref, lb_ref,
             dxa_ref, dxg_ref, dw_ref, db_ref, dlg_ref, dlb_ref):
        b, c = pl.program_id(0), pl.program_id(1)
        lg, lb = lg_ref[...], lb_ref[...]

        def ln_silu_bwd(dh, cp):
            mu = jnp.mean(cp, axis=-1, keepdims=True)
            xc = cp - mu
            rstd = lax.rsqrt(jnp.mean(xc * xc, axis=-1, keepdims=True) + LN_EPS)
            xhat = xc * rstd
            y = xhat * lg + lb
            sg = _sigmoid(y)
            dy = dh * sg * (1.0 + y * (1.0 - sg))
            dxh = dy * lg
            dc = rstd * (dxh - jnp.mean(dxh, axis=-1, keepdims=True)
                         - xhat * jnp.mean(dxh * xhat, axis=-1, keepdims=True))
            return dc, dy, xhat

        dc, dy, xhat = ln_silu_bwd(dhn_ref[...], cp_ref[...])
        dc_nx, _, _ = ln_silu_bwd(dhn_nx_ref[...], cp_nx_ref[...])
        dc_nx = jnp.where(c < nc - 1, dc_nx, 0.0)
        dcat = jnp.concatenate([dc, dc_nx], axis=0)
        dhc = jnp.zeros((tt, wc), F32)
        for k in range(kw):
            dhc = dhc + w_ref[k:k + 1, :] * _shift_up(dcat, kw - 1 - k, tt)
        av, gv = a_ref[...], g_ref[...]
        sg = _sigmoid(gv)
        valid = (c * tt + lax.broadcasted_iota(jnp.int32, (tt, 1), 0)) >= cfg.pad
        dhc = jnp.where(valid, dhc, 0.0)
        dxa_ref[...] = (dhc * sg).astype(BF16)
        dxg_ref[...] = (dhc * av * sg * (1.0 - sg)).astype(BF16)

        @pl.when((b == 0) & (c == 0))
        def _():
            dw_ref[...] = jnp.zeros_like(dw_ref)
            db_ref[...] = jnp.zeros_like(db_ref)
            dlg_ref[...] = jnp.zeros_like(dlg_ref)
            dlb_ref[...] = jnp.zeros_like(dlb_ref)

        hp = jnp.where(c > 0, ap_ref[...] * _sigmoid(gp_ref[...]), 0.0)
        hcat = jnp.concatenate([hp, av * sg], axis=0)
        for k in range(kw):
            dw_ref[k:k + 1, :] += jnp.sum(dc * _shift_up(hcat, first + k, tt), axis=0, keepdims=True)
        db_ref[...] += jnp.sum(dc, axis=0, keepdims=True)
        dlg_ref[...] += jnp.sum(dy * xhat, axis=0, keepdims=True)
        dlb_ref[...] += jnp.sum(dy, axis=0, keepdims=True)

    ca, cg = cfg.off_xa // wc, cfg.off_xg // wc
    hb = tt // CONV_HALO
    lph = cfg.lp // CONV_HALO
    last = cfg.b_loc * lph - 1

    def prev(col):
        return pl.BlockSpec((CONV_HALO, wc), lambda b, c: (jnp.maximum(b * lph + c * hb - 1, 0), col))

    nxt = pl.BlockSpec((CONV_HALO, wc), lambda b, c: (jnp.minimum(b * lph + (c + 1) * hb, last), 0))
    vec = pl.BlockSpec((1, wc), lambda b, c: (0, 0))
    blk = pl.BlockSpec((tt, wc), lambda b, c: (b * nc + c, 0))
    wspec = pl.BlockSpec((kw, wc), lambda b, c: (0, 0))
    return pl.pallas_call(
        body, name="conv_bwd", grid=(cfg.b_loc, nc),
        in_specs=[blk, nxt, blk, nxt, pl.BlockSpec((tt, wc), lambda b, c: (b * nc + c, ca)),
                  pl.BlockSpec((tt, wc), lambda b, c: (b * nc + c, cg)), prev(ca), prev(cg), wspec, vec, vec],
        out_specs=[blk, blk, wspec, vec, vec, vec],
        out_shape=[_sds((t, wc), BF16), _sds((t, wc), BF16), _sds((kw, wc), F32), _sds((1, wc), F32),
                   _sds((1, wc), F32), _sds((1, wc), F32)],
        compiler_params=_cp(("arbitrary", "arbitrary")))(
            dhn, dhn, cpre, cpre, proj, proj, proj, proj, w, ln_g.reshape(1, wc), ln_b.reshape(1, wc))


def _split_bf16(x):
    hi = x.astype(BF16)
    return hi, (x - hi.astype(F32)).astype(BF16)


def _sb_logits(cfg, qh, kblk, kb, row_t):
    z = _dot(qh, kblk, 1, 1)
    col_s = kb * Q_BLOCK + lax.broadcasted_iota(jnp.int32, (1, Q_BLOCK), 1)
    mask = (col_s < row_t) & (col_s >= cfg.pad)
    tneg = jnp.exp(-jnp.abs(z))
    lk = jnp.where(mask, -(jnp.maximum(z, 0.0) + jnp.log(1.0 + tneg)), 0.0)
    return z, mask, tneg, lk


def _tri_sum(x, tri):
    hi, lo = _split_bf16(x)
    return _dot(hi, tri, 1, 0) + _dot(lo, tri, 1, 0)


def attn_fwd(cfg, proj):
    t = proj.shape[0]
    lp = cfg.lp
    nq = lp // Q_BLOCK
    nhp = cfg.d_attn // V7X_LANES
    cq, ck, cv = cfg.off_q // V7X_LANES, cfg.off_k // V7X_LANES, cfg.off_v // V7X_LANES
    scale = 1.0 / math.sqrt(cfg.head_dim)

    def body(q_ref, k_ref, v_ref, o_ref, lt_ref, kbf, vbf):
        qb = pl.program_id(2)

        @pl.when(qb == 0)
        def _():
            kbf[...] = k_ref[...].astype(BF16)
            vbf[...] = v_ref[...].astype(BF16)

        q = q_ref[...] * scale
        lane = lax.broadcasted_iota(jnp.int32, (1, V7X_LANES), 1)
        row_t = qb * Q_BLOCK + lax.broadcasted_iota(jnp.int32, (Q_BLOCK, 1), 0)
        ri = lax.broadcasted_iota(jnp.int32, (Q_BLOCK, Q_BLOCK), 0)
        ci = lax.broadcasted_iota(jnp.int32, (Q_BLOCK, Q_BLOCK), 1)
        tri_from = (ri >= ci).astype(BF16)
        out = jnp.zeros((Q_BLOCK, V7X_LANES), F32)
        tot = jnp.zeros((Q_BLOCK, V7X_LANES), F32)
        for h in range(2):
            mh = (lane < cfg.head_dim) if h == 0 else (lane >= cfg.head_dim)
            qh = jnp.where(mh, q, 0.0).astype(BF16)

            def kb_body(i, carry, qh=qh):
                acc, r_carry = carry
                kb = qb - i
                off = pl.multiple_of(kb * Q_BLOCK, Q_BLOCK)
                z, mask, _, lk = _sb_logits(cfg, qh, kbf[pl.ds(off, Q_BLOCK), :], kb, row_t)
                w = jnp.where(mask, jnp.exp(z + _tri_sum(lk, tri_from) + r_carry), 0.0)
                acc = acc + _dot(w.astype(BF16), vbf[pl.ds(off, Q_BLOCK), :], 1, 0)
                return acc, r_carry + jnp.sum(lk, axis=1, keepdims=True)

            acc, r_tot = lax.fori_loop(0, qb + 1, kb_body,
                                       (jnp.zeros((Q_BLOCK, V7X_LANES), F32), jnp.zeros((Q_BLOCK, 1), F32)))
            out = out + jnp.where(mh, acc, 0.0)
            tot = tot + jnp.where(lane == h * cfg.head_dim, r_tot, 0.0)
        o_ref[...] = out
        lt_ref[...] = tot

    oblk = pl.BlockSpec((Q_BLOCK, V7X_LANES), lambda b, hp, qb: (b * nq + qb, hp))
    return pl.pallas_call(
        body, name="attn_fwd", grid=(cfg.b_loc, nhp, nq),
        in_specs=[pl.BlockSpec((Q_BLOCK, V7X_LANES), lambda b, hp, qb: (b * nq + qb, cq + hp)),
                  pl.BlockSpec((lp, V7X_LANES), lambda b, hp, qb: (b, ck + hp)),
                  pl.BlockSpec((lp, V7X_LANES), lambda b, hp, qb: (b, cv + hp))],
        out_specs=[oblk, oblk], out_shape=[_sds((t, cfg.d_attn), F32), _sds((t, cfg.d_attn), F32)],
        scratch_shapes=[pltpu.VMEM((lp, V7X_LANES), BF16), pltpu.VMEM((lp, V7X_LANES), BF16)],
        compiler_params=_cp(("parallel", "parallel", "arbitrary")))(proj, proj, proj)


def attn_bwd(cfg, proj, lt, do):
    t = proj.shape[0]
    lp = cfg.lp
    nq = lp // Q_BLOCK
    nhp = cfg.d_attn // V7X_LANES
    cq, ck, cv = cfg.off_q // V7X_LANES, cfg.off_k // V7X_LANES, cfg.off_v // V7X_LANES
    scale = 1.0 / math.sqrt(cfg.head_dim)

    def body(q_ref, k_ref, v_ref, lt_ref, do_ref, dq_ref, dk_ref, dv_ref, kbf, vbf):
        qb = pl.program_id(2)

        @pl.when(qb == 0)
        def _():
            kbf[...] = k_ref[...].astype(BF16)
            vbf[...] = v_ref[...].astype(BF16)
            dk_ref[...] = jnp.zeros_like(dk_ref)
            dv_ref[...] = jnp.zeros_like(dv_ref)

        q = q_ref[...] * scale
        ltv = lt_ref[...]
        dov = do_ref[...]
        lane = lax.broadcasted_iota(jnp.int32, (1, V7X_LANES), 1)
        row_t = qb * Q_BLOCK + lax.broadcasted_iota(jnp.int32, (Q_BLOCK, 1), 0)
        ri = lax.broadcasted_iota(jnp.int32, (Q_BLOCK, Q_BLOCK), 0)
        ci = lax.broadcasted_iota(jnp.int32, (Q_BLOCK, Q_BLOCK), 1)
        tri_before = (ri < ci).astype(BF16)
        tri_upto = (ri <= ci).astype(BF16)
        dq_tot = jnp.zeros((Q_BLOCK, V7X_LANES), F32)
        for h in range(2):
            mh = (lane < cfg.head_dim) if h == 0 else (lane >= cfg.head_dim)
            qh = jnp.where(mh, q, 0.0).astype(BF16)
            dohb = jnp.where(mh, dov, 0.0).astype(BF16)
            lk_tot = jnp.sum(jnp.where(lane == h * cfg.head_dim, ltv, 0.0), axis=1, keepdims=True)

            def kb_body(kb, carry, qh=qh, dohb=dohb, lk_tot=lk_tot):
                dq_acc, lk_before, e_before = carry
                off = pl.multiple_of(kb * Q_BLOCK, Q_BLOCK)
                kblk = kbf[pl.ds(off, Q_BLOCK), :]
                vblk = vbf[pl.ds(off, Q_BLOCK), :]
                z, mask, tneg, lk = _sb_logits(cfg, qh, kblk, kb, row_t)
                lk_from = lk_tot - lk_before - _tri_sum(lk, tri_before)
                w = jnp.where(mask, jnp.exp(z + lk_from), 0.0)
                e = w * _dot(dohb, vblk, 1, 1)
                e_upto = e_before + _tri_sum(e, tri_upto)
                sig = jnp.where(z >= 0.0, 1.0, tneg) / (1.0 + tneg)
                dz = jnp.where(mask, e - sig * e_upto, 0.0).astype(BF16)
                dq_acc = dq_acc + _dot(dz, kblk, 1, 0)
                dk_ref[pl.ds(off, Q_BLOCK), :] += _dot(dz, qh, 0, 0)
                dv_ref[pl.ds(off, Q_BLOCK), :] += _dot(w.astype(BF16), dohb, 0, 0)
                return (dq_acc, lk_before + jnp.sum(lk, axis=1, keepdims=True),
                        e_before + jnp.sum(e, axis=1, keepdims=True))

            zero_col = jnp.zeros((Q_BLOCK, 1), F32)
            dq_acc, _, _ = lax.fori_loop(0, qb + 1, kb_body,
                                         (jnp.zeros((Q_BLOCK, V7X_LANES), F32), zero_col, zero_col))
            dq_tot = dq_tot + jnp.where(mh, dq_acc, 0.0)
        dq_ref[...] = dq_tot * scale

    qblk = pl.BlockSpec((Q_BLOCK, V7X_LANES), lambda b, hp, qb: (b * nq + qb, hp))
    seq = pl.BlockSpec((lp, V7X_LANES), lambda b, hp, qb: (b, hp))
    out = _sds((t, cfg.d_attn), F32)
    return pl.pallas_call(
        body, name="attn_bwd", grid=(cfg.b_loc, nhp, nq),
        in_specs=[pl.BlockSpec((Q_BLOCK, V7X_LANES), lambda b, hp, qb: (b * nq + qb, cq + hp)),
                  pl.BlockSpec((lp, V7X_LANES), lambda b, hp, qb: (b, ck + hp)),
                  pl.BlockSpec((lp, V7X_LANES), lambda b, hp, qb: (b, cv + hp)), qblk, qblk],
        out_specs=[qblk, seq, seq], out_shape=[out, out, out],
        scratch_shapes=[pltpu.VMEM((lp, V7X_LANES), BF16), pltpu.VMEM((lp, V7X_LANES), BF16)],
        compiler_params=_cp(("parallel", "parallel", "arbitrary")))(proj, proj, proj, lt, do)


N_POW = 8


def _discretize(lr, li, logdt, br_t, bi_t):
    dt = jnp.exp(logdt)
    mag = jnp.exp(lr * dt)
    ab_re = mag * jnp.cos(li * dt)
    ab_im = mag * jnp.sin(li * dt)
    den = lr * lr + li * li
    nr = ab_re - 1.0
    ni = ab_im
    coef_re = (nr * lr + ni * li) / den
    coef_im = (ni * lr - nr * li) / den
    bb_re = coef_re[None] * br_t - coef_im[None] * bi_t
    bb_im = coef_re[None] * bi_t + coef_im[None] * br_t
    return ab_re, ab_im, bb_re, bb_im


def ssm_param_fwd(cfg, lr, li, logdt, br_t, bi_t):
    g, n, c = cfg.groups, cfg.ssm_state, cfg.ssm_group

    def body(lr_ref, li_ref, dt_ref, br_ref, bi_ref, ar_ref, ai_ref, bbr_ref, bbi_ref, pr_ref, pi_ref,
             tr_ref, ti_ref):
        ab_re, ab_im, bb_re, bb_im = _discretize(lr_ref[...], li_ref[...], dt_ref[...], br_ref[...], bi_ref[...])
        ar_ref[...] = ab_re
        ai_ref[...] = ab_im
        bbr_ref[...] = bb_re
        bbi_ref[...] = bb_im
        pr, pi = ab_re, ab_im
        for k in range(N_POW):
            pr_ref[k] = pr
            pi_ref[k] = pi
            pr, pi = pr * pr - pi * pi, 2.0 * pr * pi
        cr, ci = ab_re, ab_im
        for j in range(SSM_CHUNK):
            tr_ref[j] = cr
            ti_ref[j] = ci
            cr, ci = cr * ab_re - ci * ab_im, cr * ab_im + ci * ab_re

    gn, cgn = _sds((g, n), F32), _sds((c, g, n), F32)
    return pl.pallas_call(
        body, name="ssm_param_fwd",
        out_shape=[gn, gn, cgn, cgn, _sds((N_POW, g, n), F32), _sds((N_POW, g, n), F32),
                   _sds((SSM_CHUNK, g, n), F32), _sds((SSM_CHUNK, g, n), F32)])(lr, li, logdt, br_t, bi_t)


def ssm_param_bwd(cfg, lr, li, logdt, br_t, bi_t, dar, dai, dbbr, dbbi):
    g, n, c = cfg.groups, cfg.ssm_state, cfg.ssm_group

    def body(lr_ref, li_ref, dt_ref, br_ref, bi_ref, dar_ref, dai_ref, dbr_ref, dbi_ref,
             olr_ref, oli_ref, odt_ref, obr_ref, obi_ref):
        _, vjp = jax.vjp(_discretize, lr_ref[...], li_ref[...], dt_ref[...], br_ref[...], bi_ref[...])
        dlr, dli, ddt, dbr, dbi = vjp((dar_ref[...], dai_ref[...], dbr_ref[...], dbi_ref[...]))
        olr_ref[...] = dlr
        oli_ref[...] = dli
        odt_ref[...] = ddt
        obr_ref[...] = dbr
        obi_ref[...] = dbi

    gn, cgn = _sds((g, n), F32), _sds((c, g, n), F32)
    return pl.pallas_call(body, name="ssm_param_bwd", out_shape=[gn, gn, _sds((g, 1), F32), cgn, cgn])(
        lr, li, logdt, br_t, bi_t, dar, dai, dbbr, dbbi)


def _scan(xr, xi, pr_ref, pi_ref, reverse):
    ct = xr.shape[0]
    rows = lax.broadcasted_iota(jnp.int32, (ct, 1), 0)
    d, k = 1, 0
    while d < ct:
        if reverse:
            sr, si = pltpu.roll(xr, ct - d, 0), pltpu.roll(xi, ct - d, 0)
            keep = rows < ct - d
        else:
            sr, si = pltpu.roll(xr, d, 0), pltpu.roll(xi, d, 0)
            keep = rows >= d
        sr = jnp.where(keep, sr, 0.0)
        si = jnp.where(keep, si, 0.0)
        ar = pr_ref[0, k:k + 1, :]
        ai = -pi_ref[0, k:k + 1, :] if reverse else pi_ref[0, k:k + 1, :]
        xr, xi = xr + ar * sr - ai * si, xi + ar * si + ai * sr
        d *= 2
        k += 1
    return xr, xi


def _gelu(x):
    k = math.sqrt(2.0 / math.pi)
    return 0.5 * x * (1.0 + jnp.tanh(k * (x + 0.044715 * x * x * x)))


def _gelu_grad(x):
    k = math.sqrt(2.0 / math.pi)
    th = jnp.tanh(k * (x + 0.044715 * x * x * x))
    return 0.5 * (1.0 + th) + 0.5 * x * (1.0 - th * th) * k * (1.0 + 3.0 * 0.044715 * x * x)


def ssm_fwd(cfg, proj, bb_re, bb_im, ct_re, ct_im, pw_re, pw_im, tab_re, tab_im, dskip):
    t = proj.shape[0]
    nsb = cfg.d_ssm // SSM_SB
    ns = SSM_SB // cfg.ssm_group * cfg.ssm_state
    ct = SSM_CHUNK
    nc = cfg.lp // ct

    def body(u_ref, bbr_ref, bbi_ref, ctr_ref, cti_ref, pr_ref, pi_ref, tr_ref, ti_ref, d_ref,
             sr_ref, si_ref, yt_ref, y2_ref, cr_scr, ci_scr):
        c = pl.program_id(2)

        @pl.when(c == 0)
        def _():
            cr_scr[...] = jnp.zeros_like(cr_scr)
            ci_scr[...] = jnp.zeros_like(ci_scr)

        u = u_ref[...]
        ub = u.astype(BF16)
        xr, xi = _scan(_dot(ub, bbr_ref[0], 1, 0), _dot(ub, bbi_ref[0], 1, 0), pr_ref, pi_ref, False)
        cr, ci = cr_scr[0:1, :], ci_scr[0:1, :]
        tr, ti = tr_ref[0], ti_ref[0]
        sr = xr + tr * cr - ti * ci
        si = xi + tr * ci + ti * cr
        sr_ref[...] = sr
        si_ref[...] = si
        cr_scr[0:1, :] = sr_ref[ct - 1:ct, :]
        ci_scr[0:1, :] = si_ref[ct - 1:ct, :]
        y = _dot(sr.astype(BF16), ctr_ref[0], 1, 0) - _dot(si.astype(BF16), cti_ref[0], 1, 0) + d_ref[...] * u
        yt_ref[...] = y
        y2_ref[...] = _gelu(y).astype(BF16)

    def mat(r, c_):
        return pl.BlockSpec((1, r, c_), lambda sb, b, c: (sb, 0, 0))

    ublk = pl.BlockSpec((ct, SSM_SB), lambda sb, b, c: (b * nc + c, sb))
    sblk = pl.BlockSpec((ct, ns), lambda sb, b, c: (b * nc + c, sb))
    return pl.pallas_call(
        body, name="ssm_fwd", grid=(nsb, cfg.b_loc, nc),
        in_specs=[ublk, mat(SSM_SB, ns), mat(SSM_SB, ns), mat(ns, SSM_SB), mat(ns, SSM_SB), mat(N_POW, ns),
                  mat(N_POW, ns), mat(ct, ns), mat(ct, ns), pl.BlockSpec((1, SSM_SB), lambda sb, b, c: (0, sb))],
        out_specs=[sblk, sblk, ublk, ublk],
        out_shape=[_sds((t, cfg.n_state), F32), _sds((t, cfg.n_state), F32), _sds((t, cfg.d_ssm), F32),
                   _sds((t, cfg.d_ssm), BF16)],
        scratch_shapes=[pltpu.VMEM((8, ns), F32), pltpu.VMEM((8, ns), F32)],
        compiler_params=_cp(("parallel", "arbitrary", "arbitrary")))(
            proj, bb_re, bb_im, ct_re, ct_im, pw_re, pw_im, tab_re, tab_im, dskip.reshape(1, cfg.d_ssm))


def ssm_bwd(cfg, dy2, ytot, proj, s_re, s_im, cf_re, cf_im, bbt_re, bbt_im, pw_re, pw_im, tabr_re, tabr_im, dskip):
    t = proj.shape[0]
    nsb = cfg.d_ssm // SSM_SB
    ns = SSM_SB // cfg.ssm_group * cfg.ssm_state
    ct = SSM_CHUNK
    nc = cfg.lp // ct

    def body(dy_ref, yt_ref, u_ref, sr_ref, si_ref, spr_ref, spi_ref, cfr_ref, cfi_ref, btr_ref, bti_ref,
             pr_ref, pi_ref, tr_ref, ti_ref, d_ref,
             du_ref, dctr_ref, dcti_ref, dbbr_ref, dbbi_ref, dar_ref, dai_ref, dd_ref,
             cr_scr, ci_scr, ar_scr, ai_scr):
        b, c = pl.program_id(1), pl.program_id(2)
        chunk = nc - 1 - c

        @pl.when((b == 0) & (c == 0))
        def _():
            for r in (dctr_ref, dcti_ref, dbbr_ref, dbbi_ref, dar_ref, dai_ref, dd_ref):
                r[...] = jnp.zeros_like(r)

        @pl.when(c == 0)
        def _():
            cr_scr[...] = jnp.zeros_like(cr_scr)
            ci_scr[...] = jnp.zeros_like(ci_scr)

        u = u_ref[...]
        dyt = dy_ref[...] * _gelu_grad(yt_ref[...])
        dytb = dyt.astype(BF16)
        xr, xi = _scan(_dot(dytb, cfr_ref[0], 1, 0), -_dot(dytb, cfi_ref[0], 1, 0), pr_ref, pi_ref, True)
        cr, ci = cr_scr[0:1, :], ci_scr[0:1, :]
        tr, ti = tr_ref[0], -ti_ref[0]
        a_re = xr + tr * cr - ti * ci
        a_im = xi + tr * ci + ti * cr
        ar_scr[...] = a_re
        ai_scr[...] = a_im
        cr_scr[0:1, :] = ar_scr[0:1, :]
        ci_scr[0:1, :] = ai_scr[0:1, :]

        sr, si = sr_ref[...], si_ref[...]
        rows = lax.broadcasted_iota(jnp.int32, (ct, 1), 0)
        keep_prev = chunk > 0
        pr_last = jnp.where(keep_prev, spr_ref[7:8, :], 0.0)
        pi_last = jnp.where(keep_prev, spi_ref[7:8, :], 0.0)
        sp_re = jnp.where(rows == 0, pr_last, pltpu.roll(sr, 1, 0))
        sp_im = jnp.where(rows == 0, pi_last, pltpu.roll(si, 1, 0))
        dar_ref[0] += jnp.sum(a_re * sp_re + a_im * sp_im, axis=0, keepdims=True)
        dai_ref[0] += jnp.sum(a_im * sp_re - a_re * sp_im, axis=0, keepdims=True)
        dctr_ref[0] += _dot(sr.astype(BF16), dytb, 0, 0)
        dcti_ref[0] += -_dot(si.astype(BF16), dytb, 0, 0)
        ub = u.astype(BF16)
        arb, aib = a_re.astype(BF16), a_im.astype(BF16)
        dbbr_ref[0] += _dot(ub, arb, 0, 0)
        dbbi_ref[0] += _dot(ub, aib, 0, 0)
        du = dyt * d_ref[...] + _dot(arb, btr_ref[0], 1, 0) + _dot(aib, bti_ref[0], 1, 0)
        valid = (chunk * ct + rows) >= cfg.pad
        du_ref[...] = jnp.where(valid, du, 0.0).astype(BF16)
        dd_ref[...] += jnp.sum(dyt * u, axis=0, keepdims=True)

    def mat(r, c_):
        return pl.BlockSpec((1, r, c_), lambda sb, b, c: (sb, 0, 0))

    lp8 = cfg.lp // 8
    ublk = pl.BlockSpec((ct, SSM_SB), lambda sb, b, c: (b * nc + nc - 1 - c, sb))
    sblk = pl.BlockSpec((ct, ns), lambda sb, b, c: (b * nc + nc - 1 - c, sb))
    sprev = pl.BlockSpec((8, ns), lambda sb, b, c: (jnp.maximum(b * lp8 + (nc - 1 - c) * (ct // 8) - 1, 0), sb))
    dvec = pl.BlockSpec((1, SSM_SB), lambda sb, b, c: (0, sb))
    return pl.pallas_call(
        body, name="ssm_bwd", grid=(nsb, cfg.b_loc, nc),
        in_specs=[ublk, ublk, ublk, sblk, sblk, sprev, sprev, mat(SSM_SB, ns), mat(SSM_SB, ns), mat(ns, SSM_SB),
                  mat(ns, SSM_SB), mat(N_POW, ns), mat(N_POW, ns), mat(ct, ns), mat(ct, ns), dvec],
        out_specs=[ublk, mat(ns, SSM_SB), mat(ns, SSM_SB), mat(SSM_SB, ns), mat(SSM_SB, ns), mat(1, ns), mat(1, ns),
                   dvec],
        out_shape=[_sds((t, cfg.d_ssm), BF16), _sds((nsb, ns, SSM_SB), F32), _sds((nsb, ns, SSM_SB), F32),
                   _sds((nsb, SSM_SB, ns), F32), _sds((nsb, SSM_SB, ns), F32), _sds((nsb, 1, ns), F32),
                   _sds((nsb, 1, ns), F32), _sds((1, cfg.d_ssm), F32)],
        scratch_shapes=[pltpu.VMEM((8, ns), F32), pltpu.VMEM((8, ns), F32), pltpu.VMEM((ct, ns), F32),
                        pltpu.VMEM((ct, ns), F32)],
        compiler_params=_cp(("arbitrary", "arbitrary", "arbitrary")))(
            dy2, ytot, proj, s_re, s_im, s_re, s_im, cf_re, cf_im, bbt_re, bbt_im, pw_re, pw_im, tabr_re, tabr_im,
            dskip.reshape(1, cfg.d_ssm))


def _ssm_tables(cfg, lam_re, lam_im, log_dt, b_re, b_im, c_re, c_im):
    g, n, c = cfg.groups, cfg.ssm_state, cfg.ssm_group
    gsb = SSM_SB // c
    nsb = cfg.d_ssm // SSM_SB
    ns = gsb * n
    br_t, bi_t = jnp.transpose(b_re, (2, 0, 1)), jnp.transpose(b_im, (2, 0, 1))
    prm = (lam_re, lam_im, log_dt.reshape(g, 1), br_t, bi_t)
    _, _, bbr, bbi, pr, pi, tr, ti = ssm_param_fwd(cfg, *prm)
    eye = jnp.eye(gsb, dtype=F32)

    def bdiag_b(x):
        return jnp.einsum('csgn,gh->sgchn', x.reshape(c, nsb, gsb, n), eye).reshape(nsb, SSM_SB, ns)

    def bdiag_c(x):
        return jnp.einsum('sgcn,gh->sgchn', x.reshape(nsb, gsb, c, n), eye).reshape(nsb, SSM_SB, ns)

    def lanes(x):
        return jnp.transpose(x.reshape(x.shape[0], nsb, ns), (1, 0, 2))

    bb_re, bb_im = bdiag_b(bbr), bdiag_b(bbi)
    cf_re, cf_im = bdiag_c(c_re), bdiag_c(c_im)
    sw = lambda x: jnp.swapaxes(x, 1, 2)
    return dict(prm=prm, bb_re=bb_re.astype(BF16), bb_im=bb_im.astype(BF16), bbt_re=sw(bb_re).astype(BF16),
                bbt_im=sw(bb_im).astype(BF16), cf_re=cf_re.astype(BF16), cf_im=cf_im.astype(BF16),
                ct_re=sw(cf_re).astype(BF16), ct_im=sw(cf_im).astype(BF16), pw_re=lanes(pr), pw_im=lanes(pi),
                tab_re=lanes(tr), tab_im=lanes(ti), tabr_re=lanes(tr[::-1]), tabr_im=lanes(ti[::-1]))


def _ssm_param_grads(cfg, tabs, dct_re, dct_im, dbb_re, dbb_im, dab_re, dab_im):
    g, n, c = cfg.groups, cfg.ssm_state, cfg.ssm_group
    gsb = SSM_SB // c
    nsb = cfg.d_ssm // SSM_SB

    def diag_b(x):
        d = jnp.einsum('sgcgn->sgcn', x.reshape(nsb, gsb, c, gsb, n))
        return jnp.transpose(d.reshape(g, c, n), (1, 0, 2))

    def diag_c(x):
        d = jnp.einsum('sgngc->sgcn', x.reshape(nsb, gsb, n, gsb, c))
        return d.reshape(g, c, n)

    dlr, dli, ddt, dbr_t, dbi_t = ssm_param_bwd(cfg, *tabs['prm'], dab_re.reshape(g, n), dab_im.reshape(g, n),
                                                diag_b(dbb_re), diag_b(dbb_im))
    return dict(ssm_lam_re=dlr, ssm_lam_im=dli, ssm_log_dt=ddt.reshape(g),
                ssm_b_re=jnp.transpose(dbr_t, (1, 2, 0)), ssm_b_im=jnp.transpose(dbi_t, (1, 2, 0)),
                ssm_c_re=diag_c(dct_re), ssm_c_im=diag_c(dct_im))


def _mm_tiles(t, n, k):
    tm = _tile(t, 1056 if k <= 3072 else 528, 16)
    tn = _tile(n, 512, 128)
    return tm, tn


def _nt(a, wt, name, out_dtype=F32):
    tm, tn = _mm_tiles(a.shape[0], wt.shape[0], a.shape[1])
    return mm([(a, wt, 0)], nt=True, n=wt.shape[0], tm=tm, tn=tn, out_dtype=out_dtype, name=name)


def _nn(pairs, n, name, res=None, res_scale=1.0, out_dtype=F32):
    kmax = max(p[0].shape[1] for p in pairs) * len(pairs)
    tm, tn = _mm_tiles(pairs[0][0].shape[0], n, kmax)
    return mm(pairs, nt=False, n=n, tm=tm, tn=tn, out_dtype=out_dtype, name=name, res=res, res_scale=res_scale)


def local_step(cfg, x, target, wf, rep):
    d, lp, bsz, t = cfg.d_model, cfg.lp, cfg.b_loc, cfg.tokens
    hdim = cfg.d_ff
    meta = jnp.broadcast_to(wf['meta_tokens'][None], (bsz, cfg.n_meta, d))
    h = jnp.concatenate([jnp.zeros((bsz, cfg.pad, d), F32), meta, x], axis=1).reshape(t, d)

    saved = []
    for l in range(cfg.depth):
        w = {k: v[l] for k, v in wf.items() if k not in ('meta_tokens',)}
        r = {k: v[l] for k, v in rep.items() if k != 'final_norm'}
        s = {'h0': h}
        s['xn1'] = rms_fwd(h, r['ffn1_norm'], "rms_fwd")
        s['a1'], s['b1'], s['act1'] = dual_mm_act(s['xn1'], w['w13t_1'], kind='swiglu', name="ffn_up")
        h = _nn([(s['act1'], w['w2_1'], 0)], d, "ffn_down", res=h, res_scale=0.5)
        s['h1'] = h
        s['xnm'] = rms_fwd(h, r['mix_norm'], "rms_fwd")
        proj = _nt(s['xnm'], w['w_int'], "proj_in")
        s['proj'] = proj
        tabs = _ssm_tables(cfg, r['ssm_lam_re'], r['ssm_lam_im'], r['ssm_log_dt'], r['ssm_b_re'], r['ssm_b_im'],
                           r['ssm_c_re'], r['ssm_c_im'])
        s['tabs'] = tabs
        s['s_re'], s['s_im'], s['ytot'], s['y2'] = ssm_fwd(
            cfg, proj, tabs['bb_re'], tabs['bb_im'], tabs['ct_re'], tabs['ct_im'], tabs['pw_re'], tabs['pw_im'],
            tabs['tab_re'], tabs['tab_im'], r['ssm_d'])
        s['ga'], s['gg'], o_ssm = dual_mm_act(s['y2'], w['w_glut'], kind='glu', name="ssm_glu")
        s['cpre'], s['hn'] = conv_fwd(cfg, proj, w['conv_w'], r['conv_b'], r['conv_ln_g'], r['conv_ln_b'])
        s['o_conv'] = _nt(s['hn'], w['conv_w_outt'], "branch_out")
        s['o'], s['lt'] = attn_fwd(cfg, proj)
        s['o_attn'] = _nt(s['o'], w['attn_w_ot'], "branch_out")
        s['merged'] = merge_fwd(cfg, proj, o_ssm, s['o_conv'], s['o_attn'])
        h = _nn([(s['merged'], w['w_out'], 0)], d, "mix_out", res=h, res_scale=1.0)
        s['h2'] = h
        s['xn2'] = rms_fwd(h, r['ffn2_norm'], "rms_fwd")
        s['a2'], s['b2'], s['act2'] = dual_mm_act(s['xn2'], w['w13t_2'], kind='swiglu', name="ffn_up")
        h = _nn([(s['act2'], w['w2_2'], 0)], d, "ffn_down", res=h, res_scale=0.5)
        saved.append((w, r, s))

    dh3, dgf, loss_sq = final_fwd_bwd(cfg, h.reshape(bsz, lp, d), rep['final_norm'], target)
    dh = dh3.reshape(t, d)
    loss = 0.5 * loss_sq[0, 0] / d

    gl = []
    for l in reversed(range(cfg.depth)):
        w, r, s = saved[l]
        g = {}

        def ffn_bwd(dh, tag, a, b, act, xn, hin, norm):
            da, db = ffn_down_bwd(dh, w['w2_' + tag], a, b, "ffn_down_bwd")
            g['w2_' + tag] = mm_tn(act, dh, name="ffn_dw2", scale=0.5)
            dxn = _nn([(da, w['w13t_' + tag], 0), (db, w['w13t_' + tag], 1)], d, "ffn_dxn")
            g['w13t_' + tag] = jnp.concatenate([mm_tn(da, xn, name="ffn_dw13"), mm_tn(db, xn, name="ffn_dw13")], 0)
            return rms_bwd(dxn, hin, norm, dh, "rms_bwd")

        dh, dg2 = ffn_bwd(dh, '2', s['a2'], s['b2'], s['act2'], s['xn2'], s['h2'], r['ffn2_norm'])
        g['ffn2_norm'] = dg2.reshape(d)

        proj, tabs = s['proj'], s['tabs']
        dmerged = _nt(dh, w['w_out'], "mix_dmerged")
        g['w_out'] = mm_tn(s['merged'], dh, name="mix_dwout")
        dg0, dg1, dg2_, dga, dgg, do_conv, do_attn = merge_bwd(cfg, dmerged, proj, s['ga'], s['gg'], s['o_conv'],
                                                               s['o_attn'])
        half = w['w_glut'].shape[0] // 2
        dy2 = _nn([(dga, w['w_glut'], 0), (dgg, w['w_glut'], 1)], cfg.d_ssm, "ssm_dy2")
        g['w_glut'] = jnp.concatenate([mm_tn(dga, s['y2'], name="ssm_dwglu"), mm_tn(dgg, s['y2'], name="ssm_dwglu")], 0)
        du, dct_re, dct_im, dbb_re, dbb_im, dab_re, dab_im, dd = ssm_bwd(
            cfg, dy2, s['ytot'], proj, s['s_re'], s['s_im'], tabs['cf_re'], tabs['cf_im'], tabs['bbt_re'],
            tabs['bbt_im'], tabs['pw_re'], tabs['pw_im'], tabs['tabr_re'], tabs['tabr_im'], r['ssm_d'])
        g.update(_ssm_param_grads(cfg, tabs, dct_re, dct_im, dbb_re, dbb_im, dab_re, dab_im))
        g['ssm_d'] = dd.reshape(cfg.d_ssm)

        dhn = _nn([(do_conv, w['conv_w_outt'], 0)], cfg.d_conv, "branch_din")
        g['conv_w_outt'] = mm_tn(do_conv, s['hn'], name="branch_dw")
        dxa, dxg, dcw, dcb, dlg, dlb = conv_bwd(cfg, dhn, s['cpre'], proj, w['conv_w'], r['conv_ln_g'],
                                                r['conv_ln_b'])
        g['conv_w'], g['conv_b'] = dcw, dcb.reshape(cfg.d_conv)
        g['conv_ln_g'], g['conv_ln_b'] = dlg.reshape(cfg.d_conv), dlb.reshape(cfg.d_conv)

        do = _nn([(do_attn, w['attn_w_ot'], 0)], cfg.d_attn, "branch_din")
        g['attn_w_ot'] = mm_tn(do_attn, s['o'], name="branch_dw")
        dq, dk, dv = attn_bwd(cfg, proj, s['lt'], do)

        dproj = jnp.concatenate([du, dxa, dxg, dq.astype(BF16), dk.astype(BF16), dv.astype(BF16), dg0, dg1, dg2_],
                                axis=1)
        dxn = _nn([(dproj, w['w_int'], 0)], d, "proj_dxn")
        g['w_int'] = mm_tn(dproj, s['xnm'], name="proj_dw")
        dh, dgm = rms_bwd(dxn, s['h1'], r['mix_norm'], dh, "rms_bwd")
        g['mix_norm'] = dgm.reshape(d)

        dh, dg1n = ffn_bwd(dh, '1', s['a1'], s['b1'], s['act1'], s['xn1'], s['h0'], r['ffn1_norm'])
        g['ffn1_norm'] = dg1n.reshape(d)
        gl.append(g)

    gl = gl[::-1]
    dh0 = dh.reshape(bsz, lp, d)
    grad_x = dh0[:, Q_BLOCK:]
    grads = {k: jnp.stack([g[k] for g in gl]) for k in gl[0]}
    grads['meta_tokens'] = jnp.sum(dh0[:, cfg.pad:Q_BLOCK], axis=0)
    grads['final_norm'] = dgf.reshape(d)
    return loss, grad_x, grads


def _me():
    return lax.axis_index("x"), lax.axis_index("y"), lax.axis_index("c")


def _flat(px, py, pc):
    return 4 * px + 2 * py + pc


def all_gather_blocks(block):
    r, c_ = block.shape

    def body(x_ref, out_ref, send_sems, recv_sems, local_sem):
        x, y, c = _me()
        me, sibling = (x, y, c), (x, y, 1 - c)
        chips = [(1 - x, y), (x, 1 - y), (1 - x, 1 - y)]

        def slot(px, py, pc):
            return out_ref.at[_flat(px, py, pc)]

        def copy(k, blk, to, src=None):
            return pltpu.make_async_remote_copy(
                src_ref=slot(*blk) if src is None else src, dst_ref=slot(*blk), send_sem=send_sems.at[k],
                recv_sem=recv_sems.at[k], device_id=to, device_id_type=pl.DeviceIdType.MESH)

        mine = pltpu.make_async_copy(x_ref, slot(*me), local_sem)
        mine.start()
        first = [copy(0, me, sibling, src=x_ref)]
        first += [copy(1 + j, me, (*chip, c), src=x_ref) for j, chip in enumerate(chips)]
        for cp in first:
            cp.start()
        passed = [copy(4 + j, (*chip, c), sibling) for j, chip in enumerate(chips)]
        for j, chip in enumerate(chips):
            copy(1 + j, (*chip, c), me).wait_recv()
            passed[j].start()
        copy(0, sibling, me).wait_recv()
        for j, chip in enumerate(chips):
            copy(4 + j, (*chip, 1 - c), me).wait_recv()
        for cp in first + passed:
            cp.wait_send()
        mine.wait()

    return pl.pallas_call(
        body, name="all_gather_weights", out_shape=_sds((8, r, c_), block.dtype),
        in_specs=[pl.BlockSpec(memory_space=pl.ANY)], out_specs=pl.BlockSpec(memory_space=pl.ANY),
        scratch_shapes=[pltpu.SemaphoreType.DMA((7,)), pltpu.SemaphoreType.DMA((7,)), pltpu.SemaphoreType.DMA(())],
    )(block)


def all_to_all_blocks(send):
    _, r, c_ = send.shape

    def body(s_ref, out_ref, send_sems, recv_sems, local_sem):
        x, y, c = _me()
        me = _flat(x, y, c)
        mine = pltpu.make_async_copy(s_ref.at[me], out_ref.at[me], local_sem)
        mine.start()
        copies = []
        for rel in range(1, 8):
            px = 1 - x if rel & 4 else x
            py = 1 - y if rel & 2 else y
            pc = 1 - c if rel & 1 else c
            peer = _flat(px, py, pc)
            cp = pltpu.make_async_remote_copy(
                src_ref=s_ref.at[peer], dst_ref=out_ref.at[me], send_sem=send_sems.at[rel - 1],
                recv_sem=recv_sems.at[rel - 1], device_id=(px, py, pc), device_id_type=pl.DeviceIdType.MESH)
            cp.start()
            copies.append(cp)
        for cp in copies:
            cp.wait_recv()
        for cp in copies:
            cp.wait_send()
        mine.wait()

    return pl.pallas_call(
        body, name="all_to_all_grads", out_shape=_sds((8, r, c_), send.dtype),
        in_specs=[pl.BlockSpec(memory_space=pl.ANY)], out_specs=pl.BlockSpec(memory_space=pl.ANY),
        scratch_shapes=[pltpu.SemaphoreType.DMA((7,)), pltpu.SemaphoreType.DMA((7,)), pltpu.SemaphoreType.DMA(())],
    )(send)


def reduce_blocks(recv):
    nsrc, r, c_ = recv.shape
    tr = _tile(r, 128, 8)

    def body(x_ref, o_ref):
        acc = x_ref[0]
        for s in range(1, nsrc):
            acc = acc + x_ref[s]
        o_ref[...] = acc

    return pl.pallas_call(
        body, name="reduce_grads", grid=(r // tr,),
        in_specs=[pl.BlockSpec((nsrc, tr, c_), lambda i: (0, i, 0))], out_specs=pl.BlockSpec((tr, c_), lambda i: (i, 0)),
        out_shape=_sds((r, c_), F32), compiler_params=_cp(("parallel",)))(recv)


def adamw(w, g, m, v):
    shape = w.shape
    size = math.prod(shape)
    if shape[-1] < V7X_LANES and size % 1024 == 0:
        view = (size // 1024, 1024)
    else:
        view = (size // shape[-1], shape[-1])
    rows, cols = view
    tr = _tile(rows, max(8, (1 << 19) // cols // 8 * 8), 8)
    c1 = 1.0 - ADAM_B1 ** ADAM_STEP
    c2 = 1.0 - ADAM_B2 ** ADAM_STEP

    def body(w_ref, g_ref, m_ref, v_ref, d_ref, nm_ref, nv_ref):
        gv = g_ref[...]
        nm = ADAM_B1 * m_ref[...] + (1.0 - ADAM_B1) * gv
        nv = ADAM_B2 * v_ref[...] + (1.0 - ADAM_B2) * (gv * gv)
        nm_ref[...] = nm
        nv_ref[...] = nv
        d_ref[...] = -ADAM_LR * ((nm / c1) / (jnp.sqrt(nv / c2) + ADAM_EPS) + ADAM_WD * w_ref[...])

    blk = pl.BlockSpec((tr, cols), lambda i: (i, 0))
    out = _sds(view, F32)
    res = pl.pallas_call(
        body, name="adamw", grid=(rows // tr,), in_specs=[blk] * 4, out_specs=[blk] * 3, out_shape=[out] * 3,
        compiler_params=_cp(("parallel",)))(w.reshape(view), g.reshape(view), m.reshape(view), v.reshape(view))
    return tuple(a.reshape(shape) for a in res)


PACKED = [('w13t_1', 'ffn1_w13', True), ('w2_1', 'ffn1_w2', False), ('w_int', 'w_in', True), ('w_out', 'w_out', False),
          ('w13t_2', 'ffn2_w13', True), ('w2_2', 'ffn2_w2', False), ('w_glut', 'ssm_w_glu', True),
          ('conv_w_outt', 'conv_w_out', True), ('attn_w_ot', 'attn_w_o', True)]


def _rows_of(a, d):
    return a.reshape(a.shape[0] * a.shape[1] // d, d)


def _pad_rows(flat, d, mult):
    n = flat.shape[-1]
    rows = -(-n // d)
    rows = -(-rows // mult) * mult
    flat = jnp.pad(flat, [(0, 0)] * (flat.ndim - 1) + [(0, rows * d - n)])
    return flat.reshape(flat.shape[:-1] + (rows, d))


def pack_weight_shards(cfg, wts):
    d = cfg.d_model
    parts, layout = [], []
    for l in range(cfg.depth):
        for name, src, tr in PACKED:
            a = wts[src][l]
            a = (a.T if tr else a).astype(BF16)
            layout.append((name, l, a.shape))
            parts.append(_rows_of(a, d))
    small = jnp.concatenate([wts['conv_w'].reshape(-1), wts['meta_tokens'].reshape(-1)])
    small = lax.bitcast_convert_type(small, BF16).reshape(-1)
    parts.append(_pad_rows(small, d, 16))
    return jnp.concatenate(parts, axis=0), layout


def unpack_gathered(cfg, gathered, layout, wts):
    d = cfg.d_model
    out = {}
    off = 0
    for name, l, shape in layout:
        rows = shape[0] * shape[1] // d
        full = gathered[:, off:off + rows].reshape(8 * shape[0], shape[1])
        out.setdefault(name, []).append(full)
        off += rows
    wf = {k: jnp.stack(v) for k, v in out.items()}
    n_cw, n_mt = wts['conv_w'].size, wts['meta_tokens'].size
    small = gathered[:, off:].reshape(8, -1)[:, :2 * (n_cw + n_mt)].reshape(8, n_cw + n_mt, 2)
    small = lax.bitcast_convert_type(small, F32)
    cw = small[:, :n_cw].reshape((8,) + wts['conv_w'].shape)
    wf['conv_w'] = jnp.transpose(cw, (1, 2, 0, 3)).reshape(cw.shape[1], cw.shape[2], -1)
    mt = small[:, n_cw:].reshape((8,) + wts['meta_tokens'].shape)
    wf['meta_tokens'] = jnp.transpose(mt, (1, 0, 2)).reshape(mt.shape[1], -1)
    return wf


def pack_grads(cfg, grads, layout, rep_names):
    d = cfg.d_model
    parts = []
    for name, l, shape in layout:
        gfull = grads[name][l]
        parts.append(gfull.reshape(8, shape[0] * shape[1] // d, d))
    cw = grads['conv_w']
    cw = jnp.transpose(cw.reshape(cw.shape[0], cw.shape[1], 8, -1), (2, 0, 1, 3)).reshape(8, -1)
    mt = grads['meta_tokens']
    mt = jnp.transpose(mt.reshape(mt.shape[0], 8, -1), (1, 0, 2)).reshape(8, -1)
    repl = jnp.concatenate([grads[k].reshape(-1) for k in rep_names])
    small = jnp.concatenate([cw, mt, jnp.broadcast_to(repl[None], (8, repl.shape[0]))], axis=1)
    big_rows = sum(p.shape[1] for p in parts)
    small = _pad_rows(small, d, 8)
    total = big_rows + small.shape[1]
    extra = -total % 128
    if extra:
        small = jnp.pad(small, ((0, 0), (0, extra), (0, 0)))
    return jnp.concatenate(parts + [small], axis=1)


def unpack_grads(cfg, gsum, layout, wts, rep_names):
    d = cfg.d_model
    acc = {}
    off = 0
    for name, l, shape in layout:
        rows = shape[0] * shape[1] // d
        acc.setdefault(name, []).append(gsum[off:off + rows].reshape(shape))
        off += rows
    out = {}
    for name, src, tr in PACKED:
        a = jnp.stack(acc[name])
        out[src] = jnp.swapaxes(a, 1, 2) if tr else a
    flat = gsum[off:].reshape(-1)
    pos = 0
    for k in ['conv_w', 'meta_tokens'] + rep_names:
        n = wts[k].size
        out[k] = flat[pos:pos + n].reshape(wts[k].shape)
        pos += n
    return out


def train_step(cfg, x, target, wts, ms, vs):
    packed, layout = pack_weight_shards(cfg, wts)
    gathered = all_gather_blocks(packed)
    wf = unpack_gathered(cfg, gathered, layout, wts)
    rep = {k: wts[k] for k in REPLICATED}
    loss, grad_x, grads = local_step(cfg, x, target, wf, rep)
    send = pack_grads(cfg, grads, layout, REPLICATED)
    gsum = reduce_blocks(all_to_all_blocks(send))
    gw = unpack_grads(cfg, gsum, layout, wts, REPLICATED)
    loss = lax.psum(loss, ("x", "y", "c"))
    deltas, new_m, new_v = {}, {}, {}
    for k in WEIGHT_NAMES:
        deltas[k], new_m[k], new_v[k] = adamw(wts[k], gw[k], ms[k], vs[k])
    return (loss, grad_x, *[gw[k] for k in WEIGHT_NAMES], *[deltas[k] for k in WEIGHT_NAMES],
            *[new_m[k] for k in WEIGHT_NAMES], *[new_v[k] for k in WEIGHT_NAMES])


def kernel(x, meta_tokens, ffn1_norm, ffn1_w13, ffn1_w2, mix_norm, w_in, ssm_lam_re, ssm_lam_im, ssm_log_dt, ssm_b_re, ssm_b_im, ssm_c_re, ssm_c_im, ssm_d, ssm_w_glu, conv_w, conv_b, conv_ln_g, conv_ln_b, conv_w_out, attn_w_o, w_out, ffn2_norm, ffn2_w13, ffn2_w2, final_norm, loss_target, m_meta_tokens, m_ffn1_norm, m_ffn1_w13, m_ffn1_w2, m_mix_norm, m_w_in, m_ssm_lam_re, m_ssm_lam_im, m_ssm_log_dt, m_ssm_b_re, m_ssm_b_im, m_ssm_c_re, m_ssm_c_im, m_ssm_d, m_ssm_w_glu, m_conv_w, m_conv_b, m_conv_ln_g, m_conv_ln_b, m_conv_w_out, m_attn_w_o, m_w_out, m_ffn2_norm, m_ffn2_w13, m_ffn2_w2, m_final_norm, v_meta_tokens, v_ffn1_norm, v_ffn1_w13, v_ffn1_w2, v_mix_norm, v_w_in, v_ssm_lam_re, v_ssm_lam_im, v_ssm_log_dt, v_ssm_b_re, v_ssm_b_im, v_ssm_c_re, v_ssm_c_im, v_ssm_d, v_ssm_w_glu, v_conv_w, v_conv_b, v_conv_ln_g, v_conv_ln_b, v_conv_w_out, v_attn_w_o, v_w_out, v_ffn2_norm, v_ffn2_w13, v_ffn2_w2, v_final_norm):
    given = dict(locals())
    wts = {k: given[k] for k in WEIGHT_NAMES}
    ms = {k: given["m_" + k] for k in WEIGHT_NAMES}
    vs = {k: given["v_" + k] for k in WEIGHT_NAMES}
    return train_step(FULL, x, loss_target, wts, ms, vs)
```

```python
import functools
import math
from typing import NamedTuple

import jax
import jax.numpy as jnp
from jax import lax
from jax.experimental import pallas as pl
from jax.experimental.pallas import tpu as pltpu

F32 = jnp.float32
BF16 = jnp.bfloat16
RMS_EPS = 1e-6
LN_EPS = 1e-5
ADAM_LR = 0.001
ADAM_B1 = 0.9
ADAM_B2 = 0.999
ADAM_EPS = 1e-08
ADAM_WD = 0.01
ADAM_STEP = 10
V7X_LANES = 128
V7X_VMEM_LIMIT = 52 * 1024 * 1024
Q_BLOCK = 128
SSM_CHUNK = 128
SSM_SB = 128
CONV_HALO = 32

WEIGHT_NAMES = ['meta_tokens', 'ffn1_norm', 'ffn1_w13', 'ffn1_w2', 'mix_norm', 'w_in', 'ssm_lam_re', 'ssm_lam_im',
                'ssm_log_dt', 'ssm_b_re', 'ssm_b_im', 'ssm_c_re', 'ssm_c_im', 'ssm_d', 'ssm_w_glu', 'conv_w', 'conv_b',
                'conv_ln_g', 'conv_ln_b', 'conv_w_out', 'attn_w_o', 'w_out', 'ffn2_norm', 'ffn2_w13', 'ffn2_w2',
                'final_norm']
REPLICATED = ['ffn1_norm', 'mix_norm', 'ssm_lam_re', 'ssm_lam_im', 'ssm_log_dt', 'ssm_b_re', 'ssm_b_im', 'ssm_c_re',
              'ssm_c_im', 'ssm_d', 'conv_b', 'conv_ln_g', 'conv_ln_b', 'ffn2_norm', 'final_norm']


class Cfg(NamedTuple):
    d_model: int
    seq: int
    depth: int
    d_ff: int
    b_loc: int
    ndev: int = 8
    n_meta: int = 16
    n_heads: int = 8
    head_dim: int = 64
    ssm_group: int = 16
    ssm_state: int = 64
    conv_width: int = 31

    @property
    def d_ssm(self): return self.d_model // 2
    @property
    def groups(self): return self.d_ssm // self.ssm_group
    @property
    def n_state(self): return self.groups * self.ssm_state
    @property
    def d_conv(self): return self.d_model // 2
    @property
    def d_attn(self): return self.n_heads * self.head_dim
    @property
    def off_xa(self): return self.d_ssm
    @property
    def off_xg(self): return self.d_ssm + self.d_conv
    @property
    def off_q(self): return self.d_ssm + 2 * self.d_conv
    @property
    def off_k(self): return self.off_q + self.d_attn
    @property
    def off_v(self): return self.off_k + self.d_attn
    @property
    def off_gate(self): return self.off_v + self.d_attn
    @property
    def d_in(self): return self.off_gate + 3 * self.d_model
    @property
    def pad(self): return Q_BLOCK - self.n_meta
    @property
    def lp(self): return Q_BLOCK + self.seq
    @property
    def tokens(self): return self.b_loc * self.lp


FULL = Cfg(d_model=1024, seq=4096, depth=4, d_ff=2816, b_loc=2)


def _tile(n, pref, align):
    for t in range(min(pref, n), 0, -1):
        if n % t == 0 and t % align == 0:
            return t
    return n


def _cp(sem):
    return pltpu.CompilerParams(dimension_semantics=sem, vmem_limit_bytes=V7X_VMEM_LIMIT)


def _sds(shape, dtype):
    return jax.ShapeDtypeStruct(shape, dtype)


def _sigmoid(x):
    return jax.nn.sigmoid(x)


def _dot(a, b, ca, cb):
    return lax.dot_general(a, b, (((ca,), (cb,)), ((), ())), preferred_element_type=F32)


def rms_fwd(h, g, name):
    t, d = h.shape
    tm = _tile(t, 1056, 16)

    def body(h_ref, g_ref, o_ref):
        x = h_ref[...]
        r = lax.rsqrt(jnp.mean(x * x, axis=-1, keepdims=True) + RMS_EPS)
        o_ref[...] = (x * r * g_ref[...]).astype(BF16)

    return pl.pallas_call(
        body, name=name, grid=(t // tm,),
        in_specs=[pl.BlockSpec((tm, d), lambda i: (i, 0)), pl.BlockSpec((1, d), lambda i: (0, 0))],
        out_specs=pl.BlockSpec((tm, d), lambda i: (i, 0)), out_shape=_sds((t, d), BF16),
        compiler_params=_cp(("parallel",)))(h, g.reshape(1, d))


def rms_bwd(dxn, h, g, dh_res, name):
    t, d = h.shape
    tm = _tile(t, 528, 16)

    def body(dxn_ref, h_ref, g_ref, r_ref, dh_ref, dg_ref):
        x = h_ref[...]
        r = lax.rsqrt(jnp.mean(x * x, axis=-1, keepdims=True) + RMS_EPS)
        xhat = x * r
        dy = dxn_ref[...]
        dyg = dy * g_ref[...]
        dx = r * (dyg - xhat * jnp.mean(dyg * xhat, axis=-1, keepdims=True))
        dh_ref[...] = r_ref[...] + dx

        @pl.when(pl.program_id(0) == 0)
        def _():
            dg_ref[...] = jnp.zeros_like(dg_ref)

        dg_ref[...] += jnp.sum(dy * xhat, axis=0, keepdims=True)

    row = pl.BlockSpec((tm, d), lambda i: (i, 0))
    vec = pl.BlockSpec((1, d), lambda i: (0, 0))
    return pl.pallas_call(
        body, name=name, grid=(t // tm,), in_specs=[row, row, vec, row], out_specs=[row, vec],
        out_shape=[_sds((t, d), F32), _sds((1, d), F32)],
        compiler_params=_cp(("arbitrary",)))(dxn, h, g.reshape(1, d), dh_res)


def mm(pairs, *, nt, n, tm, tn, out_dtype, name, res=None, res_scale=1.0):
    m = pairs[0][0].shape[0]
    np_ = len(pairs)

    def body(*refs):
        o_ref = refs[-1]
        acc = None
        for p in range(np_):
            a = refs[2 * p][...].astype(BF16)
            b = refs[2 * p + 1][...].astype(BF16)
            d = _dot(a, b, 1, 1 if nt else 0)
            acc = d if acc is None else acc + d
        if res is not None:
            acc = refs[2 * np_][...] + res_scale * acc
        o_ref[...] = acc.astype(out_dtype)

    in_specs, args = [], []
    for a, b, kblk in pairs:
        k = a.shape[1]
        in_specs.append(pl.BlockSpec((tm, k), lambda i, j: (i, 0)))
        if nt:
            nb = n // tn
            in_specs.append(pl.BlockSpec((tn, k), functools.partial(lambda i, j, o: (j + o, 0), o=kblk * nb)))
        else:
            in_specs.append(pl.BlockSpec((k, tn), functools.partial(lambda i, j, o: (o, j), o=kblk)))
        args += [a, b]
    if res is not None:
        in_specs.append(pl.BlockSpec((tm, tn), lambda i, j: (i, j)))
        args.append(res)
    return pl.pallas_call(
        body, name=name, grid=(m // tm, n // tn), in_specs=in_specs,
        out_specs=pl.BlockSpec((tm, tn), lambda i, j: (i, j)), out_shape=_sds((m, n), out_dtype),
        compiler_params=_cp(("parallel", "arbitrary")))(*args)


def mm_tn(a, b, *, name, scale=1.0):
    t, m = a.shape
    n = b.shape[1]
    tm = _tile(m, 1536, 128)
    tn = _tile(n, 1024, 128)
    tk = _tile(t, 1056, 16)
    nk = t // tk

    def body(a_ref, b_ref, o_ref):
        k = pl.program_id(2)

        @pl.when(k == 0)
        def _():
            o_ref[...] = jnp.zeros_like(o_ref)

        o_ref[...] += _dot(a_ref[...].astype(BF16), b_ref[...].astype(BF16), 0, 0)
        if scale != 1.0:
            @pl.when(k == nk - 1)
            def _():
                o_ref[...] = o_ref[...] * scale

    return pl.pallas_call(
        body, name=name, grid=(m // tm, n // tn, nk),
        in_specs=[pl.BlockSpec((tk, tm), lambda i, j, k: (k, i)), pl.BlockSpec((tk, tn), lambda i, j, k: (k, j))],
        out_specs=pl.BlockSpec((tm, tn), lambda i, j, k: (i, j)), out_shape=_sds((m, n), F32),
        compiler_params=_cp(("parallel", "parallel", "arbitrary")))(a, b)


def dual_mm_act(x, wt, *, kind, name):
    t, k = x.shape
    hdim = wt.shape[0] // 2
    tm = _tile(t, 1056, 16)
    tn = _tile(hdim, 256, 128)
    nb = hdim // tn
    act_dtype = BF16 if kind == 'swiglu' else F32

    def body(x_ref, wa_ref, wb_ref, a_ref, b_ref, act_ref):
        xv = x_ref[...]
        a = _dot(xv, wa_ref[...], 1, 1)
        b = _dot(xv, wb_ref[...], 1, 1)
        a_ref[...] = a
        b_ref[...] = b
        if kind == 'swiglu':
            act_ref[...] = (a * _sigmoid(a) * b).astype(act_dtype)
        else:
            act_ref[...] = (a * _sigmoid(b)).astype(act_dtype)

    ob = pl.BlockSpec((tm, tn), lambda i, j: (i, j))
    return pl.pallas_call(
        body, name=name, grid=(t // tm, nb),
        in_specs=[pl.BlockSpec((tm, k), lambda i, j: (i, 0)), pl.BlockSpec((tn, k), lambda i, j: (j, 0)),
                  pl.BlockSpec((tn, k), lambda i, j: (j + nb, 0))],
        out_specs=[ob, ob, ob], out_shape=[_sds((t, hdim), F32), _sds((t, hdim), F32), _sds((t, hdim), act_dtype)],
        compiler_params=_cp(("parallel", "arbitrary")))(x, wt, wt)


def ffn_down_bwd(dh, w2, a, b, name):
    t, d = dh.shape
    hdim = w2.shape[0]
    tm = _tile(t, 1056, 16)
    tn = _tile(hdim, 256, 128)

    def body(dh_ref, w_ref, a_ref, b_ref, da_ref, db_ref):
        dact = 0.5 * _dot(dh_ref[...].astype(BF16), w_ref[...], 1, 1)
        av = a_ref[...]
        sg = _sigmoid(av)
        da_ref[...] = (dact * b_ref[...] * sg * (1.0 + av * (1.0 - sg))).astype(BF16)
        db_ref[...] = (dact * av * sg).astype(BF16)

    ob = pl.BlockSpec((tm, tn), lambda i, j: (i, j))
    return pl.pallas_call(
        body, name=name, grid=(t // tm, hdim // tn),
        in_specs=[pl.BlockSpec((tm, d), lambda i, j: (i, 0)), pl.BlockSpec((tn, d), lambda i, j: (j, 0)), ob, ob],
        out_specs=[ob, ob], out_shape=[_sds((t, hdim), BF16), _sds((t, hdim), BF16)],
        compiler_params=_cp(("parallel", "arbitrary")))(dh, w2, a, b)


def final_fwd_bwd(cfg, h, gf, target):
    bsz, lp, d = h.shape
    nq = lp // Q_BLOCK

    def body(h_ref, g_ref, t_ref, dh_ref, dg_ref, loss_ref):
        b, j = pl.program_id(0), pl.program_id(1)
        x = h_ref[0]
        r = lax.rsqrt(jnp.mean(x * x, axis=-1, keepdims=True) + RMS_EPS)
        xhat = x * r
        gv = g_ref[...]
        diff = jnp.where(j > 0, xhat * gv - t_ref[0], 0.0)
        dy = diff * (1.0 / d)
        dyg = dy * gv
        dh_ref[0] = r * (dyg - xhat * jnp.mean(dyg * xhat, axis=-1, keepdims=True))

        @pl.when((b == 0) & (j == 0))
        def _():
            dg_ref[...] = jnp.zeros_like(dg_ref)
            loss_ref[...] = jnp.zeros_like(loss_ref)

        dg_ref[...] += jnp.sum(dy * xhat, axis=0, keepdims=True)
        loss_ref[...] += jnp.sum(jnp.sum(diff * diff, axis=1, keepdims=True), axis=0, keepdims=True)

    blk = pl.BlockSpec((1, Q_BLOCK, d), lambda b, j: (b, j, 0))
    return pl.pallas_call(
        body, name="final_loss", grid=(bsz, nq),
        in_specs=[blk, pl.BlockSpec((1, d), lambda b, j: (0, 0)),
                  pl.BlockSpec((1, Q_BLOCK, d), lambda b, j: (b, jnp.maximum(j - 1, 0), 0))],
        out_specs=[blk, pl.BlockSpec((1, d), lambda b, j: (0, 0)), pl.BlockSpec((1, 1), lambda b, j: (0, 0))],
        out_shape=[_sds((bsz, lp, d), F32), _sds((1, d), F32), _sds((1, 1), F32)],
        compiler_params=_cp(("arbitrary", "arbitrary")))(h, gf.reshape(1, d), target)


def _row_valid(cfg, tm, i):
    pos = (i * tm) % cfg.lp + lax.broadcasted_iota(jnp.int32, (tm, 1), 0)
    return pos >= cfg.pad


def merge_fwd(cfg, proj, o_ssm, o_conv, o_attn):
    t, d = o_ssm.shape
    tc = math.gcd(math.gcd(d, cfg.off_gate), 512)
    tm = _tile(cfg.lp, 528, 16)
    g0 = cfg.off_gate // tc
    nc = d // tc

    def body(g0_ref, g1_ref, g2_ref, s_ref, c_ref, a_ref, o_ref):
        valid = _row_valid(cfg, tm, pl.program_id(0))
        m = (_sigmoid(g0_ref[...]) * s_ref[...] + _sigmoid(g1_ref[...]) * c_ref[...]
             + _sigmoid(g2_ref[...]) * a_ref[...])
        o_ref[...] = jnp.where(valid, m, 0.0).astype(BF16)

    gate = [pl.BlockSpec((tm, tc), functools.partial(lambda i, j, o: (i, o + j), o=g0 + k * nc)) for k in range(3)]
    blk = pl.BlockSpec((tm, tc), lambda i, j: (i, j))
    return pl.pallas_call(
        body, name="merge_fwd", grid=(t // tm, nc), in_specs=gate + [blk, blk, blk], out_specs=blk,
        out_shape=_sds((t, d), BF16), compiler_params=_cp(("parallel", "parallel")))(
            proj, proj, proj, o_ssm, o_conv, o_attn)


def merge_bwd(cfg, dmerged, proj, ga, gg, o_conv, o_attn):
    t, d = dmerged.shape
    tc = math.gcd(math.gcd(d, cfg.off_gate), 512)
    tm = _tile(cfg.lp, 528, 16)
    g0 = cfg.off_gate // tc
    nc = d // tc

    def body(dm_ref, g0_ref, g1_ref, g2_ref, ga_ref, gg_ref, c_ref, a_ref,
             dg0_ref, dg1_ref, dg2_ref, dga_ref, dgg_ref, dc_ref, da_ref):
        dm = dm_ref[...]
        s0, s1, s2 = _sigmoid(g0_ref[...]), _sigmoid(g1_ref[...]), _sigmoid(g2_ref[...])
        sg = _sigmoid(gg_ref[...])
        gav = ga_ref[...]
        o_s = gav * sg
        dg0_ref[...] = (dm * o_s * s0 * (1.0 - s0)).astype(BF16)
        dg1_ref[...] = (dm * c_ref[...] * s1 * (1.0 - s1)).astype(BF16)
        dg2_ref[...] = (dm * a_ref[...] * s2 * (1.0 - s2)).astype(BF16)
        dos = dm * s0
        dga_ref[...] = (dos * sg).astype(BF16)
        dgg_ref[...] = (dos * gav * sg * (1.0 - sg)).astype(BF16)
        dc_ref[...] = (dm * s1).astype(BF16)
        da_ref[...] = (dm * s2).astype(BF16)

    gate = [pl.BlockSpec((tm, tc), functools.partial(lambda i, j, o: (i, o + j), o=g0 + k * nc)) for k in range(3)]
    blk = pl.BlockSpec((tm, tc), lambda i, j: (i, j))
    out = _sds((t, d), BF16)
    return pl.pallas_call(
        body, name="merge_bwd", grid=(t // tm, nc), in_specs=[blk] + gate + [blk, blk, blk, blk],
        out_specs=[blk] * 7, out_shape=[out] * 7, compiler_params=_cp(("parallel", "parallel")))(
            dmerged, proj, proj, proj, ga, gg, o_conv, o_attn)


def _conv_tile(cfg):
    return _tile(cfg.lp, 384, CONV_HALO)


def _shift_up(x, off, rows):
    if off == 0:
        return x[:rows]
    return pltpu.roll(x, x.shape[0] - off, 0)[:rows]


def conv_fwd(cfg, proj, w, bias, ln_g, ln_b):
    t = proj.shape[0]
    wc = cfg.d_conv
    tt = _conv_tile(cfg)
    nc = cfg.lp // tt
    kw = cfg.conv_width
    first = CONV_HALO - (kw - 1)

    def body(a_ref, g_ref, ap_ref, gp_ref, w_ref, b_ref, lg_ref, lb_ref, cpre_ref, hn_ref):
        c = pl.program_id(1)
        hc = a_ref[...] * _sigmoid(g_ref[...])
        hp = jnp.where(c > 0, ap_ref[...] * _sigmoid(gp_ref[...]), 0.0)
        hcat = jnp.concatenate([hp, hc], axis=0)
        acc = jnp.zeros((tt, wc), F32) + b_ref[...]
        for k in range(kw):
            acc = acc + w_ref[k:k + 1, :] * _shift_up(hcat, first + k, tt)
        cpre_ref[...] = acc
        mu = jnp.mean(acc, axis=-1, keepdims=True)
        xc = acc - mu
        y = xc * lax.rsqrt(jnp.mean(xc * xc, axis=-1, keepdims=True) + LN_EPS) * lg_ref[...] + lb_ref[...]
        hn_ref[...] = (y * _sigmoid(y)).astype(BF16)

    ca, cg = cfg.off_xa // wc, cfg.off_xg // wc
    hb = tt // CONV_HALO
    lph = cfg.lp // CONV_HALO

    def halo(col):
        return pl.BlockSpec((CONV_HALO, wc), lambda b, c: (jnp.maximum(b * lph + c * hb - 1, 0), col))

    vec = pl.BlockSpec((1, wc), lambda b, c: (0, 0))
    blk = pl.BlockSpec((tt, wc), lambda b, c: (b * nc + c, 0))
    return pl.pallas_call(
        body, name="conv_fwd", grid=(cfg.b_loc, nc),
        in_specs=[pl.BlockSpec((tt, wc), lambda b, c: (b * nc + c, ca)),
                  pl.BlockSpec((tt, wc), lambda b, c: (b * nc + c, cg)), halo(ca), halo(cg),
                  pl.BlockSpec((kw, wc), lambda b, c: (0, 0)), vec, vec, vec],
        out_specs=[blk, blk], out_shape=[_sds((t, wc), F32), _sds((t, wc), BF16)],
        compiler_params=_cp(("parallel", "arbitrary")))(
            proj, proj, proj, proj, w, bias.reshape(1, wc), ln_g.reshape(1, wc), ln_b.reshape(1, wc))


def conv_bwd(cfg, dhn, cpre, proj, w, ln_g, ln_b):
    t = proj.shape[0]
    wc = cfg.d_conv
    tt = _conv_tile(cfg)
    nc = cfg.lp // tt
    kw = cfg.conv_width
    first = CONV_HALO - (kw - 1)

    def body(dhn_ref, dhn_nx_ref, cp_ref, cp_nx_ref, a_ref, g_ref, ap_ref, gp_ref, w_ref, lg_ref, lb_ref,
             dxa_ref, dxg_ref, dw_ref, db_ref, dlg_ref, dlb_ref):
        b, c = pl.program_id(0), pl.program_id(1)
        lg, lb = lg_ref[...], lb_ref[...]

        def ln_silu_bwd(dh, cp):
            mu = jnp.mean(cp, axis=-1, keepdims=True)
            xc = cp - mu
            rstd = lax.rsqrt(jnp.mean(xc * xc, axis=-1, keepdims=True) + LN_EPS)
            xhat = xc * rstd
            y = xhat * lg + lb
            sg = _sigmoid(y)
            dy = dh * sg * (1.0 + y * (1.0 - sg))
            dxh = dy * lg
            dc = rstd * (dxh - jnp.mean(dxh, axis=-1, keepdims=True)
                         - xhat * jnp.mean(dxh * xhat, axis=-1, keepdims=True))
            return dc, dy, xhat

        dc, dy, xhat = ln_silu_bwd(dhn_ref[...], cp_ref[...])
        dc_nx, _, _ = ln_silu_bwd(dhn_nx_ref[...], cp_nx_ref[...])
        dc_nx = jnp.where(c < nc - 1, dc_nx, 0.0)
        dcat = jnp.concatenate([dc, dc_nx], axis=0)
        dhc = jnp.zeros((tt, wc), F32)
        for k in range(kw):
            dhc = dhc + w_ref[k:k + 1, :] * _shift_up(dcat, kw - 1 - k, tt)
        av, gv = a_ref[...], g_ref[...]
        sg = _sigmoid(gv)
        valid = (c * tt + lax.broadcasted_iota(jnp.int32, (tt, 1), 0)) >= cfg.pad
        dhc = jnp.where(valid, dhc, 0.0)
        dxa_ref[...] = (dhc * sg).astype(BF16)
        dxg_ref[...] = (dhc * av * sg * (1.0 - sg)).astype(BF16)

        @pl.when((b == 0) & (c == 0))
        def _():
            dw_ref[...] = jnp.zeros_like(dw_ref)
            db_ref[...] = jnp.zeros_like(db_ref)
            dlg_ref[...] = jnp.zeros_like(dlg_ref)
            dlb_ref[...] = jnp.zeros_like(dlb_ref)

        hp = jnp.where(c > 0, ap_ref[...] * _sigmoid(gp_ref[...]), 0.0)
        hcat = jnp.concatenate([hp, av * sg], axis=0)
        for k in range(kw):
            dw_ref[k:k + 1, :] += jnp.sum(dc * _shift_up(hcat, first + k, tt), axis=0, keepdims=True)
        db_ref[...] += jnp.sum(dc, axis=0, keepdims=True)
        dlg_ref[...] += jnp.sum(dy * xhat, axis=0, keepdims=True)
        dlb_ref[...] += jnp.sum(dy, axis=0, keepdims=True)

    ca, cg = cfg.off_xa // wc, cfg.off_xg // wc
    hb = tt // CONV_HALO
    lph = cfg.lp // CONV_HALO
    last = cfg.b_loc * lph - 1

    def prev(col):
        return pl.BlockSpec((CONV_HALO, wc), lambda b, c: (jnp.maximum(b * lph + c * hb - 1, 0), col))

    nxt = pl.BlockSpec((CONV_HALO, wc), lambda b, c: (jnp.minimum(b * lph + (c + 1) * hb, last), 0))
    vec = pl.BlockSpec((1, wc), lambda b, c: (0, 0))
    blk = pl.BlockSpec((tt, wc), lambda b, c: (b * nc + c, 0))
    wspec = pl.BlockSpec((kw, wc), lambda b, c: (0, 0))
    return pl.pallas_call(
        body, name="conv_bwd", grid=(cfg.b_loc, nc),
        in_specs=[blk, nxt, blk, nxt, pl.BlockSpec((tt, wc), lambda b, c: (b * nc + c, ca)),
                  pl.BlockSpec((tt, wc), lambda b, c: (b * nc + c, cg)), prev(ca), prev(cg), wspec, vec, vec],
        out_specs=[blk, blk, wspec, vec, vec, vec],
        out_shape=[_sds((t, wc), BF16), _sds((t, wc), BF16), _sds((kw, wc), F32), _sds((1, wc), F32),
                   _sds((1, wc), F32), _sds((1, wc), F32)],
        compiler_params=_cp(("arbitrary", "arbitrary")))(
            dhn, dhn, cpre, cpre, proj, proj, proj, proj, w, ln_g.reshape(1, wc), ln_b.reshape(1, wc))


def _split_bf16(x):
    hi = x.astype(BF16)
    return hi, (x - hi.astype(F32)).astype(BF16)


K_CHUNK = 2 * Q_BLOCK


def _sb_logits(cfg, qh, kblk, col0, row_t):
    z = _dot(qh, kblk, 1, 1)
    col_s = col0 + lax.broadcasted_iota(jnp.int32, (1, K_CHUNK), 1)
    mask = (col_s < row_t) & (col_s >= cfg.pad)
    tneg = jnp.exp(-jnp.abs(z))
    lk = jnp.where(mask, -(jnp.maximum(z, 0.0) + jnp.log(1.0 + tneg)), 0.0)
    return z, mask, tneg, lk


def _tri_sum(x, tri):
    hi, lo = _split_bf16(x)
    return _dot(hi, tri, 1, 0) + _dot(lo, tri, 1, 0)


def attn_fwd(cfg, proj):
    t = proj.shape[0]
    lp = cfg.lp
    nq = lp // Q_BLOCK
    nhp = cfg.d_attn // V7X_LANES
    cq, ck, cv = cfg.off_q // V7X_LANES, cfg.off_k // V7X_LANES, cfg.off_v // V7X_LANES
    scale = 1.0 / math.sqrt(cfg.head_dim)

    def body(q_ref, k_ref, v_ref, o_ref, lt_ref, kbf, vbf):
        qb = pl.program_id(2)

        @pl.when(qb == 0)
        def _():
            kbf[pl.ds(0, lp), :] = k_ref[...].astype(BF16)
            vbf[pl.ds(0, lp), :] = v_ref[...].astype(BF16)
            kbf[pl.ds(lp, Q_BLOCK), :] = jnp.zeros((Q_BLOCK, V7X_LANES), BF16)
            vbf[pl.ds(lp, Q_BLOCK), :] = jnp.zeros((Q_BLOCK, V7X_LANES), BF16)

        q = q_ref[...] * scale
        lane = lax.broadcasted_iota(jnp.int32, (1, V7X_LANES), 1)
        row_t = qb * Q_BLOCK + lax.broadcasted_iota(jnp.int32, (Q_BLOCK, 1), 0)
        ri = lax.broadcasted_iota(jnp.int32, (K_CHUNK, K_CHUNK), 0)
        ci = lax.broadcasted_iota(jnp.int32, (K_CHUNK, K_CHUNK), 1)
        tri_from = (ri >= ci).astype(BF16)
        heads = [(lane < cfg.head_dim), (lane >= cfg.head_dim)]
        qhs = [jnp.where(mh, q, 0.0).astype(BF16) for mh in heads]
        last = qb // 2

        def chunk_body(i, carry):
            c = last - i
            off = pl.multiple_of(c * K_CHUNK, K_CHUNK)
            kc = kbf[pl.ds(off, K_CHUNK), :]
            vc = vbf[pl.ds(off, K_CHUNK), :]
            new = []
            for h in range(2):
                acc, r_carry = carry[2 * h], carry[2 * h + 1]
                z, mask, _, lk = _sb_logits(cfg, qhs[h], kc, c * K_CHUNK, row_t)
                w = jnp.where(mask, jnp.exp(z + _tri_sum(lk, tri_from) + r_carry), 0.0)
                new += [acc + _dot(w.astype(BF16), vc, 1, 0), r_carry + jnp.sum(lk, axis=1, keepdims=True)]
            return tuple(new)

        zacc, zcol = jnp.zeros((Q_BLOCK, V7X_LANES), F32), jnp.zeros((Q_BLOCK, 1), F32)
        acc0, r0, acc1, r1 = lax.fori_loop(0, last + 1, chunk_body, (zacc, zcol, zacc, zcol))
        o_ref[...] = jnp.where(heads[0], acc0, acc1)
        lt_ref[...] = jnp.where(lane == 0, r0, 0.0) + jnp.where(lane == cfg.head_dim, r1, 0.0)

    oblk = pl.BlockSpec((Q_BLOCK, V7X_LANES), lambda b, hp, qb: (b * nq + qb, hp))
    return pl.pallas_call(
        body, name="attn_fwd", grid=(cfg.b_loc, nhp, nq),
        in_specs=[pl.BlockSpec((Q_BLOCK, V7X_LANES), lambda b, hp, qb: (b * nq + qb, cq + hp)),
                  pl.BlockSpec((lp, V7X_LANES), lambda b, hp, qb: (b, ck + hp)),
                  pl.BlockSpec((lp, V7X_LANES), lambda b, hp, qb: (b, cv + hp))],
        out_specs=[oblk, oblk], out_shape=[_sds((t, cfg.d_attn), F32), _sds((t, cfg.d_attn), F32)],
        scratch_shapes=[pltpu.VMEM((lp + Q_BLOCK, V7X_LANES), BF16), pltpu.VMEM((lp + Q_BLOCK, V7X_LANES), BF16)],
        compiler_params=_cp(("parallel", "parallel", "arbitrary")))(proj, proj, proj)


def attn_bwd(cfg, proj, lt, do):
    t = proj.shape[0]
    lp = cfg.lp
    nq = lp // Q_BLOCK
    nhp = cfg.d_attn // V7X_LANES
    cq, ck, cv = cfg.off_q // V7X_LANES, cfg.off_k // V7X_LANES, cfg.off_v // V7X_LANES
    scale = 1.0 / math.sqrt(cfg.head_dim)

    def body(q_ref, k_ref, v_ref, lt_ref, do_ref, dq_ref, dk_ref, dv_ref, kbf, vbf, dk_acc, dv_acc):
        qb = pl.program_id(2)

        @pl.when(qb == 0)
        def _():
            kbf[pl.ds(0, lp), :] = k_ref[...].astype(BF16)
            vbf[pl.ds(0, lp), :] = v_ref[...].astype(BF16)
            kbf[pl.ds(lp, Q_BLOCK), :] = jnp.zeros((Q_BLOCK, V7X_LANES), BF16)
            vbf[pl.ds(lp, Q_BLOCK), :] = jnp.zeros((Q_BLOCK, V7X_LANES), BF16)
            dk_acc[...] = jnp.zeros_like(dk_acc)
            dv_acc[...] = jnp.zeros_like(dv_acc)

        q = q_ref[...] * scale
        ltv = lt_ref[...]
        dov = do_ref[...]
        lane = lax.broadcasted_iota(jnp.int32, (1, V7X_LANES), 1)
        row_t = qb * Q_BLOCK + lax.broadcasted_iota(jnp.int32, (Q_BLOCK, 1), 0)
        ri = lax.broadcasted_iota(jnp.int32, (K_CHUNK, K_CHUNK), 0)
        ci = lax.broadcasted_iota(jnp.int32, (K_CHUNK, K_CHUNK), 1)
        tri_before = (ri < ci).astype(BF16)
        tri_upto = (ri <= ci).astype(BF16)
        heads = [(lane < cfg.head_dim), (lane >= cfg.head_dim)]
        qhs = [jnp.where(mh, q, 0.0).astype(BF16) for mh in heads]
        dohbs = [jnp.where(mh, dov, 0.0).astype(BF16) for mh in heads]
        lk_tots = [jnp.sum(jnp.where(lane == h * cfg.head_dim, ltv, 0.0), axis=1, keepdims=True) for h in range(2)]

        def chunk_body(c, carry):
            off = pl.multiple_of(c * K_CHUNK, K_CHUNK)
            kc = kbf[pl.ds(off, K_CHUNK), :]
            vc = vbf[pl.ds(off, K_CHUNK), :]
            new = []
            dk_c = jnp.zeros((K_CHUNK, V7X_LANES), F32)
            dv_c = jnp.zeros((K_CHUNK, V7X_LANES), F32)
            for h in range(2):
                dq_acc, lk_before, e_before = carry[3 * h:3 * h + 3]
                z, mask, tneg, lk = _sb_logits(cfg, qhs[h], kc, c * K_CHUNK, row_t)
                lk_from = lk_tots[h] - lk_before - _tri_sum(lk, tri_before)
                w = jnp.where(mask, jnp.exp(z + lk_from), 0.0)
                e = w * _dot(dohbs[h], vc, 1, 1)
                e_upto = e_before + _tri_sum(e, tri_upto)
                sig = jnp.where(z >= 0.0, 1.0, tneg) / (1.0 + tneg)
                dz = jnp.where(mask, e - sig * e_upto, 0.0).astype(BF16)
                dk_c = dk_c + _dot(dz, qhs[h], 0, 0)
                dv_c = dv_c + _dot(w.astype(BF16), dohbs[h], 0, 0)
                new += [dq_acc + _dot(dz, kc, 1, 0), lk_before + jnp.sum(lk, axis=1, keepdims=True),
                        e_before + jnp.sum(e, axis=1, keepdims=True)]
            dk_acc[pl.ds(off, K_CHUNK), :] += dk_c
            dv_acc[pl.ds(off, K_CHUNK), :] += dv_c
            return tuple(new)

        zacc, zcol = jnp.zeros((Q_BLOCK, V7X_LANES), F32), jnp.zeros((Q_BLOCK, 1), F32)
        res = lax.fori_loop(0, qb // 2 + 1, chunk_body, (zacc, zcol, zcol, zacc, zcol, zcol))
        dq_ref[...] = jnp.where(heads[0], res[0], res[3]) * scale

        @pl.when(qb == nq - 1)
        def _():
            dk_ref[...] = dk_acc[pl.ds(0, lp), :]
            dv_ref[...] = dv_acc[pl.ds(0, lp), :]

    qblk = pl.BlockSpec((Q_BLOCK, V7X_LANES), lambda b, hp, qb: (b * nq + qb, hp))
    seq = pl.BlockSpec((lp, V7X_LANES), lambda b, hp, qb: (b, hp))
    out = _sds((t, cfg.d_attn), F32)
    return pl.pallas_call(
        body, name="attn_bwd", grid=(cfg.b_loc, nhp, nq),
        in_specs=[pl.BlockSpec((Q_BLOCK, V7X_LANES), lambda b, hp, qb: (b * nq + qb, cq + hp)),
                  pl.BlockSpec((lp, V7X_LANES), lambda b, hp, qb: (b, ck + hp)),
                  pl.BlockSpec((lp, V7X_LANES), lambda b, hp, qb: (b, cv + hp)), qblk, qblk],
        out_specs=[qblk, seq, seq], out_shape=[out, out, out],
        scratch_shapes=[pltpu.VMEM((lp + Q_BLOCK, V7X_LANES), BF16), pltpu.VMEM((lp + Q_BLOCK, V7X_LANES), BF16),
                        pltpu.VMEM((lp + Q_BLOCK, V7X_LANES), F32), pltpu.VMEM((lp + Q_BLOCK, V7X_LANES), F32)],
        compiler_params=_cp(("parallel", "parallel", "arbitrary")))(proj, proj, proj, lt, do)


N_POW = 8


def _discretize(lr, li, logdt, br_t, bi_t):
    dt = jnp.exp(logdt)
    mag = jnp.exp(lr * dt)
    ab_re = mag * jnp.cos(li * dt)
    ab_im = mag * jnp.sin(li * dt)
    den = lr * lr + li * li
    nr = ab_re - 1.0
    ni = ab_im
    coef_re = (nr * lr + ni * li) / den
    coef_im = (ni * lr - nr * li) / den
    bb_re = coef_re[None] * br_t - coef_im[None] * bi_t
    bb_im = coef_re[None] * bi_t + coef_im[None] * br_t
    return ab_re, ab_im, bb_re, bb_im


def ssm_param_fwd(cfg, lr, li, logdt, br_t, bi_t):
    g, n, c = cfg.groups, cfg.ssm_state, cfg.ssm_group

    def body(lr_ref, li_ref, dt_ref, br_ref, bi_ref, ar_ref, ai_ref, bbr_ref, bbi_ref, pr_ref, pi_ref,
             tr_ref, ti_ref):
        ab_re, ab_im, bb_re, bb_im = _discretize(lr_ref[...], li_ref[...], dt_ref[...], br_ref[...], bi_ref[...])
        ar_ref[...] = ab_re
        ai_ref[...] = ab_im
        bbr_ref[...] = bb_re
        bbi_ref[...] = bb_im
        pr, pi = ab_re, ab_im
        for k in range(N_POW):
            pr_ref[k] = pr
            pi_ref[k] = pi
            pr, pi = pr * pr - pi * pi, 2.0 * pr * pi
        cr, ci = ab_re, ab_im
        for j in range(SSM_CHUNK):
            tr_ref[j] = cr
            ti_ref[j] = ci
            cr, ci = cr * ab_re - ci * ab_im, cr * ab_im + ci * ab_re

    gn, cgn = _sds((g, n), F32), _sds((c, g, n), F32)
    return pl.pallas_call(
        body, name="ssm_param_fwd",
        out_shape=[gn, gn, cgn, cgn, _sds((N_POW, g, n), F32), _sds((N_POW, g, n), F32),
                   _sds((SSM_CHUNK, g, n), F32), _sds((SSM_CHUNK, g, n), F32)])(lr, li, logdt, br_t, bi_t)


def ssm_param_bwd(cfg, lr, li, logdt, br_t, bi_t, dar, dai, dbbr, dbbi):
    g, n, c = cfg.groups, cfg.ssm_state, cfg.ssm_group

    def body(lr_ref, li_ref, dt_ref, br_ref, bi_ref, dar_ref, dai_ref, dbr_ref, dbi_ref,
             olr_ref, oli_ref, odt_ref, obr_ref, obi_ref):
        _, vjp = jax.vjp(_discretize, lr_ref[...], li_ref[...], dt_ref[...], br_ref[...], bi_ref[...])
        dlr, dli, ddt, dbr, dbi = vjp((dar_ref[...], dai_ref[...], dbr_ref[...], dbi_ref[...]))
        olr_ref[...] = dlr
        oli_ref[...] = dli
        odt_ref[...] = ddt
        obr_ref[...] = dbr
        obi_ref[...] = dbi

    gn, cgn = _sds((g, n), F32), _sds((c, g, n), F32)
    return pl.pallas_call(body, name="ssm_param_bwd", out_shape=[gn, gn, _sds((g, 1), F32), cgn, cgn])(
        lr, li, logdt, br_t, bi_t, dar, dai, dbbr, dbbi)


def _scan(xr, xi, pr_ref, pi_ref, reverse):
    ct = xr.shape[0]
    rows = lax.broadcasted_iota(jnp.int32, (ct, 1), 0)
    d, k = 1, 0
    while d < ct:
        if reverse:
            sr, si = pltpu.roll(xr, ct - d, 0), pltpu.roll(xi, ct - d, 0)
            keep = rows < ct - d
        else:
            sr, si = pltpu.roll(xr, d, 0), pltpu.roll(xi, d, 0)
            keep = rows >= d
        sr = jnp.where(keep, sr, 0.0)
        si = jnp.where(keep, si, 0.0)
        ar = pr_ref[0, k:k + 1, :]
        ai = -pi_ref[0, k:k + 1, :] if reverse else pi_ref[0, k:k + 1, :]
        xr, xi = xr + ar * sr - ai * si, xi + ar * si + ai * sr
        d *= 2
        k += 1
    return xr, xi


def _gelu(x):
    k = math.sqrt(2.0 / math.pi)
    return 0.5 * x * (1.0 + jnp.tanh(k * (x + 0.044715 * x * x * x)))


def _gelu_grad(x):
    k = math.sqrt(2.0 / math.pi)
    th = jnp.tanh(k * (x + 0.044715 * x * x * x))
    return 0.5 * (1.0 + th) + 0.5 * x * (1.0 - th * th) * k * (1.0 + 3.0 * 0.044715 * x * x)


def ssm_fwd(cfg, proj, bb_re, bb_im, ct_re, ct_im, pw_re, pw_im, tab_re, tab_im, dskip):
    t = proj.shape[0]
    nsb = cfg.d_ssm // SSM_SB
    ns = SSM_SB // cfg.ssm_group * cfg.ssm_state
    ct = SSM_CHUNK
    nc = cfg.lp // ct

    def body(u_ref, bbr_ref, bbi_ref, ctr_ref, cti_ref, pr_ref, pi_ref, tr_ref, ti_ref, d_ref,
             sr_ref, si_ref, yt_ref, y2_ref, cr_scr, ci_scr):
        c = pl.program_id(2)

        @pl.when(c == 0)
        def _():
            cr_scr[...] = jnp.zeros_like(cr_scr)
            ci_scr[...] = jnp.zeros_like(ci_scr)

        u = u_ref[...]
        ub = u.astype(BF16)
        xr, xi = _scan(_dot(ub, bbr_ref[0], 1, 0), _dot(ub, bbi_ref[0], 1, 0), pr_ref, pi_ref, False)
        cr, ci = cr_scr[0:1, :], ci_scr[0:1, :]
        tr, ti = tr_ref[0], ti_ref[0]
        sr = xr + tr * cr - ti * ci
        si = xi + tr * ci + ti * cr
        sr_ref[...] = sr
        si_ref[...] = si
        cr_scr[0:1, :] = sr_ref[ct - 1:ct, :]
        ci_scr[0:1, :] = si_ref[ct - 1:ct, :]
        y = _dot(sr.astype(BF16), ctr_ref[0], 1, 0) - _dot(si.astype(BF16), cti_ref[0], 1, 0) + d_ref[...] * u
        yt_ref[...] = y
        y2_ref[...] = _gelu(y).astype(BF16)

    def mat(r, c_):
        return pl.BlockSpec((1, r, c_), lambda sb, b, c: (sb, 0, 0))

    ublk = pl.BlockSpec((ct, SSM_SB), lambda sb, b, c: (b * nc + c, sb))
    sblk = pl.BlockSpec((ct, ns), lambda sb, b, c: (b * nc + c, sb))
    return pl.pallas_call(
        body, name="ssm_fwd", grid=(nsb, cfg.b_loc, nc),
        in_specs=[ublk, mat(SSM_SB, ns), mat(SSM_SB, ns), mat(ns, SSM_SB), mat(ns, SSM_SB), mat(N_POW, ns),
                  mat(N_POW, ns), mat(ct, ns), mat(ct, ns), pl.BlockSpec((1, SSM_SB), lambda sb, b, c: (0, sb))],
        out_specs=[sblk, sblk, ublk, ublk],
        out_shape=[_sds((t, cfg.n_state), F32), _sds((t, cfg.n_state), F32), _sds((t, cfg.d_ssm), F32),
                   _sds((t, cfg.d_ssm), BF16)],
        scratch_shapes=[pltpu.VMEM((8, ns), F32), pltpu.VMEM((8, ns), F32)],
        compiler_params=_cp(("parallel", "arbitrary", "arbitrary")))(
            proj, bb_re, bb_im, ct_re, ct_im, pw_re, pw_im, tab_re, tab_im, dskip.reshape(1, cfg.d_ssm))


def ssm_bwd(cfg, dy2, ytot, proj, s_re, s_im, cf_re, cf_im, bbt_re, bbt_im, pw_re, pw_im, tabr_re, tabr_im, dskip):
    t = proj.shape[0]
    nsb = cfg.d_ssm // SSM_SB
    ns = SSM_SB // cfg.ssm_group * cfg.ssm_state
    ct = SSM_CHUNK
    nc = cfg.lp // ct

    def body(dy_ref, yt_ref, u_ref, sr_ref, si_ref, spr_ref, spi_ref, cfr_ref, cfi_ref, btr_ref, bti_ref,
             pr_ref, pi_ref, tr_ref, ti_ref, d_ref,
             du_ref, dctr_ref, dcti_ref, dbbr_ref, dbbi_ref, dar_ref, dai_ref, dd_ref,
             cr_scr, ci_scr, ar_scr, ai_scr):
        b, c = pl.program_id(1), pl.program_id(2)
        chunk = nc - 1 - c

        @pl.when((b == 0) & (c == 0))
        def _():
            for r in (dctr_ref, dcti_ref, dbbr_ref, dbbi_ref, dar_ref, dai_ref, dd_ref):
                r[...] = jnp.zeros_like(r)

        @pl.when(c == 0)
        def _():
            cr_scr[...] = jnp.zeros_like(cr_scr)
            ci_scr[...] = jnp.zeros_like(ci_scr)

        u = u_ref[...]
        dyt = dy_ref[...] * _gelu_grad(yt_ref[...])
        dytb = dyt.astype(BF16)
        xr, xi = _scan(_dot(dytb, cfr_ref[0], 1, 0), -_dot(dytb, cfi_ref[0], 1, 0), pr_ref, pi_ref, True)
        cr, ci = cr_scr[0:1, :], ci_scr[0:1, :]
        tr, ti = tr_ref[0], -ti_ref[0]
        a_re = xr + tr * cr - ti * ci
        a_im = xi + tr * ci + ti * cr
        ar_scr[...] = a_re
        ai_scr[...] = a_im
        cr_scr[0:1, :] = ar_scr[0:1, :]
        ci_scr[0:1, :] = ai_scr[0:1, :]

        sr, si = sr_ref[...], si_ref[...]
        rows = lax.broadcasted_iota(jnp.int32, (ct, 1), 0)
        keep_prev = chunk > 0
        pr_last = jnp.where(keep_prev, spr_ref[7:8, :], 0.0)
        pi_last = jnp.where(keep_prev, spi_ref[7:8, :], 0.0)
        sp_re = jnp.where(rows == 0, pr_last, pltpu.roll(sr, 1, 0))
        sp_im = jnp.where(rows == 0, pi_last, pltpu.roll(si, 1, 0))
        dar_ref[0] += jnp.sum(a_re * sp_re + a_im * sp_im, axis=0, keepdims=True)
        dai_ref[0] += jnp.sum(a_im * sp_re - a_re * sp_im, axis=0, keepdims=True)
        dctr_ref[0] += _dot(sr.astype(BF16), dytb, 0, 0)
        dcti_ref[0] += -_dot(si.astype(BF16), dytb, 0, 0)
        ub = u.astype(BF16)
        arb, aib = a_re.astype(BF16), a_im.astype(BF16)
        dbbr_ref[0] += _dot(ub, arb, 0, 0)
        dbbi_ref[0] += _dot(ub, aib, 0, 0)
        du = dyt * d_ref[...] + _dot(arb, btr_ref[0], 1, 0) + _dot(aib, bti_ref[0], 1, 0)
        valid = (chunk * ct + rows) >= cfg.pad
        du_ref[...] = jnp.where(valid, du, 0.0).astype(BF16)
        dd_ref[...] += jnp.sum(dyt * u, axis=0, keepdims=True)

    def mat(r, c_):
        return pl.BlockSpec((1, r, c_), lambda sb, b, c: (sb, 0, 0))

    lp8 = cfg.lp // 8
    ublk = pl.BlockSpec((ct, SSM_SB), lambda sb, b, c: (b * nc + nc - 1 - c, sb))
    sblk = pl.BlockSpec((ct, ns), lambda sb, b, c: (b * nc + nc - 1 - c, sb))
    sprev = pl.BlockSpec((8, ns), lambda sb, b, c: (jnp.maximum(b * lp8 + (nc - 1 - c) * (ct // 8) - 1, 0), sb))
    dvec = pl.BlockSpec((1, SSM_SB), lambda sb, b, c: (0, sb))
    return pl.pallas_call(
        body, name="ssm_bwd", grid=(nsb, cfg.b_loc, nc),
        in_specs=[ublk, ublk, ublk, sblk, sblk, sprev, sprev, mat(SSM_SB, ns), mat(SSM_SB, ns), mat(ns, SSM_SB),
                  mat(ns, SSM_SB), mat(N_POW, ns), mat(N_POW, ns), mat(ct, ns), mat(ct, ns), dvec],
        out_specs=[ublk, mat(ns, SSM_SB), mat(ns, SSM_SB), mat(SSM_SB, ns), mat(SSM_SB, ns), mat(1, ns), mat(1, ns),
                   dvec],
        out_shape=[_sds((t, cfg.d_ssm), BF16), _sds((nsb, ns, SSM_SB), F32), _sds((nsb, ns, SSM_SB), F32),
                   _sds((nsb, SSM_SB, ns), F32), _sds((nsb, SSM_SB, ns), F32), _sds((nsb, 1, ns), F32),
                   _sds((nsb, 1, ns), F32), _sds((1, cfg.d_ssm), F32)],
        scratch_shapes=[pltpu.VMEM((8, ns), F32), pltpu.VMEM((8, ns), F32), pltpu.VMEM((ct, ns), F32),
                        pltpu.VMEM((ct, ns), F32)],
        compiler_params=_cp(("arbitrary", "arbitrary", "arbitrary")))(
            dy2, ytot, proj, s_re, s_im, s_re, s_im, cf_re, cf_im, bbt_re, bbt_im, pw_re, pw_im, tabr_re, tabr_im,
            dskip.reshape(1, cfg.d_ssm))


def _ssm_tables(cfg, lam_re, lam_im, log_dt, b_re, b_im, c_re, c_im):
    g, n, c = cfg.groups, cfg.ssm_state, cfg.ssm_group
    gsb = SSM_SB // c
    nsb = cfg.d_ssm // SSM_SB
    ns = gsb * n
    br_t, bi_t = jnp.transpose(b_re, (2, 0, 1)), jnp.transpose(b_im, (2, 0, 1))
    prm = (lam_re, lam_im, log_dt.reshape(g, 1), br_t, bi_t)
    _, _, bbr, bbi, pr, pi, tr, ti = ssm_param_fwd(cfg, *prm)
    eye = jnp.eye(gsb, dtype=F32)

    def bdiag_b(x):
        return jnp.einsum('csgn,gh->sgchn', x.reshape(c, nsb, gsb, n), eye).reshape(nsb, SSM_SB, ns)

    def bdiag_c(x):
        return jnp.einsum('sgcn,gh->sgchn', x.reshape(nsb, gsb, c, n), eye).reshape(nsb, SSM_SB, ns)

    def lanes(x):
        return jnp.transpose(x.reshape(x.shape[0], nsb, ns), (1, 0, 2))

    bb_re, bb_im = bdiag_b(bbr), bdiag_b(bbi)
    cf_re, cf_im = bdiag_c(c_re), bdiag_c(c_im)
    sw = lambda x: jnp.swapaxes(x, 1, 2)
    return dict(prm=prm, bb_re=bb_re.astype(BF16), bb_im=bb_im.astype(BF16), bbt_re=sw(bb_re).astype(BF16),
                bbt_im=sw(bb_im).astype(BF16), cf_re=cf_re.astype(BF16), cf_im=cf_im.astype(BF16),
                ct_re=sw(cf_re).astype(BF16), ct_im=sw(cf_im).astype(BF16), pw_re=lanes(pr), pw_im=lanes(pi),
                tab_re=lanes(tr), tab_im=lanes(ti), tabr_re=lanes(tr[::-1]), tabr_im=lanes(ti[::-1]))


def _ssm_param_grads(cfg, tabs, dct_re, dct_im, dbb_re, dbb_im, dab_re, dab_im):
    g, n, c = cfg.groups, cfg.ssm_state, cfg.ssm_group
    gsb = SSM_SB // c
    nsb = cfg.d_ssm // SSM_SB

    def diag_b(x):
        d = jnp.einsum('sgcgn->sgcn', x.reshape(nsb, gsb, c, gsb, n))
        return jnp.transpose(d.reshape(g, c, n), (1, 0, 2))

    def diag_c(x):
        d = jnp.einsum('sgngc->sgcn', x.reshape(nsb, gsb, n, gsb, c))
        return d.reshape(g, c, n)

    dlr, dli, ddt, dbr_t, dbi_t = ssm_param_bwd(cfg, *tabs['prm'], dab_re.reshape(g, n), dab_im.reshape(g, n),
                                                diag_b(dbb_re), diag_b(dbb_im))
    return dict(ssm_lam_re=dlr, ssm_lam_im=dli, ssm_log_dt=ddt.reshape(g),
                ssm_b_re=jnp.transpose(dbr_t, (1, 2, 0)), ssm_b_im=jnp.transpose(dbi_t, (1, 2, 0)),
                ssm_c_re=diag_c(dct_re), ssm_c_im=diag_c(dct_im))


def _mm_tiles(t, n, k):
    tm = _tile(t, 1056 if k <= 3072 else 528, 16)
    tn = _tile(n, 512, 128)
    return tm, tn


def _nt(a, wt, name, out_dtype=F32):
    tm, tn = _mm_tiles(a.shape[0], wt.shape[0], a.shape[1])
    return mm([(a, wt, 0)], nt=True, n=wt.shape[0], tm=tm, tn=tn, out_dtype=out_dtype, name=name)


def _nn(pairs, n, name, res=None, res_scale=1.0, out_dtype=F32):
    kmax = max(p[0].shape[1] for p in pairs) * len(pairs)
    tm, tn = _mm_tiles(pairs[0][0].shape[0], n, kmax)
    return mm(pairs, nt=False, n=n, tm=tm, tn=tn, out_dtype=out_dtype, name=name, res=res, res_scale=res_scale)


def local_step(cfg, x, target, wf, rep):
    d, lp, bsz, t = cfg.d_model, cfg.lp, cfg.b_loc, cfg.tokens
    hdim = cfg.d_ff
    meta = jnp.broadcast_to(wf['meta_tokens'][None], (bsz, cfg.n_meta, d))
    h = jnp.concatenate([jnp.zeros((bsz, cfg.pad, d), F32), meta, x], axis=1).reshape(t, d)

    saved = []
    for l in range(cfg.depth):
        w = {k: v[l] for k, v in wf.items() if k not in ('meta_tokens',)}
        r = {k: v[l] for k, v in rep.items() if k != 'final_norm'}
        s = {'h0': h}
        s['xn1'] = rms_fwd(h, r['ffn1_norm'], "rms_fwd")
        s['a1'], s['b1'], s['act1'] = dual_mm_act(s['xn1'], w['w13t_1'], kind='swiglu', name="ffn_up")
        h = _nn([(s['act1'], w['w2_1'], 0)], d, "ffn_down", res=h, res_scale=0.5)
        s['h1'] = h
        s['xnm'] = rms_fwd(h, r['mix_norm'], "rms_fwd")
        proj = _nt(s['xnm'], w['w_int'], "proj_in")
        s['proj'] = proj
        tabs = _ssm_tables(cfg, r['ssm_lam_re'], r['ssm_lam_im'], r['ssm_log_dt'], r['ssm_b_re'], r['ssm_b_im'],
                           r['ssm_c_re'], r['ssm_c_im'])
        s['tabs'] = tabs
        s['s_re'], s['s_im'], s['ytot'], s['y2'] = ssm_fwd(
            cfg, proj, tabs['bb_re'], tabs['bb_im'], tabs['ct_re'], tabs['ct_im'], tabs['pw_re'], tabs['pw_im'],
            tabs['tab_re'], tabs['tab_im'], r['ssm_d'])
        s['ga'], s['gg'], o_ssm = dual_mm_act(s['y2'], w['w_glut'], kind='glu', name="ssm_glu")
        s['cpre'], s['hn'] = conv_fwd(cfg, proj, w['conv_w'], r['conv_b'], r['conv_ln_g'], r['conv_ln_b'])
        s['o_conv'] = _nt(s['hn'], w['conv_w_outt'], "branch_out")
        s['o'], s['lt'] = attn_fwd(cfg, proj)
        s['o_attn'] = _nt(s['o'], w['attn_w_ot'], "branch_out")
        s['merged'] = merge_fwd(cfg, proj, o_ssm, s['o_conv'], s['o_attn'])
        h = _nn([(s['merged'], w['w_out'], 0)], d, "mix_out", res=h, res_scale=1.0)
        s['h2'] = h
        s['xn2'] = rms_fwd(h, r['ffn2_norm'], "rms_fwd")
        s['a2'], s['b2'], s['act2'] = dual_mm_act(s['xn2'], w['w13t_2'], kind='swiglu', name="ffn_up")
        h = _nn([(s['act2'], w['w2_2'], 0)], d, "ffn_down", res=h, res_scale=0.5)
        saved.append((w, r, s))

    dh3, dgf, loss_sq = final_fwd_bwd(cfg, h.reshape(bsz, lp, d), rep['final_norm'], target)
    dh = dh3.reshape(t, d)
    loss = 0.5 * loss_sq[0, 0] / d

    gl = []
    for l in reversed(range(cfg.depth)):
        w, r, s = saved[l]
        g = {}

        def ffn_bwd(dh, tag, a, b, act, xn, hin, norm):
            da, db = ffn_down_bwd(dh, w['w2_' + tag], a, b, "ffn_down_bwd")
            g['w2_' + tag] = mm_tn(act, dh, name="ffn_dw2", scale=0.5)
            dxn = _nn([(da, w['w13t_' + tag], 0), (db, w['w13t_' + tag], 1)], d, "ffn_dxn")
            g['w13t_' + tag] = jnp.concatenate([mm_tn(da, xn, name="ffn_dw13"), mm_tn(db, xn, name="ffn_dw13")], 0)
            return rms_bwd(dxn, hin, norm, dh, "rms_bwd")

        dh, dg2 = ffn_bwd(dh, '2', s['a2'], s['b2'], s['act2'], s['xn2'], s['h2'], r['ffn2_norm'])
        g['ffn2_norm'] = dg2.reshape(d)

        proj, tabs = s['proj'], s['tabs']
        dmerged = _nt(dh, w['w_out'], "mix_dmerged")
        g['w_out'] = mm_tn(s['merged'], dh, name="mix_dwout")
        dg0, dg1, dg2_, dga, dgg, do_conv, do_attn = merge_bwd(cfg, dmerged, proj, s['ga'], s['gg'], s['o_conv'],
                                                               s['o_attn'])
        half = w['w_glut'].shape[0] // 2
        dy2 = _nn([(dga, w['w_glut'], 0), (dgg, w['w_glut'], 1)], cfg.d_ssm, "ssm_dy2")
        g['w_glut'] = jnp.concatenate([mm_tn(dga, s['y2'], name="ssm_dwglu"), mm_tn(dgg, s['y2'], name="ssm_dwglu")], 0)
        du, dct_re, dct_im, dbb_re, dbb_im, dab_re, dab_im, dd = ssm_bwd(
            cfg, dy2, s['ytot'], proj, s['s_re'], s['s_im'], tabs['cf_re'], tabs['cf_im'], tabs['bbt_re'],
            tabs['bbt_im'], tabs['pw_re'], tabs['pw_im'], tabs['tabr_re'], tabs['tabr_im'], r['ssm_d'])
        g.update(_ssm_param_grads(cfg, tabs, dct_re, dct_im, dbb_re, dbb_im, dab_re, dab_im))
        g['ssm_d'] = dd.reshape(cfg.d_ssm)

        dhn = _nn([(do_conv, w['conv_w_outt'], 0)], cfg.d_conv, "branch_din")
        g['conv_w_outt'] = mm_tn(do_conv, s['hn'], name="branch_dw")
        dxa, dxg, dcw, dcb, dlg, dlb = conv_bwd(cfg, dhn, s['cpre'], proj, w['conv_w'], r['conv_ln_g'],
                                                r['conv_ln_b'])
        g['conv_w'], g['conv_b'] = dcw, dcb.reshape(cfg.d_conv)
        g['conv_ln_g'], g['conv_ln_b'] = dlg.reshape(cfg.d_conv), dlb.reshape(cfg.d_conv)

        do = _nn([(do_attn, w['attn_w_ot'], 0)], cfg.d_attn, "branch_din")
        g['attn_w_ot'] = mm_tn(do_attn, s['o'], name="branch_dw")
        dq, dk, dv = attn_bwd(cfg, proj, s['lt'], do)

        dproj = jnp.concatenate([du, dxa, dxg, dq.astype(BF16), dk.astype(BF16), dv.astype(BF16), dg0, dg1, dg2_],
                                axis=1)
        dxn = _nn([(dproj, w['w_int'], 0)], d, "proj_dxn")
        g['w_int'] = mm_tn(dproj, s['xnm'], name="proj_dw")
        dh, dgm = rms_bwd(dxn, s['h1'], r['mix_norm'], dh, "rms_bwd")
        g['mix_norm'] = dgm.reshape(d)

        dh, dg1n = ffn_bwd(dh, '1', s['a1'], s['b1'], s['act1'], s['xn1'], s['h0'], r['ffn1_norm'])
        g['ffn1_norm'] = dg1n.reshape(d)
        gl.append(g)

    gl = gl[::-1]
    dh0 = dh.reshape(bsz, lp, d)
    grad_x = dh0[:, Q_BLOCK:]
    grads = {k: jnp.stack([g[k] for g in gl]) for k in gl[0]}
    grads['meta_tokens'] = jnp.sum(dh0[:, cfg.pad:Q_BLOCK], axis=0)
    grads['final_norm'] = dgf.reshape(d)
    return loss, grad_x, grads


def _me():
    return lax.axis_index("x"), lax.axis_index("y"), lax.axis_index("c")


def _flat(px, py, pc):
    return 4 * px + 2 * py + pc


def all_gather_blocks(block, name):
    r, c_ = block.shape

    def body(x_ref, out_ref, send_sems, recv_sems, local_sem):
        x, y, c = _me()
        me, sibling = (x, y, c), (x, y, 1 - c)
        chips = [(1 - x, y), (x, 1 - y), (1 - x, 1 - y)]

        def slot(px, py, pc):
            return out_ref.at[_flat(px, py, pc)]

        def copy(k, blk, to, src=None):
            return pltpu.make_async_remote_copy(
                src_ref=slot(*blk) if src is None else src, dst_ref=slot(*blk), send_sem=send_sems.at[k],
                recv_sem=recv_sems.at[k], device_id=to, device_id_type=pl.DeviceIdType.MESH)

        mine = pltpu.make_async_copy(x_ref, slot(*me), local_sem)
        mine.start()
        first = [copy(0, me, sibling, src=x_ref)]
        first += [copy(1 + j, me, (*chip, c), src=x_ref) for j, chip in enumerate(chips)]
        for cp in first:
            cp.start()
        passed = [copy(4 + j, (*chip, c), sibling) for j, chip in enumerate(chips)]
        for j, chip in enumerate(chips):
            copy(1 + j, (*chip, c), me).wait_recv()
            passed[j].start()
        copy(0, sibling, me).wait_recv()
        for j, chip in enumerate(chips):
            copy(4 + j, (*chip, 1 - c), me).wait_recv()
        for cp in first + passed:
            cp.wait_send()
        mine.wait()

    return pl.pallas_call(
        body, name=name, out_shape=_sds((8, r, c_), block.dtype),
        in_specs=[pl.BlockSpec(memory_space=pl.ANY)], out_specs=pl.BlockSpec(memory_space=pl.ANY),
        scratch_shapes=[pltpu.SemaphoreType.DMA((7,)), pltpu.SemaphoreType.DMA((7,)), pltpu.SemaphoreType.DMA(())],
    )(block)


def all_to_all_blocks(send, name):
    _, r, c_ = send.shape

    def body(s_ref, out_ref, send_sems, recv_sems, local_sem):
        x, y, c = _me()
        me = _flat(x, y, c)
        mine = pltpu.make_async_copy(s_ref.at[me], out_ref.at[me], local_sem)
        mine.start()
        copies = []
        for rel in range(1, 8):
            px = 1 - x if rel & 4 else x
            py = 1 - y if rel & 2 else y
            pc = 1 - c if rel & 1 else c
            peer = _flat(px, py, pc)
            cp = pltpu.make_async_remote_copy(
                src_ref=s_ref.at[peer], dst_ref=out_ref.at[me], send_sem=send_sems.at[rel - 1],
                recv_sem=recv_sems.at[rel - 1], device_id=(px, py, pc), device_id_type=pl.DeviceIdType.MESH)
            cp.start()
            copies.append(cp)
        for cp in copies:
            cp.wait_recv()
        for cp in copies:
            cp.wait_send()
        mine.wait()

    return pl.pallas_call(
        body, name=name, out_shape=_sds((8, r, c_), send.dtype),
        in_specs=[pl.BlockSpec(memory_space=pl.ANY)], out_specs=pl.BlockSpec(memory_space=pl.ANY),
        scratch_shapes=[pltpu.SemaphoreType.DMA((7,)), pltpu.SemaphoreType.DMA((7,)), pltpu.SemaphoreType.DMA(())],
    )(send)


def reduce_blocks(recv, name):
    nsrc, r, c_ = recv.shape
    tr = _tile(r, 128, 16)

    def body(x_ref, o_ref):
        acc = x_ref[0].astype(F32)
        for s in range(1, nsrc):
            acc = acc + x_ref[s].astype(F32)
        o_ref[...] = acc

    return pl.pallas_call(
        body, name=name, grid=(r // tr,),
        in_specs=[pl.BlockSpec((nsrc, tr, c_), lambda i: (0, i, 0))], out_specs=pl.BlockSpec((tr, c_), lambda i: (i, 0)),
        out_shape=_sds((r, c_), F32), compiler_params=_cp(("parallel",)))(recv)


def adamw(w, g, m, v):
    shape = w.shape
    size = math.prod(shape)
    if shape[-1] < V7X_LANES and size % 1024 == 0:
        view = (size // 1024, 1024)
    else:
        view = (size // shape[-1], shape[-1])
    rows, cols = view
    tr = _tile(rows, max(8, (1 << 19) // cols // 8 * 8), 8)
    c1 = 1.0 - ADAM_B1 ** ADAM_STEP
    c2 = 1.0 - ADAM_B2 ** ADAM_STEP

    def body(w_ref, g_ref, m_ref, v_ref, d_ref, nm_ref, nv_ref):
        gv = g_ref[...]
        nm = ADAM_B1 * m_ref[...] + (1.0 - ADAM_B1) * gv
        nv = ADAM_B2 * v_ref[...] + (1.0 - ADAM_B2) * (gv * gv)
        nm_ref[...] = nm
        nv_ref[...] = nv
        d_ref[...] = -ADAM_LR * ((nm / c1) / (jnp.sqrt(nv / c2) + ADAM_EPS) + ADAM_WD * w_ref[...])

    blk = pl.BlockSpec((tr, cols), lambda i: (i, 0))
    out = _sds(view, F32)
    res = pl.pallas_call(
        body, name="adamw", grid=(rows // tr,), in_specs=[blk] * 4, out_specs=[blk] * 3, out_shape=[out] * 3,
        compiler_params=_cp(("parallel",)))(w.reshape(view), g.reshape(view), m.reshape(view), v.reshape(view))
    return tuple(a.reshape(shape) for a in res)


PACKED = [('w13t_1', 'ffn1_w13', True), ('w2_1', 'ffn1_w2', False), ('w_int', 'w_in', True), ('w_out', 'w_out', False),
          ('w13t_2', 'ffn2_w13', True), ('w2_2', 'ffn2_w2', False), ('w_glut', 'ssm_w_glu', True),
          ('conv_w_outt', 'conv_w_out', True), ('attn_w_ot', 'attn_w_o', True)]


def _rows_of(a, d):
    return a.reshape(a.shape[0] * a.shape[1] // d, d)


def _pad_rows(flat, d, mult):
    n = flat.shape[-1]
    rows = -(-n // d)
    rows = -(-rows // mult) * mult
    flat = jnp.pad(flat, [(0, 0)] * (flat.ndim - 1) + [(0, rows * d - n)])
    return flat.reshape(flat.shape[:-1] + (rows, d))


def pack_weight_shards(cfg, wts):
    d = cfg.d_model
    parts, layout = [], []
    for l in range(cfg.depth):
        for name, src, tr in PACKED:
            a = wts[src][l]
            a = (a.T if tr else a).astype(BF16)
            layout.append((name, l, a.shape))
            parts.append(_rows_of(a, d))
    small = jnp.concatenate([wts['conv_w'].reshape(-1), wts['meta_tokens'].reshape(-1)])
    return jnp.concatenate(parts, axis=0), _pad_rows(small, d, 8), layout


def unpack_gathered(cfg, gathered, gathered_small, layout, wts):
    d = cfg.d_model
    out = {}
    off = 0
    for name, l, shape in layout:
        rows = shape[0] * shape[1] // d
        full = gathered[:, off:off + rows].reshape(8 * shape[0], shape[1])
        out.setdefault(name, []).append(full)
        off += rows
    wf = {k: jnp.stack(v) for k, v in out.items()}
    n_cw, n_mt = wts['conv_w'].size, wts['meta_tokens'].size
    small = gathered_small.reshape(8, -1)
    cw = small[:, :n_cw].reshape((8,) + wts['conv_w'].shape)
    wf['conv_w'] = jnp.transpose(cw, (1, 2, 0, 3)).reshape(cw.shape[1], cw.shape[2], -1)
    mt = small[:, n_cw:n_cw + n_mt].reshape((8,) + wts['meta_tokens'].shape)
    wf['meta_tokens'] = jnp.transpose(mt, (1, 0, 2)).reshape(mt.shape[1], -1)
    return wf


def pack_grads(cfg, grads, layout, rep_names):
    d = cfg.d_model
    parts = []
    for name, l, shape in layout:
        gfull = grads[name][l]
        parts.append(gfull.reshape(8, shape[0] * shape[1] // d, d).astype(BF16))
    cw = grads['conv_w']
    cw = jnp.transpose(cw.reshape(cw.shape[0], cw.shape[1], 8, -1), (2, 0, 1, 3)).reshape(8, -1)
    mt = grads['meta_tokens']
    mt = jnp.transpose(mt.reshape(mt.shape[0], 8, -1), (1, 0, 2)).reshape(8, -1)
    repl = jnp.concatenate([grads[k].reshape(-1) for k in rep_names])
    small = jnp.concatenate([cw, mt, jnp.broadcast_to(repl[None], (8, repl.shape[0]))], axis=1)
    big = jnp.concatenate(parts, axis=1)
    extra = -big.shape[1] % 128
    if extra:
        big = jnp.pad(big, ((0, 0), (0, extra), (0, 0)))
    return big, _pad_rows(small, d, 128)


def unpack_grads(cfg, gsum, gsmall, layout, wts, rep_names):
    d = cfg.d_model
    acc = {}
    off = 0
    for name, l, shape in layout:
        rows = shape[0] * shape[1] // d
        acc.setdefault(name, []).append(gsum[off:off + rows].reshape(shape))
        off += rows
    out = {}
    for name, src, tr in PACKED:
        a = jnp.stack(acc[name])
        out[src] = jnp.swapaxes(a, 1, 2) if tr else a
    flat = gsmall.reshape(-1)
    pos = 0
    for k in ['conv_w', 'meta_tokens'] + rep_names:
        n = wts[k].size
        out[k] = flat[pos:pos + n].reshape(wts[k].shape)
        pos += n
    return out


def train_step(cfg, x, target, wts, ms, vs):
    packed, packed_small, layout = pack_weight_shards(cfg, wts)
    gathered = all_gather_blocks(packed, "all_gather_weights")
    gathered_small = all_gather_blocks(packed_small, "all_gather_small_weights")
    wf = unpack_gathered(cfg, gathered, gathered_small, layout, wts)
    rep = {k: wts[k] for k in REPLICATED}
    loss, grad_x, grads = local_step(cfg, x, target, wf, rep)
    send_big, send_small = pack_grads(cfg, grads, layout, REPLICATED)
    gsum = reduce_blocks(all_to_all_blocks(send_big, "all_to_all_grads"), "reduce_grads")
    gsmall = reduce_blocks(all_to_all_blocks(send_small, "all_to_all_small_grads"), "reduce_small_grads")
    gw = unpack_grads(cfg, gsum, gsmall, layout, wts, REPLICATED)
    loss = lax.psum(loss, ("x", "y", "c"))
    deltas, new_m, new_v = {}, {}, {}
    for k in WEIGHT_NAMES:
        deltas[k], new_m[k], new_v[k] = adamw(wts[k], gw[k], ms[k], vs[k])
    return (loss, grad_x, *[gw[k] for k in WEIGHT_NAMES], *[deltas[k] for k in WEIGHT_NAMES],
            *[new_m[k] for k in WEIGHT_NAMES], *[new_v[k] for k in WEIGHT_NAMES])


def kernel(x, meta_tokens, ffn1_norm, ffn1_w13, ffn1_w2, mix_norm, w_in, ssm_lam_re, ssm_lam_im, ssm_log_dt, ssm_b_re, ssm_b_im, ssm_c_re, ssm_c_im, ssm_d, ssm_w_glu, conv_w, conv_b, conv_ln_g, conv_ln_b, conv_w_out, attn_w_o, w_out, ffn2_norm, ffn2_w13, ffn2_w2, final_norm, loss_target, m_meta_tokens, m_ffn1_norm, m_ffn1_w13, m_ffn1_w2, m_mix_norm, m_w_in, m_ssm_lam_re, m_ssm_lam_im, m_ssm_log_dt, m_ssm_b_re, m_ssm_b_im, m_ssm_c_re, m_ssm_c_im, m_ssm_d, m_ssm_w_glu, m_conv_w, m_conv_b, m_conv_ln_g, m_conv_ln_b, m_conv_w_out, m_attn_w_o, m_w_out, m_ffn2_norm, m_ffn2_w13, m_ffn2_w2, m_final_norm, v_meta_tokens, v_ffn1_norm, v_ffn1_w13, v_ffn1_w2, v_mix_norm, v_w_in, v_ssm_lam_re, v_ssm_lam_im, v_ssm_log_dt, v_ssm_b_re, v_ssm_b_im, v_ssm_c_re, v_ssm_c_im, v_ssm_d, v_ssm_w_glu, v_conv_w, v_conv_b, v_conv_ln_g, v_conv_ln_b, v_conv_w_out, v_attn_w_o, v_w_out, v_ffn2_norm, v_ffn2_w13, v_ffn2_w2, v_final_norm):
    given = dict(locals())
    wts = {k: given[k] for k in WEIGHT_NAMES}
    ms = {k: given["m_" + k] for k in WEIGHT_NAMES}
    vs = {k: given["v_" + k] for k in WEIGHT_NAMES}
    return train_step(FULL, x, loss_target, wts, ms, vs)
```

```python
import functools
import math
from typing import NamedTuple

import jax
import jax.numpy as jnp
from jax import lax
from jax.experimental import pallas as pl
from jax.experimental.pallas import tpu as pltpu

F32 = jnp.float32
BF16 = jnp.bfloat16
RMS_EPS = 1e-6
LN_EPS = 1e-5
ADAM_LR = 0.001
ADAM_B1 = 0.9
ADAM_B2 = 0.999
ADAM_EPS = 1e-08
ADAM_WD = 0.01
ADAM_STEP = 10
V7X_LANES = 128
V7X_VMEM_LIMIT = 52 * 1024 * 1024
Q_BLOCK = 128
SSM_CHUNK = 128
SSM_SB = 128
CONV_HALO = 32

WEIGHT_NAMES = ['meta_tokens', 'ffn1_norm', 'ffn1_w13', 'ffn1_w2', 'mix_norm', 'w_in', 'ssm_lam_re', 'ssm_lam_im',
                'ssm_log_dt', 'ssm_b_re', 'ssm_b_im', 'ssm_c_re', 'ssm_c_im', 'ssm_d', 'ssm_w_glu', 'conv_w', 'conv_b',
                'conv_ln_g', 'conv_ln_b', 'conv_w_out', 'attn_w_o', 'w_out', 'ffn2_norm', 'ffn2_w13', 'ffn2_w2',
                'final_norm']
REPLICATED = ['ffn1_norm', 'mix_norm', 'ssm_lam_re', 'ssm_lam_im', 'ssm_log_dt', 'ssm_b_re', 'ssm_b_im', 'ssm_c_re',
              'ssm_c_im', 'ssm_d', 'conv_b', 'conv_ln_g', 'conv_ln_b', 'ffn2_norm', 'final_norm']


class Cfg(NamedTuple):
    d_model: int
    seq: int
    depth: int
    d_ff: int
    b_loc: int
    ndev: int = 8
    n_meta: int = 16
    n_heads: int = 8
    head_dim: int = 64
    ssm_group: int = 16
    ssm_state: int = 64
    conv_width: int = 31

    @property
    def d_ssm(self): return self.d_model // 2
    @property
    def groups(self): return self.d_ssm // self.ssm_group
    @property
    def n_state(self): return self.groups * self.ssm_state
    @property
    def d_conv(self): return self.d_model // 2
    @property
    def d_attn(self): return self.n_heads * self.head_dim
    @property
    def off_xa(self): return self.d_ssm
    @property
    def off_xg(self): return self.d_ssm + self.d_conv
    @property
    def off_q(self): return self.d_ssm + 2 * self.d_conv
    @property
    def off_k(self): return self.off_q + self.d_attn
    @property
    def off_v(self): return self.off_k + self.d_attn
    @property
    def off_gate(self): return self.off_v + self.d_attn
    @property
    def d_in(self): return self.off_gate + 3 * self.d_model
    @property
    def pad(self): return Q_BLOCK - self.n_meta
    @property
    def lp(self): return Q_BLOCK + self.seq
    @property
    def tokens(self): return self.b_loc * self.lp


FULL = Cfg(d_model=1024, seq=4096, depth=4, d_ff=2816, b_loc=2)


def _tile(n, pref, align):
    for t in range(min(pref, n), 0, -1):
        if n % t == 0 and t % align == 0:
            return t
    return n


def _cp(sem):
    return pltpu.CompilerParams(dimension_semantics=sem, vmem_limit_bytes=V7X_VMEM_LIMIT)


def _sds(shape, dtype):
    return jax.ShapeDtypeStruct(shape, dtype)


def _sigmoid(x):
    return jax.nn.sigmoid(x)


def _dot(a, b, ca, cb):
    return lax.dot_general(a, b, (((ca,), (cb,)), ((), ())), preferred_element_type=F32)


def rms_fwd(h, g, name):
    t, d = h.shape
    tm = _tile(t, 1056, 16)

    def body(h_ref, g_ref, o_ref):
        x = h_ref[...]
        r = lax.rsqrt(jnp.mean(x * x, axis=-1, keepdims=True) + RMS_EPS)
        o_ref[...] = (x * r * g_ref[...]).astype(BF16)

    return pl.pallas_call(
        body, name=name, grid=(t // tm,),
        in_specs=[pl.BlockSpec((tm, d), lambda i: (i, 0)), pl.BlockSpec((1, d), lambda i: (0, 0))],
        out_specs=pl.BlockSpec((tm, d), lambda i: (i, 0)), out_shape=_sds((t, d), BF16),
        compiler_params=_cp(("parallel",)))(h, g.reshape(1, d))


def rms_bwd(dxn, h, g, dh_res, name):
    t, d = h.shape
    tm = _tile(t, 528, 16)

    def body(dxn_ref, h_ref, g_ref, r_ref, dh_ref, dg_ref):
        x = h_ref[...]
        r = lax.rsqrt(jnp.mean(x * x, axis=-1, keepdims=True) + RMS_EPS)
        xhat = x * r
        dy = dxn_ref[...]
        dyg = dy * g_ref[...]
        dx = r * (dyg - xhat * jnp.mean(dyg * xhat, axis=-1, keepdims=True))
        dh_ref[...] = r_ref[...] + dx

        @pl.when(pl.program_id(0) == 0)
        def _():
            dg_ref[...] = jnp.zeros_like(dg_ref)

        dg_ref[...] += jnp.sum(dy * xhat, axis=0, keepdims=True)

    row = pl.BlockSpec((tm, d), lambda i: (i, 0))
    vec = pl.BlockSpec((1, d), lambda i: (0, 0))
    return pl.pallas_call(
        body, name=name, grid=(t // tm,), in_specs=[row, row, vec, row], out_specs=[row, vec],
        out_shape=[_sds((t, d), F32), _sds((1, d), F32)],
        compiler_params=_cp(("arbitrary",)))(dxn, h, g.reshape(1, d), dh_res)


def mm(pairs, *, nt, n, tm, tn, out_dtype, name, res=None, res_scale=1.0):
    m = pairs[0][0].shape[0]
    np_ = len(pairs)

    def body(*refs):
        o_ref = refs[-1]
        acc = None
        for p in range(np_):
            a = refs[2 * p][...].astype(BF16)
            b = refs[2 * p + 1][...].astype(BF16)
            d = _dot(a, b, 1, 1 if nt else 0)
            acc = d if acc is None else acc + d
        if res is not None:
            acc = refs[2 * np_][...] + res_scale * acc
        o_ref[...] = acc.astype(out_dtype)

    in_specs, args = [], []
    for a, b, kblk in pairs:
        k = a.shape[1]
        in_specs.append(pl.BlockSpec((tm, k), lambda i, j: (i, 0)))
        if nt:
            nb = n // tn
            in_specs.append(pl.BlockSpec((tn, k), functools.partial(lambda i, j, o: (j + o, 0), o=kblk * nb)))
        else:
            in_specs.append(pl.BlockSpec((k, tn), functools.partial(lambda i, j, o: (o, j), o=kblk)))
        args += [a, b]
    if res is not None:
        in_specs.append(pl.BlockSpec((tm, tn), lambda i, j: (i, j)))
        args.append(res)
    return pl.pallas_call(
        body, name=name, grid=(m // tm, n // tn), in_specs=in_specs,
        out_specs=pl.BlockSpec((tm, tn), lambda i, j: (i, j)), out_shape=_sds((m, n), out_dtype),
        compiler_params=_cp(("parallel", "arbitrary")))(*args)


def mm_tn(a, b, *, name, scale=1.0):
    t, m = a.shape
    n = b.shape[1]
    tm = _tile(m, 1536, 128)
    tn = _tile(n, 1024, 128)
    tk = _tile(t, 1056, 16)
    nk = t // tk

    def body(a_ref, b_ref, o_ref):
        k = pl.program_id(2)

        @pl.when(k == 0)
        def _():
            o_ref[...] = jnp.zeros_like(o_ref)

        o_ref[...] += _dot(a_ref[...].astype(BF16), b_ref[...].astype(BF16), 0, 0)
        if scale != 1.0:
            @pl.when(k == nk - 1)
            def _():
                o_ref[...] = o_ref[...] * scale

    return pl.pallas_call(
        body, name=name, grid=(m // tm, n // tn, nk),
        in_specs=[pl.BlockSpec((tk, tm), lambda i, j, k: (k, i)), pl.BlockSpec((tk, tn), lambda i, j, k: (k, j))],
        out_specs=pl.BlockSpec((tm, tn), lambda i, j, k: (i, j)), out_shape=_sds((m, n), F32),
        compiler_params=_cp(("parallel", "parallel", "arbitrary")))(a, b)


def dual_mm_act(x, wt, *, kind, name):
    t, k = x.shape
    hdim = wt.shape[0] // 2
    tm = _tile(t, 1056, 16)
    tn = _tile(hdim, 256, 128)
    nb = hdim // tn
    act_dtype = BF16 if kind == 'swiglu' else F32

    def body(x_ref, wa_ref, wb_ref, a_ref, b_ref, act_ref):
        xv = x_ref[...]
        a = _dot(xv, wa_ref[...], 1, 1)
        b = _dot(xv, wb_ref[...], 1, 1)
        a_ref[...] = a
        b_ref[...] = b
        if kind == 'swiglu':
            act_ref[...] = (a * _sigmoid(a) * b).astype(act_dtype)
        else:
            act_ref[...] = (a * _sigmoid(b)).astype(act_dtype)

    ob = pl.BlockSpec((tm, tn), lambda i, j: (i, j))
    return pl.pallas_call(
        body, name=name, grid=(t // tm, nb),
        in_specs=[pl.BlockSpec((tm, k), lambda i, j: (i, 0)), pl.BlockSpec((tn, k), lambda i, j: (j, 0)),
                  pl.BlockSpec((tn, k), lambda i, j: (j + nb, 0))],
        out_specs=[ob, ob, ob], out_shape=[_sds((t, hdim), F32), _sds((t, hdim), F32), _sds((t, hdim), act_dtype)],
        compiler_params=_cp(("parallel", "arbitrary")))(x, wt, wt)


def ffn_down_bwd(dh, w2, a, b, name):
    t, d = dh.shape
    hdim = w2.shape[0]
    tm = _tile(t, 1056, 16)
    tn = _tile(hdim, 256, 128)

    def body(dh_ref, w_ref, a_ref, b_ref, da_ref, db_ref):
        dact = 0.5 * _dot(dh_ref[...].astype(BF16), w_ref[...], 1, 1)
        av = a_ref[...]
        sg = _sigmoid(av)
        da_ref[...] = (dact * b_ref[...] * sg * (1.0 + av * (1.0 - sg))).astype(BF16)
        db_ref[...] = (dact * av * sg).astype(BF16)

    ob = pl.BlockSpec((tm, tn), lambda i, j: (i, j))
    return pl.pallas_call(
        body, name=name, grid=(t // tm, hdim // tn),
        in_specs=[pl.BlockSpec((tm, d), lambda i, j: (i, 0)), pl.BlockSpec((tn, d), lambda i, j: (j, 0)), ob, ob],
        out_specs=[ob, ob], out_shape=[_sds((t, hdim), BF16), _sds((t, hdim), BF16)],
        compiler_params=_cp(("parallel", "arbitrary")))(dh, w2, a, b)


def final_fwd_bwd(cfg, h, gf, target):
    bsz, lp, d = h.shape
    nq = lp // Q_BLOCK

    def body(h_ref, g_ref, t_ref, dh_ref, dg_ref, loss_ref):
        b, j = pl.program_id(0), pl.program_id(1)
        x = h_ref[0]
        r = lax.rsqrt(jnp.mean(x * x, axis=-1, keepdims=True) + RMS_EPS)
        xhat = x * r
        gv = g_ref[...]
        diff = jnp.where(j > 0, xhat * gv - t_ref[0], 0.0)
        dy = diff * (1.0 / d)
        dyg = dy * gv
        dh_ref[0] = r * (dyg - xhat * jnp.mean(dyg * xhat, axis=-1, keepdims=True))

        @pl.when((b == 0) & (j == 0))
        def _():
            dg_ref[...] = jnp.zeros_like(dg_ref)
            loss_ref[...] = jnp.zeros_like(loss_ref)

        dg_ref[...] += jnp.sum(dy * xhat, axis=0, keepdims=True)
        loss_ref[...] += jnp.sum(jnp.sum(diff * diff, axis=1, keepdims=True), axis=0, keepdims=True)

    blk = pl.BlockSpec((1, Q_BLOCK, d), lambda b, j: (b, j, 0))
    return pl.pallas_call(
        body, name="final_loss", grid=(bsz, nq),
        in_specs=[blk, pl.BlockSpec((1, d), lambda b, j: (0, 0)),
                  pl.BlockSpec((1, Q_BLOCK, d), lambda b, j: (b, jnp.maximum(j - 1, 0), 0))],
        out_specs=[blk, pl.BlockSpec((1, d), lambda b, j: (0, 0)), pl.BlockSpec((1, 1), lambda b, j: (0, 0))],
        out_shape=[_sds((bsz, lp, d), F32), _sds((1, d), F32), _sds((1, 1), F32)],
        compiler_params=_cp(("arbitrary", "arbitrary")))(h, gf.reshape(1, d), target)


def _row_valid(cfg, tm, i):
    pos = (i * tm) % cfg.lp + lax.broadcasted_iota(jnp.int32, (tm, 1), 0)
    return pos >= cfg.pad


def merge_fwd(cfg, proj, o_ssm, o_conv, o_attn):
    t, d = o_ssm.shape
    tc = math.gcd(math.gcd(d, cfg.off_gate), 512)
    tm = _tile(cfg.lp, 528, 16)
    g0 = cfg.off_gate // tc
    nc = d // tc

    def body(g0_ref, g1_ref, g2_ref, s_ref, c_ref, a_ref, o_ref):
        valid = _row_valid(cfg, tm, pl.program_id(0))
        m = (_sigmoid(g0_ref[...]) * s_ref[...] + _sigmoid(g1_ref[...]) * c_ref[...]
             + _sigmoid(g2_ref[...]) * a_ref[...])
        o_ref[...] = jnp.where(valid, m, 0.0).astype(BF16)

    gate = [pl.BlockSpec((tm, tc), functools.partial(lambda i, j, o: (i, o + j), o=g0 + k * nc)) for k in range(3)]
    blk = pl.BlockSpec((tm, tc), lambda i, j: (i, j))
    return pl.pallas_call(
        body, name="merge_fwd", grid=(t // tm, nc), in_specs=gate + [blk, blk, blk], out_specs=blk,
        out_shape=_sds((t, d), BF16), compiler_params=_cp(("parallel", "parallel")))(
            proj, proj, proj, o_ssm, o_conv, o_attn)


def merge_bwd(cfg, dmerged, proj, ga, gg, o_conv, o_attn):
    t, d = dmerged.shape
    tc = math.gcd(math.gcd(d, cfg.off_gate), 512)
    tm = _tile(cfg.lp, 528, 16)
    g0 = cfg.off_gate // tc
    nc = d // tc

    def body(dm_ref, g0_ref, g1_ref, g2_ref, ga_ref, gg_ref, c_ref, a_ref,
             dg0_ref, dg1_ref, dg2_ref, dga_ref, dgg_ref, dc_ref, da_ref):
        dm = dm_ref[...]
        s0, s1, s2 = _sigmoid(g0_ref[...]), _sigmoid(g1_ref[...]), _sigmoid(g2_ref[...])
        sg = _sigmoid(gg_ref[...])
        gav = ga_ref[...]
        o_s = gav * sg
        dg0_ref[...] = (dm * o_s * s0 * (1.0 - s0)).astype(BF16)
        dg1_ref[...] = (dm * c_ref[...] * s1 * (1.0 - s1)).astype(BF16)
        dg2_ref[...] = (dm * a_ref[...] * s2 * (1.0 - s2)).astype(BF16)
        dos = dm * s0
        dga_ref[...] = (dos * sg).astype(BF16)
        dgg_ref[...] = (dos * gav * sg * (1.0 - sg)).astype(BF16)
        dc_ref[...] = (dm * s1).astype(BF16)
        da_ref[...] = (dm * s2).astype(BF16)

    gate = [pl.BlockSpec((tm, tc), functools.partial(lambda i, j, o: (i, o + j), o=g0 + k * nc)) for k in range(3)]
    blk = pl.BlockSpec((tm, tc), lambda i, j: (i, j))
    out = _sds((t, d), BF16)
    return pl.pallas_call(
        body, name="merge_bwd", grid=(t // tm, nc), in_specs=[blk] + gate + [blk, blk, blk, blk],
        out_specs=[blk] * 7, out_shape=[out] * 7, compiler_params=_cp(("parallel", "parallel")))(
            dmerged, proj, proj, proj, ga, gg, o_conv, o_attn)


def _conv_tile(cfg):
    return _tile(cfg.lp, 384, CONV_HALO)


def _shift_up(x, off, rows):
    if off == 0:
        return x[:rows]
    return pltpu.roll(x, x.shape[0] - off, 0)[:rows]


def conv_fwd(cfg, proj, w, bias, ln_g, ln_b):
    t = proj.shape[0]
    wc = cfg.d_conv
    tt = _conv_tile(cfg)
    nc = cfg.lp // tt
    kw = cfg.conv_width
    first = CONV_HALO - (kw - 1)

    def body(a_ref, g_ref, ap_ref, gp_ref, w_ref, b_ref, lg_ref, lb_ref, cpre_ref, hn_ref):
        c = pl.program_id(1)
        hc = a_ref[...] * _sigmoid(g_ref[...])
        hp = jnp.where(c > 0, ap_ref[...] * _sigmoid(gp_ref[...]), 0.0)
        hcat = jnp.concatenate([hp, hc], axis=0)
        acc = jnp.zeros((tt, wc), F32) + b_ref[...]
        for k in range(kw):
            acc = acc + w_ref[k:k + 1, :] * _shift_up(hcat, first + k, tt)
        cpre_ref[...] = acc
        mu = jnp.mean(acc, axis=-1, keepdims=True)
        xc = acc - mu
        y = xc * lax.rsqrt(jnp.mean(xc * xc, axis=-1, keepdims=True) + LN_EPS) * lg_ref[...] + lb_ref[...]
        hn_ref[...] = (y * _sigmoid(y)).astype(BF16)

    ca, cg = cfg.off_xa // wc, cfg.off_xg // wc
    hb = tt // CONV_HALO
    lph = cfg.lp // CONV_HALO

    def halo(col):
        return pl.BlockSpec((CONV_HALO, wc), lambda b, c: (jnp.maximum(b * lph + c * hb - 1, 0), col))

    vec = pl.BlockSpec((1, wc), lambda b, c: (0, 0))
    blk = pl.BlockSpec((tt, wc), lambda b, c: (b * nc + c, 0))
    return pl.pallas_call(
        body, name="conv_fwd", grid=(cfg.b_loc, nc),
        in_specs=[pl.BlockSpec((tt, wc), lambda b, c: (b * nc + c, ca)),
                  pl.BlockSpec((tt, wc), lambda b, c: (b * nc + c, cg)), halo(ca), halo(cg),
                  pl.BlockSpec((kw, wc), lambda b, c: (0, 0)), vec, vec, vec],
        out_specs=[blk, blk], out_shape=[_sds((t, wc), F32), _sds((t, wc), BF16)],
        compiler_params=_cp(("parallel", "arbitrary")))(
            proj, proj, proj, proj, w, bias.reshape(1, wc), ln_g.reshape(1, wc), ln_b.reshape(1, wc))


def conv_bwd(cfg, dhn, cpre, proj, w, ln_g, ln_b):
    t = proj.shape[0]
    wc = cfg.d_conv
    tt = _conv_tile(cfg)
    nc = cfg.lp // tt
    kw = cfg.conv_width
    first = CONV_HALO - (kw - 1)

    def body(dhn_ref, dhn_nx_ref, cp_ref, cp_nx_ref, a_ref, g_ref, ap_ref, gp_ref, w_ref, lg_ref, lb_ref,
             dxa_ref, dxg_ref, dw_ref, db_ref, dlg_ref, dlb_ref):
        b, c = pl.program_id(0), pl.program_id(1)
        lg, lb = lg_ref[...], lb_ref[...]

        def ln_silu_bwd(dh, cp):
            mu = jnp.mean(cp, axis=-1, keepdims=True)
            xc = cp - mu
            rstd = lax.rsqrt(jnp.mean(xc * xc, axis=-1, keepdims=True) + LN_EPS)
            xhat = xc * rstd
            y = xhat * lg + lb
            sg = _sigmoid(y)
            dy = dh * sg * (1.0 + y * (1.0 - sg))
            dxh = dy * lg
            dc = rstd * (dxh - jnp.mean(dxh, axis=-1, keepdims=True)
                         - xhat * jnp.mean(dxh * xhat, axis=-1, keepdims=True))
            return dc, dy, xhat

        dc, dy, xhat = ln_silu_bwd(dhn_ref[...], cp_ref[...])
        dc_nx, _, _ = ln_silu_bwd(dhn_nx_ref[...], cp_nx_ref[...])
        dc_nx = jnp.where(c < nc - 1, dc_nx, 0.0)
        dcat = jnp.concatenate([dc, dc_nx], axis=0)
        dhc = jnp.zeros((tt, wc), F32)
        for k in range(kw):
            dhc = dhc + w_ref[k:k + 1, :] * _shift_up(dcat, kw - 1 - k, tt)
        av, gv = a_ref[...], g_ref[...]
        sg = _sigmoid(gv)
        valid = (c * tt + lax.broadcasted_iota(jnp.int32, (tt, 1), 0)) >= cfg.pad
        dhc = jnp.where(valid, dhc, 0.0)
        dxa_ref[...] = (dhc * sg).astype(BF16)
        dxg_ref[...] = (dhc * av * sg * (1.0 - sg)).astype(BF16)

        @pl.when((b == 0) & (c == 0))
        def _():
            dw_ref[...] = jnp.zeros_like(dw_ref)
            db_ref[...] = jnp.zeros_like(db_ref)
            dlg_ref[...] = jnp.zeros_like(dlg_ref)
            dlb_ref[...] = jnp.zeros_like(dlb_ref)

        hp = jnp.where(c > 0, ap_ref[...] * _sigmoid(gp_ref[...]), 0.0)
        hcat = jnp.concatenate([hp, av * sg], axis=0)
        for k in range(kw):
            dw_ref[k:k + 1, :] += jnp.sum(dc * _shift_up(hcat, first + k, tt), axis=0, keepdims=True)
        db_ref[...] += jnp.sum(dc, axis=0, keepdims=True)
        dlg_ref[...] += jnp.sum(dy * xhat, axis=0, keepdims=True)
        dlb_ref[...] += jnp.sum(dy, axis=0, keepdims=True)

    ca, cg = cfg.off_xa // wc, cfg.off_xg // wc
    hb = tt // CONV_HALO
    lph = cfg.lp // CONV_HALO
    last = cfg.b_loc * lph - 1

    def prev(col):
        return pl.BlockSpec((CONV_HALO, wc), lambda b, c: (jnp.maximum(b * lph + c * hb - 1, 0), col))

    nxt = pl.BlockSpec((CONV_HALO, wc), lambda b, c: (jnp.minimum(b * lph + (c + 1) * hb, last), 0))
    vec = pl.BlockSpec((1, wc), lambda b, c: (0, 0))
    blk = pl.BlockSpec((tt, wc), lambda b, c: (b * nc + c, 0))
    wspec = pl.BlockSpec((kw, wc), lambda b, c: (0, 0))
    return pl.pallas_call(
        body, name="conv_bwd", grid=(cfg.b_loc, nc),
        in_specs=[blk, nxt, blk, nxt, pl.BlockSpec((tt, wc), lambda b, c: (b * nc + c, ca)),
                  pl.BlockSpec((tt, wc), lambda b, c: (b * nc + c, cg)), prev(ca), prev(cg), wspec, vec, vec],
        out_specs=[blk, blk, wspec, vec, vec, vec],
        out_shape=[_sds((t, wc), BF16), _sds((t, wc), BF16), _sds((kw, wc), F32), _sds((1, wc), F32),
                   _sds((1, wc), F32), _sds((1, wc), F32)],
        compiler_params=_cp(("arbitrary", "arbitrary")))(
            dhn, dhn, cpre, cpre, proj, proj, proj, proj, w, ln_g.reshape(1, wc), ln_b.reshape(1, wc))


def _split_bf16(x):
    hi = x.astype(BF16)
    return hi, (x - hi.astype(F32)).astype(BF16)


K_CHUNK = 2 * Q_BLOCK
NEG_BIG = -1e30


def _sb_logits(cfg, qh, kblk, col0, row_t):
    z = _dot(qh, kblk, 1, 1)
    col_s = col0 + lax.broadcasted_iota(jnp.int32, (1, K_CHUNK), 1)
    mask = (col_s < row_t) & (col_s >= cfg.pad)
    tneg = jnp.exp(-jnp.abs(z))
    lk = jnp.where(mask, -(jnp.maximum(z, 0.0) + jnp.log(1.0 + tneg)), 0.0)
    return z, mask, tneg, lk


def _tri_sum(x, tri):
    hi, lo = _split_bf16(x)
    return _dot(hi, tri, 1, 0) + _dot(lo, tri, 1, 0)


def attn_fwd(cfg, proj):
    t = proj.shape[0]
    lp = cfg.lp
    nq = lp // Q_BLOCK
    nhp = cfg.d_attn // V7X_LANES
    cq, ck, cv = cfg.off_q // V7X_LANES, cfg.off_k // V7X_LANES, cfg.off_v // V7X_LANES
    scale = 1.0 / math.sqrt(cfg.head_dim)

    def body(q_ref, k_ref, v_ref, o_ref, lt_ref, kbf, vbf):
        qb = pl.program_id(2)

        @pl.when(qb == 0)
        def _():
            kbf[pl.ds(0, lp), :] = k_ref[...].astype(BF16)
            vbf[pl.ds(0, lp), :] = v_ref[...].astype(BF16)
            kbf[pl.ds(lp, Q_BLOCK), :] = jnp.zeros((Q_BLOCK, V7X_LANES), BF16)
            vbf[pl.ds(lp, Q_BLOCK), :] = jnp.zeros((Q_BLOCK, V7X_LANES), BF16)

        q = q_ref[...] * scale
        lane = lax.broadcasted_iota(jnp.int32, (1, V7X_LANES), 1)
        head0 = lane < cfg.head_dim
        qs = jnp.concatenate([jnp.where(head0, q, 0.0), jnp.where(head0, 0.0, q)], axis=0).astype(BF16)
        rows2 = lax.broadcasted_iota(jnp.int32, (2 * Q_BLOCK, 1), 0)
        row_t = qb * Q_BLOCK + jnp.where(rows2 >= Q_BLOCK, rows2 - Q_BLOCK, rows2)
        ri = lax.broadcasted_iota(jnp.int32, (K_CHUNK, K_CHUNK), 0)
        ci = lax.broadcasted_iota(jnp.int32, (K_CHUNK, K_CHUNK), 1)
        tri_from = (ri >= ci).astype(BF16)
        last = qb // 2

        def step(i, carry):
            z_p, lk_p, pre_pp, rows_pp, acc, r_carry = carry
            c3 = jnp.clip(last - i + 2, 0, last)
            off3 = pl.multiple_of(c3 * K_CHUNK, K_CHUNK)
            w = jnp.exp(pre_pp + r_carry)
            acc = acc + _dot(w.astype(BF16), vbf[pl.ds(off3, K_CHUNK), :], 1, 0)
            r_carry = r_carry + rows_pp
            pre_p = z_p + _tri_sum(lk_p, tri_from)
            rows_p = jnp.sum(lk_p, axis=1, keepdims=True)
            c1 = jnp.maximum(last - i, 0)
            off1 = pl.multiple_of(c1 * K_CHUNK, K_CHUNK)
            col0 = jnp.where(i <= last, c1 * K_CHUNK, cfg.lp + K_CHUNK)
            z, mask, _, lk = _sb_logits(cfg, qs, kbf[pl.ds(off1, K_CHUNK), :], col0, row_t)
            return jnp.where(mask, z, NEG_BIG), lk, pre_p, rows_p, acc, r_carry

        zero = jnp.where(ri + ci < 0, 1.0, 0.0)
        big = zero + NEG_BIG
        zcol = jnp.where(rows2 < 0, 1.0, 0.0)
        res = lax.fori_loop(0, last + 3, step, (big, zero, big, zcol, zcol + jnp.where(lane < 0, 1.0, 0.0), zcol))
        acc, r_tot = res[4], res[5]
        o_ref[...] = jnp.where(head0, acc[:Q_BLOCK], acc[Q_BLOCK:])
        lt_ref[...] = (jnp.where(lane == 0, r_tot[:Q_BLOCK], 0.0)
                       + jnp.where(lane == cfg.head_dim, r_tot[Q_BLOCK:], 0.0))

    oblk = pl.BlockSpec((Q_BLOCK, V7X_LANES), lambda b, hp, qb: (b * nq + qb, hp))
    return pl.pallas_call(
        body, name="attn_fwd", grid=(cfg.b_loc, nhp, nq),
        in_specs=[pl.BlockSpec((Q_BLOCK, V7X_LANES), lambda b, hp, qb: (b * nq + qb, cq + hp)),
                  pl.BlockSpec((lp, V7X_LANES), lambda b, hp, qb: (b, ck + hp)),
                  pl.BlockSpec((lp, V7X_LANES), lambda b, hp, qb: (b, cv + hp))],
        out_specs=[oblk, oblk], out_shape=[_sds((t, cfg.d_attn), F32), _sds((t, cfg.d_attn), F32)],
        scratch_shapes=[pltpu.VMEM((lp + Q_BLOCK, V7X_LANES), BF16), pltpu.VMEM((lp + Q_BLOCK, V7X_LANES), BF16)],
        compiler_params=_cp(("parallel", "parallel", "arbitrary")))(proj, proj, proj)


def attn_bwd(cfg, proj, lt, do):
    t = proj.shape[0]
    lp = cfg.lp
    nq = lp // Q_BLOCK
    nhp = cfg.d_attn // V7X_LANES
    cq, ck, cv = cfg.off_q // V7X_LANES, cfg.off_k // V7X_LANES, cfg.off_v // V7X_LANES
    scale = 1.0 / math.sqrt(cfg.head_dim)

    def body(q_ref, k_ref, v_ref, lt_ref, do_ref, dq_ref, dk_ref, dv_ref, kbf, vbf, dk_acc, dv_acc):
        qb = pl.program_id(2)

        @pl.when(qb == 0)
        def _():
            kbf[pl.ds(0, lp), :] = k_ref[...].astype(BF16)
            vbf[pl.ds(0, lp), :] = v_ref[...].astype(BF16)
            kbf[pl.ds(lp, Q_BLOCK), :] = jnp.zeros((Q_BLOCK, V7X_LANES), BF16)
            vbf[pl.ds(lp, Q_BLOCK), :] = jnp.zeros((Q_BLOCK, V7X_LANES), BF16)
            dk_acc[...] = jnp.zeros_like(dk_acc)
            dv_acc[...] = jnp.zeros_like(dv_acc)

        q = q_ref[...] * scale
        ltv = lt_ref[...]
        dov = do_ref[...]
        lane = lax.broadcasted_iota(jnp.int32, (1, V7X_LANES), 1)
        head0 = lane < cfg.head_dim
        qs = jnp.concatenate([jnp.where(head0, q, 0.0), jnp.where(head0, 0.0, q)], axis=0).astype(BF16)
        dos = jnp.concatenate([jnp.where(head0, dov, 0.0), jnp.where(head0, 0.0, dov)], axis=0).astype(BF16)
        lk_tot = jnp.concatenate(
            [jnp.sum(jnp.where(lane == 0, ltv, 0.0), axis=1, keepdims=True),
             jnp.sum(jnp.where(lane == cfg.head_dim, ltv, 0.0), axis=1, keepdims=True)], axis=0)
        rows2 = lax.broadcasted_iota(jnp.int32, (2 * Q_BLOCK, 1), 0)
        row_t = qb * Q_BLOCK + jnp.where(rows2 >= Q_BLOCK, rows2 - Q_BLOCK, rows2)
        ri = lax.broadcasted_iota(jnp.int32, (K_CHUNK, K_CHUNK), 0)
        ci = lax.broadcasted_iota(jnp.int32, (K_CHUNK, K_CHUNK), 1)
        tri_before = (ri < ci).astype(BF16)
        tri_upto = (ri <= ci).astype(BF16)
        n_chunks = qb // 2 + 1

        def step(i, carry):
            e_p, w_p, sg_p, dq_acc, lk_before, e_before = carry
            c2 = jnp.clip(i - 1, 0, n_chunks - 1)
            off2 = pl.multiple_of(c2 * K_CHUNK, K_CHUNK)
            dz = (e_p - sg_p * (e_before + _tri_sum(e_p, tri_upto))).astype(BF16)
            dq_acc = dq_acc + _dot(dz, kbf[pl.ds(off2, K_CHUNK), :], 1, 0)
            dk_acc[pl.ds(off2, K_CHUNK), :] += _dot(dz, qs, 0, 0)
            dv_acc[pl.ds(off2, K_CHUNK), :] += _dot(w_p, dos, 0, 0)
            e_before = e_before + jnp.sum(e_p, axis=1, keepdims=True)

            c1 = jnp.minimum(i, n_chunks - 1)
            off1 = pl.multiple_of(c1 * K_CHUNK, K_CHUNK)
            col0 = jnp.where(i < n_chunks, c1 * K_CHUNK, cfg.lp + K_CHUNK)
            z, mask, tneg, lk = _sb_logits(cfg, qs, kbf[pl.ds(off1, K_CHUNK), :], col0, row_t)
            dw = _dot(dos, vbf[pl.ds(off1, K_CHUNK), :], 1, 1)
            w = jnp.exp(jnp.where(mask, z, NEG_BIG) + (lk_tot - lk_before - _tri_sum(lk, tri_before)))
            sg = jnp.where(mask, jnp.where(z >= 0.0, 1.0, tneg) / (1.0 + tneg), 0.0)
            lk_before = lk_before + jnp.sum(lk, axis=1, keepdims=True)
            return w * dw, w.astype(BF16), sg, dq_acc, lk_before, e_before

        zero = jnp.where(ri + ci < 0, 1.0, 0.0)
        zcol = jnp.where(rows2 < 0, 1.0, 0.0)
        res = lax.fori_loop(0, n_chunks + 1, step,
                            (zero, zero.astype(BF16), zero, zcol + jnp.where(lane < 0, 1.0, 0.0), zcol, zcol))
        dq = res[3]
        dq_ref[...] = jnp.where(head0, dq[:Q_BLOCK], dq[Q_BLOCK:]) * scale

        @pl.when(qb == nq - 1)
        def _():
            dk_ref[...] = dk_acc[pl.ds(0, lp), :]
            dv_ref[...] = dv_acc[pl.ds(0, lp), :]

    qblk = pl.BlockSpec((Q_BLOCK, V7X_LANES), lambda b, hp, qb: (b * nq + qb, hp))
    seq = pl.BlockSpec((lp, V7X_LANES), lambda b, hp, qb: (b, hp))
    out = _sds((t, cfg.d_attn), F32)
    return pl.pallas_call(
        body, name="attn_bwd", grid=(cfg.b_loc, nhp, nq),
        in_specs=[pl.BlockSpec((Q_BLOCK, V7X_LANES), lambda b, hp, qb: (b * nq + qb, cq + hp)),
                  pl.BlockSpec((lp, V7X_LANES), lambda b, hp, qb: (b, ck + hp)),
                  pl.BlockSpec((lp, V7X_LANES), lambda b, hp, qb: (b, cv + hp)), qblk, qblk],
        out_specs=[qblk, seq, seq], out_shape=[out, out, out],
        scratch_shapes=[pltpu.VMEM((lp + Q_BLOCK, V7X_LANES), BF16), pltpu.VMEM((lp + Q_BLOCK, V7X_LANES), BF16),
                        pltpu.VMEM((lp + Q_BLOCK, V7X_LANES), F32), pltpu.VMEM((lp + Q_BLOCK, V7X_LANES), F32)],
        compiler_params=_cp(("parallel", "parallel", "arbitrary")))(proj, proj, proj, lt, do)


N_POW = 8


def _discretize(lr, li, logdt, br_t, bi_t):
    dt = jnp.exp(logdt)
    mag = jnp.exp(lr * dt)
    ab_re = mag * jnp.cos(li * dt)
    ab_im = mag * jnp.sin(li * dt)
    den = lr * lr + li * li
    nr = ab_re - 1.0
    ni = ab_im
    coef_re = (nr * lr + ni * li) / den
    coef_im = (ni * lr - nr * li) / den
    bb_re = coef_re[None] * br_t - coef_im[None] * bi_t
    bb_im = coef_re[None] * bi_t + coef_im[None] * br_t
    return ab_re, ab_im, bb_re, bb_im


def ssm_param_fwd(cfg, lr, li, logdt, br_t, bi_t):
    g, n, c = cfg.groups, cfg.ssm_state, cfg.ssm_group

    def body(lr_ref, li_ref, dt_ref, br_ref, bi_ref, ar_ref, ai_ref, bbr_ref, bbi_ref, pr_ref, pi_ref,
             tr_ref, ti_ref):
        ab_re, ab_im, bb_re, bb_im = _discretize(lr_ref[...], li_ref[...], dt_ref[...], br_ref[...], bi_ref[...])
        ar_ref[...] = ab_re
        ai_ref[...] = ab_im
        bbr_ref[...] = bb_re
        bbi_ref[...] = bb_im
        pr, pi = ab_re, ab_im
        for k in range(N_POW):
            pr_ref[k] = pr
            pi_ref[k] = pi
            pr, pi = pr * pr - pi * pi, 2.0 * pr * pi
        cr, ci = ab_re, ab_im
        for j in range(SSM_CHUNK):
            tr_ref[j] = cr
            ti_ref[j] = ci
            cr, ci = cr * ab_re - ci * ab_im, cr * ab_im + ci * ab_re

    gn, cgn = _sds((g, n), F32), _sds((c, g, n), F32)
    return pl.pallas_call(
        body, name="ssm_param_fwd",
        out_shape=[gn, gn, cgn, cgn, _sds((N_POW, g, n), F32), _sds((N_POW, g, n), F32),
                   _sds((SSM_CHUNK, g, n), F32), _sds((SSM_CHUNK, g, n), F32)])(lr, li, logdt, br_t, bi_t)


def ssm_param_bwd(cfg, lr, li, logdt, br_t, bi_t, dar, dai, dbbr, dbbi):
    g, n, c = cfg.groups, cfg.ssm_state, cfg.ssm_group

    def body(lr_ref, li_ref, dt_ref, br_ref, bi_ref, dar_ref, dai_ref, dbr_ref, dbi_ref,
             olr_ref, oli_ref, odt_ref, obr_ref, obi_ref):
        _, vjp = jax.vjp(_discretize, lr_ref[...], li_ref[...], dt_ref[...], br_ref[...], bi_ref[...])
        dlr, dli, ddt, dbr, dbi = vjp((dar_ref[...], dai_ref[...], dbr_ref[...], dbi_ref[...]))
        olr_ref[...] = dlr
        oli_ref[...] = dli
        odt_ref[...] = ddt
        obr_ref[...] = dbr
        obi_ref[...] = dbi

    gn, cgn = _sds((g, n), F32), _sds((c, g, n), F32)
    return pl.pallas_call(body, name="ssm_param_bwd", out_shape=[gn, gn, _sds((g, 1), F32), cgn, cgn])(
        lr, li, logdt, br_t, bi_t, dar, dai, dbbr, dbbi)


def _scan(xr, xi, pr_ref, pi_ref, reverse):
    ct = xr.shape[0]
    rows = lax.broadcasted_iota(jnp.int32, (ct, 1), 0)
    d, k = 1, 0
    while d < ct:
        if reverse:
            sr, si = pltpu.roll(xr, ct - d, 0), pltpu.roll(xi, ct - d, 0)
            keep = rows < ct - d
        else:
            sr, si = pltpu.roll(xr, d, 0), pltpu.roll(xi, d, 0)
            keep = rows >= d
        sr = jnp.where(keep, sr, 0.0)
        si = jnp.where(keep, si, 0.0)
        ar = pr_ref[0, k:k + 1, :]
        ai = -pi_ref[0, k:k + 1, :] if reverse else pi_ref[0, k:k + 1, :]
        xr, xi = xr + ar * sr - ai * si, xi + ar * si + ai * sr
        d *= 2
        k += 1
    return xr, xi


def _gelu(x):
    k = math.sqrt(2.0 / math.pi)
    return 0.5 * x * (1.0 + jnp.tanh(k * (x + 0.044715 * x * x * x)))


def _gelu_grad(x):
    k = math.sqrt(2.0 / math.pi)
    th = jnp.tanh(k * (x + 0.044715 * x * x * x))
    return 0.5 * (1.0 + th) + 0.5 * x * (1.0 - th * th) * k * (1.0 + 3.0 * 0.044715 * x * x)


def ssm_fwd(cfg, proj, bb_re, bb_im, ct_re, ct_im, pw_re, pw_im, tab_re, tab_im, dskip):
    t = proj.shape[0]
    nsb = cfg.d_ssm // SSM_SB
    ns = SSM_SB // cfg.ssm_group * cfg.ssm_state
    ct = SSM_CHUNK
    nc = cfg.lp // ct

    def body(u_ref, bbr_ref, bbi_ref, ctr_ref, cti_ref, pr_ref, pi_ref, tr_ref, ti_ref, d_ref,
             sr_ref, si_ref, yt_ref, y2_ref, cr_scr, ci_scr):
        c = pl.program_id(2)

        @pl.when(c == 0)
        def _():
            cr_scr[...] = jnp.zeros_like(cr_scr)
            ci_scr[...] = jnp.zeros_like(ci_scr)

        u = u_ref[...]
        ub = u.astype(BF16)
        xr, xi = _scan(_dot(ub, bbr_ref[0], 1, 0), _dot(ub, bbi_ref[0], 1, 0), pr_ref, pi_ref, False)
        cr, ci = cr_scr[0:1, :], ci_scr[0:1, :]
        tr, ti = tr_ref[0], ti_ref[0]
        sr = xr + tr * cr - ti * ci
        si = xi + tr * ci + ti * cr
        sr_ref[...] = sr
        si_ref[...] = si
        cr_scr[0:1, :] = sr_ref[ct - 1:ct, :]
        ci_scr[0:1, :] = si_ref[ct - 1:ct, :]
        y = _dot(sr.astype(BF16), ctr_ref[0], 1, 0) - _dot(si.astype(BF16), cti_ref[0], 1, 0) + d_ref[...] * u
        yt_ref[...] = y
        y2_ref[...] = _gelu(y).astype(BF16)

    def mat(r, c_):
        return pl.BlockSpec((1, r, c_), lambda sb, b, c: (sb, 0, 0))

    ublk = pl.BlockSpec((ct, SSM_SB), lambda sb, b, c: (b * nc + c, sb))
    sblk = pl.BlockSpec((ct, ns), lambda sb, b, c: (b * nc + c, sb))
    return pl.pallas_call(
        body, name="ssm_fwd", grid=(nsb, cfg.b_loc, nc),
        in_specs=[ublk, mat(SSM_SB, ns), mat(SSM_SB, ns), mat(ns, SSM_SB), mat(ns, SSM_SB), mat(N_POW, ns),
                  mat(N_POW, ns), mat(ct, ns), mat(ct, ns), pl.BlockSpec((1, SSM_SB), lambda sb, b, c: (0, sb))],
        out_specs=[sblk, sblk, ublk, ublk],
        out_shape=[_sds((t, cfg.n_state), F32), _sds((t, cfg.n_state), F32), _sds((t, cfg.d_ssm), F32),
                   _sds((t, cfg.d_ssm), BF16)],
        scratch_shapes=[pltpu.VMEM((8, ns), F32), pltpu.VMEM((8, ns), F32)],
        compiler_params=_cp(("parallel", "arbitrary", "arbitrary")))(
            proj, bb_re, bb_im, ct_re, ct_im, pw_re, pw_im, tab_re, tab_im, dskip.reshape(1, cfg.d_ssm))


def ssm_bwd(cfg, dy2, ytot, proj, s_re, s_im, cf_re, cf_im, bbt_re, bbt_im, pw_re, pw_im, tabr_re, tabr_im, dskip):
    t = proj.shape[0]
    nsb = cfg.d_ssm // SSM_SB
    ns = SSM_SB // cfg.ssm_group * cfg.ssm_state
    ct = SSM_CHUNK
    nc = cfg.lp // ct

    def body(dy_ref, yt_ref, u_ref, sr_ref, si_ref, spr_ref, spi_ref, cfr_ref, cfi_ref, btr_ref, bti_ref,
             pr_ref, pi_ref, tr_ref, ti_ref, d_ref,
             du_ref, dctr_ref, dcti_ref, dbbr_ref, dbbi_ref, dar_ref, dai_ref, dd_ref,
             cr_scr, ci_scr, ar_scr, ai_scr):
        b, c = pl.program_id(1), pl.program_id(2)
        chunk = nc - 1 - c

        @pl.when((b == 0) & (c == 0))
        def _():
            for r in (dctr_ref, dcti_ref, dbbr_ref, dbbi_ref, dar_ref, dai_ref, dd_ref):
                r[...] = jnp.zeros_like(r)

        @pl.when(c == 0)
        def _():
            cr_scr[...] = jnp.zeros_like(cr_scr)
            ci_scr[...] = jnp.zeros_like(ci_scr)

        u = u_ref[...]
        dyt = dy_ref[...] * _gelu_grad(yt_ref[...])
        dytb = dyt.astype(BF16)
        xr, xi = _scan(_dot(dytb, cfr_ref[0], 1, 0), -_dot(dytb, cfi_ref[0], 1, 0), pr_ref, pi_ref, True)
        cr, ci = cr_scr[0:1, :], ci_scr[0:1, :]
        tr, ti = tr_ref[0], -ti_ref[0]
        a_re = xr + tr * cr - ti * ci
        a_im = xi + tr * ci + ti * cr
        ar_scr[...] = a_re
        ai_scr[...] = a_im
        cr_scr[0:1, :] = ar_scr[0:1, :]
        ci_scr[0:1, :] = ai_scr[0:1, :]

        sr, si = sr_ref[...], si_ref[...]
        rows = lax.broadcasted_iota(jnp.int32, (ct, 1), 0)
        keep_prev = chunk > 0
        pr_last = jnp.where(keep_prev, spr_ref[7:8, :], 0.0)
        pi_last = jnp.where(keep_prev, spi_ref[7:8, :], 0.0)
        sp_re = jnp.where(rows == 0, pr_last, pltpu.roll(sr, 1, 0))
        sp_im = jnp.where(rows == 0, pi_last, pltpu.roll(si, 1, 0))
        dar_ref[0] += jnp.sum(a_re * sp_re + a_im * sp_im, axis=0, keepdims=True)
        dai_ref[0] += jnp.sum(a_im * sp_re - a_re * sp_im, axis=0, keepdims=True)
        dctr_ref[0] += _dot(sr.astype(BF16), dytb, 0, 0)
        dcti_ref[0] += -_dot(si.astype(BF16), dytb, 0, 0)
        ub = u.astype(BF16)
        arb, aib = a_re.astype(BF16), a_im.astype(BF16)
        dbbr_ref[0] += _dot(ub, arb, 0, 0)
        dbbi_ref[0] += _dot(ub, aib, 0, 0)
        du = dyt * d_ref[...] + _dot(arb, btr_ref[0], 1, 0) + _dot(aib, bti_ref[0], 1, 0)
        valid = (chunk * ct + rows) >= cfg.pad
        du_ref[...] = jnp.where(valid, du, 0.0).astype(BF16)
        dd_ref[...] += jnp.sum(dyt * u, axis=0, keepdims=True)

    def mat(r, c_):
        return pl.BlockSpec((1, r, c_), lambda sb, b, c: (sb, 0, 0))

    lp8 = cfg.lp // 8
    ublk = pl.BlockSpec((ct, SSM_SB), lambda sb, b, c: (b * nc + nc - 1 - c, sb))
    sblk = pl.BlockSpec((ct, ns), lambda sb, b, c: (b * nc + nc - 1 - c, sb))
    sprev = pl.BlockSpec((8, ns), lambda sb, b, c: (jnp.maximum(b * lp8 + (nc - 1 - c) * (ct // 8) - 1, 0), sb))
    dvec = pl.BlockSpec((1, SSM_SB), lambda sb, b, c: (0, sb))
    return pl.pallas_call(
        body, name="ssm_bwd", grid=(nsb, cfg.b_loc, nc),
        in_specs=[ublk, ublk, ublk, sblk, sblk, sprev, sprev, mat(SSM_SB, ns), mat(SSM_SB, ns), mat(ns, SSM_SB),
                  mat(ns, SSM_SB), mat(N_POW, ns), mat(N_POW, ns), mat(ct, ns), mat(ct, ns), dvec],
        out_specs=[ublk, mat(ns, SSM_SB), mat(ns, SSM_SB), mat(SSM_SB, ns), mat(SSM_SB, ns), mat(1, ns), mat(1, ns),
                   dvec],
        out_shape=[_sds((t, cfg.d_ssm), BF16), _sds((nsb, ns, SSM_SB), F32), _sds((nsb, ns, SSM_SB), F32),
                   _sds((nsb, SSM_SB, ns), F32), _sds((nsb, SSM_SB, ns), F32), _sds((nsb, 1, ns), F32),
                   _sds((nsb, 1, ns), F32), _sds((1, cfg.d_ssm), F32)],
        scratch_shapes=[pltpu.VMEM((8, ns), F32), pltpu.VMEM((8, ns), F32), pltpu.VMEM((ct, ns), F32),
                        pltpu.VMEM((ct, ns), F32)],
        compiler_params=_cp(("arbitrary", "arbitrary", "arbitrary")))(
            dy2, ytot, proj, s_re, s_im, s_re, s_im, cf_re, cf_im, bbt_re, bbt_im, pw_re, pw_im, tabr_re, tabr_im,
            dskip.reshape(1, cfg.d_ssm))


def _ssm_tables(cfg, lam_re, lam_im, log_dt, b_re, b_im, c_re, c_im):
    g, n, c = cfg.groups, cfg.ssm_state, cfg.ssm_group
    gsb = SSM_SB // c
    nsb = cfg.d_ssm // SSM_SB
    ns = gsb * n
    br_t, bi_t = jnp.transpose(b_re, (2, 0, 1)), jnp.transpose(b_im, (2, 0, 1))
    prm = (lam_re, lam_im, log_dt.reshape(g, 1), br_t, bi_t)
    _, _, bbr, bbi, pr, pi, tr, ti = ssm_param_fwd(cfg, *prm)
    eye = jnp.eye(gsb, dtype=F32)

    def bdiag_b(x):
        return jnp.einsum('csgn,gh->sgchn', x.reshape(c, nsb, gsb, n), eye).reshape(nsb, SSM_SB, ns)

    def bdiag_c(x):
        return jnp.einsum('sgcn,gh->sgchn', x.reshape(nsb, gsb, c, n), eye).reshape(nsb, SSM_SB, ns)

    def lanes(x):
        return jnp.transpose(x.reshape(x.shape[0], nsb, ns), (1, 0, 2))

    bb_re, bb_im = bdiag_b(bbr), bdiag_b(bbi)
    cf_re, cf_im = bdiag_c(c_re), bdiag_c(c_im)
    sw = lambda x: jnp.swapaxes(x, 1, 2)
    return dict(prm=prm, bb_re=bb_re.astype(BF16), bb_im=bb_im.astype(BF16), bbt_re=sw(bb_re).astype(BF16),
                bbt_im=sw(bb_im).astype(BF16), cf_re=cf_re.astype(BF16), cf_im=cf_im.astype(BF16),
                ct_re=sw(cf_re).astype(BF16), ct_im=sw(cf_im).astype(BF16), pw_re=lanes(pr), pw_im=lanes(pi),
                tab_re=lanes(tr), tab_im=lanes(ti), tabr_re=lanes(tr[::-1]), tabr_im=lanes(ti[::-1]))


def _ssm_param_grads(cfg, tabs, dct_re, dct_im, dbb_re, dbb_im, dab_re, dab_im):
    g, n, c = cfg.groups, cfg.ssm_state, cfg.ssm_group
    gsb = SSM_SB // c
    nsb = cfg.d_ssm // SSM_SB

    def diag_b(x):
        d = jnp.einsum('sgcgn->sgcn', x.reshape(nsb, gsb, c, gsb, n))
        return jnp.transpose(d.reshape(g, c, n), (1, 0, 2))

    def diag_c(x):
        d = jnp.einsum('sgngc->sgcn', x.reshape(nsb, gsb, n, gsb, c))
        return d.reshape(g, c, n)

    dlr, dli, ddt, dbr_t, dbi_t = ssm_param_bwd(cfg, *tabs['prm'], dab_re.reshape(g, n), dab_im.reshape(g, n),
                                                diag_b(dbb_re), diag_b(dbb_im))
    return dict(ssm_lam_re=dlr, ssm_lam_im=dli, ssm_log_dt=ddt.reshape(g),
                ssm_b_re=jnp.transpose(dbr_t, (1, 2, 0)), ssm_b_im=jnp.transpose(dbi_t, (1, 2, 0)),
                ssm_c_re=diag_c(dct_re), ssm_c_im=diag_c(dct_im))


def _mm_tiles(t, n, k):
    tm = _tile(t, 1056 if k <= 3072 else 528, 16)
    tn = _tile(n, 512, 128)
    return tm, tn


def _nt(a, wt, name, out_dtype=F32):
    tm, tn = _mm_tiles(a.shape[0], wt.shape[0], a.shape[1])
    return mm([(a, wt, 0)], nt=True, n=wt.shape[0], tm=tm, tn=tn, out_dtype=out_dtype, name=name)


def _nn(pairs, n, name, res=None, res_scale=1.0, out_dtype=F32):
    kmax = max(p[0].shape[1] for p in pairs) * len(pairs)
    tm, tn = _mm_tiles(pairs[0][0].shape[0], n, kmax)
    return mm(pairs, nt=False, n=n, tm=tm, tn=tn, out_dtype=out_dtype, name=name, res=res, res_scale=res_scale)


def local_step(cfg, x, target, wf, rep):
    d, lp, bsz, t = cfg.d_model, cfg.lp, cfg.b_loc, cfg.tokens
    hdim = cfg.d_ff
    meta = jnp.broadcast_to(wf['meta_tokens'][None], (bsz, cfg.n_meta, d))
    h = jnp.concatenate([jnp.zeros((bsz, cfg.pad, d), F32), meta, x], axis=1).reshape(t, d)

    saved = []
    for l in range(cfg.depth):
        w = {k: v[l] for k, v in wf.items() if k not in ('meta_tokens',)}
        r = {k: v[l] for k, v in rep.items() if k != 'final_norm'}
        s = {'h0': h}
        s['xn1'] = rms_fwd(h, r['ffn1_norm'], "rms_fwd")
        s['a1'], s['b1'], s['act1'] = dual_mm_act(s['xn1'], w['w13t_1'], kind='swiglu', name="ffn_up")
        h = _nn([(s['act1'], w['w2_1'], 0)], d, "ffn_down", res=h, res_scale=0.5)
        s['h1'] = h
        s['xnm'] = rms_fwd(h, r['mix_norm'], "rms_fwd")
        proj = _nt(s['xnm'], w['w_int'], "proj_in")
        s['proj'] = proj
        tabs = _ssm_tables(cfg, r['ssm_lam_re'], r['ssm_lam_im'], r['ssm_log_dt'], r['ssm_b_re'], r['ssm_b_im'],
                           r['ssm_c_re'], r['ssm_c_im'])
        s['tabs'] = tabs
        s['s_re'], s['s_im'], s['ytot'], s['y2'] = ssm_fwd(
            cfg, proj, tabs['bb_re'], tabs['bb_im'], tabs['ct_re'], tabs['ct_im'], tabs['pw_re'], tabs['pw_im'],
            tabs['tab_re'], tabs['tab_im'], r['ssm_d'])
        s['ga'], s['gg'], o_ssm = dual_mm_act(s['y2'], w['w_glut'], kind='glu', name="ssm_glu")
        s['cpre'], s['hn'] = conv_fwd(cfg, proj, w['conv_w'], r['conv_b'], r['conv_ln_g'], r['conv_ln_b'])
        s['o_conv'] = _nt(s['hn'], w['conv_w_outt'], "branch_out")
        s['o'], s['lt'] = attn_fwd(cfg, proj)
        s['o_attn'] = _nt(s['o'], w['attn_w_ot'], "branch_out")
        s['merged'] = merge_fwd(cfg, proj, o_ssm, s['o_conv'], s['o_attn'])
        h = _nn([(s['merged'], w['w_out'], 0)], d, "mix_out", res=h, res_scale=1.0)
        s['h2'] = h
        s['xn2'] = rms_fwd(h, r['ffn2_norm'], "rms_fwd")
        s['a2'], s['b2'], s['act2'] = dual_mm_act(s['xn2'], w['w13t_2'], kind='swiglu', name="ffn_up")
        h = _nn([(s['act2'], w['w2_2'], 0)], d, "ffn_down", res=h, res_scale=0.5)
        saved.append((w, r, s))

    dh3, dgf, loss_sq = final_fwd_bwd(cfg, h.reshape(bsz, lp, d), rep['final_norm'], target)
    dh = dh3.reshape(t, d)
    loss = 0.5 * loss_sq[0, 0] / d

    gl = []
    for l in reversed(range(cfg.depth)):
        w, r, s = saved[l]
        g = {}

        def ffn_bwd(dh, tag, a, b, act, xn, hin, norm):
            da, db = ffn_down_bwd(dh, w['w2_' + tag], a, b, "ffn_down_bwd")
            g['w2_' + tag] = mm_tn(act, dh, name="ffn_dw2", scale=0.5)
            dxn = _nn([(da, w['w13t_' + tag], 0), (db, w['w13t_' + tag], 1)], d, "ffn_dxn")
            g['w13t_' + tag] = jnp.concatenate([mm_tn(da, xn, name="ffn_dw13"), mm_tn(db, xn, name="ffn_dw13")], 0)
            return rms_bwd(dxn, hin, norm, dh, "rms_bwd")

        dh, dg2 = ffn_bwd(dh, '2', s['a2'], s['b2'], s['act2'], s['xn2'], s['h2'], r['ffn2_norm'])
        g['ffn2_norm'] = dg2.reshape(d)

        proj, tabs = s['proj'], s['tabs']
        dmerged = _nt(dh, w['w_out'], "mix_dmerged")
        g['w_out'] = mm_tn(s['merged'], dh, name="mix_dwout")
        dg0, dg1, dg2_, dga, dgg, do_conv, do_attn = merge_bwd(cfg, dmerged, proj, s['ga'], s['gg'], s['o_conv'],
                                                               s['o_attn'])
        half = w['w_glut'].shape[0] // 2
        dy2 = _nn([(dga, w['w_glut'], 0), (dgg, w['w_glut'], 1)], cfg.d_ssm, "ssm_dy2")
        g['w_glut'] = jnp.concatenate([mm_tn(dga, s['y2'], name="ssm_dwglu"), mm_tn(dgg, s['y2'], name="ssm_dwglu")], 0)
        du, dct_re, dct_im, dbb_re, dbb_im, dab_re, dab_im, dd = ssm_bwd(
            cfg, dy2, s['ytot'], proj, s['s_re'], s['s_im'], tabs['cf_re'], tabs['cf_im'], tabs['bbt_re'],
            tabs['bbt_im'], tabs['pw_re'], tabs['pw_im'], tabs['tabr_re'], tabs['tabr_im'], r['ssm_d'])
        g.update(_ssm_param_grads(cfg, tabs, dct_re, dct_im, dbb_re, dbb_im, dab_re, dab_im))
        g['ssm_d'] = dd.reshape(cfg.d_ssm)

        dhn = _nn([(do_conv, w['conv_w_outt'], 0)], cfg.d_conv, "branch_din")
        g['conv_w_outt'] = mm_tn(do_conv, s['hn'], name="branch_dw")
        dxa, dxg, dcw, dcb, dlg, dlb = conv_bwd(cfg, dhn, s['cpre'], proj, w['conv_w'], r['conv_ln_g'],
                                                r['conv_ln_b'])
        g['conv_w'], g['conv_b'] = dcw, dcb.reshape(cfg.d_conv)
        g['conv_ln_g'], g['conv_ln_b'] = dlg.reshape(cfg.d_conv), dlb.reshape(cfg.d_conv)

        do = _nn([(do_attn, w['attn_w_ot'], 0)], cfg.d_attn, "branch_din")
        g['attn_w_ot'] = mm_tn(do_attn, s['o'], name="branch_dw")
        dq, dk, dv = attn_bwd(cfg, proj, s['lt'], do)

        dproj = jnp.concatenate([du, dxa, dxg, dq.astype(BF16), dk.astype(BF16), dv.astype(BF16), dg0, dg1, dg2_],
                                axis=1)
        dxn = _nn([(dproj, w['w_int'], 0)], d, "proj_dxn")
        g['w_int'] = mm_tn(dproj, s['xnm'], name="proj_dw")
        dh, dgm = rms_bwd(dxn, s['h1'], r['mix_norm'], dh, "rms_bwd")
        g['mix_norm'] = dgm.reshape(d)

        dh, dg1n = ffn_bwd(dh, '1', s['a1'], s['b1'], s['act1'], s['xn1'], s['h0'], r['ffn1_norm'])
        g['ffn1_norm'] = dg1n.reshape(d)
        gl.append(g)

    gl = gl[::-1]
    dh0 = dh.reshape(bsz, lp, d)
    grad_x = dh0[:, Q_BLOCK:]
    grads = {k: jnp.stack([g[k] for g in gl]) for k in gl[0]}
    grads['meta_tokens'] = jnp.sum(dh0[:, cfg.pad:Q_BLOCK], axis=0)
    grads['final_norm'] = dgf.reshape(d)
    return loss, grad_x, grads


def _me():
    return lax.axis_index("x"), lax.axis_index("y"), lax.axis_index("c")


def _flat(px, py, pc):
    return 4 * px + 2 * py + pc


def all_gather_blocks(block, name):
    r, c_ = block.shape

    def body(x_ref, out_ref, send_sems, recv_sems, local_sem):
        x, y, c = _me()
        me, sibling = (x, y, c), (x, y, 1 - c)
        chips = [(1 - x, y), (x, 1 - y), (1 - x, 1 - y)]

        def slot(px, py, pc):
            return out_ref.at[_flat(px, py, pc)]

        def copy(k, blk, to, src=None):
            return pltpu.make_async_remote_copy(
                src_ref=slot(*blk) if src is None else src, dst_ref=slot(*blk), send_sem=send_sems.at[k],
                recv_sem=recv_sems.at[k], device_id=to, device_id_type=pl.DeviceIdType.MESH)

        mine = pltpu.make_async_copy(x_ref, slot(*me), local_sem)
        mine.start()
        first = [copy(0, me, sibling, src=x_ref)]
        first += [copy(1 + j, me, (*chip, c), src=x_ref) for j, chip in enumerate(chips)]
        for cp in first:
            cp.start()
        passed = [copy(4 + j, (*chip, c), sibling) for j, chip in enumerate(chips)]
        for j, chip in enumerate(chips):
            copy(1 + j, (*chip, c), me).wait_recv()
            passed[j].start()
        copy(0, sibling, me).wait_recv()
        for j, chip in enumerate(chips):
            copy(4 + j, (*chip, 1 - c), me).wait_recv()
        for cp in first + passed:
            cp.wait_send()
        mine.wait()

    return pl.pallas_call(
        body, name=name, out_shape=_sds((8, r, c_), block.dtype),
        in_specs=[pl.BlockSpec(memory_space=pl.ANY)], out_specs=pl.BlockSpec(memory_space=pl.ANY),
        scratch_shapes=[pltpu.SemaphoreType.DMA((7,)), pltpu.SemaphoreType.DMA((7,)), pltpu.SemaphoreType.DMA(())],
    )(block)


def all_to_all_blocks(send, name):
    _, r, c_ = send.shape

    def body(s_ref, out_ref, send_sems, recv_sems, local_sem):
        x, y, c = _me()
        me = _flat(x, y, c)
        mine = pltpu.make_async_copy(s_ref.at[me], out_ref.at[me], local_sem)
        mine.start()
        copies = []
        for rel in range(1, 8):
            px = 1 - x if rel & 4 else x
            py = 1 - y if rel & 2 else y
            pc = 1 - c if rel & 1 else c
            peer = _flat(px, py, pc)
            cp = pltpu.make_async_remote_copy(
                src_ref=s_ref.at[peer], dst_ref=out_ref.at[me], send_sem=send_sems.at[rel - 1],
                recv_sem=recv_sems.at[rel - 1], device_id=(px, py, pc), device_id_type=pl.DeviceIdType.MESH)
            cp.start()
            copies.append(cp)
        for cp in copies:
            cp.wait_recv()
        for cp in copies:
            cp.wait_send()
        mine.wait()

    return pl.pallas_call(
        body, name=name, out_shape=_sds((8, r, c_), send.dtype),
        in_specs=[pl.BlockSpec(memory_space=pl.ANY)], out_specs=pl.BlockSpec(memory_space=pl.ANY),
        scratch_shapes=[pltpu.SemaphoreType.DMA((7,)), pltpu.SemaphoreType.DMA((7,)), pltpu.SemaphoreType.DMA(())],
    )(send)


def reduce_blocks(recv, name):
    nsrc, r, c_ = recv.shape
    tr = _tile(r, 128, 16)

    def body(x_ref, o_ref):
        acc = x_ref[0].astype(F32)
        for s in range(1, nsrc):
            acc = acc + x_ref[s].astype(F32)
        o_ref[...] = acc

    return pl.pallas_call(
        body, name=name, grid=(r // tr,),
        in_specs=[pl.BlockSpec((nsrc, tr, c_), lambda i: (0, i, 0))], out_specs=pl.BlockSpec((tr, c_), lambda i: (i, 0)),
        out_shape=_sds((r, c_), F32), compiler_params=_cp(("parallel",)))(recv)


def adamw(w, g, m, v):
    shape = w.shape
    size = math.prod(shape)
    if shape[-1] < V7X_LANES and size % 1024 == 0:
        view = (size // 1024, 1024)
    else:
        view = (size // shape[-1], shape[-1])
    rows, cols = view
    tr = _tile(rows, max(8, (1 << 19) // cols // 8 * 8), 8)
    c1 = 1.0 - ADAM_B1 ** ADAM_STEP
    c2 = 1.0 - ADAM_B2 ** ADAM_STEP

    def body(w_ref, g_ref, m_ref, v_ref, d_ref, nm_ref, nv_ref):
        gv = g_ref[...]
        nm = ADAM_B1 * m_ref[...] + (1.0 - ADAM_B1) * gv
        nv = ADAM_B2 * v_ref[...] + (1.0 - ADAM_B2) * (gv * gv)
        nm_ref[...] = nm
        nv_ref[...] = nv
        d_ref[...] = -ADAM_LR * ((nm / c1) / (jnp.sqrt(nv / c2) + ADAM_EPS) + ADAM_WD * w_ref[...])

    blk = pl.BlockSpec((tr, cols), lambda i: (i, 0))
    out = _sds(view, F32)
    res = pl.pallas_call(
        body, name="adamw", grid=(rows // tr,), in_specs=[blk] * 4, out_specs=[blk] * 3, out_shape=[out] * 3,
        compiler_params=_cp(("parallel",)))(w.reshape(view), g.reshape(view), m.reshape(view), v.reshape(view))
    return tuple(a.reshape(shape) for a in res)


PACKED = [('w13t_1', 'ffn1_w13', True), ('w2_1', 'ffn1_w2', False), ('w_int', 'w_in', True), ('w_out', 'w_out', False),
          ('w13t_2', 'ffn2_w13', True), ('w2_2', 'ffn2_w2', False), ('w_glut', 'ssm_w_glu', True),
          ('conv_w_outt', 'conv_w_out', True), ('attn_w_ot', 'attn_w_o', True)]


def _rows_of(a, d):
    return a.reshape(a.shape[0] * a.shape[1] // d, d)


def _pad_rows(flat, d, mult):
    n = flat.shape[-1]
    rows = -(-n // d)
    rows = -(-rows // mult) * mult
    flat = jnp.pad(flat, [(0, 0)] * (flat.ndim - 1) + [(0, rows * d - n)])
    return flat.reshape(flat.shape[:-1] + (rows, d))


def pack_weight_shards(cfg, wts):
    d = cfg.d_model
    parts, layout = [], []
    for l in range(cfg.depth):
        for name, src, tr in PACKED:
            a = wts[src][l]
            a = (a.T if tr else a).astype(BF16)
            layout.append((name, l, a.shape))
            parts.append(_rows_of(a, d))
    small = jnp.concatenate([wts['conv_w'].reshape(-1), wts['meta_tokens'].reshape(-1)])
    return jnp.concatenate(parts, axis=0), _pad_rows(small, d, 8), layout


def unpack_gathered(cfg, gathered, gathered_small, layout, wts):
    d = cfg.d_model
    out = {}
    off = 0
    for name, l, shape in layout:
        rows = shape[0] * shape[1] // d
        full = gathered[:, off:off + rows].reshape(8 * shape[0], shape[1])
        out.setdefault(name, []).append(full)
        off += rows
    wf = {k: jnp.stack(v) for k, v in out.items()}
    n_cw, n_mt = wts['conv_w'].size, wts['meta_tokens'].size
    small = gathered_small.reshape(8, -1)
    cw = small[:, :n_cw].reshape((8,) + wts['conv_w'].shape)
    wf['conv_w'] = jnp.transpose(cw, (1, 2, 0, 3)).reshape(cw.shape[1], cw.shape[2], -1)
    mt = small[:, n_cw:n_cw + n_mt].reshape((8,) + wts['meta_tokens'].shape)
    wf['meta_tokens'] = jnp.transpose(mt, (1, 0, 2)).reshape(mt.shape[1], -1)
    return wf


def pack_grads(cfg, grads, layout, rep_names):
    d = cfg.d_model
    parts = []
    for name, l, shape in layout:
        gfull = grads[name][l]
        parts.append(gfull.reshape(8, shape[0] * shape[1] // d, d).astype(BF16))
    cw = grads['conv_w']
    cw = jnp.transpose(cw.reshape(cw.shape[0], cw.shape[1], 8, -1), (2, 0, 1, 3)).reshape(8, -1)
    mt = grads['meta_tokens']
    mt = jnp.transpose(mt.reshape(mt.shape[0], 8, -1), (1, 0, 2)).reshape(8, -1)
    repl = jnp.concatenate([grads[k].reshape(-1) for k in rep_names])
    small = jnp.concatenate([cw, mt, jnp.broadcast_to(repl[None], (8, repl.shape[0]))], axis=1)
    big = jnp.concatenate(parts, axis=1)
    extra = -big.shape[1] % 128
    if extra:
        big = jnp.pad(big, ((0, 0), (0, extra), (0, 0)))
    return big, _pad_rows(small, d, 128)


def unpack_grads(cfg, gsum, gsmall, layout, wts, rep_names):
    d = cfg.d_model
    acc = {}
    off = 0
    for name, l, shape in layout:
        rows = shape[0] * shape[1] // d
        acc.setdefault(name, []).append(gsum[off:off + rows].reshape(shape))
        off += rows
    out = {}
    for name, src, tr in PACKED:
        a = jnp.stack(acc[name])
        out[src] = jnp.swapaxes(a, 1, 2) if tr else a
    flat = gsmall.reshape(-1)
    pos = 0
    for k in ['conv_w', 'meta_tokens'] + rep_names:
        n = wts[k].size
        out[k] = flat[pos:pos + n].reshape(wts[k].shape)
        pos += n
    return out


def train_step(cfg, x, target, wts, ms, vs):
    packed, packed_small, layout = pack_weight_shards(cfg, wts)
    gathered = all_gather_blocks(packed, "all_gather_weights")
    gathered_small = all_gather_blocks(packed_small, "all_gather_small_weights")
    wf = unpack_gathered(cfg, gathered, gathered_small, layout, wts)
    rep = {k: wts[k] for k in REPLICATED}
    loss, grad_x, grads = local_step(cfg, x, target, wf, rep)
    send_big, send_small = pack_grads(cfg, grads, layout, REPLICATED)
    gsum = reduce_blocks(all_to_all_blocks(send_big, "all_to_all_grads"), "reduce_grads")
    gsmall = reduce_blocks(all_to_all_blocks(send_small, "all_to_all_small_grads"), "reduce_small_grads")
    gw = unpack_grads(cfg, gsum, gsmall, layout, wts, REPLICATED)
    loss = lax.psum(loss, ("x", "y", "c"))
    deltas, new_m, new_v = {}, {}, {}
    for k in WEIGHT_NAMES:
        deltas[k], new_m[k], new_v[k] = adamw(wts[k], gw[k], ms[k], vs[k])
    return (loss, grad_x, *[gw[k] for k in WEIGHT_NAMES], *[deltas[k] for k in WEIGHT_NAMES],
            *[new_m[k] for k in WEIGHT_NAMES], *[new_v[k] for k in WEIGHT_NAMES])


def kernel(x, meta_tokens, ffn1_norm, ffn1_w13, ffn1_w2, mix_norm, w_in, ssm_lam_re, ssm_lam_im, ssm_log_dt, ssm_b_re, ssm_b_im, ssm_c_re, ssm_c_im, ssm_d, ssm_w_glu, conv_w, conv_b, conv_ln_g, conv_ln_b, conv_w_out, attn_w_o, w_out, ffn2_norm, ffn2_w13, ffn2_w2, final_norm, loss_target, m_meta_tokens, m_ffn1_norm, m_ffn1_w13, m_ffn1_w2, m_mix_norm, m_w_in, m_ssm_lam_re, m_ssm_lam_im, m_ssm_log_dt, m_ssm_b_re, m_ssm_b_im, m_ssm_c_re, m_ssm_c_im, m_ssm_d, m_ssm_w_glu, m_conv_w, m_conv_b, m_conv_ln_g, m_conv_ln_b, m_conv_w_out, m_attn_w_o, m_w_out, m_ffn2_norm, m_ffn2_w13, m_ffn2_w2, m_final_norm, v_meta_tokens, v_ffn1_norm, v_ffn1_w13, v_ffn1_w2, v_mix_norm, v_w_in, v_ssm_lam_re, v_ssm_lam_im, v_ssm_log_dt, v_ssm_b_re, v_ssm_b_im, v_ssm_c_re, v_ssm_c_im, v_ssm_d, v_ssm_w_glu, v_conv_w, v_conv_b, v_conv_ln_g, v_conv_ln_b, v_conv_w_out, v_attn_w_o, v_w_out, v_ffn2_norm, v_ffn2_w13, v_ffn2_w2, v_final_norm):
    given = dict(locals())
    wts = {k: given[k] for k in WEIGHT_NAMES}
    ms = {k: given["m_" + k] for k in WEIGHT_NAMES}
    vs = {k: given["v_" + k] for k in WEIGHT_NAMES}
    return train_step(FULL, x, loss_target, wts, ms, vs)
```

```python
import functools
import math
from typing import NamedTuple

import jax
import jax.numpy as jnp
from jax import lax
from jax.experimental import pallas as pl
from jax.experimental.pallas import tpu as pltpu

F32 = jnp.float32
BF16 = jnp.bfloat16
RMS_EPS = 1e-6
LN_EPS = 1e-5
ADAM_LR = 0.001
ADAM_B1 = 0.9
ADAM_B2 = 0.999
ADAM_EPS = 1e-08
ADAM_WD = 0.01
ADAM_STEP = 10
V7X_LANES = 128
V7X_VMEM_LIMIT = 52 * 1024 * 1024
Q_BLOCK = 128
SSM_CHUNK = 128
SSM_SB = 128
CONV_HALO = 32

WEIGHT_NAMES = ['meta_tokens', 'ffn1_norm', 'ffn1_w13', 'ffn1_w2', 'mix_norm', 'w_in', 'ssm_lam_re', 'ssm_lam_im',
                'ssm_log_dt', 'ssm_b_re', 'ssm_b_im', 'ssm_c_re', 'ssm_c_im', 'ssm_d', 'ssm_w_glu', 'conv_w', 'conv_b',
                'conv_ln_g', 'conv_ln_b', 'conv_w_out', 'attn_w_o', 'w_out', 'ffn2_norm', 'ffn2_w13', 'ffn2_w2',
                'final_norm']
REPLICATED = ['ffn1_norm', 'mix_norm', 'ssm_lam_re', 'ssm_lam_im', 'ssm_log_dt', 'ssm_b_re', 'ssm_b_im', 'ssm_c_re',
              'ssm_c_im', 'ssm_d', 'conv_b', 'conv_ln_g', 'conv_ln_b', 'ffn2_norm', 'final_norm']


class Cfg(NamedTuple):
    d_model: int
    seq: int
    depth: int
    d_ff: int
    b_loc: int
    ndev: int = 8
    n_meta: int = 16
    n_heads: int = 8
    head_dim: int = 64
    ssm_group: int = 16
    ssm_state: int = 64
    conv_width: int = 31

    @property
    def d_ssm(self): return self.d_model // 2
    @property
    def groups(self): return self.d_ssm // self.ssm_group
    @property
    def n_state(self): return self.groups * self.ssm_state
    @property
    def d_conv(self): return self.d_model // 2
    @property
    def d_attn(self): return self.n_heads * self.head_dim
    @property
    def off_xa(self): return self.d_ssm
    @property
    def off_xg(self): return self.d_ssm + self.d_conv
    @property
    def off_q(self): return self.d_ssm + 2 * self.d_conv
    @property
    def off_k(self): return self.off_q + self.d_attn
    @property
    def off_v(self): return self.off_k + self.d_attn
    @property
    def off_gate(self): return self.off_v + self.d_attn
    @property
    def d_in(self): return self.off_gate + 3 * self.d_model
    @property
    def pad(self): return Q_BLOCK - self.n_meta
    @property
    def lp(self): return Q_BLOCK + self.seq
    @property
    def tokens(self): return self.b_loc * self.lp


FULL = Cfg(d_model=1024, seq=4096, depth=4, d_ff=2816, b_loc=2)


def _tile(n, pref, align):
    for t in range(min(pref, n), 0, -1):
        if n % t == 0 and t % align == 0:
            return t
    return n


def _cp(sem):
    return pltpu.CompilerParams(dimension_semantics=sem, vmem_limit_bytes=V7X_VMEM_LIMIT)


def _sds(shape, dtype):
    return jax.ShapeDtypeStruct(shape, dtype)


def _sigmoid(x):
    return jax.nn.sigmoid(x)


def _dot(a, b, ca, cb):
    return lax.dot_general(a, b, (((ca,), (cb,)), ((), ())), preferred_element_type=F32)


def rms_fwd(h, g, name):
    t, d = h.shape
    tm = _tile(t, 1056, 16)

    def body(h_ref, g_ref, o_ref):
        x = h_ref[...]
        r = lax.rsqrt(jnp.mean(x * x, axis=-1, keepdims=True) + RMS_EPS)
        o_ref[...] = (x * r * g_ref[...]).astype(BF16)

    return pl.pallas_call(
        body, name=name, grid=(t // tm,),
        in_specs=[pl.BlockSpec((tm, d), lambda i: (i, 0)), pl.BlockSpec((1, d), lambda i: (0, 0))],
        out_specs=pl.BlockSpec((tm, d), lambda i: (i, 0)), out_shape=_sds((t, d), BF16),
        compiler_params=_cp(("parallel",)))(h, g.reshape(1, d))


def rms_bwd(dxn, h, g, dh_res, name):
    t, d = h.shape
    tm = _tile(t, 528, 16)

    def body(dxn_ref, h_ref, g_ref, r_ref, dh_ref, dg_ref):
        x = h_ref[...]
        r = lax.rsqrt(jnp.mean(x * x, axis=-1, keepdims=True) + RMS_EPS)
        xhat = x * r
        dy = dxn_ref[...]
        dyg = dy * g_ref[...]
        dx = r * (dyg - xhat * jnp.mean(dyg * xhat, axis=-1, keepdims=True))
        dh_ref[...] = r_ref[...] + dx

        @pl.when(pl.program_id(0) == 0)
        def _():
            dg_ref[...] = jnp.zeros_like(dg_ref)

        dg_ref[...] += jnp.sum(dy * xhat, axis=0, keepdims=True)

    row = pl.BlockSpec((tm, d), lambda i: (i, 0))
    vec = pl.BlockSpec((1, d), lambda i: (0, 0))
    return pl.pallas_call(
        body, name=name, grid=(t // tm,), in_specs=[row, row, vec, row], out_specs=[row, vec],
        out_shape=[_sds((t, d), F32), _sds((1, d), F32)],
        compiler_params=_cp(("arbitrary",)))(dxn, h, g.reshape(1, d), dh_res)


def mm(pairs, *, nt, n, tm, tn, out_dtype, name, res=None, res_scale=1.0):
    m = pairs[0][0].shape[0]
    np_ = len(pairs)

    def body(*refs):
        o_ref = refs[-1]
        acc = None
        for p in range(np_):
            a = refs[2 * p][...].astype(BF16)
            b = refs[2 * p + 1][...].astype(BF16)
            d = _dot(a, b, 1, 1 if nt else 0)
            acc = d if acc is None else acc + d
        if res is not None:
            acc = refs[2 * np_][...] + res_scale * acc
        o_ref[...] = acc.astype(out_dtype)

    in_specs, args = [], []
    for a, b, kblk in pairs:
        k = a.shape[1]
        in_specs.append(pl.BlockSpec((tm, k), lambda i, j: (i, 0)))
        if nt:
            nb = n // tn
            in_specs.append(pl.BlockSpec((tn, k), functools.partial(lambda i, j, o: (j + o, 0), o=kblk * nb)))
        else:
            in_specs.append(pl.BlockSpec((k, tn), functools.partial(lambda i, j, o: (o, j), o=kblk)))
        args += [a, b]
    if res is not None:
        in_specs.append(pl.BlockSpec((tm, tn), lambda i, j: (i, j)))
        args.append(res)
    return pl.pallas_call(
        body, name=name, grid=(m // tm, n // tn), in_specs=in_specs,
        out_specs=pl.BlockSpec((tm, tn), lambda i, j: (i, j)), out_shape=_sds((m, n), out_dtype),
        compiler_params=_cp(("parallel", "arbitrary")))(*args)


def mm_tn(a, b, *, name, scale=1.0):
    t, m = a.shape
    n = b.shape[1]
    tm = _tile(m, 1536, 128)
    tn = _tile(n, 1024, 128)
    tk = _tile(t, 1056, 16)
    nk = t // tk

    def body(a_ref, b_ref, o_ref):
        k = pl.program_id(2)

        @pl.when(k == 0)
        def _():
            o_ref[...] = jnp.zeros_like(o_ref)

        o_ref[...] += _dot(a_ref[...].astype(BF16), b_ref[...].astype(BF16), 0, 0)
        if scale != 1.0:
            @pl.when(k == nk - 1)
            def _():
                o_ref[...] = o_ref[...] * scale

    return pl.pallas_call(
        body, name=name, grid=(m // tm, n // tn, nk),
        in_specs=[pl.BlockSpec((tk, tm), lambda i, j, k: (k, i)), pl.BlockSpec((tk, tn), lambda i, j, k: (k, j))],
        out_specs=pl.BlockSpec((tm, tn), lambda i, j, k: (i, j)), out_shape=_sds((m, n), F32),
        compiler_params=_cp(("parallel", "parallel", "arbitrary")))(a, b)


def dual_mm_act(x, wt, *, kind, name):
    t, k = x.shape
    hdim = wt.shape[0] // 2
    tm = _tile(t, 1056, 16)
    tn = _tile(hdim, 256, 128)
    nb = hdim // tn
    act_dtype = BF16 if kind == 'swiglu' else F32

    def body(x_ref, wa_ref, wb_ref, a_ref, b_ref, act_ref):
        xv = x_ref[...]
        a = _dot(xv, wa_ref[...], 1, 1)
        b = _dot(xv, wb_ref[...], 1, 1)
        a_ref[...] = a.astype(act_dtype)
        b_ref[...] = b.astype(act_dtype)
        if kind == 'swiglu':
            act_ref[...] = (a * _sigmoid(a) * b).astype(act_dtype)
        else:
            act_ref[...] = (a * _sigmoid(b)).astype(act_dtype)

    ob = pl.BlockSpec((tm, tn), lambda i, j: (i, j))
    return pl.pallas_call(
        body, name=name, grid=(t // tm, nb),
        in_specs=[pl.BlockSpec((tm, k), lambda i, j: (i, 0)), pl.BlockSpec((tn, k), lambda i, j: (j, 0)),
                  pl.BlockSpec((tn, k), lambda i, j: (j + nb, 0))],
        out_specs=[ob, ob, ob], out_shape=[_sds((t, hdim), act_dtype)] * 3,
        compiler_params=_cp(("parallel", "arbitrary")))(x, wt, wt)


def ffn_down_bwd(dh, w2, a, b, name):
    t, d = dh.shape
    hdim = w2.shape[0]
    tm = _tile(t, 1056, 16)
    tn = _tile(hdim, 256, 128)

    def body(dh_ref, w_ref, a_ref, b_ref, da_ref, db_ref):
        dact = 0.5 * _dot(dh_ref[...].astype(BF16), w_ref[...], 1, 1)
        av = a_ref[...].astype(F32)
        sg = _sigmoid(av)
        da_ref[...] = (dact * b_ref[...].astype(F32) * sg * (1.0 + av * (1.0 - sg))).astype(BF16)
        db_ref[...] = (dact * av * sg).astype(BF16)

    ob = pl.BlockSpec((tm, tn), lambda i, j: (i, j))
    return pl.pallas_call(
        body, name=name, grid=(t // tm, hdim // tn),
        in_specs=[pl.BlockSpec((tm, d), lambda i, j: (i, 0)), pl.BlockSpec((tn, d), lambda i, j: (j, 0)), ob, ob],
        out_specs=[ob, ob], out_shape=[_sds((t, hdim), BF16), _sds((t, hdim), BF16)],
        compiler_params=_cp(("parallel", "arbitrary")))(dh, w2, a, b)


def final_fwd_bwd(cfg, h, gf, target):
    bsz, lp, d = h.shape
    nq = lp // Q_BLOCK

    def body(h_ref, g_ref, t_ref, dh_ref, dg_ref, loss_ref):
        b, j = pl.program_id(0), pl.program_id(1)
        x = h_ref[0]
        r = lax.rsqrt(jnp.mean(x * x, axis=-1, keepdims=True) + RMS_EPS)
        xhat = x * r
        gv = g_ref[...]
        diff = jnp.where(j > 0, xhat * gv - t_ref[0], 0.0)
        dy = diff * (1.0 / d)
        dyg = dy * gv
        dh_ref[0] = r * (dyg - xhat * jnp.mean(dyg * xhat, axis=-1, keepdims=True))

        @pl.when((b == 0) & (j == 0))
        def _():
            dg_ref[...] = jnp.zeros_like(dg_ref)
            loss_ref[...] = jnp.zeros_like(loss_ref)

        dg_ref[...] += jnp.sum(dy * xhat, axis=0, keepdims=True)
        loss_ref[...] += jnp.sum(jnp.sum(diff * diff, axis=1, keepdims=True), axis=0, keepdims=True)

    blk = pl.BlockSpec((1, Q_BLOCK, d), lambda b, j: (b, j, 0))
    return pl.pallas_call(
        body, name="final_loss", grid=(bsz, nq),
        in_specs=[blk, pl.BlockSpec((1, d), lambda b, j: (0, 0)),
                  pl.BlockSpec((1, Q_BLOCK, d), lambda b, j: (b, jnp.maximum(j - 1, 0), 0))],
        out_specs=[blk, pl.BlockSpec((1, d), lambda b, j: (0, 0)), pl.BlockSpec((1, 1), lambda b, j: (0, 0))],
        out_shape=[_sds((bsz, lp, d), F32), _sds((1, d), F32), _sds((1, 1), F32)],
        compiler_params=_cp(("arbitrary", "arbitrary")))(h, gf.reshape(1, d), target)


def _row_valid(cfg, tm, i):
    pos = (i * tm) % cfg.lp + lax.broadcasted_iota(jnp.int32, (tm, 1), 0)
    return pos >= cfg.pad


def merge_fwd(cfg, proj, o_ssm, o_conv, o_attn):
    t, d = o_ssm.shape
    tc = math.gcd(math.gcd(d, cfg.off_gate), 512)
    tm = _tile(cfg.lp, 528, 16)
    g0 = cfg.off_gate // tc
    nc = d // tc

    def body(g0_ref, g1_ref, g2_ref, s_ref, c_ref, a_ref, o_ref):
        valid = _row_valid(cfg, tm, pl.program_id(0))
        m = (_sigmoid(g0_ref[...]) * s_ref[...] + _sigmoid(g1_ref[...]) * c_ref[...]
             + _sigmoid(g2_ref[...]) * a_ref[...])
        o_ref[...] = jnp.where(valid, m, 0.0).astype(BF16)

    gate = [pl.BlockSpec((tm, tc), functools.partial(lambda i, j, o: (i, o + j), o=g0 + k * nc)) for k in range(3)]
    blk = pl.BlockSpec((tm, tc), lambda i, j: (i, j))
    return pl.pallas_call(
        body, name="merge_fwd", grid=(t // tm, nc), in_specs=gate + [blk, blk, blk], out_specs=blk,
        out_shape=_sds((t, d), BF16), compiler_params=_cp(("parallel", "parallel")))(
            proj, proj, proj, o_ssm, o_conv, o_attn)


def merge_bwd(cfg, dmerged, proj, ga, gg, o_conv, o_attn):
    t, d = dmerged.shape
    tc = math.gcd(math.gcd(d, cfg.off_gate), 512)
    tm = _tile(cfg.lp, 528, 16)
    g0 = cfg.off_gate // tc
    nc = d // tc

    def body(dm_ref, g0_ref, g1_ref, g2_ref, ga_ref, gg_ref, c_ref, a_ref,
             dg0_ref, dg1_ref, dg2_ref, dga_ref, dgg_ref, dc_ref, da_ref):
        dm = dm_ref[...]
        s0, s1, s2 = _sigmoid(g0_ref[...]), _sigmoid(g1_ref[...]), _sigmoid(g2_ref[...])
        sg = _sigmoid(gg_ref[...])
        gav = ga_ref[...]
        o_s = gav * sg
        dg0_ref[...] = (dm * o_s * s0 * (1.0 - s0)).astype(BF16)
        dg1_ref[...] = (dm * c_ref[...] * s1 * (1.0 - s1)).astype(BF16)
        dg2_ref[...] = (dm * a_ref[...] * s2 * (1.0 - s2)).astype(BF16)
        dos = dm * s0
        dga_ref[...] = (dos * sg).astype(BF16)
        dgg_ref[...] = (dos * gav * sg * (1.0 - sg)).astype(BF16)
        dc_ref[...] = (dm * s1).astype(BF16)
        da_ref[...] = (dm * s2).astype(BF16)

    gate = [pl.BlockSpec((tm, tc), functools.partial(lambda i, j, o: (i, o + j), o=g0 + k * nc)) for k in range(3)]
    blk = pl.BlockSpec((tm, tc), lambda i, j: (i, j))
    out = _sds((t, d), BF16)
    return pl.pallas_call(
        body, name="merge_bwd", grid=(t // tm, nc), in_specs=[blk] + gate + [blk, blk, blk, blk],
        out_specs=[blk] * 7, out_shape=[out] * 7, compiler_params=_cp(("parallel", "parallel")))(
            dmerged, proj, proj, proj, ga, gg, o_conv, o_attn)


def _conv_tile(cfg):
    return _tile(cfg.lp, 384, CONV_HALO)


def _shift_up(x, off, rows):
    if off == 0:
        return x[:rows]
    return pltpu.roll(x, x.shape[0] - off, 0)[:rows]


def conv_fwd(cfg, proj, w, bias, ln_g, ln_b):
    t = proj.shape[0]
    wc = cfg.d_conv
    tt = _conv_tile(cfg)
    nc = cfg.lp // tt
    kw = cfg.conv_width
    first = CONV_HALO - (kw - 1)

    def body(a_ref, g_ref, ap_ref, gp_ref, w_ref, b_ref, lg_ref, lb_ref, cpre_ref, hn_ref):
        c = pl.program_id(1)
        hc = a_ref[...] * _sigmoid(g_ref[...])
        hp = jnp.where(c > 0, ap_ref[...] * _sigmoid(gp_ref[...]), 0.0)
        hcat = jnp.concatenate([hp, hc], axis=0)
        acc = jnp.zeros((tt, wc), F32) + b_ref[...]
        for k in range(kw):
            acc = acc + w_ref[k:k + 1, :] * _shift_up(hcat, first + k, tt)
        cpre_ref[...] = acc
        mu = jnp.mean(acc, axis=-1, keepdims=True)
        xc = acc - mu
        y = xc * lax.rsqrt(jnp.mean(xc * xc, axis=-1, keepdims=True) + LN_EPS) * lg_ref[...] + lb_ref[...]
        hn_ref[...] = (y * _sigmoid(y)).astype(BF16)

    ca, cg = cfg.off_xa // wc, cfg.off_xg // wc
    hb = tt // CONV_HALO
    lph = cfg.lp // CONV_HALO

    def halo(col):
        return pl.BlockSpec((CONV_HALO, wc), lambda b, c: (jnp.maximum(b * lph + c * hb - 1, 0), col))

    vec = pl.BlockSpec((1, wc), lambda b, c: (0, 0))
    blk = pl.BlockSpec((tt, wc), lambda b, c: (b * nc + c, 0))
    return pl.pallas_call(
        body, name="conv_fwd", grid=(cfg.b_loc, nc),
        in_specs=[pl.BlockSpec((tt, wc), lambda b, c: (b * nc + c, ca)),
                  pl.BlockSpec((tt, wc), lambda b, c: (b * nc + c, cg)), halo(ca), halo(cg),
                  pl.BlockSpec((kw, wc), lambda b, c: (0, 0)), vec, vec, vec],
        out_specs=[blk, blk], out_shape=[_sds((t, wc), F32), _sds((t, wc), BF16)],
        compiler_params=_cp(("parallel", "arbitrary")))(
            proj, proj, proj, proj, w, bias.reshape(1, wc), ln_g.reshape(1, wc), ln_b.reshape(1, wc))


def conv_bwd(cfg, dhn, cpre, proj, w, ln_g, ln_b):
    t = proj.shape[0]
    wc = cfg.d_conv
    tt = _conv_tile(cfg)
    nc = cfg.lp // tt
    kw = cfg.conv_width
    first = CONV_HALO - (kw - 1)

    def body(dhn_ref, dhn_nx_ref, cp_ref, cp_nx_ref, a_ref, g_ref, ap_ref, gp_ref, w_ref, lg_ref, lb_ref,
             dxa_ref, dxg_ref, dw_ref, db_ref, dlg_ref, dlb_ref):
        b, c = pl.program_id(0), pl.program_id(1)
        lg, lb = lg_ref[...], lb_ref[...]

        def ln_silu_bwd(dh, cp):
            mu = jnp.mean(cp, axis=-1, keepdims=True)
            xc = cp - mu
            rstd = lax.rsqrt(jnp.mean(xc * xc, axis=-1, keepdims=True) + LN_EPS)
            xhat = xc * rstd
            y = xhat * lg + lb
            sg = _sigmoid(y)
            dy = dh * sg * (1.0 + y * (1.0 - sg))
            dxh = dy * lg
            dc = rstd * (dxh - jnp.mean(dxh, axis=-1, keepdims=True)
                         - xhat * jnp.mean(dxh * xhat, axis=-1, keepdims=True))
            return dc, dy, xhat

        dc, dy, xhat = ln_silu_bwd(dhn_ref[...], cp_ref[...])
        dc_nx, _, _ = ln_silu_bwd(dhn_nx_ref[...], cp_nx_ref[...])
        dc_nx = jnp.where(c < nc - 1, dc_nx, 0.0)
        dcat = jnp.concatenate([dc, dc_nx], axis=0)
        dhc = jnp.zeros((tt, wc), F32)
        for k in range(kw):
            dhc = dhc + w_ref[k:k + 1, :] * _shift_up(dcat, kw - 1 - k, tt)
        av, gv = a_ref[...], g_ref[...]
        sg = _sigmoid(gv)
        valid = (c * tt + lax.broadcasted_iota(jnp.int32, (tt, 1), 0)) >= cfg.pad
        dhc = jnp.where(valid, dhc, 0.0)
        dxa_ref[...] = (dhc * sg).astype(BF16)
        dxg_ref[...] = (dhc * av * sg * (1.0 - sg)).astype(BF16)

        @pl.when((b == 0) & (c == 0))
        def _():
            dw_ref[...] = jnp.zeros_like(dw_ref)
            db_ref[...] = jnp.zeros_like(db_ref)
            dlg_ref[...] = jnp.zeros_like(dlg_ref)
            dlb_ref[...] = jnp.zeros_like(dlb_ref)

        hp = jnp.where(c > 0, ap_ref[...] * _sigmoid(gp_ref[...]), 0.0)
        hcat = jnp.concatenate([hp, av * sg], axis=0)
        for k in range(kw):
            dw_ref[k:k + 1, :] += jnp.sum(dc * _shift_up(hcat, first + k, tt), axis=0, keepdims=True)
        db_ref[...] += jnp.sum(dc, axis=0, keepdims=True)
        dlg_ref[...] += jnp.sum(dy * xhat, axis=0, keepdims=True)
        dlb_ref[...] += jnp.sum(dy, axis=0, keepdims=True)

    ca, cg = cfg.off_xa // wc, cfg.off_xg // wc
    hb = tt // CONV_HALO
    lph = cfg.lp // CONV_HALO
    last = cfg.b_loc * lph - 1

    def prev(col):
        return pl.BlockSpec((CONV_HALO, wc), lambda b, c: (jnp.maximum(b * lph + c * hb - 1, 0), col))

    nxt = pl.BlockSpec((CONV_HALO, wc), lambda b, c: (jnp.minimum(b * lph + (c + 1) * hb, last), 0))
    vec = pl.BlockSpec((1, wc), lambda b, c: (0, 0))
    blk = pl.BlockSpec((tt, wc), lambda b, c: (b * nc + c, 0))
    wspec = pl.BlockSpec((kw, wc), lambda b, c: (0, 0))
    return pl.pallas_call(
        body, name="conv_bwd", grid=(cfg.b_loc, nc),
        in_specs=[blk, nxt, blk, nxt, pl.BlockSpec((tt, wc), lambda b, c: (b * nc + c, ca)),
                  pl.BlockSpec((tt, wc), lambda b, c: (b * nc + c, cg)), prev(ca), prev(cg), wspec, vec, vec],
        out_specs=[blk, blk, wspec, vec, vec, vec],
        out_shape=[_sds((t, wc), BF16), _sds((t, wc), BF16), _sds((kw, wc), F32), _sds((1, wc), F32),
                   _sds((1, wc), F32), _sds((1, wc), F32)],
        compiler_params=_cp(("arbitrary", "arbitrary")))(
            dhn, dhn, cpre, cpre, proj, proj, proj, proj, w, ln_g.reshape(1, wc), ln_b.reshape(1, wc))


def _split_bf16(x):
    hi = x.astype(BF16)
    return hi, (x - hi.astype(F32)).astype(BF16)


K_CHUNK = 2 * Q_BLOCK
NEG_BIG = -1e30


def _sb_logits(cfg, qh, kblk, col0, row_t):
    z = _dot(qh, kblk, 1, 1)
    col_s = col0 + lax.broadcasted_iota(jnp.int32, (1, K_CHUNK), 1)
    mask = (col_s < row_t) & (col_s >= cfg.pad)
    tneg = jnp.exp(-jnp.abs(z))
    lk = jnp.where(mask, -(jnp.maximum(z, 0.0) + jnp.log(1.0 + tneg)), 0.0)
    return z, mask, tneg, lk


def _tri_sum(x, tri):
    hi, lo = _split_bf16(x)
    return _dot(hi, tri, 1, 0) + _dot(lo, tri, 1, 0)


def _grid_ends(dims):
    ids = [pl.program_id(a) for a in range(len(dims))]
    first = functools.reduce(lambda p, q: p & q, [i == 0 for i in ids])
    last = functools.reduce(lambda p, q: p & q, [i == n - 1 for i, n in zip(ids, dims)])
    return first, last


def attn_fwd(cfg, proj, gather_src=None):
    t = proj.shape[0]
    lp = cfg.lp
    nq = lp // Q_BLOCK
    nhp = cfg.d_attn // V7X_LANES
    cq, ck, cv = cfg.off_q // V7X_LANES, cfg.off_k // V7X_LANES, cfg.off_v // V7X_LANES
    scale = 1.0 / math.sqrt(cfg.head_dim)
    grid = (cfg.b_loc, nhp, nq)

    def body(*refs):
        if gather_src is None:
            q_ref, k_ref, v_ref, o_ref, lt_ref, kbf, vbf = refs
        else:
            q_ref, k_ref, v_ref, src_ref, o_ref, lt_ref, dst_ref, kbf, vbf = refs[:9]
            exchange = (src_ref, dst_ref) + tuple(refs[9:])
            first_step, last_step = _grid_ends(grid)
            pl.when(first_step)(lambda: gather_start(*exchange))
        qb = pl.program_id(2)

        @pl.when(qb == 0)
        def _():
            kbf[pl.ds(0, lp), :] = k_ref[...].astype(BF16)
            vbf[pl.ds(0, lp), :] = v_ref[...].astype(BF16)
            kbf[pl.ds(lp, Q_BLOCK), :] = jnp.zeros((Q_BLOCK, V7X_LANES), BF16)
            vbf[pl.ds(lp, Q_BLOCK), :] = jnp.zeros((Q_BLOCK, V7X_LANES), BF16)

        q = q_ref[...] * scale
        lane = lax.broadcasted_iota(jnp.int32, (1, V7X_LANES), 1)
        head0 = lane < cfg.head_dim
        qs = jnp.concatenate([jnp.where(head0, q, 0.0), jnp.where(head0, 0.0, q)], axis=0).astype(BF16)
        rows2 = lax.broadcasted_iota(jnp.int32, (2 * Q_BLOCK, 1), 0)
        row_t = qb * Q_BLOCK + jnp.where(rows2 >= Q_BLOCK, rows2 - Q_BLOCK, rows2)
        ri = lax.broadcasted_iota(jnp.int32, (K_CHUNK, K_CHUNK), 0)
        ci = lax.broadcasted_iota(jnp.int32, (K_CHUNK, K_CHUNK), 1)
        tri_from = (ri >= ci).astype(BF16)
        last = qb // 2

        def step(i, carry):
            z_p, lk_p, pre_pp, rows_pp, acc, r_carry = carry
            c3 = jnp.clip(last - i + 2, 0, last)
            off3 = pl.multiple_of(c3 * K_CHUNK, K_CHUNK)
            w = jnp.exp(pre_pp + r_carry)
            acc = acc + _dot(w.astype(BF16), vbf[pl.ds(off3, K_CHUNK), :], 1, 0)
            r_carry = r_carry + rows_pp
            pre_p = z_p + _tri_sum(lk_p, tri_from)
            rows_p = jnp.sum(lk_p, axis=1, keepdims=True)
            c1 = jnp.maximum(last - i, 0)
            off1 = pl.multiple_of(c1 * K_CHUNK, K_CHUNK)
            col0 = jnp.where(i <= last, c1 * K_CHUNK, cfg.lp + K_CHUNK)
            z, mask, _, lk = _sb_logits(cfg, qs, kbf[pl.ds(off1, K_CHUNK), :], col0, row_t)
            return jnp.where(mask, z, NEG_BIG), lk, pre_p, rows_p, acc, r_carry

        zero = jnp.where(ri + ci < 0, 1.0, 0.0)
        big = zero + NEG_BIG
        zcol = jnp.where(rows2 < 0, 1.0, 0.0)
        res = lax.fori_loop(0, last + 3, step, (big, zero, big, zcol, zcol + jnp.where(lane < 0, 1.0, 0.0), zcol))
        acc, r_tot = res[4], res[5]
        o_ref[...] = jnp.where(head0, acc[:Q_BLOCK], acc[Q_BLOCK:])
        lt_ref[...] = (jnp.where(lane == 0, r_tot[:Q_BLOCK], 0.0)
                       + jnp.where(lane == cfg.head_dim, r_tot[Q_BLOCK:], 0.0))
        if gather_src is not None:
            pl.when(last_step)(lambda: gather_finish(*exchange))

    oblk = pl.BlockSpec((Q_BLOCK, V7X_LANES), lambda b, hp, qb: (b * nq + qb, hp))
    hbm = pl.BlockSpec(memory_space=pl.ANY)
    comm = gather_src is not None
    return pl.pallas_call(
        body, name="attn_fwd_gather" if comm else "attn_fwd", grid=grid,
        in_specs=[pl.BlockSpec((Q_BLOCK, V7X_LANES), lambda b, hp, qb: (b * nq + qb, cq + hp)),
                  pl.BlockSpec((lp, V7X_LANES), lambda b, hp, qb: (b, ck + hp)),
                  pl.BlockSpec((lp, V7X_LANES), lambda b, hp, qb: (b, cv + hp))] + ([hbm] if comm else []),
        out_specs=[oblk, oblk] + ([hbm] if comm else []),
        out_shape=[_sds((t, cfg.d_attn), F32), _sds((t, cfg.d_attn), F32)]
        + ([_sds((8,) + gather_src.shape, gather_src.dtype)] if comm else []),
        scratch_shapes=[pltpu.VMEM((lp + Q_BLOCK, V7X_LANES), BF16), pltpu.VMEM((lp + Q_BLOCK, V7X_LANES), BF16)]
        + (exchange_semaphores() if comm else []),
        compiler_params=_cp(("arbitrary", "arbitrary", "arbitrary")))(*((proj, proj, proj) + ((gather_src,) if comm else ())))


def attn_bwd(cfg, proj, lt, do, send=None):
    t = proj.shape[0]
    lp = cfg.lp
    nq = lp // Q_BLOCK
    nhp = cfg.d_attn // V7X_LANES
    cq, ck, cv = cfg.off_q // V7X_LANES, cfg.off_k // V7X_LANES, cfg.off_v // V7X_LANES
    scale = 1.0 / math.sqrt(cfg.head_dim)
    grid = (cfg.b_loc, nhp, nq)

    def body(*refs):
        if send is None:
            q_ref, k_ref, v_ref, lt_ref, do_ref, dq_ref, dk_ref, dv_ref, kbf, vbf, dk_acc, dv_acc = refs
        else:
            (q_ref, k_ref, v_ref, lt_ref, do_ref, src_ref, dq_ref, dk_ref, dv_ref, dst_ref,
             kbf, vbf, dk_acc, dv_acc) = refs[:14]
            exchange = (src_ref, dst_ref) + tuple(refs[14:])
            first_step, last_step = _grid_ends(grid)
            pl.when(first_step)(lambda: all_to_all_start(*exchange))
        qb = pl.program_id(2)

        @pl.when(qb == 0)
        def _():
            kbf[pl.ds(0, lp), :] = k_ref[...].astype(BF16)
            vbf[pl.ds(0, lp), :] = v_ref[...].astype(BF16)
            kbf[pl.ds(lp, Q_BLOCK), :] = jnp.zeros((Q_BLOCK, V7X_LANES), BF16)
            vbf[pl.ds(lp, Q_BLOCK), :] = jnp.zeros((Q_BLOCK, V7X_LANES), BF16)
            dk_acc[...] = jnp.zeros_like(dk_acc)
            dv_acc[...] = jnp.zeros_like(dv_acc)

        q = q_ref[...] * scale
        ltv = lt_ref[...]
        dov = do_ref[...]
        lane = lax.broadcasted_iota(jnp.int32, (1, V7X_LANES), 1)
        head0 = lane < cfg.head_dim
        qs = jnp.concatenate([jnp.where(head0, q, 0.0), jnp.where(head0, 0.0, q)], axis=0).astype(BF16)
        dos = jnp.concatenate([jnp.where(head0, dov, 0.0), jnp.where(head0, 0.0, dov)], axis=0).astype(BF16)
        lk_tot = jnp.concatenate(
            [jnp.sum(jnp.where(lane == 0, ltv, 0.0), axis=1, keepdims=True),
             jnp.sum(jnp.where(lane == cfg.head_dim, ltv, 0.0), axis=1, keepdims=True)], axis=0)
        rows2 = lax.broadcasted_iota(jnp.int32, (2 * Q_BLOCK, 1), 0)
        row_t = qb * Q_BLOCK + jnp.where(rows2 >= Q_BLOCK, rows2 - Q_BLOCK, rows2)
        ri = lax.broadcasted_iota(jnp.int32, (K_CHUNK, K_CHUNK), 0)
        ci = lax.broadcasted_iota(jnp.int32, (K_CHUNK, K_CHUNK), 1)
        tri_before = (ri < ci).astype(BF16)
        tri_upto = (ri <= ci).astype(BF16)
        n_chunks = qb // 2 + 1

        def step(i, carry):
            e_p, w_p, sg_p, dq_acc, lk_before, e_before = carry
            c2 = jnp.clip(i - 1, 0, n_chunks - 1)
            off2 = pl.multiple_of(c2 * K_CHUNK, K_CHUNK)
            dz = (e_p - sg_p * (e_before + _tri_sum(e_p, tri_upto))).astype(BF16)
            dq_acc = dq_acc + _dot(dz, kbf[pl.ds(off2, K_CHUNK), :], 1, 0)
            dk_acc[pl.ds(off2, K_CHUNK), :] += _dot(dz, qs, 0, 0)
            dv_acc[pl.ds(off2, K_CHUNK), :] += _dot(w_p, dos, 0, 0)
            e_before = e_before + jnp.sum(e_p, axis=1, keepdims=True)

            c1 = jnp.minimum(i, n_chunks - 1)
            off1 = pl.multiple_of(c1 * K_CHUNK, K_CHUNK)
            col0 = jnp.where(i < n_chunks, c1 * K_CHUNK, cfg.lp + K_CHUNK)
            z, mask, tneg, lk = _sb_logits(cfg, qs, kbf[pl.ds(off1, K_CHUNK), :], col0, row_t)
            dw = _dot(dos, vbf[pl.ds(off1, K_CHUNK), :], 1, 1)
            w = jnp.exp(jnp.where(mask, z, NEG_BIG) + (lk_tot - lk_before - _tri_sum(lk, tri_before)))
            sg = jnp.where(mask, jnp.where(z >= 0.0, 1.0, tneg) / (1.0 + tneg), 0.0)
            lk_before = lk_before + jnp.sum(lk, axis=1, keepdims=True)
            return w * dw, w.astype(BF16), sg, dq_acc, lk_before, e_before

        zero = jnp.where(ri + ci < 0, 1.0, 0.0)
        zcol = jnp.where(rows2 < 0, 1.0, 0.0)
        res = lax.fori_loop(0, n_chunks + 1, step,
                            (zero, zero.astype(BF16), zero, zcol + jnp.where(lane < 0, 1.0, 0.0), zcol, zcol))
        dq = res[3]
        dq_ref[...] = jnp.where(head0, dq[:Q_BLOCK], dq[Q_BLOCK:]) * scale

        @pl.when(qb == nq - 1)
        def _():
            dk_ref[...] = dk_acc[pl.ds(0, lp), :]
            dv_ref[...] = dv_acc[pl.ds(0, lp), :]

        if send is not None:
            pl.when(last_step)(lambda: all_to_all_finish(*exchange))

    qblk = pl.BlockSpec((Q_BLOCK, V7X_LANES), lambda b, hp, qb: (b * nq + qb, hp))
    seq = pl.BlockSpec((lp, V7X_LANES), lambda b, hp, qb: (b, hp))
    hbm = pl.BlockSpec(memory_space=pl.ANY)
    out = _sds((t, cfg.d_attn), F32)
    comm = send is not None
    return pl.pallas_call(
        body, name="attn_bwd_exchange" if comm else "attn_bwd", grid=grid,
        in_specs=[pl.BlockSpec((Q_BLOCK, V7X_LANES), lambda b, hp, qb: (b * nq + qb, cq + hp)),
                  pl.BlockSpec((lp, V7X_LANES), lambda b, hp, qb: (b, ck + hp)),
                  pl.BlockSpec((lp, V7X_LANES), lambda b, hp, qb: (b, cv + hp)), qblk, qblk] + ([hbm] if comm else []),
        out_specs=[qblk, seq, seq] + ([hbm] if comm else []),
        out_shape=[out, out, out] + ([_sds(send.shape, send.dtype)] if comm else []),
        scratch_shapes=[pltpu.VMEM((lp + Q_BLOCK, V7X_LANES), BF16), pltpu.VMEM((lp + Q_BLOCK, V7X_LANES), BF16),
                        pltpu.VMEM((lp + Q_BLOCK, V7X_LANES), F32), pltpu.VMEM((lp + Q_BLOCK, V7X_LANES), F32)]
        + (exchange_semaphores() if comm else []),
        compiler_params=_cp(("arbitrary", "arbitrary", "arbitrary")))(
            *((proj, proj, proj, lt, do) + ((send,) if comm else ())))


N_POW = 8


def _discretize(lr, li, logdt, br_t, bi_t):
    dt = jnp.exp(logdt)
    mag = jnp.exp(lr * dt)
    ab_re = mag * jnp.cos(li * dt)
    ab_im = mag * jnp.sin(li * dt)
    den = lr * lr + li * li
    nr = ab_re - 1.0
    ni = ab_im
    coef_re = (nr * lr + ni * li) / den
    coef_im = (ni * lr - nr * li) / den
    bb_re = coef_re[None] * br_t - coef_im[None] * bi_t
    bb_im = coef_re[None] * bi_t + coef_im[None] * br_t
    return ab_re, ab_im, bb_re, bb_im


def ssm_param_fwd(cfg, lr, li, logdt, br_t, bi_t):
    g, n, c = cfg.groups, cfg.ssm_state, cfg.ssm_group

    def body(lr_ref, li_ref, dt_ref, br_ref, bi_ref, ar_ref, ai_ref, bbr_ref, bbi_ref, pr_ref, pi_ref,
             tr_ref, ti_ref):
        ab_re, ab_im, bb_re, bb_im = _discretize(lr_ref[...], li_ref[...], dt_ref[...], br_ref[...], bi_ref[...])
        ar_ref[...] = ab_re
        ai_ref[...] = ab_im
        bbr_ref[...] = bb_re
        bbi_ref[...] = bb_im
        pr, pi = ab_re, ab_im
        for k in range(N_POW):
            pr_ref[k] = pr
            pi_ref[k] = pi
            pr, pi = pr * pr - pi * pi, 2.0 * pr * pi
        cr, ci = ab_re, ab_im
        for j in range(SSM_CHUNK):
            tr_ref[j] = cr
            ti_ref[j] = ci
            cr, ci = cr * ab_re - ci * ab_im, cr * ab_im + ci * ab_re

    gn, cgn = _sds((g, n), F32), _sds((c, g, n), F32)
    return pl.pallas_call(
        body, name="ssm_param_fwd",
        out_shape=[gn, gn, cgn, cgn, _sds((N_POW, g, n), F32), _sds((N_POW, g, n), F32),
                   _sds((SSM_CHUNK, g, n), F32), _sds((SSM_CHUNK, g, n), F32)])(lr, li, logdt, br_t, bi_t)


def ssm_param_bwd(cfg, lr, li, logdt, br_t, bi_t, dar, dai, dbbr, dbbi):
    g, n, c = cfg.groups, cfg.ssm_state, cfg.ssm_group

    def body(lr_ref, li_ref, dt_ref, br_ref, bi_ref, dar_ref, dai_ref, dbr_ref, dbi_ref,
             olr_ref, oli_ref, odt_ref, obr_ref, obi_ref):
        _, vjp = jax.vjp(_discretize, lr_ref[...], li_ref[...], dt_ref[...], br_ref[...], bi_ref[...])
        dlr, dli, ddt, dbr, dbi = vjp((dar_ref[...], dai_ref[...], dbr_ref[...], dbi_ref[...]))
        olr_ref[...] = dlr
        oli_ref[...] = dli
        odt_ref[...] = ddt
        obr_ref[...] = dbr
        obi_ref[...] = dbi

    gn, cgn = _sds((g, n), F32), _sds((c, g, n), F32)
    return pl.pallas_call(body, name="ssm_param_bwd", out_shape=[gn, gn, _sds((g, 1), F32), cgn, cgn])(
        lr, li, logdt, br_t, bi_t, dar, dai, dbbr, dbbi)


def _scan(xr, xi, pr_ref, pi_ref, reverse):
    ct = xr.shape[0]
    rows = lax.broadcasted_iota(jnp.int32, (ct, 1), 0)
    d, k = 1, 0
    while d < ct:
        if reverse:
            sr, si = pltpu.roll(xr, ct - d, 0), pltpu.roll(xi, ct - d, 0)
            keep = rows < ct - d
        else:
            sr, si = pltpu.roll(xr, d, 0), pltpu.roll(xi, d, 0)
            keep = rows >= d
        sr = jnp.where(keep, sr, 0.0)
        si = jnp.where(keep, si, 0.0)
        ar = pr_ref[0, k:k + 1, :]
        ai = -pi_ref[0, k:k + 1, :] if reverse else pi_ref[0, k:k + 1, :]
        xr, xi = xr + ar * sr - ai * si, xi + ar * si + ai * sr
        d *= 2
        k += 1
    return xr, xi


def _gelu(x):
    k = math.sqrt(2.0 / math.pi)
    return 0.5 * x * (1.0 + jnp.tanh(k * (x + 0.044715 * x * x * x)))


def _gelu_grad(x):
    k = math.sqrt(2.0 / math.pi)
    th = jnp.tanh(k * (x + 0.044715 * x * x * x))
    return 0.5 * (1.0 + th) + 0.5 * x * (1.0 - th * th) * k * (1.0 + 3.0 * 0.044715 * x * x)


def ssm_fwd(cfg, proj, bb_re, bb_im, ct_re, ct_im, pw_re, pw_im, tab_re, tab_im, dskip):
    t = proj.shape[0]
    nsb = cfg.d_ssm // SSM_SB
    ns = SSM_SB // cfg.ssm_group * cfg.ssm_state
    ct = SSM_CHUNK
    nc = cfg.lp // ct

    def body(u_ref, bbr_ref, bbi_ref, ctr_ref, cti_ref, pr_ref, pi_ref, tr_ref, ti_ref, d_ref,
             sr_ref, si_ref, yt_ref, y2_ref, cr_scr, ci_scr):
        c = pl.program_id(2)

        @pl.when(c == 0)
        def _():
            cr_scr[...] = jnp.zeros_like(cr_scr)
            ci_scr[...] = jnp.zeros_like(ci_scr)

        u = u_ref[...]
        ub = u.astype(BF16)
        xr, xi = _scan(_dot(ub, bbr_ref[0], 1, 0), _dot(ub, bbi_ref[0], 1, 0), pr_ref, pi_ref, False)
        cr, ci = cr_scr[0:1, :], ci_scr[0:1, :]
        tr, ti = tr_ref[0], ti_ref[0]
        sr = xr + tr * cr - ti * ci
        si = xi + tr * ci + ti * cr
        sr_ref[...] = sr
        si_ref[...] = si
        cr_scr[0:1, :] = sr_ref[ct - 1:ct, :]
        ci_scr[0:1, :] = si_ref[ct - 1:ct, :]
        y = _dot(sr.astype(BF16), ctr_ref[0], 1, 0) - _dot(si.astype(BF16), cti_ref[0], 1, 0) + d_ref[...] * u
        yt_ref[...] = y
        y2_ref[...] = _gelu(y).astype(BF16)

    def mat(r, c_):
        return pl.BlockSpec((1, r, c_), lambda sb, b, c: (sb, 0, 0))

    ublk = pl.BlockSpec((ct, SSM_SB), lambda sb, b, c: (b * nc + c, sb))
    sblk = pl.BlockSpec((ct, ns), lambda sb, b, c: (b * nc + c, sb))
    return pl.pallas_call(
        body, name="ssm_fwd", grid=(nsb, cfg.b_loc, nc),
        in_specs=[ublk, mat(SSM_SB, ns), mat(SSM_SB, ns), mat(ns, SSM_SB), mat(ns, SSM_SB), mat(N_POW, ns),
                  mat(N_POW, ns), mat(ct, ns), mat(ct, ns), pl.BlockSpec((1, SSM_SB), lambda sb, b, c: (0, sb))],
        out_specs=[sblk, sblk, ublk, ublk],
        out_shape=[_sds((t, cfg.n_state), F32), _sds((t, cfg.n_state), F32), _sds((t, cfg.d_ssm), F32),
                   _sds((t, cfg.d_ssm), BF16)],
        scratch_shapes=[pltpu.VMEM((8, ns), F32), pltpu.VMEM((8, ns), F32)],
        compiler_params=_cp(("parallel", "arbitrary", "arbitrary")))(
            proj, bb_re, bb_im, ct_re, ct_im, pw_re, pw_im, tab_re, tab_im, dskip.reshape(1, cfg.d_ssm))


def ssm_bwd(cfg, dy2, ytot, proj, s_re, s_im, cf_re, cf_im, bbt_re, bbt_im, pw_re, pw_im, tabr_re, tabr_im, dskip):
    t = proj.shape[0]
    nsb = cfg.d_ssm // SSM_SB
    ns = SSM_SB // cfg.ssm_group * cfg.ssm_state
    ct = SSM_CHUNK
    nc = cfg.lp // ct

    def body(dy_ref, yt_ref, u_ref, sr_ref, si_ref, spr_ref, spi_ref, cfr_ref, cfi_ref, btr_ref, bti_ref,
             pr_ref, pi_ref, tr_ref, ti_ref, d_ref,
             du_ref, dctr_ref, dcti_ref, dbbr_ref, dbbi_ref, dar_ref, dai_ref, dd_ref,
             cr_scr, ci_scr, ar_scr, ai_scr):
        b, c = pl.program_id(1), pl.program_id(2)
        chunk = nc - 1 - c

        @pl.when((b == 0) & (c == 0))
        def _():
            for r in (dctr_ref, dcti_ref, dbbr_ref, dbbi_ref, dar_ref, dai_ref, dd_ref):
                r[...] = jnp.zeros_like(r)

        @pl.when(c == 0)
        def _():
            cr_scr[...] = jnp.zeros_like(cr_scr)
            ci_scr[...] = jnp.zeros_like(ci_scr)

        u = u_ref[...]
        dyt = dy_ref[...] * _gelu_grad(yt_ref[...])
        dytb = dyt.astype(BF16)
        xr, xi = _scan(_dot(dytb, cfr_ref[0], 1, 0), -_dot(dytb, cfi_ref[0], 1, 0), pr_ref, pi_ref, True)
        cr, ci = cr_scr[0:1, :], ci_scr[0:1, :]
        tr, ti = tr_ref[0], -ti_ref[0]
        a_re = xr + tr * cr - ti * ci
        a_im = xi + tr * ci + ti * cr
        ar_scr[...] = a_re
        ai_scr[...] = a_im
        cr_scr[0:1, :] = ar_scr[0:1, :]
        ci_scr[0:1, :] = ai_scr[0:1, :]

        sr, si = sr_ref[...], si_ref[...]
        rows = lax.broadcasted_iota(jnp.int32, (ct, 1), 0)
        keep_prev = chunk > 0
        pr_last = jnp.where(keep_prev, spr_ref[7:8, :], 0.0)
        pi_last = jnp.where(keep_prev, spi_ref[7:8, :], 0.0)
        sp_re = jnp.where(rows == 0, pr_last, pltpu.roll(sr, 1, 0))
        sp_im = jnp.where(rows == 0, pi_last, pltpu.roll(si, 1, 0))
        dar_ref[0] += jnp.sum(a_re * sp_re + a_im * sp_im, axis=0, keepdims=True)
        dai_ref[0] += jnp.sum(a_im * sp_re - a_re * sp_im, axis=0, keepdims=True)
        dctr_ref[0] += _dot(sr.astype(BF16), dytb, 0, 0)
        dcti_ref[0] += -_dot(si.astype(BF16), dytb, 0, 0)
        ub = u.astype(BF16)
        arb, aib = a_re.astype(BF16), a_im.astype(BF16)
        dbbr_ref[0] += _dot(ub, arb, 0, 0)
        dbbi_ref[0] += _dot(ub, aib, 0, 0)
        du = dyt * d_ref[...] + _dot(arb, btr_ref[0], 1, 0) + _dot(aib, bti_ref[0], 1, 0)
        valid = (chunk * ct + rows) >= cfg.pad
        du_ref[...] = jnp.where(valid, du, 0.0).astype(BF16)
        dd_ref[...] += jnp.sum(dyt * u, axis=0, keepdims=True)

    def mat(r, c_):
        return pl.BlockSpec((1, r, c_), lambda sb, b, c: (sb, 0, 0))

    lp8 = cfg.lp // 8
    ublk = pl.BlockSpec((ct, SSM_SB), lambda sb, b, c: (b * nc + nc - 1 - c, sb))
    sblk = pl.BlockSpec((ct, ns), lambda sb, b, c: (b * nc + nc - 1 - c, sb))
    sprev = pl.BlockSpec((8, ns), lambda sb, b, c: (jnp.maximum(b * lp8 + (nc - 1 - c) * (ct // 8) - 1, 0), sb))
    dvec = pl.BlockSpec((1, SSM_SB), lambda sb, b, c: (0, sb))
    return pl.pallas_call(
        body, name="ssm_bwd", grid=(nsb, cfg.b_loc, nc),
        in_specs=[ublk, ublk, ublk, sblk, sblk, sprev, sprev, mat(SSM_SB, ns), mat(SSM_SB, ns), mat(ns, SSM_SB),
                  mat(ns, SSM_SB), mat(N_POW, ns), mat(N_POW, ns), mat(ct, ns), mat(ct, ns), dvec],
        out_specs=[ublk, mat(ns, SSM_SB), mat(ns, SSM_SB), mat(SSM_SB, ns), mat(SSM_SB, ns), mat(1, ns), mat(1, ns),
                   dvec],
        out_shape=[_sds((t, cfg.d_ssm), BF16), _sds((nsb, ns, SSM_SB), F32), _sds((nsb, ns, SSM_SB), F32),
                   _sds((nsb, SSM_SB, ns), F32), _sds((nsb, SSM_SB, ns), F32), _sds((nsb, 1, ns), F32),
                   _sds((nsb, 1, ns), F32), _sds((1, cfg.d_ssm), F32)],
        scratch_shapes=[pltpu.VMEM((8, ns), F32), pltpu.VMEM((8, ns), F32), pltpu.VMEM((ct, ns), F32),
                        pltpu.VMEM((ct, ns), F32)],
        compiler_params=_cp(("arbitrary", "arbitrary", "arbitrary")))(
            dy2, ytot, proj, s_re, s_im, s_re, s_im, cf_re, cf_im, bbt_re, bbt_im, pw_re, pw_im, tabr_re, tabr_im,
            dskip.reshape(1, cfg.d_ssm))


def _ssm_tables(cfg, lam_re, lam_im, log_dt, b_re, b_im, c_re, c_im):
    g, n, c = cfg.groups, cfg.ssm_state, cfg.ssm_group
    gsb = SSM_SB // c
    nsb = cfg.d_ssm // SSM_SB
    ns = gsb * n
    br_t, bi_t = jnp.transpose(b_re, (2, 0, 1)), jnp.transpose(b_im, (2, 0, 1))
    prm = (lam_re, lam_im, log_dt.reshape(g, 1), br_t, bi_t)
    _, _, bbr, bbi, pr, pi, tr, ti = ssm_param_fwd(cfg, *prm)
    eye = jnp.eye(gsb, dtype=F32)

    def bdiag_b(x):
        return jnp.einsum('csgn,gh->sgchn', x.reshape(c, nsb, gsb, n), eye).reshape(nsb, SSM_SB, ns)

    def bdiag_c(x):
        return jnp.einsum('sgcn,gh->sgchn', x.reshape(nsb, gsb, c, n), eye).reshape(nsb, SSM_SB, ns)

    def lanes(x):
        return jnp.transpose(x.reshape(x.shape[0], nsb, ns), (1, 0, 2))

    bb_re, bb_im = bdiag_b(bbr), bdiag_b(bbi)
    cf_re, cf_im = bdiag_c(c_re), bdiag_c(c_im)
    sw = lambda x: jnp.swapaxes(x, 1, 2)
    return dict(prm=prm, bb_re=bb_re.astype(BF16), bb_im=bb_im.astype(BF16), bbt_re=sw(bb_re).astype(BF16),
                bbt_im=sw(bb_im).astype(BF16), cf_re=cf_re.astype(BF16), cf_im=cf_im.astype(BF16),
                ct_re=sw(cf_re).astype(BF16), ct_im=sw(cf_im).astype(BF16), pw_re=lanes(pr), pw_im=lanes(pi),
                tab_re=lanes(tr), tab_im=lanes(ti), tabr_re=lanes(tr[::-1]), tabr_im=lanes(ti[::-1]))


def _ssm_param_grads(cfg, tabs, dct_re, dct_im, dbb_re, dbb_im, dab_re, dab_im):
    g, n, c = cfg.groups, cfg.ssm_state, cfg.ssm_group
    gsb = SSM_SB // c
    nsb = cfg.d_ssm // SSM_SB

    def diag_b(x):
        d = jnp.einsum('sgcgn->sgcn', x.reshape(nsb, gsb, c, gsb, n))
        return jnp.transpose(d.reshape(g, c, n), (1, 0, 2))

    def diag_c(x):
        d = jnp.einsum('sgngc->sgcn', x.reshape(nsb, gsb, n, gsb, c))
        return d.reshape(g, c, n)

    dlr, dli, ddt, dbr_t, dbi_t = ssm_param_bwd(cfg, *tabs['prm'], dab_re.reshape(g, n), dab_im.reshape(g, n),
                                                diag_b(dbb_re), diag_b(dbb_im))
    return dict(ssm_lam_re=dlr, ssm_lam_im=dli, ssm_log_dt=ddt.reshape(g),
                ssm_b_re=jnp.transpose(dbr_t, (1, 2, 0)), ssm_b_im=jnp.transpose(dbi_t, (1, 2, 0)),
                ssm_c_re=diag_c(dct_re), ssm_c_im=diag_c(dct_im))


def _mm_tiles(t, n, k):
    tm = _tile(t, 1056 if k <= 3072 else 528, 16)
    tn = _tile(n, 512, 128)
    return tm, tn


def _nt(a, wt, name, out_dtype=F32):
    tm, tn = _mm_tiles(a.shape[0], wt.shape[0], a.shape[1])
    return mm([(a, wt, 0)], nt=True, n=wt.shape[0], tm=tm, tn=tn, out_dtype=out_dtype, name=name)


def _nn(pairs, n, name, res=None, res_scale=1.0, out_dtype=F32):
    kmax = max(p[0].shape[1] for p in pairs) * len(pairs)
    tm, tn = _mm_tiles(pairs[0][0].shape[0], n, kmax)
    return mm(pairs, nt=False, n=n, tm=tm, tn=tn, out_dtype=out_dtype, name=name, res=res, res_scale=res_scale)


def local_step(cfg, x, target, rep, ex):
    d, lp, bsz, t = cfg.d_model, cfg.lp, cfg.b_loc, cfg.tokens
    meta = jnp.broadcast_to(ex.meta_tokens[None], (bsz, cfg.n_meta, d))
    h = jnp.concatenate([jnp.zeros((bsz, cfg.pad, d), F32), meta, x], axis=1).reshape(t, d)

    saved = []
    for l in range(cfg.depth):
        w = ex.weights(l)
        r = {k: v[l] for k, v in rep.items() if k != 'final_norm'}
        s = {'h0': h}
        s['xn1'] = rms_fwd(h, r['ffn1_norm'], "rms_fwd")
        s['a1'], s['b1'], s['act1'] = dual_mm_act(s['xn1'], w['w13t_1'], kind='swiglu', name="ffn_up")
        h = _nn([(s['act1'], w['w2_1'], 0)], d, "ffn_down", res=h, res_scale=0.5)
        s['h1'] = h
        s['xnm'] = rms_fwd(h, r['mix_norm'], "rms_fwd")
        proj = _nt(s['xnm'], w['w_int'], "proj_in")
        s['proj'] = proj
        tabs = _ssm_tables(cfg, r['ssm_lam_re'], r['ssm_lam_im'], r['ssm_log_dt'], r['ssm_b_re'], r['ssm_b_im'],
                           r['ssm_c_re'], r['ssm_c_im'])
        s['tabs'] = tabs
        s['s_re'], s['s_im'], s['ytot'], s['y2'] = ssm_fwd(
            cfg, proj, tabs['bb_re'], tabs['bb_im'], tabs['ct_re'], tabs['ct_im'], tabs['pw_re'], tabs['pw_im'],
            tabs['tab_re'], tabs['tab_im'], r['ssm_d'])
        s['ga'], s['gg'], o_ssm = dual_mm_act(s['y2'], w['w_glut'], kind='glu', name="ssm_glu")
        s['cpre'], s['hn'] = conv_fwd(cfg, proj, w['conv_w'], r['conv_b'], r['conv_ln_g'], r['conv_ln_b'])
        s['o_conv'] = _nt(s['hn'], w['conv_w_outt'], "branch_out")
        src = ex.gather_src(l)
        s['o'], s['lt'], *arrived = attn_fwd(cfg, proj, src)
        if src is not None:
            ex.gathered(l, arrived[0])
        s['o_attn'] = _nt(s['o'], w['attn_w_ot'], "branch_out")
        s['merged'] = merge_fwd(cfg, proj, o_ssm, s['o_conv'], s['o_attn'])
        h = _nn([(s['merged'], w['w_out'], 0)], d, "mix_out", res=h, res_scale=1.0)
        s['h2'] = h
        s['xn2'] = rms_fwd(h, r['ffn2_norm'], "rms_fwd")
        s['a2'], s['b2'], s['act2'] = dual_mm_act(s['xn2'], w['w13t_2'], kind='swiglu', name="ffn_up")
        h = _nn([(s['act2'], w['w2_2'], 0)], d, "ffn_down", res=h, res_scale=0.5)
        saved.append((w, r, s))

    dh3, dgf, loss_sq = final_fwd_bwd(cfg, h.reshape(bsz, lp, d), rep['final_norm'], target)
    dh = dh3.reshape(t, d)
    loss = 0.5 * loss_sq[0, 0] / d

    gl = []
    for l in reversed(range(cfg.depth)):
        w, r, s = saved[l]
        g = {}

        def ffn_bwd(dh, tag, a, b, act, xn, hin, norm):
            da, db = ffn_down_bwd(dh, w['w2_' + tag], a, b, "ffn_down_bwd")
            g['w2_' + tag] = mm_tn(act, dh, name="ffn_dw2", scale=0.5)
            dxn = _nn([(da, w['w13t_' + tag], 0), (db, w['w13t_' + tag], 1)], d, "ffn_dxn")
            g['w13t_' + tag] = jnp.concatenate([mm_tn(da, xn, name="ffn_dw13"), mm_tn(db, xn, name="ffn_dw13")], 0)
            return rms_bwd(dxn, hin, norm, dh, "rms_bwd")

        dh, dg2 = ffn_bwd(dh, '2', s['a2'], s['b2'], s['act2'], s['xn2'], s['h2'], r['ffn2_norm'])
        g['ffn2_norm'] = dg2.reshape(d)

        proj, tabs = s['proj'], s['tabs']
        dmerged = _nt(dh, w['w_out'], "mix_dmerged")
        g['w_out'] = mm_tn(s['merged'], dh, name="mix_dwout")
        dg0, dg1, dg2_, dga, dgg, do_conv, do_attn = merge_bwd(cfg, dmerged, proj, s['ga'], s['gg'], s['o_conv'],
                                                               s['o_attn'])
        dy2 = _nn([(dga, w['w_glut'], 0), (dgg, w['w_glut'], 1)], cfg.d_ssm, "ssm_dy2")
        g['w_glut'] = jnp.concatenate([mm_tn(dga, s['y2'], name="ssm_dwglu"), mm_tn(dgg, s['y2'], name="ssm_dwglu")], 0)
        du, dct_re, dct_im, dbb_re, dbb_im, dab_re, dab_im, dd = ssm_bwd(
            cfg, dy2, s['ytot'], proj, s['s_re'], s['s_im'], tabs['cf_re'], tabs['cf_im'], tabs['bbt_re'],
            tabs['bbt_im'], tabs['pw_re'], tabs['pw_im'], tabs['tabr_re'], tabs['tabr_im'], r['ssm_d'])
        g.update(_ssm_param_grads(cfg, tabs, dct_re, dct_im, dbb_re, dbb_im, dab_re, dab_im))
        g['ssm_d'] = dd.reshape(cfg.d_ssm)

        dhn = _nn([(do_conv, w['conv_w_outt'], 0)], cfg.d_conv, "branch_din")
        g['conv_w_outt'] = mm_tn(do_conv, s['hn'], name="branch_dw")
        dxa, dxg, dcw, dcb, dlg, dlb = conv_bwd(cfg, dhn, s['cpre'], proj, w['conv_w'], r['conv_ln_g'],
                                                r['conv_ln_b'])
        g['conv_w'], g['conv_b'] = dcw, dcb.reshape(cfg.d_conv)
        g['conv_ln_g'], g['conv_ln_b'] = dlg.reshape(cfg.d_conv), dlb.reshape(cfg.d_conv)

        do = _nn([(do_attn, w['attn_w_ot'], 0)], cfg.d_attn, "branch_din")
        g['attn_w_ot'] = mm_tn(do_attn, s['o'], name="branch_dw")
        send = ex.send_src(l)
        dq, dk, dv, *arrived = attn_bwd(cfg, proj, s['lt'], do, send)
        if send is not None:
            ex.received(l, arrived[0])

        dproj = jnp.concatenate([du, dxa, dxg, dq.astype(BF16), dk.astype(BF16), dv.astype(BF16), dg0, dg1, dg2_],
                                axis=1)
        dxn = _nn([(dproj, w['w_int'], 0)], d, "proj_dxn")
        g['w_int'] = mm_tn(dproj, s['xnm'], name="proj_dw")
        dh, dgm = rms_bwd(dxn, s['h1'], r['mix_norm'], dh, "rms_bwd")
        g['mix_norm'] = dgm.reshape(d)

        dh, dg1n = ffn_bwd(dh, '1', s['a1'], s['b1'], s['act1'], s['xn1'], s['h0'], r['ffn1_norm'])
        g['ffn1_norm'] = dg1n.reshape(d)
        ex.matrix_grads(l, {name: g.pop(name) for name, _, _ in PACKED})
        gl.append(g)

    gl = gl[::-1]
    dh0 = dh.reshape(bsz, lp, d)
    grad_x = dh0[:, Q_BLOCK:]
    grads = {k: jnp.stack([g[k] for g in gl]) for k in gl[0]}
    grads['meta_tokens'] = jnp.sum(dh0[:, cfg.pad:Q_BLOCK], axis=0)
    grads['final_norm'] = dgf.reshape(d)
    return loss, grad_x, grads


def _me():
    return lax.axis_index("x"), lax.axis_index("y"), lax.axis_index("c")


def _flat(px, py, pc):
    return 4 * px + 2 * py + pc


def all_gather_blocks(block, name):
    r, c_ = block.shape

    def body(x_ref, out_ref, send_sems, recv_sems, local_sem):
        gather_start(x_ref, out_ref, send_sems, recv_sems, local_sem)
        gather_finish(x_ref, out_ref, send_sems, recv_sems, local_sem)

    return pl.pallas_call(
        body, name=name, out_shape=_sds((8, r, c_), block.dtype),
        in_specs=[pl.BlockSpec(memory_space=pl.ANY)], out_specs=pl.BlockSpec(memory_space=pl.ANY),
        scratch_shapes=exchange_semaphores())(block)


def exchange_semaphores():
    return [pltpu.SemaphoreType.DMA((7,)), pltpu.SemaphoreType.DMA((7,)), pltpu.SemaphoreType.DMA(())]


def _gather_copies(x_ref, out_ref, send_sems, recv_sems, local_sem):
    x, y, c = _me()
    me, sibling = (x, y, c), (x, y, 1 - c)
    chips = [(1 - x, y), (x, 1 - y), (1 - x, 1 - y)]

    def slot(px, py, pc):
        return out_ref.at[_flat(px, py, pc)]

    def copy(k, blk, to, src=None):
        return pltpu.make_async_remote_copy(
            src_ref=slot(*blk) if src is None else src, dst_ref=slot(*blk), send_sem=send_sems.at[k],
            recv_sem=recv_sems.at[k], device_id=to, device_id_type=pl.DeviceIdType.MESH)

    mine = pltpu.make_async_copy(x_ref, slot(*me), local_sem)
    first = [copy(0, me, sibling, src=x_ref)] + [copy(1 + j, me, (*chip, c), src=x_ref) for j, chip in enumerate(chips)]
    passed = [copy(4 + j, (*chip, c), sibling) for j, chip in enumerate(chips)]
    over_ici = [copy(1 + j, (*chip, c), me) for j, chip in enumerate(chips)]
    from_sibling = [copy(0, sibling, me)] + [copy(4 + j, (*chip, 1 - c), me) for j, chip in enumerate(chips)]
    return mine, first, passed, over_ici, from_sibling


def gather_start(*refs):
    mine, first, _, _, _ = _gather_copies(*refs)
    mine.start()
    for cp in first:
        cp.start()


def gather_finish(*refs):
    mine, first, passed, over_ici, from_sibling = _gather_copies(*refs)
    for arrival, forward in zip(over_ici, passed):
        arrival.wait_recv()
        forward.start()
    for arrival in from_sibling:
        arrival.wait_recv()
    for cp in first + passed:
        cp.wait_send()
    mine.wait()


def all_to_all_blocks(send, name):
    _, r, c_ = send.shape

    def body(s_ref, out_ref, send_sems, recv_sems, local_sem):
        all_to_all_start(s_ref, out_ref, send_sems, recv_sems, local_sem)
        all_to_all_finish(s_ref, out_ref, send_sems, recv_sems, local_sem)

    return pl.pallas_call(
        body, name=name, out_shape=_sds((8, r, c_), send.dtype),
        in_specs=[pl.BlockSpec(memory_space=pl.ANY)], out_specs=pl.BlockSpec(memory_space=pl.ANY),
        scratch_shapes=exchange_semaphores())(send)


def _all_to_all_copies(s_ref, out_ref, send_sems, recv_sems, local_sem):
    x, y, c = _me()
    me = _flat(x, y, c)
    mine = pltpu.make_async_copy(s_ref.at[me], out_ref.at[me], local_sem)
    copies = []
    for rel in range(1, 8):
        px = 1 - x if rel & 4 else x
        py = 1 - y if rel & 2 else y
        pc = 1 - c if rel & 1 else c
        copies.append(pltpu.make_async_remote_copy(
            src_ref=s_ref.at[_flat(px, py, pc)], dst_ref=out_ref.at[me], send_sem=send_sems.at[rel - 1],
            recv_sem=recv_sems.at[rel - 1], device_id=(px, py, pc), device_id_type=pl.DeviceIdType.MESH))
    return mine, copies


def all_to_all_start(*refs):
    mine, copies = _all_to_all_copies(*refs)
    mine.start()
    for cp in copies:
        cp.start()


def all_to_all_finish(*refs):
    mine, copies = _all_to_all_copies(*refs)
    for cp in copies:
        cp.wait_recv()
    for cp in copies:
        cp.wait_send()
    mine.wait()


def reduce_blocks(recv, name):
    nsrc, r, c_ = recv.shape
    tr = _tile(r, 128, 16)

    def body(x_ref, o_ref):
        acc = x_ref[0].astype(F32)
        for s in range(1, nsrc):
            acc = acc + x_ref[s].astype(F32)
        o_ref[...] = acc

    return pl.pallas_call(
        body, name=name, grid=(r // tr,),
        in_specs=[pl.BlockSpec((nsrc, tr, c_), lambda i: (0, i, 0))], out_specs=pl.BlockSpec((tr, c_), lambda i: (i, 0)),
        out_shape=_sds((r, c_), F32), compiler_params=_cp(("parallel",)))(recv)


def adamw(w, g, m, v):
    shape = w.shape
    size = math.prod(shape)
    if shape[-1] < V7X_LANES and size % 1024 == 0:
        view = (size // 1024, 1024)
    else:
        view = (size // shape[-1], shape[-1])
    rows, cols = view
    tr = _tile(rows, max(8, (1 << 19) // cols // 8 * 8), 8)
    c1 = 1.0 - ADAM_B1 ** ADAM_STEP
    c2 = 1.0 - ADAM_B2 ** ADAM_STEP

    def body(w_ref, g_ref, m_ref, v_ref, d_ref, nm_ref, nv_ref):
        gv = g_ref[...]
        nm = ADAM_B1 * m_ref[...] + (1.0 - ADAM_B1) * gv
        nv = ADAM_B2 * v_ref[...] + (1.0 - ADAM_B2) * (gv * gv)
        nm_ref[...] = nm
        nv_ref[...] = nv
        d_ref[...] = -ADAM_LR * ((nm / c1) / (jnp.sqrt(nv / c2) + ADAM_EPS) + ADAM_WD * w_ref[...])

    blk = pl.BlockSpec((tr, cols), lambda i: (i, 0))
    out = _sds(view, F32)
    res = pl.pallas_call(
        body, name="adamw", grid=(rows // tr,), in_specs=[blk] * 4, out_specs=[blk] * 3, out_shape=[out] * 3,
        compiler_params=_cp(("parallel",)))(w.reshape(view), g.reshape(view), m.reshape(view), v.reshape(view))
    return tuple(a.reshape(shape) for a in res)


PACKED = [('w13t_1', 'ffn1_w13', True), ('w2_1', 'ffn1_w2', False), ('w_int', 'w_in', True), ('w_out', 'w_out', False),
          ('w13t_2', 'ffn2_w13', True), ('w2_2', 'ffn2_w2', False), ('w_glut', 'ssm_w_glu', True),
          ('conv_w_outt', 'conv_w_out', True), ('attn_w_ot', 'attn_w_o', True)]


def _rows_of(a, d):
    return a.reshape(a.shape[0] * a.shape[1] // d, d)


def _pad_rows(flat, d, mult):
    n = flat.shape[-1]
    rows = -(-n // d)
    rows = -(-rows // mult) * mult
    flat = jnp.pad(flat, [(0, 0)] * (flat.ndim - 1) + [(0, rows * d - n)])
    return flat.reshape(flat.shape[:-1] + (rows, d))


def pack_weight_shards(cfg, wts):
    d = cfg.d_model
    blocks, layout = [], []
    for l in range(cfg.depth):
        parts = []
        for name, src, tr in PACKED:
            a = wts[src][l]
            a = (a.T if tr else a).astype(BF16)
            if l == 0:
                layout.append((name, a.shape))
            parts.append(_rows_of(a, d))
        blocks.append(jnp.concatenate(parts, axis=0))
    small = jnp.concatenate([wts['conv_w'].reshape(-1), wts['meta_tokens'].reshape(-1)])
    return blocks, _pad_rows(small, d, 8), layout


def unpack_gathered_layer(cfg, gathered, layout):
    d = cfg.d_model
    out = {}
    off = 0
    for name, shape in layout:
        rows = shape[0] * shape[1] // d
        out[name] = gathered[:, off:off + rows].reshape(8 * shape[0], shape[1])
        off += rows
    return out


def unpack_gathered_small(gathered_small, wts):
    n_cw, n_mt = wts['conv_w'].size, wts['meta_tokens'].size
    small = gathered_small.reshape(8, -1)
    cw = small[:, :n_cw].reshape((8,) + wts['conv_w'].shape)
    conv_w = jnp.transpose(cw, (1, 2, 0, 3)).reshape(cw.shape[1], cw.shape[2], -1)
    mt = small[:, n_cw:n_cw + n_mt].reshape((8,) + wts['meta_tokens'].shape)
    return conv_w, jnp.transpose(mt, (1, 0, 2)).reshape(mt.shape[1], -1)


def pack_layer_grads(cfg, g, layout):
    d = cfg.d_model
    parts = [g[name].reshape(8, shape[0] * shape[1] // d, d).astype(BF16) for name, shape in layout]
    return jnp.concatenate(parts, axis=1)


def pack_small_grads(cfg, grads, rep_names):
    cw = grads['conv_w']
    cw = jnp.transpose(cw.reshape(cw.shape[0], cw.shape[1], 8, -1), (2, 0, 1, 3)).reshape(8, -1)
    mt = grads['meta_tokens']
    mt = jnp.transpose(mt.reshape(mt.shape[0], 8, -1), (1, 0, 2)).reshape(8, -1)
    repl = jnp.concatenate([grads[k].reshape(-1) for k in rep_names])
    small = jnp.concatenate([cw, mt, jnp.broadcast_to(repl[None], (8, repl.shape[0]))], axis=1)
    return _pad_rows(small, cfg.d_model, 128)


def unpack_grads(cfg, gsums, gsmall, layout, wts, rep_names):
    d = cfg.d_model
    acc = {}
    for gsum in gsums:
        off = 0
        for name, shape in layout:
            rows = shape[0] * shape[1] // d
            acc.setdefault(name, []).append(gsum[off:off + rows].reshape(shape))
            off += rows
    out = {}
    for name, src, tr in PACKED:
        a = jnp.stack(acc[name])
        out[src] = jnp.swapaxes(a, 1, 2) if tr else a
    flat = gsmall.reshape(-1)
    pos = 0
    for k in ['conv_w', 'meta_tokens'] + rep_names:
        n = wts[k].size
        out[k] = flat[pos:pos + n].reshape(wts[k].shape)
        pos += n
    return out


class WeightExchange:
    def __init__(self, cfg, wts):
        self.cfg = cfg
        self.blocks, small, self.layout = pack_weight_shards(cfg, wts)
        self.conv_w, self.meta_tokens = unpack_gathered_small(all_gather_blocks(small, "all_gather_small_weights"), wts)
        self.full = {0: unpack_gathered_layer(cfg, all_gather_blocks(self.blocks[0], "all_gather_weights"), self.layout)}
        self.to_send, self.arrived = {}, {}

    def weights(self, l):
        return dict(self.full[l], conv_w=self.conv_w[l])

    def gather_src(self, l):
        return self.blocks[l + 1] if l + 1 < self.cfg.depth else None

    def gathered(self, l, blocks):
        self.full[l + 1] = unpack_gathered_layer(self.cfg, blocks, self.layout)

    def matrix_grads(self, l, g):
        self.to_send[l] = pack_layer_grads(self.cfg, g, self.layout)

    def send_src(self, l):
        return self.to_send.get(l + 1)

    def received(self, l, blocks):
        self.arrived[l + 1] = blocks

    def finish(self, grads, wts):
        self.arrived[0] = all_to_all_blocks(self.to_send[0], "all_to_all_grads")
        gsums = [reduce_blocks(self.arrived[l], "reduce_grads") for l in range(self.cfg.depth)]
        small = all_to_all_blocks(pack_small_grads(self.cfg, grads, REPLICATED), "all_to_all_small_grads")
        return unpack_grads(self.cfg, gsums, reduce_blocks(small, "reduce_small_grads"), self.layout, wts, REPLICATED)


def train_step(cfg, x, target, wts, ms, vs):
    ex = WeightExchange(cfg, wts)
    rep = {k: wts[k] for k in REPLICATED}
    loss, grad_x, grads = local_step(cfg, x, target, rep, ex)
    gw = ex.finish(grads, wts)
    loss = lax.psum(loss, ("x", "y", "c"))
    deltas, new_m, new_v = {}, {}, {}
    for k in WEIGHT_NAMES:
        deltas[k], new_m[k], new_v[k] = adamw(wts[k], gw[k], ms[k], vs[k])
    return (loss, grad_x, *[gw[k] for k in WEIGHT_NAMES], *[deltas[k] for k in WEIGHT_NAMES],
            *[new_m[k] for k in WEIGHT_NAMES], *[new_v[k] for k in WEIGHT_NAMES])


def kernel(x, meta_tokens, ffn1_norm, ffn1_w13, ffn1_w2, mix_norm, w_in, ssm_lam_re, ssm_lam_im, ssm_log_dt, ssm_b_re, ssm_b_im, ssm_c_re, ssm_c_im, ssm_d, ssm_w_glu, conv_w, conv_b, conv_ln_g, conv_ln_b, conv_w_out, attn_w_o, w_out, ffn2_norm, ffn2_w13, ffn2_w2, final_norm, loss_target, m_meta_tokens, m_ffn1_norm, m_ffn1_w13, m_ffn1_w2, m_mix_norm, m_w_in, m_ssm_lam_re, m_ssm_lam_im, m_ssm_log_dt, m_ssm_b_re, m_ssm_b_im, m_ssm_c_re, m_ssm_c_im, m_ssm_d, m_ssm_w_glu, m_conv_w, m_conv_b, m_conv_ln_g, m_conv_ln_b, m_conv_w_out, m_attn_w_o, m_w_out, m_ffn2_norm, m_ffn2_w13, m_ffn2_w2, m_final_norm, v_meta_tokens, v_ffn1_norm, v_ffn1_w13, v_ffn1_w2, v_mix_norm, v_w_in, v_ssm_lam_re, v_ssm_lam_im, v_ssm_log_dt, v_ssm_b_re, v_ssm_b_im, v_ssm_c_re, v_ssm_c_im, v_ssm_d, v_ssm_w_glu, v_conv_w, v_conv_b, v_conv_ln_g, v_conv_ln_b, v_conv_w_out, v_attn_w_o, v_w_out, v_ffn2_norm, v_ffn2_w13, v_ffn2_w2, v_final_norm):
    given = dict(locals())
    wts = {k: given[k] for k in WEIGHT_NAMES}
    ms = {k: given["m_" + k] for k in WEIGHT_NAMES}
    vs = {k: given["v_" + k] for k in WEIGHT_NAMES}
    return train_step(FULL, x, loss_target, wts, ms, vs)
```

```python
import functools
import math
from typing import NamedTuple

import jax
import jax.numpy as jnp
from jax import lax
from jax.experimental import pallas as pl
from jax.experimental.pallas import tpu as pltpu

F32 = jnp.float32
BF16 = jnp.bfloat16
RMS_EPS = 1e-6
LN_EPS = 1e-5
ADAM_LR = 0.001
ADAM_B1 = 0.9
ADAM_B2 = 0.999
ADAM_EPS = 1e-08
ADAM_WD = 0.01
ADAM_STEP = 10
V7X_LANES = 128
V7X_VMEM_LIMIT = 52 * 1024 * 1024
Q_BLOCK = 128
SSM_CHUNK = 128
SSM_SB = 128
CONV_HALO = 32

WEIGHT_NAMES = ['meta_tokens', 'ffn1_norm', 'ffn1_w13', 'ffn1_w2', 'mix_norm', 'w_in', 'ssm_lam_re', 'ssm_lam_im',
                'ssm_log_dt', 'ssm_b_re', 'ssm_b_im', 'ssm_c_re', 'ssm_c_im', 'ssm_d', 'ssm_w_glu', 'conv_w', 'conv_b',
                'conv_ln_g', 'conv_ln_b', 'conv_w_out', 'attn_w_o', 'w_out', 'ffn2_norm', 'ffn2_w13', 'ffn2_w2',
                'final_norm']
REPLICATED = ['ffn1_norm', 'mix_norm', 'ssm_lam_re', 'ssm_lam_im', 'ssm_log_dt', 'ssm_b_re', 'ssm_b_im', 'ssm_c_re',
              'ssm_c_im', 'ssm_d', 'conv_b', 'conv_ln_g', 'conv_ln_b', 'ffn2_norm', 'final_norm']


class Cfg(NamedTuple):
    d_model: int
    seq: int
    depth: int
    d_ff: int
    b_loc: int
    ndev: int = 8
    n_meta: int = 16
    n_heads: int = 8
    head_dim: int = 64
    ssm_group: int = 16
    ssm_state: int = 64
    conv_width: int = 31

    @property
    def d_ssm(self): return self.d_model // 2
    @property
    def groups(self): return self.d_ssm // self.ssm_group
    @property
    def n_state(self): return self.groups * self.ssm_state
    @property
    def d_conv(self): return self.d_model // 2
    @property
    def d_attn(self): return self.n_heads * self.head_dim
    @property
    def off_xa(self): return self.d_ssm
    @property
    def off_xg(self): return self.d_ssm + self.d_conv
    @property
    def off_q(self): return self.d_ssm + 2 * self.d_conv
    @property
    def off_k(self): return self.off_q + self.d_attn
    @property
    def off_v(self): return self.off_k + self.d_attn
    @property
    def off_gate(self): return self.off_v + self.d_attn
    @property
    def d_in(self): return self.off_gate + 3 * self.d_model
    @property
    def pad(self): return Q_BLOCK - self.n_meta
    @property
    def lp(self): return Q_BLOCK + self.seq
    @property
    def tokens(self): return self.b_loc * self.lp


FULL = Cfg(d_model=1024, seq=4096, depth=4, d_ff=2816, b_loc=2)


def _tile(n, pref, align):
    for t in range(min(pref, n), 0, -1):
        if n % t == 0 and t % align == 0:
            return t
    return n


def _cp(sem):
    return pltpu.CompilerParams(dimension_semantics=sem, vmem_limit_bytes=V7X_VMEM_LIMIT)


def _sds(shape, dtype):
    return jax.ShapeDtypeStruct(shape, dtype)


def _sigmoid(x):
    return jax.nn.sigmoid(x)


def _dot(a, b, ca, cb):
    return lax.dot_general(a, b, (((ca,), (cb,)), ((), ())), preferred_element_type=F32)


def rms_fwd(h, g, name):
    t, d = h.shape
    tm = _tile(t, 1056, 16)

    def body(h_ref, g_ref, o_ref):
        x = h_ref[...]
        r = lax.rsqrt(jnp.mean(x * x, axis=-1, keepdims=True) + RMS_EPS)
        o_ref[...] = (x * r * g_ref[...]).astype(BF16)

    return pl.pallas_call(
        body, name=name, grid=(t // tm,),
        in_specs=[pl.BlockSpec((tm, d), lambda i: (i, 0)), pl.BlockSpec((1, d), lambda i: (0, 0))],
        out_specs=pl.BlockSpec((tm, d), lambda i: (i, 0)), out_shape=_sds((t, d), BF16),
        compiler_params=_cp(("parallel",)))(h, g.reshape(1, d))


def rms_bwd(dxn, h, g, dh_res, name):
    t, d = h.shape
    tm = _tile(t, 528, 16)

    def body(dxn_ref, h_ref, g_ref, r_ref, dh_ref, dh16_ref, dg_ref):
        x = h_ref[...]
        r = lax.rsqrt(jnp.mean(x * x, axis=-1, keepdims=True) + RMS_EPS)
        xhat = x * r
        dy = dxn_ref[...]
        dyg = dy * g_ref[...]
        dh = r_ref[...] + r * (dyg - xhat * jnp.mean(dyg * xhat, axis=-1, keepdims=True))
        dh_ref[...] = dh
        dh16_ref[...] = dh.astype(BF16)

        @pl.when(pl.program_id(0) == 0)
        def _():
            dg_ref[...] = jnp.zeros_like(dg_ref)

        dg_ref[...] += jnp.sum(dy * xhat, axis=0, keepdims=True)

    row = pl.BlockSpec((tm, d), lambda i: (i, 0))
    vec = pl.BlockSpec((1, d), lambda i: (0, 0))
    return pl.pallas_call(
        body, name=name, grid=(t // tm,), in_specs=[row, row, vec, row], out_specs=[row, row, vec],
        out_shape=[_sds((t, d), F32), _sds((t, d), BF16), _sds((1, d), F32)],
        compiler_params=_cp(("arbitrary",)))(dxn, h, g.reshape(1, d), dh_res)


def mm(pairs, *, nt, n, tm, tn, out_dtype, name, res=None, res_scale=1.0):
    m = pairs[0][0].shape[0]
    np_ = len(pairs)

    def body(*refs):
        o_ref = refs[-1]
        acc = None
        for p in range(np_):
            a = refs[2 * p][...].astype(BF16)
            b = refs[2 * p + 1][...].astype(BF16)
            d = _dot(a, b, 1, 1 if nt else 0)
            acc = d if acc is None else acc + d
        if res is not None:
            acc = refs[2 * np_][...] + res_scale * acc
        o_ref[...] = acc.astype(out_dtype)

    in_specs, args = [], []
    for a, b, kblk in pairs:
        k = a.shape[1]
        in_specs.append(pl.BlockSpec((tm, k), lambda i, j: (i, 0)))
        if nt:
            nb = n // tn
            in_specs.append(pl.BlockSpec((tn, k), functools.partial(lambda i, j, o: (j + o, 0), o=kblk * nb)))
        else:
            in_specs.append(pl.BlockSpec((k, tn), functools.partial(lambda i, j, o: (o, j), o=kblk)))
        args += [a, b]
    if res is not None:
        in_specs.append(pl.BlockSpec((tm, tn), lambda i, j: (i, j)))
        args.append(res)
    return pl.pallas_call(
        body, name=name, grid=(m // tm, n // tn), in_specs=in_specs,
        out_specs=pl.BlockSpec((tm, tn), lambda i, j: (i, j)), out_shape=_sds((m, n), out_dtype),
        compiler_params=_cp(("parallel", "arbitrary")))(*args)


def mm_tn(a, b, *, name, scale=1.0):
    t, m = a.shape
    n = b.shape[1]
    tm = _tile(m, 1536, 128)
    tn = _tile(n, 1024, 128)
    tk = _tile(t, 1056, 16)
    nk = t // tk

    def body(a_ref, b_ref, o_ref):
        k = pl.program_id(2)

        @pl.when(k == 0)
        def _():
            o_ref[...] = jnp.zeros_like(o_ref)

        o_ref[...] += _dot(a_ref[...].astype(BF16), b_ref[...].astype(BF16), 0, 0)
        if scale != 1.0:
            @pl.when(k == nk - 1)
            def _():
                o_ref[...] = o_ref[...] * scale

    return pl.pallas_call(
        body, name=name, grid=(m // tm, n // tn, nk),
        in_specs=[pl.BlockSpec((tk, tm), lambda i, j, k: (k, i)), pl.BlockSpec((tk, tn), lambda i, j, k: (k, j))],
        out_specs=pl.BlockSpec((tm, tn), lambda i, j, k: (i, j)), out_shape=_sds((m, n), F32),
        compiler_params=_cp(("parallel", "parallel", "arbitrary")))(a, b)


def dual_mm_act(x, wt, *, kind, name):
    t, k = x.shape
    hdim = wt.shape[0] // 2
    tm = _tile(t, 1056, 16)
    tn = _tile(hdim, 256, 128)
    nb = hdim // tn
    act_dtype = BF16 if kind == 'swiglu' else F32

    def body(x_ref, wa_ref, wb_ref, a_ref, b_ref, act_ref):
        xv = x_ref[...]
        a = _dot(xv, wa_ref[...], 1, 1)
        b = _dot(xv, wb_ref[...], 1, 1)
        a_ref[...] = a.astype(act_dtype)
        b_ref[...] = b.astype(act_dtype)
        if kind == 'swiglu':
            act_ref[...] = (a * _sigmoid(a) * b).astype(act_dtype)
        else:
            act_ref[...] = (a * _sigmoid(b)).astype(act_dtype)

    ob = pl.BlockSpec((tm, tn), lambda i, j: (i, j))
    return pl.pallas_call(
        body, name=name, grid=(t // tm, nb),
        in_specs=[pl.BlockSpec((tm, k), lambda i, j: (i, 0)), pl.BlockSpec((tn, k), lambda i, j: (j, 0)),
                  pl.BlockSpec((tn, k), lambda i, j: (j + nb, 0))],
        out_specs=[ob, ob, ob], out_shape=[_sds((t, hdim), act_dtype)] * 3,
        compiler_params=_cp(("parallel", "arbitrary")))(x, wt, wt)


def ffn_down_bwd(dh, w2, a, b, name):
    t, d = dh.shape
    hdim = w2.shape[0]
    tm = _tile(t, 1056, 16)
    tn = _tile(hdim, 256, 128)

    def body(dh_ref, w_ref, a_ref, b_ref, da_ref, db_ref):
        dact = 0.5 * _dot(dh_ref[...].astype(BF16), w_ref[...], 1, 1)
        av = a_ref[...].astype(F32)
        sg = _sigmoid(av)
        da_ref[...] = (dact * b_ref[...].astype(F32) * sg * (1.0 + av * (1.0 - sg))).astype(BF16)
        db_ref[...] = (dact * av * sg).astype(BF16)

    ob = pl.BlockSpec((tm, tn), lambda i, j: (i, j))
    return pl.pallas_call(
        body, name=name, grid=(t // tm, hdim // tn),
        in_specs=[pl.BlockSpec((tm, d), lambda i, j: (i, 0)), pl.BlockSpec((tn, d), lambda i, j: (j, 0)), ob, ob],
        out_specs=[ob, ob], out_shape=[_sds((t, hdim), BF16), _sds((t, hdim), BF16)],
        compiler_params=_cp(("parallel", "arbitrary")))(dh, w2, a, b)


def final_fwd_bwd(cfg, h, gf, target):
    bsz, lp, d = h.shape
    nq = lp // Q_BLOCK

    def body(h_ref, g_ref, t_ref, dh_ref, dh16_ref, dg_ref, loss_ref):
        b, j = pl.program_id(0), pl.program_id(1)
        x = h_ref[0]
        r = lax.rsqrt(jnp.mean(x * x, axis=-1, keepdims=True) + RMS_EPS)
        xhat = x * r
        gv = g_ref[...]
        diff = jnp.where(j > 0, xhat * gv - t_ref[0], 0.0)
        dy = diff * (1.0 / d)
        dyg = dy * gv
        dh = r * (dyg - xhat * jnp.mean(dyg * xhat, axis=-1, keepdims=True))
        dh_ref[0] = dh
        dh16_ref[0] = dh.astype(BF16)

        @pl.when((b == 0) & (j == 0))
        def _():
            dg_ref[...] = jnp.zeros_like(dg_ref)
            loss_ref[...] = jnp.zeros_like(loss_ref)

        dg_ref[...] += jnp.sum(dy * xhat, axis=0, keepdims=True)
        loss_ref[...] += jnp.sum(jnp.sum(diff * diff, axis=1, keepdims=True), axis=0, keepdims=True)

    blk = pl.BlockSpec((1, Q_BLOCK, d), lambda b, j: (b, j, 0))
    return pl.pallas_call(
        body, name="final_loss", grid=(bsz, nq),
        in_specs=[blk, pl.BlockSpec((1, d), lambda b, j: (0, 0)),
                  pl.BlockSpec((1, Q_BLOCK, d), lambda b, j: (b, jnp.maximum(j - 1, 0), 0))],
        out_specs=[blk, blk, pl.BlockSpec((1, d), lambda b, j: (0, 0)), pl.BlockSpec((1, 1), lambda b, j: (0, 0))],
        out_shape=[_sds((bsz, lp, d), F32), _sds((bsz, lp, d), BF16), _sds((1, d), F32), _sds((1, 1), F32)],
        compiler_params=_cp(("arbitrary", "arbitrary")))(h, gf.reshape(1, d), target)


def _row_valid(cfg, tm, i):
    pos = (i * tm) % cfg.lp + lax.broadcasted_iota(jnp.int32, (tm, 1), 0)
    return pos >= cfg.pad


def merge_fwd(cfg, proj, o_ssm, o_conv, o_attn):
    t, d = o_ssm.shape
    tc = math.gcd(math.gcd(d, cfg.off_gate), 512)
    tm = _tile(cfg.lp, 528, 16)
    g0 = cfg.off_gate // tc
    nc = d // tc

    def body(g0_ref, g1_ref, g2_ref, s_ref, c_ref, a_ref, o_ref):
        valid = _row_valid(cfg, tm, pl.program_id(0))
        m = (_sigmoid(g0_ref[...]) * s_ref[...] + _sigmoid(g1_ref[...]) * c_ref[...]
             + _sigmoid(g2_ref[...]) * a_ref[...])
        o_ref[...] = jnp.where(valid, m, 0.0).astype(BF16)

    gate = [pl.BlockSpec((tm, tc), functools.partial(lambda i, j, o: (i, o + j), o=g0 + k * nc)) for k in range(3)]
    blk = pl.BlockSpec((tm, tc), lambda i, j: (i, j))
    return pl.pallas_call(
        body, name="merge_fwd", grid=(t // tm, nc), in_specs=gate + [blk, blk, blk], out_specs=blk,
        out_shape=_sds((t, d), BF16), compiler_params=_cp(("parallel", "parallel")))(
            proj, proj, proj, o_ssm, o_conv, o_attn)


def merge_bwd(cfg, dmerged, proj, ga, gg, o_conv, o_attn):
    t, d = dmerged.shape
    tc = math.gcd(math.gcd(d, cfg.off_gate), 512)
    tm = _tile(cfg.lp, 528, 16)
    g0 = cfg.off_gate // tc
    nc = d // tc

    def body(dm_ref, g0_ref, g1_ref, g2_ref, ga_ref, gg_ref, c_ref, a_ref,
             dg0_ref, dg1_ref, dg2_ref, dga_ref, dgg_ref, dc_ref, da_ref):
        dm = dm_ref[...]
        s0, s1, s2 = _sigmoid(g0_ref[...]), _sigmoid(g1_ref[...]), _sigmoid(g2_ref[...])
        sg = _sigmoid(gg_ref[...])
        gav = ga_ref[...]
        o_s = gav * sg
        dg0_ref[...] = (dm * o_s * s0 * (1.0 - s0)).astype(BF16)
        dg1_ref[...] = (dm * c_ref[...] * s1 * (1.0 - s1)).astype(BF16)
        dg2_ref[...] = (dm * a_ref[...] * s2 * (1.0 - s2)).astype(BF16)
        dos = dm * s0
        dga_ref[...] = (dos * sg).astype(BF16)
        dgg_ref[...] = (dos * gav * sg * (1.0 - sg)).astype(BF16)
        dc_ref[...] = (dm * s1).astype(BF16)
        da_ref[...] = (dm * s2).astype(BF16)

    gate = [pl.BlockSpec((tm, tc), functools.partial(lambda i, j, o: (i, o + j), o=g0 + k * nc)) for k in range(3)]
    blk = pl.BlockSpec((tm, tc), lambda i, j: (i, j))
    out = _sds((t, d), BF16)
    return pl.pallas_call(
        body, name="merge_bwd", grid=(t // tm, nc), in_specs=[blk] + gate + [blk, blk, blk, blk],
        out_specs=[blk] * 7, out_shape=[out] * 7, compiler_params=_cp(("parallel", "parallel")))(
            dmerged, proj, proj, proj, ga, gg, o_conv, o_attn)


def _conv_tile(cfg):
    return _tile(cfg.lp, 384, CONV_HALO)


def _shift_up(x, off, rows):
    if off == 0:
        return x[:rows]
    return pltpu.roll(x, x.shape[0] - off, 0)[:rows]


def conv_fwd(cfg, proj, w, bias, ln_g, ln_b):
    t = proj.shape[0]
    wc = cfg.d_conv
    tt = _conv_tile(cfg)
    nc = cfg.lp // tt
    kw = cfg.conv_width
    first = CONV_HALO - (kw - 1)

    def body(a_ref, g_ref, ap_ref, gp_ref, w_ref, b_ref, lg_ref, lb_ref, cpre_ref, hn_ref):
        c = pl.program_id(1)
        hc = a_ref[...] * _sigmoid(g_ref[...])
        hp = jnp.where(c > 0, ap_ref[...] * _sigmoid(gp_ref[...]), 0.0)
        hcat = jnp.concatenate([hp, hc], axis=0)
        acc = jnp.zeros((tt, wc), F32) + b_ref[...]
        for k in range(kw):
            acc = acc + w_ref[k:k + 1, :] * _shift_up(hcat, first + k, tt)
        cpre_ref[...] = acc
        mu = jnp.mean(acc, axis=-1, keepdims=True)
        xc = acc - mu
        y = xc * lax.rsqrt(jnp.mean(xc * xc, axis=-1, keepdims=True) + LN_EPS) * lg_ref[...] + lb_ref[...]
        hn_ref[...] = (y * _sigmoid(y)).astype(BF16)

    ca, cg = cfg.off_xa // wc, cfg.off_xg // wc
    hb = tt // CONV_HALO
    lph = cfg.lp // CONV_HALO

    def halo(col):
        return pl.BlockSpec((CONV_HALO, wc), lambda b, c: (jnp.maximum(b * lph + c * hb - 1, 0), col))

    vec = pl.BlockSpec((1, wc), lambda b, c: (0, 0))
    blk = pl.BlockSpec((tt, wc), lambda b, c: (b * nc + c, 0))
    return pl.pallas_call(
        body, name="conv_fwd", grid=(cfg.b_loc, nc),
        in_specs=[pl.BlockSpec((tt, wc), lambda b, c: (b * nc + c, ca)),
                  pl.BlockSpec((tt, wc), lambda b, c: (b * nc + c, cg)), halo(ca), halo(cg),
                  pl.BlockSpec((kw, wc), lambda b, c: (0, 0)), vec, vec, vec],
        out_specs=[blk, blk], out_shape=[_sds((t, wc), F32), _sds((t, wc), BF16)],
        compiler_params=_cp(("parallel", "arbitrary")))(
            proj, proj, proj, proj, w, bias.reshape(1, wc), ln_g.reshape(1, wc), ln_b.reshape(1, wc))


def conv_bwd(cfg, dhn, cpre, proj, w, ln_g, ln_b):
    t = proj.shape[0]
    wc = cfg.d_conv
    tt = _conv_tile(cfg)
    nc = cfg.lp // tt
    kw = cfg.conv_width
    first = CONV_HALO - (kw - 1)

    def body(dhn_ref, dhn_nx_ref, cp_ref, cp_nx_ref, a_ref, g_ref, ap_ref, gp_ref, w_ref, lg_ref, lb_ref,
             dxa_ref, dxg_ref, dw_ref, db_ref, dlg_ref, dlb_ref):
        b, c = pl.program_id(0), pl.program_id(1)
        lg, lb = lg_ref[...], lb_ref[...]

        def ln_silu_bwd(dh, cp):
            mu = jnp.mean(cp, axis=-1, keepdims=True)
            xc = cp - mu
            rstd = lax.rsqrt(jnp.mean(xc * xc, axis=-1, keepdims=True) + LN_EPS)
            xhat = xc * rstd
            y = xhat * lg + lb
            sg = _sigmoid(y)
            dy = dh * sg * (1.0 + y * (1.0 - sg))
            dxh = dy * lg
            dc = rstd * (dxh - jnp.mean(dxh, axis=-1, keepdims=True)
                         - xhat * jnp.mean(dxh * xhat, axis=-1, keepdims=True))
            return dc, dy, xhat

        dc, dy, xhat = ln_silu_bwd(dhn_ref[...], cp_ref[...])
        dc_nx, _, _ = ln_silu_bwd(dhn_nx_ref[...], cp_nx_ref[...])
        dc_nx = jnp.where(c < nc - 1, dc_nx, 0.0)
        dcat = jnp.concatenate([dc, dc_nx], axis=0)
        dhc = jnp.zeros((tt, wc), F32)
        for k in range(kw):
            dhc = dhc + w_ref[k:k + 1, :] * _shift_up(dcat, kw - 1 - k, tt)
        av, gv = a_ref[...], g_ref[...]
        sg = _sigmoid(gv)
        valid = (c * tt + lax.broadcasted_iota(jnp.int32, (tt, 1), 0)) >= cfg.pad
        dhc = jnp.where(valid, dhc, 0.0)
        dxa_ref[...] = (dhc * sg).astype(BF16)
        dxg_ref[...] = (dhc * av * sg * (1.0 - sg)).astype(BF16)

        @pl.when((b == 0) & (c == 0))
        def _():
            dw_ref[...] = jnp.zeros_like(dw_ref)
            db_ref[...] = jnp.zeros_like(db_ref)
            dlg_ref[...] = jnp.zeros_like(dlg_ref)
            dlb_ref[...] = jnp.zeros_like(dlb_ref)

        hp = jnp.where(c > 0, ap_ref[...] * _sigmoid(gp_ref[...]), 0.0)
        hcat = jnp.concatenate([hp, av * sg], axis=0)
        for k in range(kw):
            dw_ref[k:k + 1, :] += jnp.sum(dc * _shift_up(hcat, first + k, tt), axis=0, keepdims=True)
        db_ref[...] += jnp.sum(dc, axis=0, keepdims=True)
        dlg_ref[...] += jnp.sum(dy * xhat, axis=0, keepdims=True)
        dlb_ref[...] += jnp.sum(dy, axis=0, keepdims=True)

    ca, cg = cfg.off_xa // wc, cfg.off_xg // wc
    hb = tt // CONV_HALO
    lph = cfg.lp // CONV_HALO
    last = cfg.b_loc * lph - 1

    def prev(col):
        return pl.BlockSpec((CONV_HALO, wc), lambda b, c: (jnp.maximum(b * lph + c * hb - 1, 0), col))

    nxt = pl.BlockSpec((CONV_HALO, wc), lambda b, c: (jnp.minimum(b * lph + (c + 1) * hb, last), 0))
    vec = pl.BlockSpec((1, wc), lambda b, c: (0, 0))
    blk = pl.BlockSpec((tt, wc), lambda b, c: (b * nc + c, 0))
    wspec = pl.BlockSpec((kw, wc), lambda b, c: (0, 0))
    return pl.pallas_call(
        body, name="conv_bwd", grid=(cfg.b_loc, nc),
        in_specs=[blk, nxt, blk, nxt, pl.BlockSpec((tt, wc), lambda b, c: (b * nc + c, ca)),
                  pl.BlockSpec((tt, wc), lambda b, c: (b * nc + c, cg)), prev(ca), prev(cg), wspec, vec, vec],
        out_specs=[blk, blk, wspec, vec, vec, vec],
        out_shape=[_sds((t, wc), BF16), _sds((t, wc), BF16), _sds((kw, wc), F32), _sds((1, wc), F32),
                   _sds((1, wc), F32), _sds((1, wc), F32)],
        compiler_params=_cp(("arbitrary", "arbitrary")))(
            dhn, dhn, cpre, cpre, proj, proj, proj, proj, w, ln_g.reshape(1, wc), ln_b.reshape(1, wc))


def _split_bf16(x):
    hi = x.astype(BF16)
    return hi, (x - hi.astype(F32)).astype(BF16)


K_CHUNK = 2 * Q_BLOCK
NEG_BIG = -1e30


def _sb_logits(cfg, qh, kblk, col0, row_t, masked):
    z = _dot(qh, kblk, 1, 1)
    lk = -(jnp.maximum(z, 0.0) + jnp.log(1.0 + jnp.exp(-jnp.abs(z))))
    if masked:
        col_s = col0 + lax.broadcasted_iota(jnp.int32, (1, K_CHUNK), 1)
        mask = (col_s < row_t) & (col_s >= cfg.pad)
        z, lk = jnp.where(mask, z, NEG_BIG), jnp.where(mask, lk, 0.0)
    return z, lk


def _loop_ends_masked(n_plain_from, n_plain_to, n_total, step, carry):
    carry = lax.fori_loop(0, n_plain_from, step(True), carry)
    carry = lax.fori_loop(n_plain_from, n_plain_to, step(False), carry)
    return lax.fori_loop(n_plain_to, n_total, step(True), carry)


def _tri_sum(x, tri):
    hi, lo = _split_bf16(x)
    return _dot(hi, tri, 1, 0) + _dot(lo, tri, 1, 0)


def _grid_ends(dims):
    ids = [pl.program_id(a) for a in range(len(dims))]
    first = functools.reduce(lambda p, q: p & q, [i == 0 for i in ids])
    last = functools.reduce(lambda p, q: p & q, [i == n - 1 for i, n in zip(ids, dims)])
    return first, last


def attn_fwd(cfg, proj, gather_src=None):
    t = proj.shape[0]
    lp = cfg.lp
    nq = lp // Q_BLOCK
    nhp = cfg.d_attn // V7X_LANES
    cq, ck, cv = cfg.off_q // V7X_LANES, cfg.off_k // V7X_LANES, cfg.off_v // V7X_LANES
    scale = 1.0 / math.sqrt(cfg.head_dim)
    grid = (cfg.b_loc, nhp, nq)

    def body(*refs):
        if gather_src is None:
            q_ref, k_ref, v_ref, o_ref, lt_ref, kbf, vbf = refs
        else:
            q_ref, k_ref, v_ref, src_ref, o_ref, lt_ref, dst_ref, kbf, vbf = refs[:9]
            exchange = (src_ref, dst_ref) + tuple(refs[9:])
            first_step, last_step = _grid_ends(grid)
            pl.when(first_step)(lambda: gather_start(*exchange))
        qb = pl.program_id(2)

        @pl.when(qb == 0)
        def _():
            kbf[pl.ds(0, lp), :] = k_ref[...].astype(BF16)
            vbf[pl.ds(0, lp), :] = v_ref[...].astype(BF16)
            kbf[pl.ds(lp, Q_BLOCK), :] = jnp.zeros((Q_BLOCK, V7X_LANES), BF16)
            vbf[pl.ds(lp, Q_BLOCK), :] = jnp.zeros((Q_BLOCK, V7X_LANES), BF16)

        q = q_ref[...] * scale
        lane = lax.broadcasted_iota(jnp.int32, (1, V7X_LANES), 1)
        head0 = lane < cfg.head_dim
        qs = jnp.concatenate([jnp.where(head0, q, 0.0), jnp.where(head0, 0.0, q)], axis=0).astype(BF16)
        rows2 = lax.broadcasted_iota(jnp.int32, (2 * Q_BLOCK, 1), 0)
        row_t = qb * Q_BLOCK + jnp.where(rows2 >= Q_BLOCK, rows2 - Q_BLOCK, rows2)
        ri = lax.broadcasted_iota(jnp.int32, (K_CHUNK, K_CHUNK), 0)
        ci = lax.broadcasted_iota(jnp.int32, (K_CHUNK, K_CHUNK), 1)
        tri_from = (ri >= ci).astype(BF16)
        last = qb // 2

        def step(masked):
            def body(i, carry):
                z_p, lk_p, pre_pp, rows_pp, acc, r_carry = carry
                c3 = jnp.clip(last - i + 2, 0, last)
                off3 = pl.multiple_of(c3 * K_CHUNK, K_CHUNK)
                w = jnp.exp(pre_pp + r_carry)
                acc = acc + _dot(w.astype(BF16), vbf[pl.ds(off3, K_CHUNK), :], 1, 0)
                r_carry = r_carry + rows_pp
                pre_p = z_p + _tri_sum(lk_p, tri_from)
                rows_p = jnp.sum(lk_p, axis=1, keepdims=True)
                c1 = jnp.maximum(last - i, 0)
                off1 = pl.multiple_of(c1 * K_CHUNK, K_CHUNK)
                col0 = jnp.where(i <= last, c1 * K_CHUNK, cfg.lp + K_CHUNK)
                z, lk = _sb_logits(cfg, qs, kbf[pl.ds(off1, K_CHUNK), :], col0, row_t, masked)
                return z, lk, pre_p, rows_p, acc, r_carry
            return body

        zero = jnp.where(ri + ci < 0, 1.0, 0.0)
        big = zero + NEG_BIG
        zcol = jnp.where(rows2 < 0, 1.0, 0.0)
        res = _loop_ends_masked(1, jnp.maximum(last, 1), last + 3, step,
                                (big, zero, big, zcol, zcol + jnp.where(lane < 0, 1.0, 0.0), zcol))
        acc, r_tot = res[4], res[5]
        o_ref[...] = jnp.where(head0, acc[:Q_BLOCK], acc[Q_BLOCK:]).astype(BF16)
        lt_ref[...] = (jnp.where(lane == 0, r_tot[:Q_BLOCK], 0.0)
                       + jnp.where(lane == cfg.head_dim, r_tot[Q_BLOCK:], 0.0))
        if gather_src is not None:
            pl.when(last_step)(lambda: gather_finish(*exchange))

    oblk = pl.BlockSpec((Q_BLOCK, V7X_LANES), lambda b, hp, qb: (b * nq + qb, hp))
    hbm = pl.BlockSpec(memory_space=pl.ANY)
    comm = gather_src is not None
    return pl.pallas_call(
        body, name="attn_fwd_gather" if comm else "attn_fwd", grid=grid,
        in_specs=[pl.BlockSpec((Q_BLOCK, V7X_LANES), lambda b, hp, qb: (b * nq + qb, cq + hp)),
                  pl.BlockSpec((lp, V7X_LANES), lambda b, hp, qb: (b, ck + hp)),
                  pl.BlockSpec((lp, V7X_LANES), lambda b, hp, qb: (b, cv + hp))] + ([hbm] if comm else []),
        out_specs=[oblk, oblk] + ([hbm] if comm else []),
        out_shape=[_sds((t, cfg.d_attn), BF16), _sds((t, cfg.d_attn), F32)]
        + ([_sds((8,) + gather_src.shape, gather_src.dtype)] if comm else []),
        scratch_shapes=[pltpu.VMEM((lp + Q_BLOCK, V7X_LANES), BF16), pltpu.VMEM((lp + Q_BLOCK, V7X_LANES), BF16)]
        + (exchange_semaphores() if comm else []),
        compiler_params=_cp(("arbitrary", "arbitrary", "arbitrary")))(*((proj, proj, proj) + ((gather_src,) if comm else ())))


def attn_bwd(cfg, proj, lt, do, send=None):
    t = proj.shape[0]
    lp = cfg.lp
    nq = lp // Q_BLOCK
    nhp = cfg.d_attn // V7X_LANES
    cq, ck, cv = cfg.off_q // V7X_LANES, cfg.off_k // V7X_LANES, cfg.off_v // V7X_LANES
    scale = 1.0 / math.sqrt(cfg.head_dim)
    grid = (cfg.b_loc, nhp, nq)

    def body(*refs):
        if send is None:
            q_ref, k_ref, v_ref, lt_ref, do_ref, dq_ref, dk_ref, dv_ref, kbf, vbf, dk_acc, dv_acc = refs
        else:
            (q_ref, k_ref, v_ref, lt_ref, do_ref, src_ref, dq_ref, dk_ref, dv_ref, dst_ref,
             kbf, vbf, dk_acc, dv_acc) = refs[:14]
            exchange = (src_ref, dst_ref) + tuple(refs[14:])
            first_step, last_step = _grid_ends(grid)
            pl.when(first_step)(lambda: all_to_all_start(*exchange))
        qb = pl.program_id(2)

        @pl.when(qb == 0)
        def _():
            kbf[pl.ds(0, lp), :] = k_ref[...].astype(BF16)
            vbf[pl.ds(0, lp), :] = v_ref[...].astype(BF16)
            kbf[pl.ds(lp, Q_BLOCK), :] = jnp.zeros((Q_BLOCK, V7X_LANES), BF16)
            vbf[pl.ds(lp, Q_BLOCK), :] = jnp.zeros((Q_BLOCK, V7X_LANES), BF16)
            dk_acc[...] = jnp.zeros_like(dk_acc)
            dv_acc[...] = jnp.zeros_like(dv_acc)

        q = q_ref[...] * scale
        ltv = lt_ref[...]
        dov = do_ref[...]
        lane = lax.broadcasted_iota(jnp.int32, (1, V7X_LANES), 1)
        head0 = lane < cfg.head_dim
        qs = jnp.concatenate([jnp.where(head0, q, 0.0), jnp.where(head0, 0.0, q)], axis=0).astype(BF16)
        dos = jnp.concatenate([jnp.where(head0, dov, 0.0), jnp.where(head0, 0.0, dov)], axis=0).astype(BF16)
        lk_tot = jnp.concatenate(
            [jnp.sum(jnp.where(lane == 0, ltv, 0.0), axis=1, keepdims=True),
             jnp.sum(jnp.where(lane == cfg.head_dim, ltv, 0.0), axis=1, keepdims=True)], axis=0)
        rows2 = lax.broadcasted_iota(jnp.int32, (2 * Q_BLOCK, 1), 0)
        row_t = qb * Q_BLOCK + jnp.where(rows2 >= Q_BLOCK, rows2 - Q_BLOCK, rows2)
        ri = lax.broadcasted_iota(jnp.int32, (K_CHUNK, K_CHUNK), 0)
        ci = lax.broadcasted_iota(jnp.int32, (K_CHUNK, K_CHUNK), 1)
        tri_before = (ri < ci).astype(BF16)
        tri_upto = (ri <= ci).astype(BF16)
        n_chunks = qb // 2 + 1

        def step(masked):
            def body(i, carry):
                e_p, w_p, sg_p, dq_acc, lk_before, e_before = carry
                c2 = jnp.clip(i - 1, 0, n_chunks - 1)
                off2 = pl.multiple_of(c2 * K_CHUNK, K_CHUNK)
                dz = (e_p - sg_p * (e_before + _tri_sum(e_p, tri_upto))).astype(BF16)
                dq_acc = dq_acc + _dot(dz, kbf[pl.ds(off2, K_CHUNK), :], 1, 0)
                dk_acc[pl.ds(off2, K_CHUNK), :] += _dot(dz, qs, 0, 0)
                dv_acc[pl.ds(off2, K_CHUNK), :] += _dot(w_p, dos, 0, 0)
                e_before = e_before + jnp.sum(e_p, axis=1, keepdims=True)

                c1 = jnp.minimum(i, n_chunks - 1)
                off1 = pl.multiple_of(c1 * K_CHUNK, K_CHUNK)
                col0 = jnp.where(i < n_chunks, c1 * K_CHUNK, cfg.lp + K_CHUNK)
                z, lk = _sb_logits(cfg, qs, kbf[pl.ds(off1, K_CHUNK), :], col0, row_t, masked)
                dw = _dot(dos, vbf[pl.ds(off1, K_CHUNK), :], 1, 1)
                w = jnp.exp(z + (lk_tot - lk_before - _tri_sum(lk, tri_before)))
                sg = 1.0 - jnp.exp(lk)
                lk_before = lk_before + jnp.sum(lk, axis=1, keepdims=True)
                return w * dw, w.astype(BF16), sg, dq_acc, lk_before, e_before
            return body

        zero = jnp.where(ri + ci < 0, 1.0, 0.0)
        zcol = jnp.where(rows2 < 0, 1.0, 0.0)
        res = _loop_ends_masked(1, jnp.maximum(n_chunks - 1, 1), n_chunks + 1, step,
                                (zero, zero.astype(BF16), zero, zcol + jnp.where(lane < 0, 1.0, 0.0), zcol, zcol))
        dq = res[3]
        dq_ref[...] = (jnp.where(head0, dq[:Q_BLOCK], dq[Q_BLOCK:]) * scale).astype(BF16)

        @pl.when(qb == nq - 1)
        def _():
            dk_ref[...] = dk_acc[pl.ds(0, lp), :].astype(BF16)
            dv_ref[...] = dv_acc[pl.ds(0, lp), :].astype(BF16)

        if send is not None:
            pl.when(last_step)(lambda: all_to_all_finish(*exchange))

    qblk = pl.BlockSpec((Q_BLOCK, V7X_LANES), lambda b, hp, qb: (b * nq + qb, hp))
    seq = pl.BlockSpec((lp, V7X_LANES), lambda b, hp, qb: (b, hp))
    hbm = pl.BlockSpec(memory_space=pl.ANY)
    out = _sds((t, cfg.d_attn), BF16)
    comm = send is not None
    return pl.pallas_call(
        body, name="attn_bwd_exchange" if comm else "attn_bwd", grid=grid,
        in_specs=[pl.BlockSpec((Q_BLOCK, V7X_LANES), lambda b, hp, qb: (b * nq + qb, cq + hp)),
                  pl.BlockSpec((lp, V7X_LANES), lambda b, hp, qb: (b, ck + hp)),
                  pl.BlockSpec((lp, V7X_LANES), lambda b, hp, qb: (b, cv + hp)), qblk, qblk] + ([hbm] if comm else []),
        out_specs=[qblk, seq, seq] + ([hbm] if comm else []),
        out_shape=[out, out, out] + ([_sds(send.shape, send.dtype)] if comm else []),
        scratch_shapes=[pltpu.VMEM((lp + Q_BLOCK, V7X_LANES), BF16), pltpu.VMEM((lp + Q_BLOCK, V7X_LANES), BF16),
                        pltpu.VMEM((lp + Q_BLOCK, V7X_LANES), F32), pltpu.VMEM((lp + Q_BLOCK, V7X_LANES), F32)]
        + (exchange_semaphores() if comm else []),
        compiler_params=_cp(("arbitrary", "arbitrary", "arbitrary")))(
            *((proj, proj, proj, lt, do) + ((send,) if comm else ())))


N_POW = 8


def _discretize(lr, li, logdt, br_t, bi_t):
    dt = jnp.exp(logdt)
    mag = jnp.exp(lr * dt)
    ab_re = mag * jnp.cos(li * dt)
    ab_im = mag * jnp.sin(li * dt)
    den = lr * lr + li * li
    nr = ab_re - 1.0
    ni = ab_im
    coef_re = (nr * lr + ni * li) / den
    coef_im = (ni * lr - nr * li) / den
    bb_re = coef_re[None] * br_t - coef_im[None] * bi_t
    bb_im = coef_re[None] * bi_t + coef_im[None] * br_t
    return ab_re, ab_im, bb_re, bb_im


def ssm_param_fwd(cfg, lr, li, logdt, br_t, bi_t):
    g, n, c = cfg.groups, cfg.ssm_state, cfg.ssm_group

    def body(lr_ref, li_ref, dt_ref, br_ref, bi_ref, ar_ref, ai_ref, bbr_ref, bbi_ref, pr_ref, pi_ref,
             tr_ref, ti_ref):
        ab_re, ab_im, bb_re, bb_im = _discretize(lr_ref[...], li_ref[...], dt_ref[...], br_ref[...], bi_ref[...])
        ar_ref[...] = ab_re
        ai_ref[...] = ab_im
        bbr_ref[...] = bb_re
        bbi_ref[...] = bb_im
        pr, pi = ab_re, ab_im
        for k in range(N_POW):
            pr_ref[k] = pr
            pi_ref[k] = pi
            pr, pi = pr * pr - pi * pi, 2.0 * pr * pi
        cr, ci = ab_re, ab_im
        for j in range(SSM_CHUNK):
            tr_ref[j] = cr
            ti_ref[j] = ci
            cr, ci = cr * ab_re - ci * ab_im, cr * ab_im + ci * ab_re

    gn, cgn = _sds((g, n), F32), _sds((c, g, n), F32)
    return pl.pallas_call(
        body, name="ssm_param_fwd",
        out_shape=[gn, gn, cgn, cgn, _sds((N_POW, g, n), F32), _sds((N_POW, g, n), F32),
                   _sds((SSM_CHUNK, g, n), F32), _sds((SSM_CHUNK, g, n), F32)])(lr, li, logdt, br_t, bi_t)


def ssm_param_bwd(cfg, lr, li, logdt, br_t, bi_t, dar, dai, dbbr, dbbi):
    g, n, c = cfg.groups, cfg.ssm_state, cfg.ssm_group

    def body(lr_ref, li_ref, dt_ref, br_ref, bi_ref, dar_ref, dai_ref, dbr_ref, dbi_ref,
             olr_ref, oli_ref, odt_ref, obr_ref, obi_ref):
        _, vjp = jax.vjp(_discretize, lr_ref[...], li_ref[...], dt_ref[...], br_ref[...], bi_ref[...])
        dlr, dli, ddt, dbr, dbi = vjp((dar_ref[...], dai_ref[...], dbr_ref[...], dbi_ref[...]))
        olr_ref[...] = dlr
        oli_ref[...] = dli
        odt_ref[...] = ddt
        obr_ref[...] = dbr
        obi_ref[...] = dbi

    gn, cgn = _sds((g, n), F32), _sds((c, g, n), F32)
    return pl.pallas_call(body, name="ssm_param_bwd", out_shape=[gn, gn, _sds((g, 1), F32), cgn, cgn])(
        lr, li, logdt, br_t, bi_t, dar, dai, dbbr, dbbi)


def _scan(xr, xi, pr_ref, pi_ref, reverse):
    ct = xr.shape[0]
    rows = lax.broadcasted_iota(jnp.int32, (ct, 1), 0)
    d, k = 1, 0
    while d < ct:
        if reverse:
            sr, si = pltpu.roll(xr, ct - d, 0), pltpu.roll(xi, ct - d, 0)
            keep = rows < ct - d
        else:
            sr, si = pltpu.roll(xr, d, 0), pltpu.roll(xi, d, 0)
            keep = rows >= d
        sr = jnp.where(keep, sr, 0.0)
        si = jnp.where(keep, si, 0.0)
        ar = pr_ref[0, k:k + 1, :]
        ai = -pi_ref[0, k:k + 1, :] if reverse else pi_ref[0, k:k + 1, :]
        xr, xi = xr + ar * sr - ai * si, xi + ar * si + ai * sr
        d *= 2
        k += 1
    return xr, xi


def _gelu(x):
    k = math.sqrt(2.0 / math.pi)
    return 0.5 * x * (1.0 + jnp.tanh(k * (x + 0.044715 * x * x * x)))


def _gelu_grad(x):
    k = math.sqrt(2.0 / math.pi)
    th = jnp.tanh(k * (x + 0.044715 * x * x * x))
    return 0.5 * (1.0 + th) + 0.5 * x * (1.0 - th * th) * k * (1.0 + 3.0 * 0.044715 * x * x)


def ssm_fwd(cfg, proj, bb_re, bb_im, ct_re, ct_im, pw_re, pw_im, tab_re, tab_im, dskip):
    t = proj.shape[0]
    nsb = cfg.d_ssm // SSM_SB
    ns = SSM_SB // cfg.ssm_group * cfg.ssm_state
    ct = SSM_CHUNK
    nc = cfg.lp // ct

    def body(u_ref, bbr_ref, bbi_ref, ctr_ref, cti_ref, pr_ref, pi_ref, tr_ref, ti_ref, d_ref,
             sr_ref, si_ref, yt_ref, y2_ref, cr_scr, ci_scr):
        c = pl.program_id(2)

        @pl.when(c == 0)
        def _():
            cr_scr[...] = jnp.zeros_like(cr_scr)
            ci_scr[...] = jnp.zeros_like(ci_scr)

        u = u_ref[...]
        ub = u.astype(BF16)
        xr, xi = _scan(_dot(ub, bbr_ref[0], 1, 0), _dot(ub, bbi_ref[0], 1, 0), pr_ref, pi_ref, False)
        cr, ci = cr_scr[0:1, :], ci_scr[0:1, :]
        tr, ti = tr_ref[0], ti_ref[0]
        sr = xr + tr * cr - ti * ci
        si = xi + tr * ci + ti * cr
        sr_ref[...] = sr
        si_ref[...] = si
        cr_scr[0:1, :] = sr_ref[ct - 1:ct, :]
        ci_scr[0:1, :] = si_ref[ct - 1:ct, :]
        y = _dot(sr.astype(BF16), ctr_ref[0], 1, 0) - _dot(si.astype(BF16), cti_ref[0], 1, 0) + d_ref[...] * u
        yt_ref[...] = y
        y2_ref[...] = _gelu(y).astype(BF16)

    def mat(r, c_):
        return pl.BlockSpec((1, r, c_), lambda sb, b, c: (sb, 0, 0))

    ublk = pl.BlockSpec((ct, SSM_SB), lambda sb, b, c: (b * nc + c, sb))
    sblk = pl.BlockSpec((ct, ns), lambda sb, b, c: (b * nc + c, sb))
    return pl.pallas_call(
        body, name="ssm_fwd", grid=(nsb, cfg.b_loc, nc),
        in_specs=[ublk, mat(SSM_SB, ns), mat(SSM_SB, ns), mat(ns, SSM_SB), mat(ns, SSM_SB), mat(N_POW, ns),
                  mat(N_POW, ns), mat(ct, ns), mat(ct, ns), pl.BlockSpec((1, SSM_SB), lambda sb, b, c: (0, sb))],
        out_specs=[sblk, sblk, ublk, ublk],
        out_shape=[_sds((t, cfg.n_state), F32), _sds((t, cfg.n_state), F32), _sds((t, cfg.d_ssm), F32),
                   _sds((t, cfg.d_ssm), BF16)],
        scratch_shapes=[pltpu.VMEM((8, ns), F32), pltpu.VMEM((8, ns), F32)],
        compiler_params=_cp(("parallel", "arbitrary", "arbitrary")))(
            proj, bb_re, bb_im, ct_re, ct_im, pw_re, pw_im, tab_re, tab_im, dskip.reshape(1, cfg.d_ssm))


def ssm_bwd(cfg, dy2, ytot, proj, s_re, s_im, cf_re, cf_im, bbt_re, bbt_im, pw_re, pw_im, tabr_re, tabr_im, dskip):
    t = proj.shape[0]
    nsb = cfg.d_ssm // SSM_SB
    ns = SSM_SB // cfg.ssm_group * cfg.ssm_state
    ct = SSM_CHUNK
    nc = cfg.lp // ct

    def body(dy_ref, yt_ref, u_ref, sr_ref, si_ref, spr_ref, spi_ref, cfr_ref, cfi_ref, btr_ref, bti_ref,
             pr_ref, pi_ref, tr_ref, ti_ref, d_ref,
             du_ref, dctr_ref, dcti_ref, dbbr_ref, dbbi_ref, dar_ref, dai_ref, dd_ref,
             cr_scr, ci_scr, ar_scr, ai_scr):
        b, c = pl.program_id(1), pl.program_id(2)
        chunk = nc - 1 - c

        @pl.when((b == 0) & (c == 0))
        def _():
            for r in (dctr_ref, dcti_ref, dbbr_ref, dbbi_ref, dar_ref, dai_ref, dd_ref):
                r[...] = jnp.zeros_like(r)

        @pl.when(c == 0)
        def _():
            cr_scr[...] = jnp.zeros_like(cr_scr)
            ci_scr[...] = jnp.zeros_like(ci_scr)

        u = u_ref[...]
        dyt = dy_ref[...] * _gelu_grad(yt_ref[...])
        dytb = dyt.astype(BF16)
        xr, xi = _scan(_dot(dytb, cfr_ref[0], 1, 0), -_dot(dytb, cfi_ref[0], 1, 0), pr_ref, pi_ref, True)
        cr, ci = cr_scr[0:1, :], ci_scr[0:1, :]
        tr, ti = tr_ref[0], -ti_ref[0]
        a_re = xr + tr * cr - ti * ci
        a_im = xi + tr * ci + ti * cr
        ar_scr[...] = a_re
        ai_scr[...] = a_im
        cr_scr[0:1, :] = ar_scr[0:1, :]
        ci_scr[0:1, :] = ai_scr[0:1, :]

        sr, si = sr_ref[...], si_ref[...]
        rows = lax.broadcasted_iota(jnp.int32, (ct, 1), 0)
        keep_prev = chunk > 0
        pr_last = jnp.where(keep_prev, spr_ref[7:8, :], 0.0)
        pi_last = jnp.where(keep_prev, spi_ref[7:8, :], 0.0)
        sp_re = jnp.where(rows == 0, pr_last, pltpu.roll(sr, 1, 0))
        sp_im = jnp.where(rows == 0, pi_last, pltpu.roll(si, 1, 0))
        dar_ref[0] += jnp.sum(a_re * sp_re + a_im * sp_im, axis=0, keepdims=True)
        dai_ref[0] += jnp.sum(a_im * sp_re - a_re * sp_im, axis=0, keepdims=True)
        dctr_ref[0] += _dot(sr.astype(BF16), dytb, 0, 0)
        dcti_ref[0] += -_dot(si.astype(BF16), dytb, 0, 0)
        ub = u.astype(BF16)
        arb, aib = a_re.astype(BF16), a_im.astype(BF16)
        dbbr_ref[0] += _dot(ub, arb, 0, 0)
        dbbi_ref[0] += _dot(ub, aib, 0, 0)
        du = dyt * d_ref[...] + _dot(arb, btr_ref[0], 1, 0) + _dot(aib, bti_ref[0], 1, 0)
        valid = (chunk * ct + rows) >= cfg.pad
        du_ref[...] = jnp.where(valid, du, 0.0).astype(BF16)
        dd_ref[...] += jnp.sum(dyt * u, axis=0, keepdims=True)

    def mat(r, c_):
        return pl.BlockSpec((1, r, c_), lambda sb, b, c: (sb, 0, 0))

    lp8 = cfg.lp // 8
    ublk = pl.BlockSpec((ct, SSM_SB), lambda sb, b, c: (b * nc + nc - 1 - c, sb))
    sblk = pl.BlockSpec((ct, ns), lambda sb, b, c: (b * nc + nc - 1 - c, sb))
    sprev = pl.BlockSpec((8, ns), lambda sb, b, c: (jnp.maximum(b * lp8 + (nc - 1 - c) * (ct // 8) - 1, 0), sb))
    dvec = pl.BlockSpec((1, SSM_SB), lambda sb, b, c: (0, sb))
    return pl.pallas_call(
        body, name="ssm_bwd", grid=(nsb, cfg.b_loc, nc),
        in_specs=[ublk, ublk, ublk, sblk, sblk, sprev, sprev, mat(SSM_SB, ns), mat(SSM_SB, ns), mat(ns, SSM_SB),
                  mat(ns, SSM_SB), mat(N_POW, ns), mat(N_POW, ns), mat(ct, ns), mat(ct, ns), dvec],
        out_specs=[ublk, mat(ns, SSM_SB), mat(ns, SSM_SB), mat(SSM_SB, ns), mat(SSM_SB, ns), mat(1, ns), mat(1, ns),
                   dvec],
        out_shape=[_sds((t, cfg.d_ssm), BF16), _sds((nsb, ns, SSM_SB), F32), _sds((nsb, ns, SSM_SB), F32),
                   _sds((nsb, SSM_SB, ns), F32), _sds((nsb, SSM_SB, ns), F32), _sds((nsb, 1, ns), F32),
                   _sds((nsb, 1, ns), F32), _sds((1, cfg.d_ssm), F32)],
        scratch_shapes=[pltpu.VMEM((8, ns), F32), pltpu.VMEM((8, ns), F32), pltpu.VMEM((ct, ns), F32),
                        pltpu.VMEM((ct, ns), F32)],
        compiler_params=_cp(("arbitrary", "arbitrary", "arbitrary")))(
            dy2, ytot, proj, s_re, s_im, s_re, s_im, cf_re, cf_im, bbt_re, bbt_im, pw_re, pw_im, tabr_re, tabr_im,
            dskip.reshape(1, cfg.d_ssm))


def _ssm_tables(cfg, lam_re, lam_im, log_dt, b_re, b_im, c_re, c_im):
    g, n, c = cfg.groups, cfg.ssm_state, cfg.ssm_group
    gsb = SSM_SB // c
    nsb = cfg.d_ssm // SSM_SB
    ns = gsb * n
    br_t, bi_t = jnp.transpose(b_re, (2, 0, 1)), jnp.transpose(b_im, (2, 0, 1))
    prm = (lam_re, lam_im, log_dt.reshape(g, 1), br_t, bi_t)
    _, _, bbr, bbi, pr, pi, tr, ti = ssm_param_fwd(cfg, *prm)
    eye = jnp.eye(gsb, dtype=F32)

    def bdiag_b(x):
        return jnp.einsum('csgn,gh->sgchn', x.reshape(c, nsb, gsb, n), eye).reshape(nsb, SSM_SB, ns)

    def bdiag_c(x):
        return jnp.einsum('sgcn,gh->sgchn', x.reshape(nsb, gsb, c, n), eye).reshape(nsb, SSM_SB, ns)

    def lanes(x):
        return jnp.transpose(x.reshape(x.shape[0], nsb, ns), (1, 0, 2))

    bb_re, bb_im = bdiag_b(bbr), bdiag_b(bbi)
    cf_re, cf_im = bdiag_c(c_re), bdiag_c(c_im)
    sw = lambda x: jnp.swapaxes(x, 1, 2)
    return dict(prm=prm, bb_re=bb_re.astype(BF16), bb_im=bb_im.astype(BF16), bbt_re=sw(bb_re).astype(BF16),
                bbt_im=sw(bb_im).astype(BF16), cf_re=cf_re.astype(BF16), cf_im=cf_im.astype(BF16),
                ct_re=sw(cf_re).astype(BF16), ct_im=sw(cf_im).astype(BF16), pw_re=lanes(pr), pw_im=lanes(pi),
                tab_re=lanes(tr), tab_im=lanes(ti), tabr_re=lanes(tr[::-1]), tabr_im=lanes(ti[::-1]))


def _ssm_param_grads(cfg, tabs, dct_re, dct_im, dbb_re, dbb_im, dab_re, dab_im):
    g, n, c = cfg.groups, cfg.ssm_state, cfg.ssm_group
    gsb = SSM_SB // c
    nsb = cfg.d_ssm // SSM_SB

    def diag_b(x):
        d = jnp.einsum('sgcgn->sgcn', x.reshape(nsb, gsb, c, gsb, n))
        return jnp.transpose(d.reshape(g, c, n), (1, 0, 2))

    def diag_c(x):
        d = jnp.einsum('sgngc->sgcn', x.reshape(nsb, gsb, n, gsb, c))
        return d.reshape(g, c, n)

    dlr, dli, ddt, dbr_t, dbi_t = ssm_param_bwd(cfg, *tabs['prm'], dab_re.reshape(g, n), dab_im.reshape(g, n),
                                                diag_b(dbb_re), diag_b(dbb_im))
    return dict(ssm_lam_re=dlr, ssm_lam_im=dli, ssm_log_dt=ddt.reshape(g),
                ssm_b_re=jnp.transpose(dbr_t, (1, 2, 0)), ssm_b_im=jnp.transpose(dbi_t, (1, 2, 0)),
                ssm_c_re=diag_c(dct_re), ssm_c_im=diag_c(dct_im))


def _mm_tiles(t, n, k):
    tm = _tile(t, 1056 if k <= 3072 else 528, 16)
    tn = _tile(n, 512, 128)
    return tm, tn


def _nt(a, wt, name, out_dtype=F32):
    tm, tn = _mm_tiles(a.shape[0], wt.shape[0], a.shape[1])
    return mm([(a, wt, 0)], nt=True, n=wt.shape[0], tm=tm, tn=tn, out_dtype=out_dtype, name=name)


def _nn(pairs, n, name, res=None, res_scale=1.0, out_dtype=F32):
    kmax = max(p[0].shape[1] for p in pairs) * len(pairs)
    tm, tn = _mm_tiles(pairs[0][0].shape[0], n, kmax)
    return mm(pairs, nt=False, n=n, tm=tm, tn=tn, out_dtype=out_dtype, name=name, res=res, res_scale=res_scale)


def local_step(cfg, x, target, rep, ex):
    d, lp, bsz, t = cfg.d_model, cfg.lp, cfg.b_loc, cfg.tokens
    meta = jnp.broadcast_to(ex.meta_tokens[None], (bsz, cfg.n_meta, d))
    h = jnp.concatenate([jnp.zeros((bsz, cfg.pad, d), F32), meta, x], axis=1).reshape(t, d)

    saved = []
    for l in range(cfg.depth):
        w = ex.weights(l)
        r = {k: v[l] for k, v in rep.items() if k != 'final_norm'}
        s = {'h0': h}
        s['xn1'] = rms_fwd(h, r['ffn1_norm'], "rms_fwd")
        s['a1'], s['b1'], s['act1'] = dual_mm_act(s['xn1'], w['w13t_1'], kind='swiglu', name="ffn_up")
        h = _nn([(s['act1'], w['w2_1'], 0)], d, "ffn_down", res=h, res_scale=0.5)
        s['h1'] = h
        s['xnm'] = rms_fwd(h, r['mix_norm'], "rms_fwd")
        proj = _nt(s['xnm'], w['w_int'], "proj_in")
        s['proj'] = proj
        tabs = _ssm_tables(cfg, r['ssm_lam_re'], r['ssm_lam_im'], r['ssm_log_dt'], r['ssm_b_re'], r['ssm_b_im'],
                           r['ssm_c_re'], r['ssm_c_im'])
        s['tabs'] = tabs
        s['s_re'], s['s_im'], s['ytot'], s['y2'] = ssm_fwd(
            cfg, proj, tabs['bb_re'], tabs['bb_im'], tabs['ct_re'], tabs['ct_im'], tabs['pw_re'], tabs['pw_im'],
            tabs['tab_re'], tabs['tab_im'], r['ssm_d'])
        s['ga'], s['gg'], o_ssm = dual_mm_act(s['y2'], w['w_glut'], kind='glu', name="ssm_glu")
        s['cpre'], s['hn'] = conv_fwd(cfg, proj, w['conv_w'], r['conv_b'], r['conv_ln_g'], r['conv_ln_b'])
        s['o_conv'] = _nt(s['hn'], w['conv_w_outt'], "branch_out")
        src = ex.gather_src(l)
        s['o'], s['lt'], *arrived = attn_fwd(cfg, proj, src)
        if src is not None:
            ex.gathered(l, arrived[0])
        s['o_attn'] = _nt(s['o'], w['attn_w_ot'], "branch_out")
        s['merged'] = merge_fwd(cfg, proj, o_ssm, s['o_conv'], s['o_attn'])
        h = _nn([(s['merged'], w['w_out'], 0)], d, "mix_out", res=h, res_scale=1.0)
        s['h2'] = h
        s['xn2'] = rms_fwd(h, r['ffn2_norm'], "rms_fwd")
        s['a2'], s['b2'], s['act2'] = dual_mm_act(s['xn2'], w['w13t_2'], kind='swiglu', name="ffn_up")
        h = _nn([(s['act2'], w['w2_2'], 0)], d, "ffn_down", res=h, res_scale=0.5)
        saved.append((w, r, s))

    dh3, dh3_16, dgf, loss_sq = final_fwd_bwd(cfg, h.reshape(bsz, lp, d), rep['final_norm'], target)
    dh, dh16 = dh3.reshape(t, d), dh3_16.reshape(t, d)
    loss = 0.5 * loss_sq[0, 0] / d

    gl = []
    for l in reversed(range(cfg.depth)):
        w, r, s = saved[l]
        g = {}

        def ffn_bwd(dh, dh16, tag, a, b, act, xn, hin, norm):
            da, db = ffn_down_bwd(dh16, w['w2_' + tag], a, b, "ffn_down_bwd")
            g['w2_' + tag] = mm_tn(act, dh16, name="ffn_dw2", scale=0.5)
            dxn = _nn([(da, w['w13t_' + tag], 0), (db, w['w13t_' + tag], 1)], d, "ffn_dxn")
            g['w13t_' + tag] = jnp.concatenate([mm_tn(da, xn, name="ffn_dw13"), mm_tn(db, xn, name="ffn_dw13")], 0)
            return rms_bwd(dxn, hin, norm, dh, "rms_bwd")

        dh, dh16, dg2 = ffn_bwd(dh, dh16, '2', s['a2'], s['b2'], s['act2'], s['xn2'], s['h2'], r['ffn2_norm'])
        g['ffn2_norm'] = dg2.reshape(d)

        proj, tabs = s['proj'], s['tabs']
        dmerged = _nt(dh16, w['w_out'], "mix_dmerged")
        g['w_out'] = mm_tn(s['merged'], dh16, name="mix_dwout")
        dg0, dg1, dg2_, dga, dgg, do_conv, do_attn = merge_bwd(cfg, dmerged, proj, s['ga'], s['gg'], s['o_conv'],
                                                               s['o_attn'])
        dy2 = _nn([(dga, w['w_glut'], 0), (dgg, w['w_glut'], 1)], cfg.d_ssm, "ssm_dy2")
        g['w_glut'] = jnp.concatenate([mm_tn(dga, s['y2'], name="ssm_dwglu"), mm_tn(dgg, s['y2'], name="ssm_dwglu")], 0)
        du, dct_re, dct_im, dbb_re, dbb_im, dab_re, dab_im, dd = ssm_bwd(
            cfg, dy2, s['ytot'], proj, s['s_re'], s['s_im'], tabs['cf_re'], tabs['cf_im'], tabs['bbt_re'],
            tabs['bbt_im'], tabs['pw_re'], tabs['pw_im'], tabs['tabr_re'], tabs['tabr_im'], r['ssm_d'])
        g.update(_ssm_param_grads(cfg, tabs, dct_re, dct_im, dbb_re, dbb_im, dab_re, dab_im))
        g['ssm_d'] = dd.reshape(cfg.d_ssm)

        dhn = _nn([(do_conv, w['conv_w_outt'], 0)], cfg.d_conv, "branch_din")
        g['conv_w_outt'] = mm_tn(do_conv, s['hn'], name="branch_dw")
        dxa, dxg, dcw, dcb, dlg, dlb = conv_bwd(cfg, dhn, s['cpre'], proj, w['conv_w'], r['conv_ln_g'],
                                                r['conv_ln_b'])
        g['conv_w'], g['conv_b'] = dcw, dcb.reshape(cfg.d_conv)
        g['conv_ln_g'], g['conv_ln_b'] = dlg.reshape(cfg.d_conv), dlb.reshape(cfg.d_conv)

        do = _nn([(do_attn, w['attn_w_ot'], 0)], cfg.d_attn, "branch_din")
        g['attn_w_ot'] = mm_tn(do_attn, s['o'], name="branch_dw")
        send = ex.send_src(l)
        dq, dk, dv, *arrived = attn_bwd(cfg, proj, s['lt'], do, send)
        if send is not None:
            ex.received(l, arrived[0])

        dproj = jnp.concatenate([du, dxa, dxg, dq, dk, dv, dg0, dg1, dg2_],
                                axis=1)
        dxn = _nn([(dproj, w['w_int'], 0)], d, "proj_dxn")
        g['w_int'] = mm_tn(dproj, s['xnm'], name="proj_dw")
        dh, dh16, dgm = rms_bwd(dxn, s['h1'], r['mix_norm'], dh, "rms_bwd")
        g['mix_norm'] = dgm.reshape(d)

        dh, dh16, dg1n = ffn_bwd(dh, dh16, '1', s['a1'], s['b1'], s['act1'], s['xn1'], s['h0'], r['ffn1_norm'])
        g['ffn1_norm'] = dg1n.reshape(d)
        ex.matrix_grads(l, {name: g.pop(name) for name, _, _ in PACKED})
        gl.append(g)

    gl = gl[::-1]
    dh0 = dh.reshape(bsz, lp, d)
    grad_x = dh0[:, Q_BLOCK:]
    grads = {k: jnp.stack([g[k] for g in gl]) for k in gl[0]}
    grads['meta_tokens'] = jnp.sum(dh0[:, cfg.pad:Q_BLOCK], axis=0)
    grads['final_norm'] = dgf.reshape(d)
    return loss, grad_x, grads


def _me():
    return lax.axis_index("x"), lax.axis_index("y"), lax.axis_index("c")


def _flat(px, py, pc):
    return 4 * px + 2 * py + pc


def all_gather_blocks(block, name):
    r, c_ = block.shape

    def body(x_ref, out_ref, send_sems, recv_sems, local_sem):
        gather_start(x_ref, out_ref, send_sems, recv_sems, local_sem)
        gather_finish(x_ref, out_ref, send_sems, recv_sems, local_sem)

    return pl.pallas_call(
        body, name=name, out_shape=_sds((8, r, c_), block.dtype),
        in_specs=[pl.BlockSpec(memory_space=pl.ANY)], out_specs=pl.BlockSpec(memory_space=pl.ANY),
        scratch_shapes=exchange_semaphores())(block)


def exchange_semaphores():
    return [pltpu.SemaphoreType.DMA((7,)), pltpu.SemaphoreType.DMA((7,)), pltpu.SemaphoreType.DMA(())]


def _gather_copies(x_ref, out_ref, send_sems, recv_sems, local_sem):
    x, y, c = _me()
    me, sibling = (x, y, c), (x, y, 1 - c)
    chips = [(1 - x, y), (x, 1 - y), (1 - x, 1 - y)]

    def slot(px, py, pc):
        return out_ref.at[_flat(px, py, pc)]

    def copy(k, blk, to, src=None):
        return pltpu.make_async_remote_copy(
            src_ref=slot(*blk) if src is None else src, dst_ref=slot(*blk), send_sem=send_sems.at[k],
            recv_sem=recv_sems.at[k], device_id=to, device_id_type=pl.DeviceIdType.MESH)

    mine = pltpu.make_async_copy(x_ref, slot(*me), local_sem)
    first = [copy(0, me, sibling, src=x_ref)] + [copy(1 + j, me, (*chip, c), src=x_ref) for j, chip in enumerate(chips)]
    passed = [copy(4 + j, (*chip, c), sibling) for j, chip in enumerate(chips)]
    over_ici = [copy(1 + j, (*chip, c), me) for j, chip in enumerate(chips)]
    from_sibling = [copy(0, sibling, me)] + [copy(4 + j, (*chip, 1 - c), me) for j, chip in enumerate(chips)]
    return mine, first, passed, over_ici, from_sibling


def gather_start(*refs):
    mine, first, _, _, _ = _gather_copies(*refs)
    mine.start()
    for cp in first:
        cp.start()


def gather_finish(*refs):
    mine, first, passed, over_ici, from_sibling = _gather_copies(*refs)
    for arrival, forward in zip(over_ici, passed):
        arrival.wait_recv()
        forward.start()
    for arrival in from_sibling:
        arrival.wait_recv()
    for cp in first + passed:
        cp.wait_send()
    mine.wait()


def all_to_all_blocks(send, name):
    _, r, c_ = send.shape

    def body(s_ref, out_ref, send_sems, recv_sems, local_sem):
        all_to_all_start(s_ref, out_ref, send_sems, recv_sems, local_sem)
        all_to_all_finish(s_ref, out_ref, send_sems, recv_sems, local_sem)

    return pl.pallas_call(
        body, name=name, out_shape=_sds((8, r, c_), send.dtype),
        in_specs=[pl.BlockSpec(memory_space=pl.ANY)], out_specs=pl.BlockSpec(memory_space=pl.ANY),
        scratch_shapes=exchange_semaphores())(send)


def _all_to_all_copies(s_ref, out_ref, send_sems, recv_sems, local_sem):
    x, y, c = _me()
    me = _flat(x, y, c)
    mine = pltpu.make_async_copy(s_ref.at[me], out_ref.at[me], local_sem)
    copies = []
    for rel in range(1, 8):
        px = 1 - x if rel & 4 else x
        py = 1 - y if rel & 2 else y
        pc = 1 - c if rel & 1 else c
        copies.append(pltpu.make_async_remote_copy(
            src_ref=s_ref.at[_flat(px, py, pc)], dst_ref=out_ref.at[me], send_sem=send_sems.at[rel - 1],
            recv_sem=recv_sems.at[rel - 1], device_id=(px, py, pc), device_id_type=pl.DeviceIdType.MESH))
    return mine, copies


def all_to_all_start(*refs):
    mine, copies = _all_to_all_copies(*refs)
    mine.start()
    for cp in copies:
        cp.start()


def all_to_all_finish(*refs):
    mine, copies = _all_to_all_copies(*refs)
    for cp in copies:
        cp.wait_recv()
    for cp in copies:
        cp.wait_send()
    mine.wait()


def reduce_blocks(recv, name):
    nsrc, r, c_ = recv.shape
    tr = _tile(r, 128, 16)

    def body(x_ref, o_ref):
        acc = x_ref[0].astype(F32)
        for s in range(1, nsrc):
            acc = acc + x_ref[s].astype(F32)
        o_ref[...] = acc

    return pl.pallas_call(
        body, name=name, grid=(r // tr,),
        in_specs=[pl.BlockSpec((nsrc, tr, c_), lambda i: (0, i, 0))], out_specs=pl.BlockSpec((tr, c_), lambda i: (i, 0)),
        out_shape=_sds((r, c_), F32), compiler_params=_cp(("parallel",)))(recv)


def adamw(w, g, m, v):
    shape = w.shape
    size = math.prod(shape)
    if shape[-1] < V7X_LANES and size % 1024 == 0:
        view = (size // 1024, 1024)
    else:
        view = (size // shape[-1], shape[-1])
    rows, cols = view
    tr = _tile(rows, max(8, (1 << 19) // cols // 8 * 8), 8)
    c1 = 1.0 - ADAM_B1 ** ADAM_STEP
    c2 = 1.0 - ADAM_B2 ** ADAM_STEP

    def body(w_ref, g_ref, m_ref, v_ref, d_ref, nm_ref, nv_ref):
        gv = g_ref[...]
        nm = ADAM_B1 * m_ref[...] + (1.0 - ADAM_B1) * gv
        nv = ADAM_B2 * v_ref[...] + (1.0 - ADAM_B2) * (gv * gv)
        nm_ref[...] = nm
        nv_ref[...] = nv
        d_ref[...] = -ADAM_LR * ((nm / c1) / (jnp.sqrt(nv / c2) + ADAM_EPS) + ADAM_WD * w_ref[...])

    blk = pl.BlockSpec((tr, cols), lambda i: (i, 0))
    out = _sds(view, F32)
    res = pl.pallas_call(
        body, name="adamw", grid=(rows // tr,), in_specs=[blk] * 4, out_specs=[blk] * 3, out_shape=[out] * 3,
        compiler_params=_cp(("parallel",)))(w.reshape(view), g.reshape(view), m.reshape(view), v.reshape(view))
    return tuple(a.reshape(shape) for a in res)


PACKED = [('w13t_1', 'ffn1_w13', True), ('w2_1', 'ffn1_w2', False), ('w_int', 'w_in', True), ('w_out', 'w_out', False),
          ('w13t_2', 'ffn2_w13', True), ('w2_2', 'ffn2_w2', False), ('w_glut', 'ssm_w_glu', True),
          ('conv_w_outt', 'conv_w_out', True), ('attn_w_ot', 'attn_w_o', True)]


def _rows_of(a, d):
    return a.reshape(a.shape[0] * a.shape[1] // d, d)


def _pad_rows(flat, d, mult):
    n = flat.shape[-1]
    rows = -(-n // d)
    rows = -(-rows // mult) * mult
    flat = jnp.pad(flat, [(0, 0)] * (flat.ndim - 1) + [(0, rows * d - n)])
    return flat.reshape(flat.shape[:-1] + (rows, d))


def pack_weight_shards(cfg, wts):
    d = cfg.d_model
    blocks, layout = [], []
    for l in range(cfg.depth):
        parts = []
        for name, src, tr in PACKED:
            a = wts[src][l]
            a = (a.T if tr else a).astype(BF16)
            if l == 0:
                layout.append((name, a.shape))
            parts.append(_rows_of(a, d))
        blocks.append(jnp.concatenate(parts, axis=0))
    small = jnp.concatenate([wts['conv_w'].reshape(-1), wts['meta_tokens'].reshape(-1)])
    return blocks, _pad_rows(small, d, 8), layout


def unpack_gathered_layer(cfg, gathered, layout):
    d = cfg.d_model
    out = {}
    off = 0
    for name, shape in layout:
        rows = shape[0] * shape[1] // d
        out[name] = gathered[:, off:off + rows].reshape(8 * shape[0], shape[1])
        off += rows
    return out


def unpack_gathered_small(gathered_small, wts):
    n_cw, n_mt = wts['conv_w'].size, wts['meta_tokens'].size
    small = gathered_small.reshape(8, -1)
    cw = small[:, :n_cw].reshape((8,) + wts['conv_w'].shape)
    conv_w = jnp.transpose(cw, (1, 2, 0, 3)).reshape(cw.shape[1], cw.shape[2], -1)
    mt = small[:, n_cw:n_cw + n_mt].reshape((8,) + wts['meta_tokens'].shape)
    return conv_w, jnp.transpose(mt, (1, 0, 2)).reshape(mt.shape[1], -1)


def pack_layer_grads(cfg, g, layout):
    d = cfg.d_model
    parts = [g[name].reshape(8, shape[0] * shape[1] // d, d).astype(BF16) for name, shape in layout]
    return jnp.concatenate(parts, axis=1)


def pack_small_grads(cfg, grads, rep_names):
    cw = grads['conv_w']
    cw = jnp.transpose(cw.reshape(cw.shape[0], cw.shape[1], 8, -1), (2, 0, 1, 3)).reshape(8, -1)
    mt = grads['meta_tokens']
    mt = jnp.transpose(mt.reshape(mt.shape[0], 8, -1), (1, 0, 2)).reshape(8, -1)
    repl = jnp.concatenate([grads[k].reshape(-1) for k in rep_names])
    small = jnp.concatenate([cw, mt, jnp.broadcast_to(repl[None], (8, repl.shape[0]))], axis=1)
    return _pad_rows(small, cfg.d_model, 128)


def unpack_grads(cfg, gsums, gsmall, layout, wts, rep_names):
    d = cfg.d_model
    acc = {}
    for gsum in gsums:
        off = 0
        for name, shape in layout:
            rows = shape[0] * shape[1] // d
            acc.setdefault(name, []).append(gsum[off:off + rows].reshape(shape))
            off += rows
    out = {}
    for name, src, tr in PACKED:
        a = jnp.stack(acc[name])
        out[src] = jnp.swapaxes(a, 1, 2) if tr else a
    flat = gsmall.reshape(-1)
    pos = 0
    for k in ['conv_w', 'meta_tokens'] + rep_names:
        n = wts[k].size
        out[k] = flat[pos:pos + n].reshape(wts[k].shape)
        pos += n
    return out


class WeightExchange:
    def __init__(self, cfg, wts):
        self.cfg = cfg
        self.blocks, small, self.layout = pack_weight_shards(cfg, wts)
        self.conv_w, self.meta_tokens = unpack_gathered_small(all_gather_blocks(small, "all_gather_small_weights"), wts)
        self.full = {0: unpack_gathered_layer(cfg, all_gather_blocks(self.blocks[0], "all_gather_weights"), self.layout)}
        self.to_send, self.arrived = {}, {}

    def weights(self, l):
        return dict(self.full[l], conv_w=self.conv_w[l])

    def gather_src(self, l):
        return self.blocks[l + 1] if l + 1 < self.cfg.depth else None

    def gathered(self, l, blocks):
        self.full[l + 1] = unpack_gathered_layer(self.cfg, blocks, self.layout)

    def matrix_grads(self, l, g):
        self.to_send[l] = pack_layer_grads(self.cfg, g, self.layout)

    def send_src(self, l):
        return self.to_send.get(l + 1)

    def received(self, l, blocks):
        self.arrived[l + 1] = blocks

    def finish(self, grads, wts):
        self.arrived[0] = all_to_all_blocks(self.to_send[0], "all_to_all_grads")
        gsums = [reduce_blocks(self.arrived[l], "reduce_grads") for l in range(self.cfg.depth)]
        small = all_to_all_blocks(pack_small_grads(self.cfg, grads, REPLICATED), "all_to_all_small_grads")
        return unpack_grads(self.cfg, gsums, reduce_blocks(small, "reduce_small_grads"), self.layout, wts, REPLICATED)


def train_step(cfg, x, target, wts, ms, vs):
    ex = WeightExchange(cfg, wts)
    rep = {k: wts[k] for k in REPLICATED}
    loss, grad_x, grads = local_step(cfg, x, target, rep, ex)
    gw = ex.finish(grads, wts)
    loss = lax.psum(loss, ("x", "y", "c"))
    deltas, new_m, new_v = {}, {}, {}
    for k in WEIGHT_NAMES:
        deltas[k], new_m[k], new_v[k] = adamw(wts[k], gw[k], ms[k], vs[k])
    return (loss, grad_x, *[gw[k] for k in WEIGHT_NAMES], *[deltas[k] for k in WEIGHT_NAMES],
            *[new_m[k] for k in WEIGHT_NAMES], *[new_v[k] for k in WEIGHT_NAMES])


def kernel(x, meta_tokens, ffn1_norm, ffn1_w13, ffn1_w2, mix_norm, w_in, ssm_lam_re, ssm_lam_im, ssm_log_dt, ssm_b_re, ssm_b_im, ssm_c_re, ssm_c_im, ssm_d, ssm_w_glu, conv_w, conv_b, conv_ln_g, conv_ln_b, conv_w_out, attn_w_o, w_out, ffn2_norm, ffn2_w13, ffn2_w2, final_norm, loss_target, m_meta_tokens, m_ffn1_norm, m_ffn1_w13, m_ffn1_w2, m_mix_norm, m_w_in, m_ssm_lam_re, m_ssm_lam_im, m_ssm_log_dt, m_ssm_b_re, m_ssm_b_im, m_ssm_c_re, m_ssm_c_im, m_ssm_d, m_ssm_w_glu, m_conv_w, m_conv_b, m_conv_ln_g, m_conv_ln_b, m_conv_w_out, m_attn_w_o, m_w_out, m_ffn2_norm, m_ffn2_w13, m_ffn2_w2, m_final_norm, v_meta_tokens, v_ffn1_norm, v_ffn1_w13, v_ffn1_w2, v_mix_norm, v_w_in, v_ssm_lam_re, v_ssm_lam_im, v_ssm_log_dt, v_ssm_b_re, v_ssm_b_im, v_ssm_c_re, v_ssm_c_im, v_ssm_d, v_ssm_w_glu, v_conv_w, v_conv_b, v_conv_ln_g, v_conv_ln_b, v_conv_w_out, v_attn_w_o, v_w_out, v_ffn2_norm, v_ffn2_w13, v_ffn2_w2, v_final_norm):
    given = dict(locals())
    wts = {k: given[k] for k in WEIGHT_NAMES}
    ms = {k: given["m_" + k] for k in WEIGHT_NAMES}
    vs = {k: given["v_" + k] for k in WEIGHT_NAMES}
    return train_step(FULL, x, loss_target, wts, ms, vs)
```

```python
import functools
import math
from typing import NamedTuple

import jax
import jax.numpy as jnp
from jax import lax
from jax.experimental import pallas as pl
from jax.experimental.pallas import tpu as pltpu

F32 = jnp.float32
BF16 = jnp.bfloat16
RMS_EPS = 1e-6
LN_EPS = 1e-5
ADAM_LR = 0.001
ADAM_B1 = 0.9
ADAM_B2 = 0.999
ADAM_EPS = 1e-08
ADAM_WD = 0.01
ADAM_STEP = 10
V7X_LANES = 128
V7X_VMEM_LIMIT = 52 * 1024 * 1024
Q_BLOCK = 128
SSM_CHUNK = 128
SSM_SB = 128
CONV_HALO = 32

WEIGHT_NAMES = ['meta_tokens', 'ffn1_norm', 'ffn1_w13', 'ffn1_w2', 'mix_norm', 'w_in', 'ssm_lam_re', 'ssm_lam_im',
                'ssm_log_dt', 'ssm_b_re', 'ssm_b_im', 'ssm_c_re', 'ssm_c_im', 'ssm_d', 'ssm_w_glu', 'conv_w', 'conv_b',
                'conv_ln_g', 'conv_ln_b', 'conv_w_out', 'attn_w_o', 'w_out', 'ffn2_norm', 'ffn2_w13', 'ffn2_w2',
                'final_norm']
REPLICATED = ['ffn1_norm', 'mix_norm', 'ssm_lam_re', 'ssm_lam_im', 'ssm_log_dt', 'ssm_b_re', 'ssm_b_im', 'ssm_c_re',
              'ssm_c_im', 'ssm_d', 'conv_b', 'conv_ln_g', 'conv_ln_b', 'ffn2_norm', 'final_norm']


class Cfg(NamedTuple):
    d_model: int
    seq: int
    depth: int
    d_ff: int
    b_loc: int
    ndev: int = 8
    n_meta: int = 16
    n_heads: int = 8
    head_dim: int = 64
    ssm_group: int = 16
    ssm_state: int = 64
    conv_width: int = 31

    @property
    def d_ssm(self): return self.d_model // 2
    @property
    def groups(self): return self.d_ssm // self.ssm_group
    @property
    def n_state(self): return self.groups * self.ssm_state
    @property
    def d_conv(self): return self.d_model // 2
    @property
    def d_attn(self): return self.n_heads * self.head_dim
    @property
    def off_xa(self): return self.d_ssm
    @property
    def off_xg(self): return self.d_ssm + self.d_conv
    @property
    def off_q(self): return self.d_ssm + 2 * self.d_conv
    @property
    def off_k(self): return self.off_q + self.d_attn
    @property
    def off_v(self): return self.off_k + self.d_attn
    @property
    def off_gate(self): return self.off_v + self.d_attn
    @property
    def d_in(self): return self.off_gate + 3 * self.d_model
    @property
    def pad(self): return Q_BLOCK - self.n_meta
    @property
    def lp(self): return Q_BLOCK + self.seq
    @property
    def tokens(self): return self.b_loc * self.lp


FULL = Cfg(d_model=1024, seq=4096, depth=4, d_ff=2816, b_loc=2)


def _tile(n, pref, align):
    for t in range(min(pref, n), 0, -1):
        if n % t == 0 and t % align == 0:
            return t
    return n


def _cp(sem):
    return pltpu.CompilerParams(dimension_semantics=sem, vmem_limit_bytes=V7X_VMEM_LIMIT)


def _sds(shape, dtype):
    return jax.ShapeDtypeStruct(shape, dtype)


def _sigmoid(x):
    return jax.nn.sigmoid(x)


def _dot(a, b, ca, cb):
    return lax.dot_general(a, b, (((ca,), (cb,)), ((), ())), preferred_element_type=F32)


def rms_fwd(h, g, name):
    t, d = h.shape
    tm = _tile(t, 1056, 16)

    def body(h_ref, g_ref, o_ref):
        x = h_ref[...]
        r = lax.rsqrt(jnp.mean(x * x, axis=-1, keepdims=True) + RMS_EPS)
        o_ref[...] = (x * r * g_ref[...]).astype(BF16)

    return pl.pallas_call(
        body, name=name, grid=(t // tm,),
        in_specs=[pl.BlockSpec((tm, d), lambda i: (i, 0)), pl.BlockSpec((1, d), lambda i: (0, 0))],
        out_specs=pl.BlockSpec((tm, d), lambda i: (i, 0)), out_shape=_sds((t, d), BF16),
        compiler_params=_cp(("parallel",)))(h, g.reshape(1, d))


def rms_bwd(dxn, h, g, dh_res, name):
    t, d = h.shape
    tm = _tile(t, 528, 16)

    def body(dxn_ref, h_ref, g_ref, r_ref, dh_ref, dh16_ref, dg_ref):
        x = h_ref[...]
        r = lax.rsqrt(jnp.mean(x * x, axis=-1, keepdims=True) + RMS_EPS)
        xhat = x * r
        dy = dxn_ref[...]
        dyg = dy * g_ref[...]
        dh = r_ref[...] + r * (dyg - xhat * jnp.mean(dyg * xhat, axis=-1, keepdims=True))
        dh_ref[...] = dh
        dh16_ref[...] = dh.astype(BF16)

        @pl.when(pl.program_id(0) == 0)
        def _():
            dg_ref[...] = jnp.zeros_like(dg_ref)

        dg_ref[...] += jnp.sum(dy * xhat, axis=0, keepdims=True)

    row = pl.BlockSpec((tm, d), lambda i: (i, 0))
    vec = pl.BlockSpec((1, d), lambda i: (0, 0))
    return pl.pallas_call(
        body, name=name, grid=(t // tm,), in_specs=[row, row, vec, row], out_specs=[row, row, vec],
        out_shape=[_sds((t, d), F32), _sds((t, d), BF16), _sds((1, d), F32)],
        compiler_params=_cp(("arbitrary",)))(dxn, h, g.reshape(1, d), dh_res)


def mm(pairs, *, nt, n, tm, tn, out_dtype, name, res=None, res_scale=1.0):
    m = pairs[0][0].shape[0]
    np_ = len(pairs)

    def body(*refs):
        o_ref = refs[-1]
        acc = None
        for p in range(np_):
            a = refs[2 * p][...].astype(BF16)
            b = refs[2 * p + 1][...].astype(BF16)
            d = _dot(a, b, 1, 1 if nt else 0)
            acc = d if acc is None else acc + d
        if res is not None:
            acc = refs[2 * np_][...] + res_scale * acc
        o_ref[...] = acc.astype(out_dtype)

    in_specs, args = [], []
    for a, b, kblk in pairs:
        k = a.shape[1]
        in_specs.append(pl.BlockSpec((tm, k), lambda i, j: (i, 0)))
        if nt:
            nb = n // tn
            in_specs.append(pl.BlockSpec((tn, k), functools.partial(lambda i, j, o: (j + o, 0), o=kblk * nb)))
        else:
            in_specs.append(pl.BlockSpec((k, tn), functools.partial(lambda i, j, o: (o, j), o=kblk)))
        args += [a, b]
    if res is not None:
        in_specs.append(pl.BlockSpec((tm, tn), lambda i, j: (i, j)))
        args.append(res)
    return pl.pallas_call(
        body, name=name, grid=(m // tm, n // tn), in_specs=in_specs,
        out_specs=pl.BlockSpec((tm, tn), lambda i, j: (i, j)), out_shape=_sds((m, n), out_dtype),
        compiler_params=_cp(("parallel", "arbitrary")))(*args)


def mm_tn(a, b, *, name, scale=1.0):
    t, m = a.shape
    n = b.shape[1]
    tm = _tile(m, 1536, 128)
    tn = _tile(n, 1024, 128)
    tk = _tile(t, 1056, 16)
    nk = t // tk

    def body(a_ref, b_ref, o_ref):
        k = pl.program_id(2)

        @pl.when(k == 0)
        def _():
            o_ref[...] = jnp.zeros_like(o_ref)

        o_ref[...] += _dot(a_ref[...].astype(BF16), b_ref[...].astype(BF16), 0, 0)
        if scale != 1.0:
            @pl.when(k == nk - 1)
            def _():
                o_ref[...] = o_ref[...] * scale

    return pl.pallas_call(
        body, name=name, grid=(m // tm, n // tn, nk),
        in_specs=[pl.BlockSpec((tk, tm), lambda i, j, k: (k, i)), pl.BlockSpec((tk, tn), lambda i, j, k: (k, j))],
        out_specs=pl.BlockSpec((tm, tn), lambda i, j, k: (i, j)), out_shape=_sds((m, n), F32),
        compiler_params=_cp(("parallel", "parallel", "arbitrary")))(a, b)


def dual_mm_act(x, wt, *, kind, name):
    t, k = x.shape
    hdim = wt.shape[0] // 2
    tm = _tile(t, 528, 16)
    tn = _tile(hdim, 1408, 128)
    nb = hdim // tn
    act_dtype = BF16 if kind == 'swiglu' else F32

    def body(x_ref, wa_ref, wb_ref, a_ref, b_ref, act_ref):
        xv = x_ref[...]
        a = _dot(xv, wa_ref[...], 1, 1)
        b = _dot(xv, wb_ref[...], 1, 1)
        a_ref[...] = a.astype(act_dtype)
        b_ref[...] = b.astype(act_dtype)
        if kind == 'swiglu':
            act_ref[...] = (a * _sigmoid(a) * b).astype(act_dtype)
        else:
            act_ref[...] = (a * _sigmoid(b)).astype(act_dtype)

    ob = pl.BlockSpec((tm, tn), lambda j, i: (i, j))
    return pl.pallas_call(
        body, name=name, grid=(nb, t // tm),
        in_specs=[pl.BlockSpec((tm, k), lambda j, i: (i, 0)), pl.BlockSpec((tn, k), lambda j, i: (j, 0)),
                  pl.BlockSpec((tn, k), lambda j, i: (j + nb, 0))],
        out_specs=[ob, ob, ob], out_shape=[_sds((t, hdim), act_dtype)] * 3,
        compiler_params=_cp(("parallel", "arbitrary")))(x, wt, wt)


def ffn_down_bwd(dh, w2, a, b, name):
    t, d = dh.shape
    hdim = w2.shape[0]
    tm = _tile(t, 528, 16)
    tn = _tile(hdim, 1408, 128)

    def body(dh_ref, w_ref, a_ref, b_ref, da_ref, db_ref):
        dact = 0.5 * _dot(dh_ref[...].astype(BF16), w_ref[...], 1, 1)
        av = a_ref[...].astype(F32)
        sg = _sigmoid(av)
        da_ref[...] = (dact * b_ref[...].astype(F32) * sg * (1.0 + av * (1.0 - sg))).astype(BF16)
        db_ref[...] = (dact * av * sg).astype(BF16)

    ob = pl.BlockSpec((tm, tn), lambda j, i: (i, j))
    return pl.pallas_call(
        body, name=name, grid=(hdim // tn, t // tm),
        in_specs=[pl.BlockSpec((tm, d), lambda j, i: (i, 0)), pl.BlockSpec((tn, d), lambda j, i: (j, 0)), ob, ob],
        out_specs=[ob, ob], out_shape=[_sds((t, hdim), BF16), _sds((t, hdim), BF16)],
        compiler_params=_cp(("parallel", "arbitrary")))(dh, w2, a, b)


def final_fwd_bwd(cfg, h, gf, target):
    bsz, lp, d = h.shape
    nq = lp // Q_BLOCK

    def body(h_ref, g_ref, t_ref, dh_ref, dh16_ref, dg_ref, loss_ref):
        b, j = pl.program_id(0), pl.program_id(1)
        x = h_ref[0]
        r = lax.rsqrt(jnp.mean(x * x, axis=-1, keepdims=True) + RMS_EPS)
        xhat = x * r
        gv = g_ref[...]
        diff = jnp.where(j > 0, xhat * gv - t_ref[0], 0.0)
        dy = diff * (1.0 / d)
        dyg = dy * gv
        dh = r * (dyg - xhat * jnp.mean(dyg * xhat, axis=-1, keepdims=True))
        dh_ref[0] = dh
        dh16_ref[0] = dh.astype(BF16)

        @pl.when((b == 0) & (j == 0))
        def _():
            dg_ref[...] = jnp.zeros_like(dg_ref)
            loss_ref[...] = jnp.zeros_like(loss_ref)

        dg_ref[...] += jnp.sum(dy * xhat, axis=0, keepdims=True)
        loss_ref[...] += jnp.sum(jnp.sum(diff * diff, axis=1, keepdims=True), axis=0, keepdims=True)

    blk = pl.BlockSpec((1, Q_BLOCK, d), lambda b, j: (b, j, 0))
    return pl.pallas_call(
        body, name="final_loss", grid=(bsz, nq),
        in_specs=[blk, pl.BlockSpec((1, d), lambda b, j: (0, 0)),
                  pl.BlockSpec((1, Q_BLOCK, d), lambda b, j: (b, jnp.maximum(j - 1, 0), 0))],
        out_specs=[blk, blk, pl.BlockSpec((1, d), lambda b, j: (0, 0)), pl.BlockSpec((1, 1), lambda b, j: (0, 0))],
        out_shape=[_sds((bsz, lp, d), F32), _sds((bsz, lp, d), BF16), _sds((1, d), F32), _sds((1, 1), F32)],
        compiler_params=_cp(("arbitrary", "arbitrary")))(h, gf.reshape(1, d), target)


def _row_valid(cfg, tm, i):
    pos = (i * tm) % cfg.lp + lax.broadcasted_iota(jnp.int32, (tm, 1), 0)
    return pos >= cfg.pad


def merge_fwd(cfg, proj, o_ssm, o_conv, o_attn):
    t, d = o_ssm.shape
    tc = math.gcd(math.gcd(d, cfg.off_gate), 512)
    tm = _tile(cfg.lp, 528, 16)
    g0 = cfg.off_gate // tc
    nc = d // tc

    def body(g0_ref, g1_ref, g2_ref, s_ref, c_ref, a_ref, o_ref):
        valid = _row_valid(cfg, tm, pl.program_id(0))
        m = (_sigmoid(g0_ref[...]) * s_ref[...] + _sigmoid(g1_ref[...]) * c_ref[...]
             + _sigmoid(g2_ref[...]) * a_ref[...])
        o_ref[...] = jnp.where(valid, m, 0.0).astype(BF16)

    gate = [pl.BlockSpec((tm, tc), functools.partial(lambda i, j, o: (i, o + j), o=g0 + k * nc)) for k in range(3)]
    blk = pl.BlockSpec((tm, tc), lambda i, j: (i, j))
    return pl.pallas_call(
        body, name="merge_fwd", grid=(t // tm, nc), in_specs=gate + [blk, blk, blk], out_specs=blk,
        out_shape=_sds((t, d), BF16), compiler_params=_cp(("parallel", "parallel")))(
            proj, proj, proj, o_ssm, o_conv, o_attn)


def merge_bwd(cfg, dmerged, proj, ga, gg, o_conv, o_attn):
    t, d = dmerged.shape
    tc = math.gcd(math.gcd(d, cfg.off_gate), 512)
    tm = _tile(cfg.lp, 528, 16)
    g0 = cfg.off_gate // tc
    nc = d // tc

    def body(dm_ref, g0_ref, g1_ref, g2_ref, ga_ref, gg_ref, c_ref, a_ref,
             dg0_ref, dg1_ref, dg2_ref, dga_ref, dgg_ref, dc_ref, da_ref):
        dm = dm_ref[...]
        s0, s1, s2 = _sigmoid(g0_ref[...]), _sigmoid(g1_ref[...]), _sigmoid(g2_ref[...])
        sg = _sigmoid(gg_ref[...])
        gav = ga_ref[...]
        o_s = gav * sg
        dg0_ref[...] = (dm * o_s * s0 * (1.0 - s0)).astype(BF16)
        dg1_ref[...] = (dm * c_ref[...] * s1 * (1.0 - s1)).astype(BF16)
        dg2_ref[...] = (dm * a_ref[...] * s2 * (1.0 - s2)).astype(BF16)
        dos = dm * s0
        dga_ref[...] = (dos * sg).astype(BF16)
        dgg_ref[...] = (dos * gav * sg * (1.0 - sg)).astype(BF16)
        dc_ref[...] = (dm * s1).astype(BF16)
        da_ref[...] = (dm * s2).astype(BF16)

    gate = [pl.BlockSpec((tm, tc), functools.partial(lambda i, j, o: (i, o + j), o=g0 + k * nc)) for k in range(3)]
    blk = pl.BlockSpec((tm, tc), lambda i, j: (i, j))
    out = _sds((t, d), BF16)
    return pl.pallas_call(
        body, name="merge_bwd", grid=(t // tm, nc), in_specs=[blk] + gate + [blk, blk, blk, blk],
        out_specs=[blk] * 7, out_shape=[out] * 7, compiler_params=_cp(("parallel", "parallel")))(
            dmerged, proj, proj, proj, ga, gg, o_conv, o_attn)


def _conv_tile(cfg):
    return _tile(cfg.lp, 384, CONV_HALO)


def _shift_up(x, off, rows):
    if off == 0:
        return x[:rows]
    return pltpu.roll(x, x.shape[0] - off, 0)[:rows]


def conv_fwd(cfg, proj, w, bias, ln_g, ln_b):
    t = proj.shape[0]
    wc = cfg.d_conv
    tt = _conv_tile(cfg)
    nc = cfg.lp // tt
    kw = cfg.conv_width
    first = CONV_HALO - (kw - 1)

    def body(a_ref, g_ref, ap_ref, gp_ref, w_ref, b_ref, lg_ref, lb_ref, cpre_ref, hn_ref):
        c = pl.program_id(1)
        hc = a_ref[...] * _sigmoid(g_ref[...])
        hp = jnp.where(c > 0, ap_ref[...] * _sigmoid(gp_ref[...]), 0.0)
        hcat = jnp.concatenate([hp, hc], axis=0)
        acc = jnp.zeros((tt, wc), F32) + b_ref[...]
        for k in range(kw):
            acc = acc + w_ref[k:k + 1, :] * _shift_up(hcat, first + k, tt)
        cpre_ref[...] = acc
        mu = jnp.mean(acc, axis=-1, keepdims=True)
        xc = acc - mu
        y = xc * lax.rsqrt(jnp.mean(xc * xc, axis=-1, keepdims=True) + LN_EPS) * lg_ref[...] + lb_ref[...]
        hn_ref[...] = (y * _sigmoid(y)).astype(BF16)

    ca, cg = cfg.off_xa // wc, cfg.off_xg // wc
    hb = tt // CONV_HALO
    lph = cfg.lp // CONV_HALO

    def halo(col):
        return pl.BlockSpec((CONV_HALO, wc), lambda b, c: (jnp.maximum(b * lph + c * hb - 1, 0), col))

    vec = pl.BlockSpec((1, wc), lambda b, c: (0, 0))
    blk = pl.BlockSpec((tt, wc), lambda b, c: (b * nc + c, 0))
    return pl.pallas_call(
        body, name="conv_fwd", grid=(cfg.b_loc, nc),
        in_specs=[pl.BlockSpec((tt, wc), lambda b, c: (b * nc + c, ca)),
                  pl.BlockSpec((tt, wc), lambda b, c: (b * nc + c, cg)), halo(ca), halo(cg),
                  pl.BlockSpec((kw, wc), lambda b, c: (0, 0)), vec, vec, vec],
        out_specs=[blk, blk], out_shape=[_sds((t, wc), F32), _sds((t, wc), BF16)],
        compiler_params=_cp(("parallel", "arbitrary")))(
            proj, proj, proj, proj, w, bias.reshape(1, wc), ln_g.reshape(1, wc), ln_b.reshape(1, wc))


def conv_bwd(cfg, dhn, cpre, proj, w, ln_g, ln_b):
    t = proj.shape[0]
    wc = cfg.d_conv
    tt = _conv_tile(cfg)
    nc = cfg.lp // tt
    kw = cfg.conv_width
    first = CONV_HALO - (kw - 1)

    def body(dhn_ref, dhn_nx_ref, cp_ref, cp_nx_ref, a_ref, g_ref, ap_ref, gp_ref, w_ref, lg_ref, lb_ref,
             dxa_ref, dxg_ref, dw_ref, db_ref, dlg_ref, dlb_ref):
        b, c = pl.program_id(0), pl.program_id(1)
        lg, lb = lg_ref[...], lb_ref[...]

        def ln_silu_bwd(dh, cp):
            mu = jnp.mean(cp, axis=-1, keepdims=True)
            xc = cp - mu
            rstd = lax.rsqrt(jnp.mean(xc * xc, axis=-1, keepdims=True) + LN_EPS)
            xhat = xc * rstd
            y = xhat * lg + lb
            sg = _sigmoid(y)
            dy = dh * sg * (1.0 + y * (1.0 - sg))
            dxh = dy * lg
            dc = rstd * (dxh - jnp.mean(dxh, axis=-1, keepdims=True)
                         - xhat * jnp.mean(dxh * xhat, axis=-1, keepdims=True))
            return dc, dy, xhat

        dc, dy, xhat = ln_silu_bwd(dhn_ref[...], cp_ref[...])
        dc_nx, _, _ = ln_silu_bwd(dhn_nx_ref[...], cp_nx_ref[...])
        dc_nx = jnp.where(c < nc - 1, dc_nx, 0.0)
        dcat = jnp.concatenate([dc, dc_nx], axis=0)
        dhc = jnp.zeros((tt, wc), F32)
        for k in range(kw):
            dhc = dhc + w_ref[k:k + 1, :] * _shift_up(dcat, kw - 1 - k, tt)
        av, gv = a_ref[...], g_ref[...]
        sg = _sigmoid(gv)
        valid = (c * tt + lax.broadcasted_iota(jnp.int32, (tt, 1), 0)) >= cfg.pad
        dhc = jnp.where(valid, dhc, 0.0)
        dxa_ref[...] = (dhc * sg).astype(BF16)
        dxg_ref[...] = (dhc * av * sg * (1.0 - sg)).astype(BF16)

        @pl.when((b == 0) & (c == 0))
        def _():
            dw_ref[...] = jnp.zeros_like(dw_ref)
            db_ref[...] = jnp.zeros_like(db_ref)
            dlg_ref[...] = jnp.zeros_like(dlg_ref)
            dlb_ref[...] = jnp.zeros_like(dlb_ref)

        hp = jnp.where(c > 0, ap_ref[...] * _sigmoid(gp_ref[...]), 0.0)
        hcat = jnp.concatenate([hp, av * sg], axis=0)
        for k in range(kw):
            dw_ref[k:k + 1, :] += jnp.sum(dc * _shift_up(hcat, first + k, tt), axis=0, keepdims=True)
        db_ref[...] += jnp.sum(dc, axis=0, keepdims=True)
        dlg_ref[...] += jnp.sum(dy * xhat, axis=0, keepdims=True)
        dlb_ref[...] += jnp.sum(dy, axis=0, keepdims=True)

    ca, cg = cfg.off_xa // wc, cfg.off_xg // wc
    hb = tt // CONV_HALO
    lph = cfg.lp // CONV_HALO
    last = cfg.b_loc * lph - 1

    def prev(col):
        return pl.BlockSpec((CONV_HALO, wc), lambda b, c: (jnp.maximum(b * lph + c * hb - 1, 0), col))

    nxt = pl.BlockSpec((CONV_HALO, wc), lambda b, c: (jnp.minimum(b * lph + (c + 1) * hb, last), 0))
    vec = pl.BlockSpec((1, wc), lambda b, c: (0, 0))
    blk = pl.BlockSpec((tt, wc), lambda b, c: (b * nc + c, 0))
    wspec = pl.BlockSpec((kw, wc), lambda b, c: (0, 0))
    return pl.pallas_call(
        body, name="conv_bwd", grid=(cfg.b_loc, nc),
        in_specs=[blk, nxt, blk, nxt, pl.BlockSpec((tt, wc), lambda b, c: (b * nc + c, ca)),
                  pl.BlockSpec((tt, wc), lambda b, c: (b * nc + c, cg)), prev(ca), prev(cg), wspec, vec, vec],
        out_specs=[blk, blk, wspec, vec, vec, vec],
        out_shape=[_sds((t, wc), BF16), _sds((t, wc), BF16), _sds((kw, wc), F32), _sds((1, wc), F32),
                   _sds((1, wc), F32), _sds((1, wc), F32)],
        compiler_params=_cp(("arbitrary", "arbitrary")))(
            dhn, dhn, cpre, cpre, proj, proj, proj, proj, w, ln_g.reshape(1, wc), ln_b.reshape(1, wc))


def _split_bf16(x):
    hi = x.astype(BF16)
    return hi, (x - hi.astype(F32)).astype(BF16)


K_CHUNK = 2 * Q_BLOCK
NEG_BIG = -1e30


def _sb_logits(cfg, qh, kblk, col0, row_t, masked):
    z = _dot(qh, kblk, 1, 1)
    lk = -(jnp.maximum(z, 0.0) + jnp.log(1.0 + jnp.exp(-jnp.abs(z))))
    if masked:
        col_s = col0 + lax.broadcasted_iota(jnp.int32, (1, K_CHUNK), 1)
        mask = (col_s < row_t) & (col_s >= cfg.pad)
        z, lk = jnp.where(mask, z, NEG_BIG), jnp.where(mask, lk, 0.0)
    return z, lk


def _tri_sum(x, tri):
    hi, lo = _split_bf16(x)
    return _dot(hi, tri, 1, 0) + _dot(lo, tri, 1, 0)


def _grid_ends(dims):
    ids = [pl.program_id(a) for a in range(len(dims))]
    first = functools.reduce(lambda p, q: p & q, [i == 0 for i in ids])
    last = functools.reduce(lambda p, q: p & q, [i == n - 1 for i, n in zip(ids, dims)])
    return first, last


def attn_fwd(cfg, proj, gather_src=None):
    t = proj.shape[0]
    lp = cfg.lp
    nq = lp // Q_BLOCK
    nhp = cfg.d_attn // V7X_LANES
    cq, ck, cv = cfg.off_q // V7X_LANES, cfg.off_k // V7X_LANES, cfg.off_v // V7X_LANES
    scale = 1.0 / math.sqrt(cfg.head_dim)
    grid = (cfg.b_loc, nhp, nq)

    def body(*refs):
        if gather_src is None:
            q_ref, k_ref, v_ref, o_ref, lt_ref, kbf, vbf = refs
        else:
            q_ref, k_ref, v_ref, src_ref, o_ref, lt_ref, dst_ref, kbf, vbf = refs[:9]
            exchange = (src_ref, dst_ref) + tuple(refs[9:])
            first_step, last_step = _grid_ends(grid)
            pl.when(first_step)(lambda: gather_start(*exchange))
        qb = pl.program_id(2)

        @pl.when(qb == 0)
        def _():
            kbf[pl.ds(0, lp), :] = k_ref[...].astype(BF16)
            vbf[pl.ds(0, lp), :] = v_ref[...].astype(BF16)
            kbf[pl.ds(lp, Q_BLOCK), :] = jnp.zeros((Q_BLOCK, V7X_LANES), BF16)
            vbf[pl.ds(lp, Q_BLOCK), :] = jnp.zeros((Q_BLOCK, V7X_LANES), BF16)

        q = q_ref[...] * scale
        lane = lax.broadcasted_iota(jnp.int32, (1, V7X_LANES), 1)
        head0 = lane < cfg.head_dim
        qs = jnp.concatenate([jnp.where(head0, q, 0.0), jnp.where(head0, 0.0, q)], axis=0).astype(BF16)
        rows2 = lax.broadcasted_iota(jnp.int32, (2 * Q_BLOCK, 1), 0)
        row_t = qb * Q_BLOCK + jnp.where(rows2 >= Q_BLOCK, rows2 - Q_BLOCK, rows2)
        ri = lax.broadcasted_iota(jnp.int32, (K_CHUNK, K_CHUNK), 0)
        ci = lax.broadcasted_iota(jnp.int32, (K_CHUNK, K_CHUNK), 1)
        tri_from = (ri >= ci).astype(BF16)
        last = qb // 2

        def stage1(k, masked):
            c = jnp.clip(last - k, 0, last)
            off = pl.multiple_of(c * K_CHUNK, K_CHUNK)
            col0 = jnp.where(k <= last, c * K_CHUNK, cfg.lp + K_CHUNK)
            return _sb_logits(cfg, qs, kbf[pl.ds(off, K_CHUNK), :], col0, row_t, masked)

        def stage2(z, lk):
            return z + _tri_sum(lk, tri_from), jnp.sum(lk, axis=1, keepdims=True)

        def stage3(pre, rows, acc, r_carry, k):
            off = pl.multiple_of(jnp.clip(last - k, 0, last) * K_CHUNK, K_CHUNK)
            w = jnp.exp(pre + r_carry)
            return acc + _dot(w.astype(BF16), vbf[pl.ds(off, K_CHUNK), :], 1, 0), r_carry + rows

        def step(masked):
            def body(k, carry):
                z_p, lk_p, pre_pp, rows_pp, acc, r_carry = carry
                acc, r_carry = stage3(pre_pp, rows_pp, acc, r_carry, k - 2)
                pre_p, rows_p = stage2(z_p, lk_p)
                z, lk = stage1(k, masked)
                return z, lk, pre_p, rows_p, acc, r_carry
            return body

        zcol = jnp.where(rows2 < 0, 1.0, 0.0)
        first = stage1(0, True)
        carry = stage1(1, True) + stage2(*first) + (zcol + jnp.where(lane < 0, 1.0, 0.0), zcol)
        carry = lax.fori_loop(2, jnp.maximum(last, 2), step(False), carry)
        carry = lax.fori_loop(jnp.maximum(last, 2), jnp.maximum(last + 1, 2), step(True), carry)
        z_p, lk_p, pre_pp, rows_pp, acc, r_tot = carry
        k_end = jnp.maximum(last + 1, 2)
        acc, r_tot = stage3(pre_pp, rows_pp, acc, r_tot, k_end - 2)
        acc, r_tot = stage3(*stage2(z_p, lk_p), acc, r_tot, k_end - 1)
        o_ref[...] = jnp.where(head0, acc[:Q_BLOCK], acc[Q_BLOCK:]).astype(BF16)
        lt_ref[...] = (jnp.where(lane == 0, r_tot[:Q_BLOCK], 0.0)
                       + jnp.where(lane == cfg.head_dim, r_tot[Q_BLOCK:], 0.0))
        if gather_src is not None:
            pl.when(last_step)(lambda: gather_finish(*exchange))

    oblk = pl.BlockSpec((Q_BLOCK, V7X_LANES), lambda b, hp, qb: (b * nq + qb, hp))
    hbm = pl.BlockSpec(memory_space=pl.ANY)
    comm = gather_src is not None
    return pl.pallas_call(
        body, name="attn_fwd_gather" if comm else "attn_fwd", grid=grid,
        in_specs=[pl.BlockSpec((Q_BLOCK, V7X_LANES), lambda b, hp, qb: (b * nq + qb, cq + hp)),
                  pl.BlockSpec((lp, V7X_LANES), lambda b, hp, qb: (b, ck + hp)),
                  pl.BlockSpec((lp, V7X_LANES), lambda b, hp, qb: (b, cv + hp))] + ([hbm] if comm else []),
        out_specs=[oblk, oblk] + ([hbm] if comm else []),
        out_shape=[_sds((t, cfg.d_attn), BF16), _sds((t, cfg.d_attn), F32)]
        + ([_sds((8,) + gather_src.shape, gather_src.dtype)] if comm else []),
        scratch_shapes=[pltpu.VMEM((lp + Q_BLOCK, V7X_LANES), BF16), pltpu.VMEM((lp + Q_BLOCK, V7X_LANES), BF16)]
        + (exchange_semaphores() if comm else []),
        compiler_params=_cp(("arbitrary", "arbitrary", "arbitrary")))(*((proj, proj, proj) + ((gather_src,) if comm else ())))


def attn_bwd(cfg, proj, lt, do, send=None):
    t = proj.shape[0]
    lp = cfg.lp
    nq = lp // Q_BLOCK
    nhp = cfg.d_attn // V7X_LANES
    cq, ck, cv = cfg.off_q // V7X_LANES, cfg.off_k // V7X_LANES, cfg.off_v // V7X_LANES
    scale = 1.0 / math.sqrt(cfg.head_dim)
    grid = (cfg.b_loc, nhp, nq)

    def body(*refs):
        if send is None:
            q_ref, k_ref, v_ref, lt_ref, do_ref, dq_ref, dk_ref, dv_ref, kbf, vbf, dk_acc, dv_acc = refs
        else:
            (q_ref, k_ref, v_ref, lt_ref, do_ref, src_ref, dq_ref, dk_ref, dv_ref, dst_ref,
             kbf, vbf, dk_acc, dv_acc) = refs[:14]
            exchange = (src_ref, dst_ref) + tuple(refs[14:])
            first_step, last_step = _grid_ends(grid)
            pl.when(first_step)(lambda: all_to_all_start(*exchange))
        qb = pl.program_id(2)

        @pl.when(qb == 0)
        def _():
            kbf[pl.ds(0, lp), :] = k_ref[...].astype(BF16)
            vbf[pl.ds(0, lp), :] = v_ref[...].astype(BF16)
            kbf[pl.ds(lp, Q_BLOCK), :] = jnp.zeros((Q_BLOCK, V7X_LANES), BF16)
            vbf[pl.ds(lp, Q_BLOCK), :] = jnp.zeros((Q_BLOCK, V7X_LANES), BF16)
            dk_acc[...] = jnp.zeros_like(dk_acc)
            dv_acc[...] = jnp.zeros_like(dv_acc)

        q = q_ref[...] * scale
        ltv = lt_ref[...]
        dov = do_ref[...]
        lane = lax.broadcasted_iota(jnp.int32, (1, V7X_LANES), 1)
        head0 = lane < cfg.head_dim
        qs = jnp.concatenate([jnp.where(head0, q, 0.0), jnp.where(head0, 0.0, q)], axis=0).astype(BF16)
        dos = jnp.concatenate([jnp.where(head0, dov, 0.0), jnp.where(head0, 0.0, dov)], axis=0).astype(BF16)
        lk_tot = jnp.concatenate(
            [jnp.sum(jnp.where(lane == 0, ltv, 0.0), axis=1, keepdims=True),
             jnp.sum(jnp.where(lane == cfg.head_dim, ltv, 0.0), axis=1, keepdims=True)], axis=0)
        rows2 = lax.broadcasted_iota(jnp.int32, (2 * Q_BLOCK, 1), 0)
        row_t = qb * Q_BLOCK + jnp.where(rows2 >= Q_BLOCK, rows2 - Q_BLOCK, rows2)
        ri = lax.broadcasted_iota(jnp.int32, (K_CHUNK, K_CHUNK), 0)
        ci = lax.broadcasted_iota(jnp.int32, (K_CHUNK, K_CHUNK), 1)
        tri_before = (ri < ci).astype(BF16)
        tri_upto = (ri <= ci).astype(BF16)
        n_chunks = qb // 2 + 1

        def stage1(k, masked, lk_before):
            off = pl.multiple_of(k * K_CHUNK, K_CHUNK)
            z, lk = _sb_logits(cfg, qs, kbf[pl.ds(off, K_CHUNK), :], k * K_CHUNK, row_t, masked)
            dw = _dot(dos, vbf[pl.ds(off, K_CHUNK), :], 1, 1)
            w = jnp.exp(z + (lk_tot - lk_before - _tri_sum(lk, tri_before)))
            sg = 1.0 - jnp.exp(lk)
            return w * dw, w.astype(BF16), sg, lk_before + jnp.sum(lk, axis=1, keepdims=True)

        def stage2(e, w16, sg, k, dq_acc, e_before):
            off = pl.multiple_of(k * K_CHUNK, K_CHUNK)
            dz = (e - sg * (e_before + _tri_sum(e, tri_upto))).astype(BF16)
            dk_acc[pl.ds(off, K_CHUNK), :] += _dot(dz, qs, 0, 0)
            dv_acc[pl.ds(off, K_CHUNK), :] += _dot(w16, dos, 0, 0)
            return dq_acc + _dot(dz, kbf[pl.ds(off, K_CHUNK), :], 1, 0), e_before + jnp.sum(e, axis=1, keepdims=True)

        def step(masked):
            def body(k, carry):
                e_p, w_p, sg_p, dq_acc, lk_before, e_before = carry
                dq_acc, e_before = stage2(e_p, w_p, sg_p, k - 1, dq_acc, e_before)
                e, w16, sg, lk_before = stage1(k, masked, lk_before)
                return e, w16, sg, dq_acc, lk_before, e_before
            return body

        zcol = jnp.where(rows2 < 0, 1.0, 0.0)
        e, w16, sg, lk_before = stage1(0, True, zcol)
        carry = (e, w16, sg, zcol + jnp.where(lane < 0, 1.0, 0.0), lk_before, zcol)
        carry = lax.fori_loop(1, jnp.maximum(n_chunks - 1, 1), step(False), carry)
        carry = lax.fori_loop(jnp.maximum(n_chunks - 1, 1), n_chunks, step(True), carry)
        dq, _ = stage2(carry[0], carry[1], carry[2], n_chunks - 1, carry[3], carry[5])
        dq_ref[...] = (jnp.where(head0, dq[:Q_BLOCK], dq[Q_BLOCK:]) * scale).astype(BF16)

        @pl.when(qb == nq - 1)
        def _():
            dk_ref[...] = dk_acc[pl.ds(0, lp), :].astype(BF16)
            dv_ref[...] = dv_acc[pl.ds(0, lp), :].astype(BF16)

        if send is not None:
            pl.when(last_step)(lambda: all_to_all_finish(*exchange))

    qblk = pl.BlockSpec((Q_BLOCK, V7X_LANES), lambda b, hp, qb: (b * nq + qb, hp))
    seq = pl.BlockSpec((lp, V7X_LANES), lambda b, hp, qb: (b, hp))
    hbm = pl.BlockSpec(memory_space=pl.ANY)
    out = _sds((t, cfg.d_attn), BF16)
    comm = send is not None
    return pl.pallas_call(
        body, name="attn_bwd_exchange" if comm else "attn_bwd", grid=grid,
        in_specs=[pl.BlockSpec((Q_BLOCK, V7X_LANES), lambda b, hp, qb: (b * nq + qb, cq + hp)),
                  pl.BlockSpec((lp, V7X_LANES), lambda b, hp, qb: (b, ck + hp)),
                  pl.BlockSpec((lp, V7X_LANES), lambda b, hp, qb: (b, cv + hp)), qblk, qblk] + ([hbm] if comm else []),
        out_specs=[qblk, seq, seq] + ([hbm] if comm else []),
        out_shape=[out, out, out] + ([_sds(send.shape, send.dtype)] if comm else []),
        scratch_shapes=[pltpu.VMEM((lp + Q_BLOCK, V7X_LANES), BF16), pltpu.VMEM((lp + Q_BLOCK, V7X_LANES), BF16),
                        pltpu.VMEM((lp + Q_BLOCK, V7X_LANES), F32), pltpu.VMEM((lp + Q_BLOCK, V7X_LANES), F32)]
        + (exchange_semaphores() if comm else []),
        compiler_params=_cp(("arbitrary", "arbitrary", "arbitrary")))(
            *((proj, proj, proj, lt, do) + ((send,) if comm else ())))


N_POW = 8


def _discretize(lr, li, logdt, br_t, bi_t):
    dt = jnp.exp(logdt)
    mag = jnp.exp(lr * dt)
    ab_re = mag * jnp.cos(li * dt)
    ab_im = mag * jnp.sin(li * dt)
    den = lr * lr + li * li
    nr = ab_re - 1.0
    ni = ab_im
    coef_re = (nr * lr + ni * li) / den
    coef_im = (ni * lr - nr * li) / den
    bb_re = coef_re[None] * br_t - coef_im[None] * bi_t
    bb_im = coef_re[None] * bi_t + coef_im[None] * br_t
    return ab_re, ab_im, bb_re, bb_im


def ssm_param_fwd(cfg, lr, li, logdt, br_t, bi_t):
    g, n, c = cfg.groups, cfg.ssm_state, cfg.ssm_group

    def body(lr_ref, li_ref, dt_ref, br_ref, bi_ref, ar_ref, ai_ref, bbr_ref, bbi_ref, pr_ref, pi_ref,
             tr_ref, ti_ref):
        ab_re, ab_im, bb_re, bb_im = _discretize(lr_ref[...], li_ref[...], dt_ref[...], br_ref[...], bi_ref[...])
        ar_ref[...] = ab_re
        ai_ref[...] = ab_im
        bbr_ref[...] = bb_re
        bbi_ref[...] = bb_im
        pr, pi = ab_re, ab_im
        for k in range(N_POW):
            pr_ref[k] = pr
            pi_ref[k] = pi
            pr, pi = pr * pr - pi * pi, 2.0 * pr * pi
        cr, ci = ab_re, ab_im
        for j in range(SSM_CHUNK):
            tr_ref[j] = cr
            ti_ref[j] = ci
            cr, ci = cr * ab_re - ci * ab_im, cr * ab_im + ci * ab_re

    gn, cgn = _sds((g, n), F32), _sds((c, g, n), F32)
    return pl.pallas_call(
        body, name="ssm_param_fwd",
        out_shape=[gn, gn, cgn, cgn, _sds((N_POW, g, n), F32), _sds((N_POW, g, n), F32),
                   _sds((SSM_CHUNK, g, n), F32), _sds((SSM_CHUNK, g, n), F32)])(lr, li, logdt, br_t, bi_t)


def ssm_param_bwd(cfg, lr, li, logdt, br_t, bi_t, dar, dai, dbbr, dbbi):
    g, n, c = cfg.groups, cfg.ssm_state, cfg.ssm_group

    def body(lr_ref, li_ref, dt_ref, br_ref, bi_ref, dar_ref, dai_ref, dbr_ref, dbi_ref,
             olr_ref, oli_ref, odt_ref, obr_ref, obi_ref):
        _, vjp = jax.vjp(_discretize, lr_ref[...], li_ref[...], dt_ref[...], br_ref[...], bi_ref[...])
        dlr, dli, ddt, dbr, dbi = vjp((dar_ref[...], dai_ref[...], dbr_ref[...], dbi_ref[...]))
        olr_ref[...] = dlr
        oli_ref[...] = dli
        odt_ref[...] = ddt
        obr_ref[...] = dbr
        obi_ref[...] = dbi

    gn, cgn = _sds((g, n), F32), _sds((c, g, n), F32)
    return pl.pallas_call(body, name="ssm_param_bwd", out_shape=[gn, gn, _sds((g, 1), F32), cgn, cgn])(
        lr, li, logdt, br_t, bi_t, dar, dai, dbbr, dbbi)


def _scan(xr, xi, pr_ref, pi_ref, reverse):
    ct = xr.shape[0]
    rows = lax.broadcasted_iota(jnp.int32, (ct, 1), 0)
    d, k = 1, 0
    while d < ct:
        if reverse:
            sr, si = pltpu.roll(xr, ct - d, 0), pltpu.roll(xi, ct - d, 0)
            keep = rows < ct - d
        else:
            sr, si = pltpu.roll(xr, d, 0), pltpu.roll(xi, d, 0)
            keep = rows >= d
        sr = jnp.where(keep, sr, 0.0)
        si = jnp.where(keep, si, 0.0)
        ar = pr_ref[0, k:k + 1, :]
        ai = -pi_ref[0, k:k + 1, :] if reverse else pi_ref[0, k:k + 1, :]
        xr, xi = xr + ar * sr - ai * si, xi + ar * si + ai * sr
        d *= 2
        k += 1
    return xr, xi


def _gelu(x):
    k = math.sqrt(2.0 / math.pi)
    return 0.5 * x * (1.0 + jnp.tanh(k * (x + 0.044715 * x * x * x)))


def _gelu_grad(x):
    k = math.sqrt(2.0 / math.pi)
    th = jnp.tanh(k * (x + 0.044715 * x * x * x))
    return 0.5 * (1.0 + th) + 0.5 * x * (1.0 - th * th) * k * (1.0 + 3.0 * 0.044715 * x * x)


def ssm_fwd(cfg, proj, bb_re, bb_im, ct_re, ct_im, pw_re, pw_im, tab_re, tab_im, dskip):
    t = proj.shape[0]
    nsb = cfg.d_ssm // SSM_SB
    ns = SSM_SB // cfg.ssm_group * cfg.ssm_state
    ct = SSM_CHUNK
    nc = cfg.lp // ct

    def body(u_ref, bbr_ref, bbi_ref, ctr_ref, cti_ref, pr_ref, pi_ref, tr_ref, ti_ref, d_ref,
             sr_ref, si_ref, yt_ref, y2_ref, cr_scr, ci_scr):
        c = pl.program_id(2)

        @pl.when(c == 0)
        def _():
            cr_scr[...] = jnp.zeros_like(cr_scr)
            ci_scr[...] = jnp.zeros_like(ci_scr)

        u = u_ref[...]
        ub = u.astype(BF16)
        xr, xi = _scan(_dot(ub, bbr_ref[0], 1, 0), _dot(ub, bbi_ref[0], 1, 0), pr_ref, pi_ref, False)
        cr, ci = cr_scr[0:1, :], ci_scr[0:1, :]
        tr, ti = tr_ref[0], ti_ref[0]
        sr = xr + tr * cr - ti * ci
        si = xi + tr * ci + ti * cr
        sr_ref[...] = sr
        si_ref[...] = si
        cr_scr[0:1, :] = sr_ref[ct - 1:ct, :]
        ci_scr[0:1, :] = si_ref[ct - 1:ct, :]
        y = _dot(sr.astype(BF16), ctr_ref[0], 1, 0) - _dot(si.astype(BF16), cti_ref[0], 1, 0) + d_ref[...] * u
        yt_ref[...] = y
        y2_ref[...] = _gelu(y).astype(BF16)

    def mat(r, c_):
        return pl.BlockSpec((1, r, c_), lambda sb, b, c: (sb, 0, 0))

    ublk = pl.BlockSpec((ct, SSM_SB), lambda sb, b, c: (b * nc + c, sb))
    sblk = pl.BlockSpec((ct, ns), lambda sb, b, c: (b * nc + c, sb))
    return pl.pallas_call(
        body, name="ssm_fwd", grid=(nsb, cfg.b_loc, nc),
        in_specs=[ublk, mat(SSM_SB, ns), mat(SSM_SB, ns), mat(ns, SSM_SB), mat(ns, SSM_SB), mat(N_POW, ns),
                  mat(N_POW, ns), mat(ct, ns), mat(ct, ns), pl.BlockSpec((1, SSM_SB), lambda sb, b, c: (0, sb))],
        out_specs=[sblk, sblk, ublk, ublk],
        out_shape=[_sds((t, cfg.n_state), F32), _sds((t, cfg.n_state), F32), _sds((t, cfg.d_ssm), F32),
                   _sds((t, cfg.d_ssm), BF16)],
        scratch_shapes=[pltpu.VMEM((8, ns), F32), pltpu.VMEM((8, ns), F32)],
        compiler_params=_cp(("parallel", "arbitrary", "arbitrary")))(
            proj, bb_re, bb_im, ct_re, ct_im, pw_re, pw_im, tab_re, tab_im, dskip.reshape(1, cfg.d_ssm))


def ssm_bwd(cfg, dy2, ytot, proj, s_re, s_im, cf_re, cf_im, bbt_re, bbt_im, pw_re, pw_im, tabr_re, tabr_im, dskip):
    t = proj.shape[0]
    nsb = cfg.d_ssm // SSM_SB
    ns = SSM_SB // cfg.ssm_group * cfg.ssm_state
    ct = SSM_CHUNK
    nc = cfg.lp // ct

    def body(dy_ref, yt_ref, u_ref, sr_ref, si_ref, spr_ref, spi_ref, cfr_ref, cfi_ref, btr_ref, bti_ref,
             pr_ref, pi_ref, tr_ref, ti_ref, d_ref,
             du_ref, dctr_ref, dcti_ref, dbbr_ref, dbbi_ref, dar_ref, dai_ref, dd_ref,
             cr_scr, ci_scr, ar_scr, ai_scr):
        b, c = pl.program_id(1), pl.program_id(2)
        chunk = nc - 1 - c

        @pl.when((b == 0) & (c == 0))
        def _():
            for r in (dctr_ref, dcti_ref, dbbr_ref, dbbi_ref, dar_ref, dai_ref, dd_ref):
                r[...] = jnp.zeros_like(r)

        @pl.when(c == 0)
        def _():
            cr_scr[...] = jnp.zeros_like(cr_scr)
            ci_scr[...] = jnp.zeros_like(ci_scr)

        u = u_ref[...]
        dyt = dy_ref[...] * _gelu_grad(yt_ref[...])
        dytb = dyt.astype(BF16)
        xr, xi = _scan(_dot(dytb, cfr_ref[0], 1, 0), -_dot(dytb, cfi_ref[0], 1, 0), pr_ref, pi_ref, True)
        cr, ci = cr_scr[0:1, :], ci_scr[0:1, :]
        tr, ti = tr_ref[0], -ti_ref[0]
        a_re = xr + tr * cr - ti * ci
        a_im = xi + tr * ci + ti * cr
        ar_scr[...] = a_re
        ai_scr[...] = a_im
        cr_scr[0:1, :] = ar_scr[0:1, :]
        ci_scr[0:1, :] = ai_scr[0:1, :]

        sr, si = sr_ref[...], si_ref[...]
        rows = lax.broadcasted_iota(jnp.int32, (ct, 1), 0)
        keep_prev = chunk > 0
        pr_last = jnp.where(keep_prev, spr_ref[7:8, :], 0.0)
        pi_last = jnp.where(keep_prev, spi_ref[7:8, :], 0.0)
        sp_re = jnp.where(rows == 0, pr_last, pltpu.roll(sr, 1, 0))
        sp_im = jnp.where(rows == 0, pi_last, pltpu.roll(si, 1, 0))
        dar_ref[0] += jnp.sum(a_re * sp_re + a_im * sp_im, axis=0, keepdims=True)
        dai_ref[0] += jnp.sum(a_im * sp_re - a_re * sp_im, axis=0, keepdims=True)
        dctr_ref[0] += _dot(sr.astype(BF16), dytb, 0, 0)
        dcti_ref[0] += -_dot(si.astype(BF16), dytb, 0, 0)
        ub = u.astype(BF16)
        arb, aib = a_re.astype(BF16), a_im.astype(BF16)
        dbbr_ref[0] += _dot(ub, arb, 0, 0)
        dbbi_ref[0] += _dot(ub, aib, 0, 0)
        du = dyt * d_ref[...] + _dot(arb, btr_ref[0], 1, 0) + _dot(aib, bti_ref[0], 1, 0)
        valid = (chunk * ct + rows) >= cfg.pad
        du_ref[...] = jnp.where(valid, du, 0.0).astype(BF16)
        dd_ref[...] += jnp.sum(dyt * u, axis=0, keepdims=True)

    def mat(r, c_):
        return pl.BlockSpec((1, r, c_), lambda sb, b, c: (sb, 0, 0))

    lp8 = cfg.lp // 8
    ublk = pl.BlockSpec((ct, SSM_SB), lambda sb, b, c: (b * nc + nc - 1 - c, sb))
    sblk = pl.BlockSpec((ct, ns), lambda sb, b, c: (b * nc + nc - 1 - c, sb))
    sprev = pl.BlockSpec((8, ns), lambda sb, b, c: (jnp.maximum(b * lp8 + (nc - 1 - c) * (ct // 8) - 1, 0), sb))
    dvec = pl.BlockSpec((1, SSM_SB), lambda sb, b, c: (0, sb))
    return pl.pallas_call(
        body, name="ssm_bwd", grid=(nsb, cfg.b_loc, nc),
        in_specs=[ublk, ublk, ublk, sblk, sblk, sprev, sprev, mat(SSM_SB, ns), mat(SSM_SB, ns), mat(ns, SSM_SB),
                  mat(ns, SSM_SB), mat(N_POW, ns), mat(N_POW, ns), mat(ct, ns), mat(ct, ns), dvec],
        out_specs=[ublk, mat(ns, SSM_SB), mat(ns, SSM_SB), mat(SSM_SB, ns), mat(SSM_SB, ns), mat(1, ns), mat(1, ns),
                   dvec],
        out_shape=[_sds((t, cfg.d_ssm), BF16), _sds((nsb, ns, SSM_SB), F32), _sds((nsb, ns, SSM_SB), F32),
                   _sds((nsb, SSM_SB, ns), F32), _sds((nsb, SSM_SB, ns), F32), _sds((nsb, 1, ns), F32),
                   _sds((nsb, 1, ns), F32), _sds((1, cfg.d_ssm), F32)],
        scratch_shapes=[pltpu.VMEM((8, ns), F32), pltpu.VMEM((8, ns), F32), pltpu.VMEM((ct, ns), F32),
                        pltpu.VMEM((ct, ns), F32)],
        compiler_params=_cp(("arbitrary", "arbitrary", "arbitrary")))(
            dy2, ytot, proj, s_re, s_im, s_re, s_im, cf_re, cf_im, bbt_re, bbt_im, pw_re, pw_im, tabr_re, tabr_im,
            dskip.reshape(1, cfg.d_ssm))


def _ssm_tables(cfg, lam_re, lam_im, log_dt, b_re, b_im, c_re, c_im):
    g, n, c = cfg.groups, cfg.ssm_state, cfg.ssm_group
    gsb = SSM_SB // c
    nsb = cfg.d_ssm // SSM_SB
    ns = gsb * n
    br_t, bi_t = jnp.transpose(b_re, (2, 0, 1)), jnp.transpose(b_im, (2, 0, 1))
    prm = (lam_re, lam_im, log_dt.reshape(g, 1), br_t, bi_t)
    _, _, bbr, bbi, pr, pi, tr, ti = ssm_param_fwd(cfg, *prm)
    eye = jnp.eye(gsb, dtype=F32)

    def bdiag_b(x):
        return jnp.einsum('csgn,gh->sgchn', x.reshape(c, nsb, gsb, n), eye).reshape(nsb, SSM_SB, ns)

    def bdiag_c(x):
        return jnp.einsum('sgcn,gh->sgchn', x.reshape(nsb, gsb, c, n), eye).reshape(nsb, SSM_SB, ns)

    def lanes(x):
        return jnp.transpose(x.reshape(x.shape[0], nsb, ns), (1, 0, 2))

    bb_re, bb_im = bdiag_b(bbr), bdiag_b(bbi)
    cf_re, cf_im = bdiag_c(c_re), bdiag_c(c_im)
    sw = lambda x: jnp.swapaxes(x, 1, 2)
    return dict(prm=prm, bb_re=bb_re.astype(BF16), bb_im=bb_im.astype(BF16), bbt_re=sw(bb_re).astype(BF16),
                bbt_im=sw(bb_im).astype(BF16), cf_re=cf_re.astype(BF16), cf_im=cf_im.astype(BF16),
                ct_re=sw(cf_re).astype(BF16), ct_im=sw(cf_im).astype(BF16), pw_re=lanes(pr), pw_im=lanes(pi),
                tab_re=lanes(tr), tab_im=lanes(ti), tabr_re=lanes(tr[::-1]), tabr_im=lanes(ti[::-1]))


def _ssm_param_grads(cfg, tabs, dct_re, dct_im, dbb_re, dbb_im, dab_re, dab_im):
    g, n, c = cfg.groups, cfg.ssm_state, cfg.ssm_group
    gsb = SSM_SB // c
    nsb = cfg.d_ssm // SSM_SB

    def diag_b(x):
        d = jnp.einsum('sgcgn->sgcn', x.reshape(nsb, gsb, c, gsb, n))
        return jnp.transpose(d.reshape(g, c, n), (1, 0, 2))

    def diag_c(x):
        d = jnp.einsum('sgngc->sgcn', x.reshape(nsb, gsb, n, gsb, c))
        return d.reshape(g, c, n)

    dlr, dli, ddt, dbr_t, dbi_t = ssm_param_bwd(cfg, *tabs['prm'], dab_re.reshape(g, n), dab_im.reshape(g, n),
                                                diag_b(dbb_re), diag_b(dbb_im))
    return dict(ssm_lam_re=dlr, ssm_lam_im=dli, ssm_log_dt=ddt.reshape(g),
                ssm_b_re=jnp.transpose(dbr_t, (1, 2, 0)), ssm_b_im=jnp.transpose(dbi_t, (1, 2, 0)),
                ssm_c_re=diag_c(dct_re), ssm_c_im=diag_c(dct_im))


def _mm_tiles(t, n, k):
    tm = _tile(t, 1056 if k <= 3072 else 528, 16)
    tn = _tile(n, 512, 128)
    return tm, tn


def _nt(a, wt, name, out_dtype=F32):
    tm, tn = _mm_tiles(a.shape[0], wt.shape[0], a.shape[1])
    return mm([(a, wt, 0)], nt=True, n=wt.shape[0], tm=tm, tn=tn, out_dtype=out_dtype, name=name)


def _nn(pairs, n, name, res=None, res_scale=1.0, out_dtype=F32):
    kmax = max(p[0].shape[1] for p in pairs) * len(pairs)
    tm, tn = _mm_tiles(pairs[0][0].shape[0], n, kmax)
    return mm(pairs, nt=False, n=n, tm=tm, tn=tn, out_dtype=out_dtype, name=name, res=res, res_scale=res_scale)


def local_step(cfg, x, target, rep, ex):
    d, lp, bsz, t = cfg.d_model, cfg.lp, cfg.b_loc, cfg.tokens
    meta = jnp.broadcast_to(ex.meta_tokens[None], (bsz, cfg.n_meta, d))
    h = jnp.concatenate([jnp.zeros((bsz, cfg.pad, d), F32), meta, x], axis=1).reshape(t, d)

    saved = []
    for l in range(cfg.depth):
        w = ex.weights(l)
        r = {k: v[l] for k, v in rep.items() if k != 'final_norm'}
        s = {'h0': h}
        s['xn1'] = rms_fwd(h, r['ffn1_norm'], "rms_fwd")
        s['a1'], s['b1'], s['act1'] = dual_mm_act(s['xn1'], w['w13t_1'], kind='swiglu', name="ffn_up")
        h = _nn([(s['act1'], w['w2_1'], 0)], d, "ffn_down", res=h, res_scale=0.5)
        s['h1'] = h
        s['xnm'] = rms_fwd(h, r['mix_norm'], "rms_fwd")
        proj = _nt(s['xnm'], w['w_int'], "proj_in")
        s['proj'] = proj
        tabs = _ssm_tables(cfg, r['ssm_lam_re'], r['ssm_lam_im'], r['ssm_log_dt'], r['ssm_b_re'], r['ssm_b_im'],
                           r['ssm_c_re'], r['ssm_c_im'])
        s['tabs'] = tabs
        s['s_re'], s['s_im'], s['ytot'], s['y2'] = ssm_fwd(
            cfg, proj, tabs['bb_re'], tabs['bb_im'], tabs['ct_re'], tabs['ct_im'], tabs['pw_re'], tabs['pw_im'],
            tabs['tab_re'], tabs['tab_im'], r['ssm_d'])
        s['ga'], s['gg'], o_ssm = dual_mm_act(s['y2'], w['w_glut'], kind='glu', name="ssm_glu")
        s['cpre'], s['hn'] = conv_fwd(cfg, proj, w['conv_w'], r['conv_b'], r['conv_ln_g'], r['conv_ln_b'])
        s['o_conv'] = _nt(s['hn'], w['conv_w_outt'], "branch_out")
        src = ex.gather_src(l)
        s['o'], s['lt'], *arrived = attn_fwd(cfg, proj, src)
        if src is not None:
            ex.gathered(l, arrived[0])
        s['o_attn'] = _nt(s['o'], w['attn_w_ot'], "branch_out")
        s['merged'] = merge_fwd(cfg, proj, o_ssm, s['o_conv'], s['o_attn'])
        h = _nn([(s['merged'], w['w_out'], 0)], d, "mix_out", res=h, res_scale=1.0)
        s['h2'] = h
        s['xn2'] = rms_fwd(h, r['ffn2_norm'], "rms_fwd")
        s['a2'], s['b2'], s['act2'] = dual_mm_act(s['xn2'], w['w13t_2'], kind='swiglu', name="ffn_up")
        h = _nn([(s['act2'], w['w2_2'], 0)], d, "ffn_down", res=h, res_scale=0.5)
        saved.append((w, r, s))

    dh3, dh3_16, dgf, loss_sq = final_fwd_bwd(cfg, h.reshape(bsz, lp, d), rep['final_norm'], target)
    dh, dh16 = dh3.reshape(t, d), dh3_16.reshape(t, d)
    loss = 0.5 * loss_sq[0, 0] / d

    gl = []
    for l in reversed(range(cfg.depth)):
        w, r, s = saved[l]
        g = {}

        def ffn_bwd(dh, dh16, tag, a, b, act, xn, hin, norm):
            da, db = ffn_down_bwd(dh16, w['w2_' + tag], a, b, "ffn_down_bwd")
            g['w2_' + tag] = mm_tn(act, dh16, name="ffn_dw2", scale=0.5)
            dxn = _nn([(da, w['w13t_' + tag], 0), (db, w['w13t_' + tag], 1)], d, "ffn_dxn")
            g['w13t_' + tag] = jnp.concatenate([mm_tn(da, xn, name="ffn_dw13"), mm_tn(db, xn, name="ffn_dw13")], 0)
            return rms_bwd(dxn, hin, norm, dh, "rms_bwd")

        dh, dh16, dg2 = ffn_bwd(dh, dh16, '2', s['a2'], s['b2'], s['act2'], s['xn2'], s['h2'], r['ffn2_norm'])
        g['ffn2_norm'] = dg2.reshape(d)

        proj, tabs = s['proj'], s['tabs']
        dmerged = _nt(dh16, w['w_out'], "mix_dmerged")
        g['w_out'] = mm_tn(s['merged'], dh16, name="mix_dwout")
        dg0, dg1, dg2_, dga, dgg, do_conv, do_attn = merge_bwd(cfg, dmerged, proj, s['ga'], s['gg'], s['o_conv'],
                                                               s['o_attn'])
        dy2 = _nn([(dga, w['w_glut'], 0), (dgg, w['w_glut'], 1)], cfg.d_ssm, "ssm_dy2")
        g['w_glut'] = jnp.concatenate([mm_tn(dga, s['y2'], name="ssm_dwglu"), mm_tn(dgg, s['y2'], name="ssm_dwglu")], 0)
        du, dct_re, dct_im, dbb_re, dbb_im, dab_re, dab_im, dd = ssm_bwd(
            cfg, dy2, s['ytot'], proj, s['s_re'], s['s_im'], tabs['cf_re'], tabs['cf_im'], tabs['bbt_re'],
            tabs['bbt_im'], tabs['pw_re'], tabs['pw_im'], tabs['tabr_re'], tabs['tabr_im'], r['ssm_d'])
        g.update(_ssm_param_grads(cfg, tabs, dct_re, dct_im, dbb_re, dbb_im, dab_re, dab_im))
        g['ssm_d'] = dd.reshape(cfg.d_ssm)

        dhn = _nn([(do_conv, w['conv_w_outt'], 0)], cfg.d_conv, "branch_din")
        g['conv_w_outt'] = mm_tn(do_conv, s['hn'], name="branch_dw")
        dxa, dxg, dcw, dcb, dlg, dlb = conv_bwd(cfg, dhn, s['cpre'], proj, w['conv_w'], r['conv_ln_g'],
                                                r['conv_ln_b'])
        g['conv_w'], g['conv_b'] = dcw, dcb.reshape(cfg.d_conv)
        g['conv_ln_g'], g['conv_ln_b'] = dlg.reshape(cfg.d_conv), dlb.reshape(cfg.d_conv)

        do = _nn([(do_attn, w['attn_w_ot'], 0)], cfg.d_attn, "branch_din")
        g['attn_w_ot'] = mm_tn(do_attn, s['o'], name="branch_dw")
        send = ex.send_src(l)
        dq, dk, dv, *arrived = attn_bwd(cfg, proj, s['lt'], do, send)
        if send is not None:
            ex.received(l, arrived[0])

        dproj = jnp.concatenate([du, dxa, dxg, dq, dk, dv, dg0, dg1, dg2_],
                                axis=1)
        dxn = _nn([(dproj, w['w_int'], 0)], d, "proj_dxn")
        g['w_int'] = mm_tn(dproj, s['xnm'], name="proj_dw")
        dh, dh16, dgm = rms_bwd(dxn, s['h1'], r['mix_norm'], dh, "rms_bwd")
        g['mix_norm'] = dgm.reshape(d)

        dh, dh16, dg1n = ffn_bwd(dh, dh16, '1', s['a1'], s['b1'], s['act1'], s['xn1'], s['h0'], r['ffn1_norm'])
        g['ffn1_norm'] = dg1n.reshape(d)
        ex.matrix_grads(l, {name: g.pop(name) for name, _, _ in PACKED})
        gl.append(g)

    gl = gl[::-1]
    dh0 = dh.reshape(bsz, lp, d)
    grad_x = dh0[:, Q_BLOCK:]
    grads = {k: jnp.stack([g[k] for g in gl]) for k in gl[0]}
    grads['meta_tokens'] = jnp.sum(dh0[:, cfg.pad:Q_BLOCK], axis=0)
    grads['final_norm'] = dgf.reshape(d)
    return loss, grad_x, grads


def _me():
    return lax.axis_index("x"), lax.axis_index("y"), lax.axis_index("c")


def _flat(px, py, pc):
    return 4 * px + 2 * py + pc


def all_gather_blocks(block, name):
    r, c_ = block.shape

    def body(x_ref, out_ref, send_sems, recv_sems, local_sem):
        gather_start(x_ref, out_ref, send_sems, recv_sems, local_sem)
        gather_finish(x_ref, out_ref, send_sems, recv_sems, local_sem)

    return pl.pallas_call(
        body, name=name, out_shape=_sds((8, r, c_), block.dtype),
        in_specs=[pl.BlockSpec(memory_space=pl.ANY)], out_specs=pl.BlockSpec(memory_space=pl.ANY),
        scratch_shapes=exchange_semaphores())(block)


def exchange_semaphores():
    return [pltpu.SemaphoreType.DMA((7,)), pltpu.SemaphoreType.DMA((7,)), pltpu.SemaphoreType.DMA(())]


def _gather_copies(x_ref, out_ref, send_sems, recv_sems, local_sem):
    x, y, c = _me()
    me, sibling = (x, y, c), (x, y, 1 - c)
    chips = [(1 - x, y), (x, 1 - y), (1 - x, 1 - y)]

    def slot(px, py, pc):
        return out_ref.at[_flat(px, py, pc)]

    def copy(k, blk, to, src=None):
        return pltpu.make_async_remote_copy(
            src_ref=slot(*blk) if src is None else src, dst_ref=slot(*blk), send_sem=send_sems.at[k],
            recv_sem=recv_sems.at[k], device_id=to, device_id_type=pl.DeviceIdType.MESH)

    mine = pltpu.make_async_copy(x_ref, slot(*me), local_sem)
    first = [copy(0, me, sibling, src=x_ref)] + [copy(1 + j, me, (*chip, c), src=x_ref) for j, chip in enumerate(chips)]
    passed = [copy(4 + j, (*chip, c), sibling) for j, chip in enumerate(chips)]
    over_ici = [copy(1 + j, (*chip, c), me) for j, chip in enumerate(chips)]
    from_sibling = [copy(0, sibling, me)] + [copy(4 + j, (*chip, 1 - c), me) for j, chip in enumerate(chips)]
    return mine, first, passed, over_ici, from_sibling


def gather_start(*refs):
    mine, first, _, _, _ = _gather_copies(*refs)
    mine.start()
    for cp in first:
        cp.start()


def gather_finish(*refs):
    mine, first, passed, over_ici, from_sibling = _gather_copies(*refs)
    for arrival, forward in zip(over_ici, passed):
        arrival.wait_recv()
        forward.start()
    for arrival in from_sibling:
        arrival.wait_recv()
    for cp in first + passed:
        cp.wait_send()
    mine.wait()


def all_to_all_blocks(send, name):
    _, r, c_ = send.shape

    def body(s_ref, out_ref, send_sems, recv_sems, local_sem):
        all_to_all_start(s_ref, out_ref, send_sems, recv_sems, local_sem)
        all_to_all_finish(s_ref, out_ref, send_sems, recv_sems, local_sem)

    return pl.pallas_call(
        body, name=name, out_shape=_sds((8, r, c_), send.dtype),
        in_specs=[pl.BlockSpec(memory_space=pl.ANY)], out_specs=pl.BlockSpec(memory_space=pl.ANY),
        scratch_shapes=exchange_semaphores())(send)


def _all_to_all_copies(s_ref, out_ref, send_sems, recv_sems, local_sem):
    x, y, c = _me()
    me = _flat(x, y, c)
    mine = pltpu.make_async_copy(s_ref.at[me], out_ref.at[me], local_sem)
    copies = []
    for rel in range(1, 8):
        px = 1 - x if rel & 4 else x
        py = 1 - y if rel & 2 else y
        pc = 1 - c if rel & 1 else c
        copies.append(pltpu.make_async_remote_copy(
            src_ref=s_ref.at[_flat(px, py, pc)], dst_ref=out_ref.at[me], send_sem=send_sems.at[rel - 1],
            recv_sem=recv_sems.at[rel - 1], device_id=(px, py, pc), device_id_type=pl.DeviceIdType.MESH))
    return mine, copies


def all_to_all_start(*refs):
    mine, copies = _all_to_all_copies(*refs)
    mine.start()
    for cp in copies:
        cp.start()


def all_to_all_finish(*refs):
    mine, copies = _all_to_all_copies(*refs)
    for cp in copies:
        cp.wait_recv()
    for cp in copies:
        cp.wait_send()
    mine.wait()


def reduce_blocks(recv, name):
    nsrc, r, c_ = recv.shape
    tr = _tile(r, 128, 16)

    def body(x_ref, o_ref):
        acc = x_ref[0].astype(F32)
        for s in range(1, nsrc):
            acc = acc + x_ref[s].astype(F32)
        o_ref[...] = acc

    return pl.pallas_call(
        body, name=name, grid=(r // tr,),
        in_specs=[pl.BlockSpec((nsrc, tr, c_), lambda i: (0, i, 0))], out_specs=pl.BlockSpec((tr, c_), lambda i: (i, 0)),
        out_shape=_sds((r, c_), F32), compiler_params=_cp(("parallel",)))(recv)


def adamw(w, g, m, v):
    shape = w.shape
    size = math.prod(shape)
    if shape[-1] < V7X_LANES and size % 1024 == 0:
        view = (size // 1024, 1024)
    else:
        view = (size // shape[-1], shape[-1])
    rows, cols = view
    tr = _tile(rows, max(8, (1 << 19) // cols // 8 * 8), 8)
    c1 = 1.0 - ADAM_B1 ** ADAM_STEP
    c2 = 1.0 - ADAM_B2 ** ADAM_STEP

    def body(w_ref, g_ref, m_ref, v_ref, d_ref, nm_ref, nv_ref):
        gv = g_ref[...]
        nm = ADAM_B1 * m_ref[...] + (1.0 - ADAM_B1) * gv
        nv = ADAM_B2 * v_ref[...] + (1.0 - ADAM_B2) * (gv * gv)
        nm_ref[...] = nm
        nv_ref[...] = nv
        d_ref[...] = -ADAM_LR * ((nm / c1) / (jnp.sqrt(nv / c2) + ADAM_EPS) + ADAM_WD * w_ref[...])

    blk = pl.BlockSpec((tr, cols), lambda i: (i, 0))
    out = _sds(view, F32)
    res = pl.pallas_call(
        body, name="adamw", grid=(rows // tr,), in_specs=[blk] * 4, out_specs=[blk] * 3, out_shape=[out] * 3,
        compiler_params=_cp(("parallel",)))(w.reshape(view), g.reshape(view), m.reshape(view), v.reshape(view))
    return tuple(a.reshape(shape) for a in res)


PACKED = [('w13t_1', 'ffn1_w13', True), ('w2_1', 'ffn1_w2', False), ('w_int', 'w_in', True), ('w_out', 'w_out', False),
          ('w13t_2', 'ffn2_w13', True), ('w2_2', 'ffn2_w2', False), ('w_glut', 'ssm_w_glu', True),
          ('conv_w_outt', 'conv_w_out', True), ('attn_w_ot', 'attn_w_o', True)]


def _rows_of(a, d):
    return a.reshape(a.shape[0] * a.shape[1] // d, d)


def _pad_rows(flat, d, mult):
    n = flat.shape[-1]
    rows = -(-n // d)
    rows = -(-rows // mult) * mult
    flat = jnp.pad(flat, [(0, 0)] * (flat.ndim - 1) + [(0, rows * d - n)])
    return flat.reshape(flat.shape[:-1] + (rows, d))


def pack_weight_shards(cfg, wts):
    d = cfg.d_model
    blocks, layout = [], []
    for l in range(cfg.depth):
        parts = []
        for name, src, tr in PACKED:
            a = wts[src][l]
            a = (a.T if tr else a).astype(BF16)
            if l == 0:
                layout.append((name, a.shape))
            parts.append(_rows_of(a, d))
        blocks.append(jnp.concatenate(parts, axis=0))
    small = jnp.concatenate([wts['conv_w'].reshape(-1), wts['meta_tokens'].reshape(-1)])
    return blocks, _pad_rows(small, d, 8), layout


def unpack_gathered_layer(cfg, gathered, layout):
    d = cfg.d_model
    out = {}
    off = 0
    for name, shape in layout:
        rows = shape[0] * shape[1] // d
        out[name] = gathered[:, off:off + rows].reshape(8 * shape[0], shape[1])
        off += rows
    return out


def unpack_gathered_small(gathered_small, wts):
    n_cw, n_mt = wts['conv_w'].size, wts['meta_tokens'].size
    small = gathered_small.reshape(8, -1)
    cw = small[:, :n_cw].reshape((8,) + wts['conv_w'].shape)
    conv_w = jnp.transpose(cw, (1, 2, 0, 3)).reshape(cw.shape[1], cw.shape[2], -1)
    mt = small[:, n_cw:n_cw + n_mt].reshape((8,) + wts['meta_tokens'].shape)
    return conv_w, jnp.transpose(mt, (1, 0, 2)).reshape(mt.shape[1], -1)


def pack_layer_grads(cfg, g, layout):
    d = cfg.d_model
    parts = [g[name].reshape(8, shape[0] * shape[1] // d, d).astype(BF16) for name, shape in layout]
    return jnp.concatenate(parts, axis=1)


def pack_small_grads(cfg, grads, rep_names):
    cw = grads['conv_w']
    cw = jnp.transpose(cw.reshape(cw.shape[0], cw.shape[1], 8, -1), (2, 0, 1, 3)).reshape(8, -1)
    mt = grads['meta_tokens']
    mt = jnp.transpose(mt.reshape(mt.shape[0], 8, -1), (1, 0, 2)).reshape(8, -1)
    repl = jnp.concatenate([grads[k].reshape(-1) for k in rep_names])
    small = jnp.concatenate([cw, mt, jnp.broadcast_to(repl[None], (8, repl.shape[0]))], axis=1)
    return _pad_rows(small, cfg.d_model, 128)


def unpack_grads(cfg, gsums, gsmall, layout, wts, rep_names):
    d = cfg.d_model
    acc = {}
    for gsum in gsums:
        off = 0
        for name, shape in layout:
            rows = shape[0] * shape[1] // d
            acc.setdefault(name, []).append(gsum[off:off + rows].reshape(shape))
            off += rows
    out = {}
    for name, src, tr in PACKED:
        a = jnp.stack(acc[name])
        out[src] = jnp.swapaxes(a, 1, 2) if tr else a
    flat = gsmall.reshape(-1)
    pos = 0
    for k in ['conv_w', 'meta_tokens'] + rep_names:
        n = wts[k].size
        out[k] = flat[pos:pos + n].reshape(wts[k].shape)
        pos += n
    return out


class WeightExchange:
    def __init__(self, cfg, wts):
        self.cfg = cfg
        self.blocks, small, self.layout = pack_weight_shards(cfg, wts)
        self.conv_w, self.meta_tokens = unpack_gathered_small(all_gather_blocks(small, "all_gather_small_weights"), wts)
        self.full = {0: unpack_gathered_layer(cfg, all_gather_blocks(self.blocks[0], "all_gather_weights"), self.layout)}
        self.to_send, self.arrived = {}, {}

    def weights(self, l):
        return dict(self.full[l], conv_w=self.conv_w[l])

    def gather_src(self, l):
        return self.blocks[l + 1] if l + 1 < self.cfg.depth else None

    def gathered(self, l, blocks):
        self.full[l + 1] = unpack_gathered_layer(self.cfg, blocks, self.layout)

    def matrix_grads(self, l, g):
        self.to_send[l] = pack_layer_grads(self.cfg, g, self.layout)

    def send_src(self, l):
        return self.to_send.get(l + 1)

    def received(self, l, blocks):
        self.arrived[l + 1] = blocks

    def finish(self, grads, wts):
        self.arrived[0] = all_to_all_blocks(self.to_send[0], "all_to_all_grads")
        gsums = [reduce_blocks(self.arrived[l], "reduce_grads") for l in range(self.cfg.depth)]
        small = all_to_all_blocks(pack_small_grads(self.cfg, grads, REPLICATED), "all_to_all_small_grads")
        return unpack_grads(self.cfg, gsums, reduce_blocks(small, "reduce_small_grads"), self.layout, wts, REPLICATED)


def train_step(cfg, x, target, wts, ms, vs):
    ex = WeightExchange(cfg, wts)
    rep = {k: wts[k] for k in REPLICATED}
    loss, grad_x, grads = local_step(cfg, x, target, rep, ex)
    gw = ex.finish(grads, wts)
    loss = lax.psum(loss, ("x", "y", "c"))
    deltas, new_m, new_v = {}, {}, {}
    for k in WEIGHT_NAMES:
        deltas[k], new_m[k], new_v[k] = adamw(wts[k], gw[k], ms[k], vs[k])
    return (loss, grad_x, *[gw[k] for k in WEIGHT_NAMES], *[deltas[k] for k in WEIGHT_NAMES],
            *[new_m[k] for k in WEIGHT_NAMES], *[new_v[k] for k in WEIGHT_NAMES])


def kernel(x, meta_tokens, ffn1_norm, ffn1_w13, ffn1_w2, mix_norm, w_in, ssm_lam_re, ssm_lam_im, ssm_log_dt, ssm_b_re, ssm_b_im, ssm_c_re, ssm_c_im, ssm_d, ssm_w_glu, conv_w, conv_b, conv_ln_g, conv_ln_b, conv_w_out, attn_w_o, w_out, ffn2_norm, ffn2_w13, ffn2_w2, final_norm, loss_target, m_meta_tokens, m_ffn1_norm, m_ffn1_w13, m_ffn1_w2, m_mix_norm, m_w_in, m_ssm_lam_re, m_ssm_lam_im, m_ssm_log_dt, m_ssm_b_re, m_ssm_b_im, m_ssm_c_re, m_ssm_c_im, m_ssm_d, m_ssm_w_glu, m_conv_w, m_conv_b, m_conv_ln_g, m_conv_ln_b, m_conv_w_out, m_attn_w_o, m_w_out, m_ffn2_norm, m_ffn2_w13, m_ffn2_w2, m_final_norm, v_meta_tokens, v_ffn1_norm, v_ffn1_w13, v_ffn1_w2, v_mix_norm, v_w_in, v_ssm_lam_re, v_ssm_lam_im, v_ssm_log_dt, v_ssm_b_re, v_ssm_b_im, v_ssm_c_re, v_ssm_c_im, v_ssm_d, v_ssm_w_glu, v_conv_w, v_conv_b, v_conv_ln_g, v_conv_ln_b, v_conv_w_out, v_attn_w_o, v_w_out, v_ffn2_norm, v_ffn2_w13, v_ffn2_w2, v_final_norm):
    given = dict(locals())
    wts = {k: given[k] for k in WEIGHT_NAMES}
    ms = {k: given["m_" + k] for k in WEIGHT_NAMES}
    vs = {k: given["v_" + k] for k in WEIGHT_NAMES}
    return train_step(FULL, x, loss_target, wts, ms, vs)
```

```python
import functools
import math
from typing import NamedTuple

import jax
import jax.numpy as jnp
from jax import lax
from jax.experimental import pallas as pl
from jax.experimental.pallas import tpu as pltpu

F32 = jnp.float32
BF16 = jnp.bfloat16
RMS_EPS = 1e-6
LN_EPS = 1e-5
ADAM_LR = 0.001
ADAM_B1 = 0.9
ADAM_B2 = 0.999
ADAM_EPS = 1e-08
ADAM_WD = 0.01
ADAM_STEP = 10
V7X_LANES = 128
V7X_VMEM_LIMIT = 52 * 1024 * 1024
Q_BLOCK = 128
SSM_CHUNK = 128
SSM_SB = 128
CONV_HALO = 32

WEIGHT_NAMES = ['meta_tokens', 'ffn1_norm', 'ffn1_w13', 'ffn1_w2', 'mix_norm', 'w_in', 'ssm_lam_re', 'ssm_lam_im',
                'ssm_log_dt', 'ssm_b_re', 'ssm_b_im', 'ssm_c_re', 'ssm_c_im', 'ssm_d', 'ssm_w_glu', 'conv_w', 'conv_b',
                'conv_ln_g', 'conv_ln_b', 'conv_w_out', 'attn_w_o', 'w_out', 'ffn2_norm', 'ffn2_w13', 'ffn2_w2',
                'final_norm']
REPLICATED = ['ffn1_norm', 'mix_norm', 'ssm_lam_re', 'ssm_lam_im', 'ssm_log_dt', 'ssm_b_re', 'ssm_b_im', 'ssm_c_re',
              'ssm_c_im', 'ssm_d', 'conv_b', 'conv_ln_g', 'conv_ln_b', 'ffn2_norm', 'final_norm']


class Cfg(NamedTuple):
    d_model: int
    seq: int
    depth: int
    d_ff: int
    b_loc: int
    ndev: int = 8
    n_meta: int = 16
    n_heads: int = 8
    head_dim: int = 64
    ssm_group: int = 16
    ssm_state: int = 64
    conv_width: int = 31

    @property
    def d_ssm(self): return self.d_model // 2
    @property
    def groups(self): return self.d_ssm // self.ssm_group
    @property
    def n_state(self): return self.groups * self.ssm_state
    @property
    def d_conv(self): return self.d_model // 2
    @property
    def d_attn(self): return self.n_heads * self.head_dim
    @property
    def off_xa(self): return self.d_ssm
    @property
    def off_xg(self): return self.d_ssm + self.d_conv
    @property
    def off_q(self): return self.d_ssm + 2 * self.d_conv
    @property
    def off_k(self): return self.off_q + self.d_attn
    @property
    def off_v(self): return self.off_k + self.d_attn
    @property
    def off_gate(self): return self.off_v + self.d_attn
    @property
    def d_in(self): return self.off_gate + 3 * self.d_model
    @property
    def pad(self): return Q_BLOCK - self.n_meta
    @property
    def lp(self): return Q_BLOCK + self.seq
    @property
    def tokens(self): return self.b_loc * self.lp


FULL = Cfg(d_model=1024, seq=4096, depth=4, d_ff=2816, b_loc=2)


def _tile(n, pref, align):
    for t in range(min(pref, n), 0, -1):
        if n % t == 0 and t % align == 0:
            return t
    return n


def _cp(sem):
    return pltpu.CompilerParams(dimension_semantics=sem, vmem_limit_bytes=V7X_VMEM_LIMIT)


def _sds(shape, dtype):
    return jax.ShapeDtypeStruct(shape, dtype)


def _sigmoid(x):
    return jax.nn.sigmoid(x)


def _dot(a, b, ca, cb):
    return lax.dot_general(a, b, (((ca,), (cb,)), ((), ())), preferred_element_type=F32)


def rms_fwd(h, g, name):
    t, d = h.shape
    tm = _tile(t, 1056, 16)

    def body(h_ref, g_ref, o_ref):
        x = h_ref[...]
        r = lax.rsqrt(jnp.mean(x * x, axis=-1, keepdims=True) + RMS_EPS)
        o_ref[...] = (x * r * g_ref[...]).astype(BF16)

    return pl.pallas_call(
        body, name=name, grid=(t // tm,),
        in_specs=[pl.BlockSpec((tm, d), lambda i: (i, 0)), pl.BlockSpec((1, d), lambda i: (0, 0))],
        out_specs=pl.BlockSpec((tm, d), lambda i: (i, 0)), out_shape=_sds((t, d), BF16),
        compiler_params=_cp(("parallel",)))(h, g.reshape(1, d))


def rms_bwd(dxn, h, g, dh_res, name):
    t, d = h.shape
    tm = _tile(t, 528, 16)

    def body(dxn_ref, h_ref, g_ref, r_ref, dh_ref, dh16_ref, dg_ref):
        x = h_ref[...]
        r = lax.rsqrt(jnp.mean(x * x, axis=-1, keepdims=True) + RMS_EPS)
        xhat = x * r
        dy = dxn_ref[...]
        dyg = dy * g_ref[...]
        dh = r_ref[...] + r * (dyg - xhat * jnp.mean(dyg * xhat, axis=-1, keepdims=True))
        dh_ref[...] = dh
        dh16_ref[...] = dh.astype(BF16)

        @pl.when(pl.program_id(0) == 0)
        def _():
            dg_ref[...] = jnp.zeros_like(dg_ref)

        dg_ref[...] += jnp.sum(dy * xhat, axis=0, keepdims=True)

    row = pl.BlockSpec((tm, d), lambda i: (i, 0))
    vec = pl.BlockSpec((1, d), lambda i: (0, 0))
    return pl.pallas_call(
        body, name=name, grid=(t // tm,), in_specs=[row, row, vec, row], out_specs=[row, row, vec],
        out_shape=[_sds((t, d), F32), _sds((t, d), BF16), _sds((1, d), F32)],
        compiler_params=_cp(("arbitrary",)))(dxn, h, g.reshape(1, d), dh_res)


def mm(pairs, *, nt, n, tm, tn, out_dtype, name, res=None, res_scale=1.0):
    m = pairs[0][0].shape[0]
    np_ = len(pairs)

    def body(*refs):
        o_ref = refs[-1]
        acc = None
        for p in range(np_):
            a = refs[2 * p][...].astype(BF16)
            b = refs[2 * p + 1][...].astype(BF16)
            d = _dot(a, b, 1, 1 if nt else 0)
            acc = d if acc is None else acc + d
        if res is not None:
            acc = refs[2 * np_][...] + res_scale * acc
        o_ref[...] = acc.astype(out_dtype)

    in_specs, args = [], []
    for a, b, kblk in pairs:
        k = a.shape[1]
        in_specs.append(pl.BlockSpec((tm, k), lambda i, j: (i, 0)))
        if nt:
            nb = n // tn
            in_specs.append(pl.BlockSpec((tn, k), functools.partial(lambda i, j, o: (j + o, 0), o=kblk * nb)))
        else:
            in_specs.append(pl.BlockSpec((k, tn), functools.partial(lambda i, j, o: (o, j), o=kblk)))
        args += [a, b]
    if res is not None:
        in_specs.append(pl.BlockSpec((tm, tn), lambda i, j: (i, j)))
        args.append(res)
    return pl.pallas_call(
        body, name=name, grid=(m // tm, n // tn), in_specs=in_specs,
        out_specs=pl.BlockSpec((tm, tn), lambda i, j: (i, j)), out_shape=_sds((m, n), out_dtype),
        compiler_params=_cp(("parallel", "arbitrary")))(*args)


def mm_tn(a, b, *, name, scale=1.0):
    t, m = a.shape
    n = b.shape[1]
    tm = _tile(m, 1536, 128)
    tn = _tile(n, 1024, 128)
    tk = _tile(t, 1056, 16)
    nk = t // tk

    def body(a_ref, b_ref, o_ref):
        k = pl.program_id(2)

        @pl.when(k == 0)
        def _():
            o_ref[...] = jnp.zeros_like(o_ref)

        o_ref[...] += _dot(a_ref[...].astype(BF16), b_ref[...].astype(BF16), 0, 0)
        if scale != 1.0:
            @pl.when(k == nk - 1)
            def _():
                o_ref[...] = o_ref[...] * scale

    return pl.pallas_call(
        body, name=name, grid=(m // tm, n // tn, nk),
        in_specs=[pl.BlockSpec((tk, tm), lambda i, j, k: (k, i)), pl.BlockSpec((tk, tn), lambda i, j, k: (k, j))],
        out_specs=pl.BlockSpec((tm, tn), lambda i, j, k: (i, j)), out_shape=_sds((m, n), F32),
        compiler_params=_cp(("parallel", "parallel", "arbitrary")))(a, b)


def dual_mm_act(x, wt, *, kind, name):
    t, k = x.shape
    hdim = wt.shape[0] // 2
    tm = _tile(t, 528, 16)
    tn = _tile(hdim, 1408, 128)
    nb = hdim // tn
    act_dtype = BF16 if kind == 'swiglu' else F32

    def body(x_ref, wa_ref, wb_ref, a_ref, b_ref, act_ref):
        xv = x_ref[...]
        a = _dot(xv, wa_ref[...], 1, 1)
        b = _dot(xv, wb_ref[...], 1, 1)
        a_ref[...] = a.astype(act_dtype)
        b_ref[...] = b.astype(act_dtype)
        if kind == 'swiglu':
            act_ref[...] = (a * _sigmoid(a) * b).astype(act_dtype)
        else:
            act_ref[...] = (a * _sigmoid(b)).astype(act_dtype)

    ob = pl.BlockSpec((tm, tn), lambda j, i: (i, j))
    return pl.pallas_call(
        body, name=name, grid=(nb, t // tm),
        in_specs=[pl.BlockSpec((tm, k), lambda j, i: (i, 0)), pl.BlockSpec((tn, k), lambda j, i: (j, 0)),
                  pl.BlockSpec((tn, k), lambda j, i: (j + nb, 0))],
        out_specs=[ob, ob, ob], out_shape=[_sds((t, hdim), act_dtype)] * 3,
        compiler_params=_cp(("parallel", "arbitrary")))(x, wt, wt)


def ffn_down_bwd(dh, w2, a, b, name):
    t, d = dh.shape
    hdim = w2.shape[0]
    tm = _tile(t, 528, 16)
    tn = _tile(hdim, 1408, 128)

    def body(dh_ref, w_ref, a_ref, b_ref, da_ref, db_ref):
        dact = 0.5 * _dot(dh_ref[...].astype(BF16), w_ref[...], 1, 1)
        av = a_ref[...].astype(F32)
        sg = _sigmoid(av)
        da_ref[...] = (dact * b_ref[...].astype(F32) * sg * (1.0 + av * (1.0 - sg))).astype(BF16)
        db_ref[...] = (dact * av * sg).astype(BF16)

    ob = pl.BlockSpec((tm, tn), lambda j, i: (i, j))
    return pl.pallas_call(
        body, name=name, grid=(hdim // tn, t // tm),
        in_specs=[pl.BlockSpec((tm, d), lambda j, i: (i, 0)), pl.BlockSpec((tn, d), lambda j, i: (j, 0)), ob, ob],
        out_specs=[ob, ob], out_shape=[_sds((t, hdim), BF16), _sds((t, hdim), BF16)],
        compiler_params=_cp(("parallel", "arbitrary")))(dh, w2, a, b)


def final_fwd_bwd(cfg, h, gf, target):
    bsz, lp, d = h.shape
    nq = lp // Q_BLOCK

    def body(h_ref, g_ref, t_ref, dh_ref, dh16_ref, dg_ref, loss_ref):
        b, j = pl.program_id(0), pl.program_id(1)
        x = h_ref[0]
        r = lax.rsqrt(jnp.mean(x * x, axis=-1, keepdims=True) + RMS_EPS)
        xhat = x * r
        gv = g_ref[...]
        diff = jnp.where(j > 0, xhat * gv - t_ref[0], 0.0)
        dy = diff * (1.0 / d)
        dyg = dy * gv
        dh = r * (dyg - xhat * jnp.mean(dyg * xhat, axis=-1, keepdims=True))
        dh_ref[0] = dh
        dh16_ref[0] = dh.astype(BF16)

        @pl.when((b == 0) & (j == 0))
        def _():
            dg_ref[...] = jnp.zeros_like(dg_ref)
            loss_ref[...] = jnp.zeros_like(loss_ref)

        dg_ref[...] += jnp.sum(dy * xhat, axis=0, keepdims=True)
        loss_ref[...] += jnp.sum(jnp.sum(diff * diff, axis=1, keepdims=True), axis=0, keepdims=True)

    blk = pl.BlockSpec((1, Q_BLOCK, d), lambda b, j: (b, j, 0))
    return pl.pallas_call(
        body, name="final_loss", grid=(bsz, nq),
        in_specs=[blk, pl.BlockSpec((1, d), lambda b, j: (0, 0)),
                  pl.BlockSpec((1, Q_BLOCK, d), lambda b, j: (b, jnp.maximum(j - 1, 0), 0))],
        out_specs=[blk, blk, pl.BlockSpec((1, d), lambda b, j: (0, 0)), pl.BlockSpec((1, 1), lambda b, j: (0, 0))],
        out_shape=[_sds((bsz, lp, d), F32), _sds((bsz, lp, d), BF16), _sds((1, d), F32), _sds((1, 1), F32)],
        compiler_params=_cp(("arbitrary", "arbitrary")))(h, gf.reshape(1, d), target)


def _row_valid(cfg, tm, i):
    pos = (i * tm) % cfg.lp + lax.broadcasted_iota(jnp.int32, (tm, 1), 0)
    return pos >= cfg.pad


def merge_fwd(cfg, proj, o_ssm, o_conv, o_attn):
    t, d = o_ssm.shape
    tc = math.gcd(math.gcd(d, cfg.off_gate), 512)
    tm = _tile(cfg.lp, 528, 16)
    g0 = cfg.off_gate // tc
    nc = d // tc

    def body(g0_ref, g1_ref, g2_ref, s_ref, c_ref, a_ref, o_ref):
        valid = _row_valid(cfg, tm, pl.program_id(0))
        m = (_sigmoid(g0_ref[...]) * s_ref[...] + _sigmoid(g1_ref[...]) * c_ref[...]
             + _sigmoid(g2_ref[...]) * a_ref[...])
        o_ref[...] = jnp.where(valid, m, 0.0).astype(BF16)

    gate = [pl.BlockSpec((tm, tc), functools.partial(lambda i, j, o: (i, o + j), o=g0 + k * nc)) for k in range(3)]
    blk = pl.BlockSpec((tm, tc), lambda i, j: (i, j))
    return pl.pallas_call(
        body, name="merge_fwd", grid=(t // tm, nc), in_specs=gate + [blk, blk, blk], out_specs=blk,
        out_shape=_sds((t, d), BF16), compiler_params=_cp(("parallel", "parallel")))(
            proj, proj, proj, o_ssm, o_conv, o_attn)


def merge_bwd(cfg, dmerged, proj, ga, gg, o_conv, o_attn):
    t, d = dmerged.shape
    tc = math.gcd(math.gcd(d, cfg.off_gate), 512)
    tm = _tile(cfg.lp, 528, 16)
    g0 = cfg.off_gate // tc
    nc = d // tc

    def body(dm_ref, g0_ref, g1_ref, g2_ref, ga_ref, gg_ref, c_ref, a_ref,
             dg0_ref, dg1_ref, dg2_ref, dga_ref, dgg_ref, dc_ref, da_ref):
        dm = dm_ref[...]
        s0, s1, s2 = _sigmoid(g0_ref[...]), _sigmoid(g1_ref[...]), _sigmoid(g2_ref[...])
        sg = _sigmoid(gg_ref[...])
        gav = ga_ref[...]
        o_s = gav * sg
        dg0_ref[...] = (dm * o_s * s0 * (1.0 - s0)).astype(BF16)
        dg1_ref[...] = (dm * c_ref[...] * s1 * (1.0 - s1)).astype(BF16)
        dg2_ref[...] = (dm * a_ref[...] * s2 * (1.0 - s2)).astype(BF16)
        dos = dm * s0
        dga_ref[...] = (dos * sg).astype(BF16)
        dgg_ref[...] = (dos * gav * sg * (1.0 - sg)).astype(BF16)
        dc_ref[...] = (dm * s1).astype(BF16)
        da_ref[...] = (dm * s2).astype(BF16)

    gate = [pl.BlockSpec((tm, tc), functools.partial(lambda i, j, o: (i, o + j), o=g0 + k * nc)) for k in range(3)]
    blk = pl.BlockSpec((tm, tc), lambda i, j: (i, j))
    out = _sds((t, d), BF16)
    return pl.pallas_call(
        body, name="merge_bwd", grid=(t // tm, nc), in_specs=[blk] + gate + [blk, blk, blk, blk],
        out_specs=[blk] * 7, out_shape=[out] * 7, compiler_params=_cp(("parallel", "parallel")))(
            dmerged, proj, proj, proj, ga, gg, o_conv, o_attn)


def _conv_tile(cfg):
    return _tile(cfg.lp, 384, CONV_HALO)


def _shift_up(x, off, rows):
    if off == 0:
        return x[:rows]
    return pltpu.roll(x, x.shape[0] - off, 0)[:rows]


def conv_fwd(cfg, proj, w, bias, ln_g, ln_b):
    t = proj.shape[0]
    wc = cfg.d_conv
    tt = _conv_tile(cfg)
    nc = cfg.lp // tt
    kw = cfg.conv_width
    first = CONV_HALO - (kw - 1)

    def body(a_ref, g_ref, ap_ref, gp_ref, w_ref, b_ref, lg_ref, lb_ref, cpre_ref, hn_ref):
        c = pl.program_id(1)
        hc = a_ref[...] * _sigmoid(g_ref[...])
        hp = jnp.where(c > 0, ap_ref[...] * _sigmoid(gp_ref[...]), 0.0)
        hcat = jnp.concatenate([hp, hc], axis=0)
        acc = jnp.zeros((tt, wc), F32) + b_ref[...]
        for k in range(kw):
            acc = acc + w_ref[k:k + 1, :] * _shift_up(hcat, first + k, tt)
        cpre_ref[...] = acc
        mu = jnp.mean(acc, axis=-1, keepdims=True)
        xc = acc - mu
        y = xc * lax.rsqrt(jnp.mean(xc * xc, axis=-1, keepdims=True) + LN_EPS) * lg_ref[...] + lb_ref[...]
        hn_ref[...] = (y * _sigmoid(y)).astype(BF16)

    ca, cg = cfg.off_xa // wc, cfg.off_xg // wc
    hb = tt // CONV_HALO
    lph = cfg.lp // CONV_HALO

    def halo(col):
        return pl.BlockSpec((CONV_HALO, wc), lambda b, c: (jnp.maximum(b * lph + c * hb - 1, 0), col))

    vec = pl.BlockSpec((1, wc), lambda b, c: (0, 0))
    blk = pl.BlockSpec((tt, wc), lambda b, c: (b * nc + c, 0))
    return pl.pallas_call(
        body, name="conv_fwd", grid=(cfg.b_loc, nc),
        in_specs=[pl.BlockSpec((tt, wc), lambda b, c: (b * nc + c, ca)),
                  pl.BlockSpec((tt, wc), lambda b, c: (b * nc + c, cg)), halo(ca), halo(cg),
                  pl.BlockSpec((kw, wc), lambda b, c: (0, 0)), vec, vec, vec],
        out_specs=[blk, blk], out_shape=[_sds((t, wc), F32), _sds((t, wc), BF16)],
        compiler_params=_cp(("parallel", "arbitrary")))(
            proj, proj, proj, proj, w, bias.reshape(1, wc), ln_g.reshape(1, wc), ln_b.reshape(1, wc))


def conv_bwd(cfg, dhn, cpre, proj, w, ln_g, ln_b):
    t = proj.shape[0]
    wc = cfg.d_conv
    tt = _conv_tile(cfg)
    nc = cfg.lp // tt
    kw = cfg.conv_width
    first = CONV_HALO - (kw - 1)

    def body(dhn_ref, dhn_nx_ref, cp_ref, cp_nx_ref, a_ref, g_ref, ap_ref, gp_ref, w_ref, lg_ref, lb_ref,
             dxa_ref, dxg_ref, dw_ref, db_ref, dlg_ref, dlb_ref):
        b, c = pl.program_id(0), pl.program_id(1)
        lg, lb = lg_ref[...], lb_ref[...]

        def ln_silu_bwd(dh, cp):
            mu = jnp.mean(cp, axis=-1, keepdims=True)
            xc = cp - mu
            rstd = lax.rsqrt(jnp.mean(xc * xc, axis=-1, keepdims=True) + LN_EPS)
            xhat = xc * rstd
            y = xhat * lg + lb
            sg = _sigmoid(y)
            dy = dh * sg * (1.0 + y * (1.0 - sg))
            dxh = dy * lg
            dc = rstd * (dxh - jnp.mean(dxh, axis=-1, keepdims=True)
                         - xhat * jnp.mean(dxh * xhat, axis=-1, keepdims=True))
            return dc, dy, xhat

        dc, dy, xhat = ln_silu_bwd(dhn_ref[...], cp_ref[...])
        dc_nx, _, _ = ln_silu_bwd(dhn_nx_ref[...], cp_nx_ref[...])
        dc_nx = jnp.where(c < nc - 1, dc_nx, 0.0)
        dcat = jnp.concatenate([dc, dc_nx], axis=0)
        dhc = jnp.zeros((tt, wc), F32)
        for k in range(kw):
            dhc = dhc + w_ref[k:k + 1, :] * _shift_up(dcat, kw - 1 - k, tt)
        av, gv = a_ref[...], g_ref[...]
        sg = _sigmoid(gv)
        valid = (c * tt + lax.broadcasted_iota(jnp.int32, (tt, 1), 0)) >= cfg.pad
        dhc = jnp.where(valid, dhc, 0.0)
        dxa_ref[...] = (dhc * sg).astype(BF16)
        dxg_ref[...] = (dhc * av * sg * (1.0 - sg)).astype(BF16)

        @pl.when((b == 0) & (c == 0))
        def _():
            dw_ref[...] = jnp.zeros_like(dw_ref)
            db_ref[...] = jnp.zeros_like(db_ref)
            dlg_ref[...] = jnp.zeros_like(dlg_ref)
            dlb_ref[...] = jnp.zeros_like(dlb_ref)

        hp = jnp.where(c > 0, ap_ref[...] * _sigmoid(gp_ref[...]), 0.0)
        hcat = jnp.concatenate([hp, av * sg], axis=0)
        for k in range(kw):
            dw_ref[k:k + 1, :] += jnp.sum(dc * _shift_up(hcat, first + k, tt), axis=0, keepdims=True)
        db_ref[...] += jnp.sum(dc, axis=0, keepdims=True)
        dlg_ref[...] += jnp.sum(dy * xhat, axis=0, keepdims=True)
        dlb_ref[...] += jnp.sum(dy, axis=0, keepdims=True)

    ca, cg = cfg.off_xa // wc, cfg.off_xg // wc
    hb = tt // CONV_HALO
    lph = cfg.lp // CONV_HALO
    last = cfg.b_loc * lph - 1

    def prev(col):
        return pl.BlockSpec((CONV_HALO, wc), lambda b, c: (jnp.maximum(b * lph + c * hb - 1, 0), col))

    nxt = pl.BlockSpec((CONV_HALO, wc), lambda b, c: (jnp.minimum(b * lph + (c + 1) * hb, last), 0))
    vec = pl.BlockSpec((1, wc), lambda b, c: (0, 0))
    blk = pl.BlockSpec((tt, wc), lambda b, c: (b * nc + c, 0))
    wspec = pl.BlockSpec((kw, wc), lambda b, c: (0, 0))
    return pl.pallas_call(
        body, name="conv_bwd", grid=(cfg.b_loc, nc),
        in_specs=[blk, nxt, blk, nxt, pl.BlockSpec((tt, wc), lambda b, c: (b * nc + c, ca)),
                  pl.BlockSpec((tt, wc), lambda b, c: (b * nc + c, cg)), prev(ca), prev(cg), wspec, vec, vec],
        out_specs=[blk, blk, wspec, vec, vec, vec],
        out_shape=[_sds((t, wc), BF16), _sds((t, wc), BF16), _sds((kw, wc), F32), _sds((1, wc), F32),
                   _sds((1, wc), F32), _sds((1, wc), F32)],
        compiler_params=_cp(("arbitrary", "arbitrary")))(
            dhn, dhn, cpre, cpre, proj, proj, proj, proj, w, ln_g.reshape(1, wc), ln_b.reshape(1, wc))


def _split_bf16(x):
    hi = x.astype(BF16)
    return hi, (x - hi.astype(F32)).astype(BF16)


K_CHUNK = 2 * Q_BLOCK
NEG_BIG = -1e30


def _sb_logits(cfg, qh, kblk, col0, row_t, masked):
    z = _dot(qh, kblk, 1, 1)
    lk = -(jnp.maximum(z, 0.0) + jnp.log(1.0 + jnp.exp(-jnp.abs(z))))
    if masked:
        col_s = col0 + lax.broadcasted_iota(jnp.int32, (1, K_CHUNK), 1)
        mask = (col_s < row_t) & (col_s >= cfg.pad)
        z, lk = jnp.where(mask, z, NEG_BIG), jnp.where(mask, lk, 0.0)
    return z, lk


def _tri_sum(x, tri):
    hi, lo = _split_bf16(x)
    return _dot(hi, tri, 1, 0) + _dot(lo, tri, 1, 0)


def _grid_ends(dims):
    ids = [pl.program_id(a) for a in range(len(dims))]
    first = functools.reduce(lambda p, q: p & q, [i == 0 for i in ids])
    last = functools.reduce(lambda p, q: p & q, [i == n - 1 for i, n in zip(ids, dims)])
    return first, last


def attn_fwd(cfg, proj, gather_src=None):
    t = proj.shape[0]
    lp = cfg.lp
    nq = lp // Q_BLOCK
    nhp = cfg.d_attn // V7X_LANES
    cq, ck, cv = cfg.off_q // V7X_LANES, cfg.off_k // V7X_LANES, cfg.off_v // V7X_LANES
    scale = 1.0 / math.sqrt(cfg.head_dim)
    grid = (cfg.b_loc, nhp, nq)

    def body(*refs):
        if gather_src is None:
            q_ref, k_ref, v_ref, o_ref, lt_ref, kbf, vbf, tri_from = refs
        else:
            q_ref, k_ref, v_ref, src_ref, o_ref, lt_ref, dst_ref, kbf, vbf, tri_from = refs[:10]
            exchange = (src_ref, dst_ref) + tuple(refs[10:])
            first_step, last_step = _grid_ends(grid)
            pl.when(first_step)(lambda: gather_start(*exchange))
        qb = pl.program_id(2)

        @pl.when(qb == 0)
        def _():
            kbf[pl.ds(0, lp), :] = k_ref[...].astype(BF16)
            vbf[pl.ds(0, lp), :] = v_ref[...].astype(BF16)
            kbf[pl.ds(lp, Q_BLOCK), :] = jnp.zeros((Q_BLOCK, V7X_LANES), BF16)
            vbf[pl.ds(lp, Q_BLOCK), :] = jnp.zeros((Q_BLOCK, V7X_LANES), BF16)
            ri = lax.broadcasted_iota(jnp.int32, (K_CHUNK, K_CHUNK), 0)
            ci = lax.broadcasted_iota(jnp.int32, (K_CHUNK, K_CHUNK), 1)
            tri_from[...] = (ri >= ci).astype(BF16)

        q = q_ref[...] * scale
        lane = lax.broadcasted_iota(jnp.int32, (1, V7X_LANES), 1)
        head0 = lane < cfg.head_dim
        qs = jnp.concatenate([jnp.where(head0, q, 0.0), jnp.where(head0, 0.0, q)], axis=0).astype(BF16)
        rows2 = lax.broadcasted_iota(jnp.int32, (2 * Q_BLOCK, 1), 0)
        row_t = qb * Q_BLOCK + jnp.where(rows2 >= Q_BLOCK, rows2 - Q_BLOCK, rows2)
        last = qb // (K_CHUNK // Q_BLOCK)

        def stage1(k, masked):
            c = jnp.clip(last - k, 0, last)
            off = pl.multiple_of(c * K_CHUNK, K_CHUNK)
            col0 = jnp.where(k <= last, c * K_CHUNK, cfg.lp + K_CHUNK)
            return _sb_logits(cfg, qs, kbf[pl.ds(off, K_CHUNK), :], col0, row_t, masked)

        def stage2(z, lk):
            return z + _tri_sum(lk, tri_from[...]), jnp.sum(lk, axis=1, keepdims=True)

        def stage3(pre, rows, acc, r_carry, k):
            off = pl.multiple_of(jnp.clip(last - k, 0, last) * K_CHUNK, K_CHUNK)
            w = jnp.exp(pre + r_carry)
            return acc + _dot(w.astype(BF16), vbf[pl.ds(off, K_CHUNK), :], 1, 0), r_carry + rows

        def step(masked):
            def body(k, carry):
                z_p, lk_p, pre_pp, rows_pp, acc, r_carry = carry
                acc, r_carry = stage3(pre_pp, rows_pp, acc, r_carry, k - 2)
                pre_p, rows_p = stage2(z_p, lk_p)
                z, lk = stage1(k, masked)
                return z, lk, pre_p, rows_p, acc, r_carry
            return body

        zcol = jnp.where(rows2 < 0, 1.0, 0.0)
        first = stage1(0, True)
        carry = stage1(1, True) + stage2(*first) + (zcol + jnp.where(lane < 0, 1.0, 0.0), zcol)
        carry = lax.fori_loop(2, jnp.maximum(last, 2), step(False), carry)
        carry = lax.fori_loop(jnp.maximum(last, 2), jnp.maximum(last + 1, 2), step(True), carry)
        z_p, lk_p, pre_pp, rows_pp, acc, r_tot = carry
        k_end = jnp.maximum(last + 1, 2)
        acc, r_tot = stage3(pre_pp, rows_pp, acc, r_tot, k_end - 2)
        acc, r_tot = stage3(*stage2(z_p, lk_p), acc, r_tot, k_end - 1)
        o_ref[...] = jnp.where(head0, acc[:Q_BLOCK], acc[Q_BLOCK:]).astype(BF16)
        lt_ref[...] = (jnp.where(lane == 0, r_tot[:Q_BLOCK], 0.0)
                       + jnp.where(lane == cfg.head_dim, r_tot[Q_BLOCK:], 0.0))
        if gather_src is not None:
            pl.when(last_step)(lambda: gather_finish(*exchange))

    oblk = pl.BlockSpec((Q_BLOCK, V7X_LANES), lambda b, hp, qb: (b * nq + qb, hp))
    hbm = pl.BlockSpec(memory_space=pl.ANY)
    comm = gather_src is not None
    return pl.pallas_call(
        body, name="attn_fwd_gather" if comm else "attn_fwd", grid=grid,
        in_specs=[pl.BlockSpec((Q_BLOCK, V7X_LANES), lambda b, hp, qb: (b * nq + qb, cq + hp)),
                  pl.BlockSpec((lp, V7X_LANES), lambda b, hp, qb: (b, ck + hp)),
                  pl.BlockSpec((lp, V7X_LANES), lambda b, hp, qb: (b, cv + hp))] + ([hbm] if comm else []),
        out_specs=[oblk, oblk] + ([hbm] if comm else []),
        out_shape=[_sds((t, cfg.d_attn), BF16), _sds((t, cfg.d_attn), F32)]
        + ([_sds((8,) + gather_src.shape, gather_src.dtype)] if comm else []),
        scratch_shapes=[pltpu.VMEM((lp + Q_BLOCK, V7X_LANES), BF16), pltpu.VMEM((lp + Q_BLOCK, V7X_LANES), BF16),
                        pltpu.VMEM((K_CHUNK, K_CHUNK), BF16)] + (exchange_semaphores() if comm else []),
        compiler_params=_cp(("arbitrary", "arbitrary", "arbitrary")))(*((proj, proj, proj) + ((gather_src,) if comm else ())))


def attn_bwd(cfg, proj, lt, do, send=None):
    t = proj.shape[0]
    lp = cfg.lp
    nq = lp // Q_BLOCK
    nhp = cfg.d_attn // V7X_LANES
    cq, ck, cv = cfg.off_q // V7X_LANES, cfg.off_k // V7X_LANES, cfg.off_v // V7X_LANES
    scale = 1.0 / math.sqrt(cfg.head_dim)
    grid = (cfg.b_loc, nhp, nq)

    def body(*refs):
        if send is None:
            (q_ref, k_ref, v_ref, lt_ref, do_ref, dq_ref, dk_ref, dv_ref, kbf, vbf, dk_acc, dv_acc,
             tri_before, tri_upto) = refs
        else:
            (q_ref, k_ref, v_ref, lt_ref, do_ref, src_ref, dq_ref, dk_ref, dv_ref, dst_ref,
             kbf, vbf, dk_acc, dv_acc, tri_before, tri_upto) = refs[:16]
            exchange = (src_ref, dst_ref) + tuple(refs[16:])
            first_step, last_step = _grid_ends(grid)
            pl.when(first_step)(lambda: all_to_all_start(*exchange))
        qb = pl.program_id(2)

        @pl.when(qb == 0)
        def _():
            kbf[pl.ds(0, lp), :] = k_ref[...].astype(BF16)
            vbf[pl.ds(0, lp), :] = v_ref[...].astype(BF16)
            kbf[pl.ds(lp, Q_BLOCK), :] = jnp.zeros((Q_BLOCK, V7X_LANES), BF16)
            vbf[pl.ds(lp, Q_BLOCK), :] = jnp.zeros((Q_BLOCK, V7X_LANES), BF16)
            dk_acc[...] = jnp.zeros_like(dk_acc)
            dv_acc[...] = jnp.zeros_like(dv_acc)
            ri = lax.broadcasted_iota(jnp.int32, (K_CHUNK, K_CHUNK), 0)
            ci = lax.broadcasted_iota(jnp.int32, (K_CHUNK, K_CHUNK), 1)
            tri_before[...] = (ri < ci).astype(BF16)
            tri_upto[...] = (ri <= ci).astype(BF16)

        q = q_ref[...] * scale
        ltv = lt_ref[...]
        dov = do_ref[...]
        lane = lax.broadcasted_iota(jnp.int32, (1, V7X_LANES), 1)
        head0 = lane < cfg.head_dim
        qs = jnp.concatenate([jnp.where(head0, q, 0.0), jnp.where(head0, 0.0, q)], axis=0).astype(BF16)
        dos = jnp.concatenate([jnp.where(head0, dov, 0.0), jnp.where(head0, 0.0, dov)], axis=0).astype(BF16)
        lk_tot = jnp.concatenate(
            [jnp.sum(jnp.where(lane == 0, ltv, 0.0), axis=1, keepdims=True),
             jnp.sum(jnp.where(lane == cfg.head_dim, ltv, 0.0), axis=1, keepdims=True)], axis=0)
        rows2 = lax.broadcasted_iota(jnp.int32, (2 * Q_BLOCK, 1), 0)
        row_t = qb * Q_BLOCK + jnp.where(rows2 >= Q_BLOCK, rows2 - Q_BLOCK, rows2)
        n_chunks = qb // (K_CHUNK // Q_BLOCK) + 1

        def stage1(k, masked, lk_before):
            off = pl.multiple_of(k * K_CHUNK, K_CHUNK)
            z, lk = _sb_logits(cfg, qs, kbf[pl.ds(off, K_CHUNK), :], k * K_CHUNK, row_t, masked)
            dw = _dot(dos, vbf[pl.ds(off, K_CHUNK), :], 1, 1)
            w = jnp.exp(z + (lk_tot - lk_before - _tri_sum(lk, tri_before[...])))
            sg = 1.0 - jnp.exp(lk)
            return w * dw, w.astype(BF16), sg, lk_before + jnp.sum(lk, axis=1, keepdims=True)

        def stage2(e, w16, sg, k, dq_acc, e_before):
            off = pl.multiple_of(k * K_CHUNK, K_CHUNK)
            dz = (e - sg * (e_before + _tri_sum(e, tri_upto[...]))).astype(BF16)
            dk_acc[pl.ds(off, K_CHUNK), :] += _dot(dz, qs, 0, 0)
            dv_acc[pl.ds(off, K_CHUNK), :] += _dot(w16, dos, 0, 0)
            return dq_acc + _dot(dz, kbf[pl.ds(off, K_CHUNK), :], 1, 0), e_before + jnp.sum(e, axis=1, keepdims=True)

        def step(masked):
            def body(k, carry):
                e_p, w_p, sg_p, dq_acc, lk_before, e_before = carry
                dq_acc, e_before = stage2(e_p, w_p, sg_p, k - 1, dq_acc, e_before)
                e, w16, sg, lk_before = stage1(k, masked, lk_before)
                return e, w16, sg, dq_acc, lk_before, e_before
            return body

        zcol = jnp.where(rows2 < 0, 1.0, 0.0)
        e, w16, sg, lk_before = stage1(0, True, zcol)
        carry = (e, w16, sg, zcol + jnp.where(lane < 0, 1.0, 0.0), lk_before, zcol)
        carry = lax.fori_loop(1, jnp.maximum(n_chunks - 1, 1), step(False), carry)
        carry = lax.fori_loop(jnp.maximum(n_chunks - 1, 1), n_chunks, step(True), carry)
        dq, _ = stage2(carry[0], carry[1], carry[2], n_chunks - 1, carry[3], carry[5])
        dq_ref[...] = (jnp.where(head0, dq[:Q_BLOCK], dq[Q_BLOCK:]) * scale).astype(BF16)

        @pl.when(qb == nq - 1)
        def _():
            dk_ref[...] = dk_acc[pl.ds(0, lp), :].astype(BF16)
            dv_ref[...] = dv_acc[pl.ds(0, lp), :].astype(BF16)

        if send is not None:
            pl.when(last_step)(lambda: all_to_all_finish(*exchange))

    qblk = pl.BlockSpec((Q_BLOCK, V7X_LANES), lambda b, hp, qb: (b * nq + qb, hp))
    seq = pl.BlockSpec((lp, V7X_LANES), lambda b, hp, qb: (b, hp))
    hbm = pl.BlockSpec(memory_space=pl.ANY)
    out = _sds((t, cfg.d_attn), BF16)
    comm = send is not None
    return pl.pallas_call(
        body, name="attn_bwd_exchange" if comm else "attn_bwd", grid=grid,
        in_specs=[pl.BlockSpec((Q_BLOCK, V7X_LANES), lambda b, hp, qb: (b * nq + qb, cq + hp)),
                  pl.BlockSpec((lp, V7X_LANES), lambda b, hp, qb: (b, ck + hp)),
                  pl.BlockSpec((lp, V7X_LANES), lambda b, hp, qb: (b, cv + hp)), qblk, qblk] + ([hbm] if comm else []),
        out_specs=[qblk, seq, seq] + ([hbm] if comm else []),
        out_shape=[out, out, out] + ([_sds(send.shape, send.dtype)] if comm else []),
        scratch_shapes=[pltpu.VMEM((lp + Q_BLOCK, V7X_LANES), BF16), pltpu.VMEM((lp + Q_BLOCK, V7X_LANES), BF16),
                        pltpu.VMEM((lp + Q_BLOCK, V7X_LANES), F32), pltpu.VMEM((lp + Q_BLOCK, V7X_LANES), F32),
                        pltpu.VMEM((K_CHUNK, K_CHUNK), BF16), pltpu.VMEM((K_CHUNK, K_CHUNK), BF16)]
        + (exchange_semaphores() if comm else []),
        compiler_params=_cp(("arbitrary", "arbitrary", "arbitrary")))(
            *((proj, proj, proj, lt, do) + ((send,) if comm else ())))


N_POW = 8


def _discretize(lr, li, logdt, br_t, bi_t):
    dt = jnp.exp(logdt)
    mag = jnp.exp(lr * dt)
    ab_re = mag * jnp.cos(li * dt)
    ab_im = mag * jnp.sin(li * dt)
    den = lr * lr + li * li
    nr = ab_re - 1.0
    ni = ab_im
    coef_re = (nr * lr + ni * li) / den
    coef_im = (ni * lr - nr * li) / den
    bb_re = coef_re[None] * br_t - coef_im[None] * bi_t
    bb_im = coef_re[None] * bi_t + coef_im[None] * br_t
    return ab_re, ab_im, bb_re, bb_im


def ssm_param_fwd(cfg, lr, li, logdt, br_t, bi_t):
    g, n, c = cfg.groups, cfg.ssm_state, cfg.ssm_group

    def body(lr_ref, li_ref, dt_ref, br_ref, bi_ref, ar_ref, ai_ref, bbr_ref, bbi_ref, pr_ref, pi_ref,
             tr_ref, ti_ref):
        ab_re, ab_im, bb_re, bb_im = _discretize(lr_ref[...], li_ref[...], dt_ref[...], br_ref[...], bi_ref[...])
        ar_ref[...] = ab_re
        ai_ref[...] = ab_im
        bbr_ref[...] = bb_re
        bbi_ref[...] = bb_im
        pr, pi = ab_re, ab_im
        for k in range(N_POW):
            pr_ref[k] = pr
            pi_ref[k] = pi
            pr, pi = pr * pr - pi * pi, 2.0 * pr * pi
        cr, ci = ab_re, ab_im
        for j in range(SSM_CHUNK):
            tr_ref[j] = cr
            ti_ref[j] = ci
            cr, ci = cr * ab_re - ci * ab_im, cr * ab_im + ci * ab_re

    gn, cgn = _sds((g, n), F32), _sds((c, g, n), F32)
    return pl.pallas_call(
        body, name="ssm_param_fwd",
        out_shape=[gn, gn, cgn, cgn, _sds((N_POW, g, n), F32), _sds((N_POW, g, n), F32),
                   _sds((SSM_CHUNK, g, n), F32), _sds((SSM_CHUNK, g, n), F32)])(lr, li, logdt, br_t, bi_t)


def ssm_param_bwd(cfg, lr, li, logdt, br_t, bi_t, dar, dai, dbbr, dbbi):
    g, n, c = cfg.groups, cfg.ssm_state, cfg.ssm_group

    def body(lr_ref, li_ref, dt_ref, br_ref, bi_ref, dar_ref, dai_ref, dbr_ref, dbi_ref,
             olr_ref, oli_ref, odt_ref, obr_ref, obi_ref):
        _, vjp = jax.vjp(_discretize, lr_ref[...], li_ref[...], dt_ref[...], br_ref[...], bi_ref[...])
        dlr, dli, ddt, dbr, dbi = vjp((dar_ref[...], dai_ref[...], dbr_ref[...], dbi_ref[...]))
        olr_ref[...] = dlr
        oli_ref[...] = dli
        odt_ref[...] = ddt
        obr_ref[...] = dbr
        obi_ref[...] = dbi

    gn, cgn = _sds((g, n), F32), _sds((c, g, n), F32)
    return pl.pallas_call(body, name="ssm_param_bwd", out_shape=[gn, gn, _sds((g, 1), F32), cgn, cgn])(
        lr, li, logdt, br_t, bi_t, dar, dai, dbbr, dbbi)


def _scan(xr, xi, pr_ref, pi_ref, reverse):
    ct = xr.shape[0]
    rows = lax.broadcasted_iota(jnp.int32, (ct, 1), 0)
    d, k = 1, 0
    while d < ct:
        if reverse:
            sr, si = pltpu.roll(xr, ct - d, 0), pltpu.roll(xi, ct - d, 0)
            keep = rows < ct - d
        else:
            sr, si = pltpu.roll(xr, d, 0), pltpu.roll(xi, d, 0)
            keep = rows >= d
        sr = jnp.where(keep, sr, 0.0)
        si = jnp.where(keep, si, 0.0)
        ar = pr_ref[0, k:k + 1, :]
        ai = -pi_ref[0, k:k + 1, :] if reverse else pi_ref[0, k:k + 1, :]
        xr, xi = xr + ar * sr - ai * si, xi + ar * si + ai * sr
        d *= 2
        k += 1
    return xr, xi


def _gelu(x):
    k = math.sqrt(2.0 / math.pi)
    return 0.5 * x * (1.0 + jnp.tanh(k * (x + 0.044715 * x * x * x)))


def _gelu_grad(x):
    k = math.sqrt(2.0 / math.pi)
    th = jnp.tanh(k * (x + 0.044715 * x * x * x))
    return 0.5 * (1.0 + th) + 0.5 * x * (1.0 - th * th) * k * (1.0 + 3.0 * 0.044715 * x * x)


def ssm_fwd(cfg, proj, bb_re, bb_im, ct_re, ct_im, pw_re, pw_im, tab_re, tab_im, dskip):
    t = proj.shape[0]
    nsb = cfg.d_ssm // SSM_SB
    ns = SSM_SB // cfg.ssm_group * cfg.ssm_state
    ct = SSM_CHUNK
    nc = cfg.lp // ct

    def body(u_ref, bbr_ref, bbi_ref, ctr_ref, cti_ref, pr_ref, pi_ref, tr_ref, ti_ref, d_ref,
             sr_ref, si_ref, yt_ref, y2_ref, cr_scr, ci_scr):
        c = pl.program_id(2)

        @pl.when(c == 0)
        def _():
            cr_scr[...] = jnp.zeros_like(cr_scr)
            ci_scr[...] = jnp.zeros_like(ci_scr)

        u = u_ref[...]
        ub = u.astype(BF16)
        xr, xi = _scan(_dot(ub, bbr_ref[0], 1, 0), _dot(ub, bbi_ref[0], 1, 0), pr_ref, pi_ref, False)
        cr, ci = cr_scr[0:1, :], ci_scr[0:1, :]
        tr, ti = tr_ref[0], ti_ref[0]
        sr = xr + tr * cr - ti * ci
        si = xi + tr * ci + ti * cr
        sr_ref[...] = sr
        si_ref[...] = si
        cr_scr[0:1, :] = sr_ref[ct - 1:ct, :]
        ci_scr[0:1, :] = si_ref[ct - 1:ct, :]
        y = _dot(sr.astype(BF16), ctr_ref[0], 1, 0) - _dot(si.astype(BF16), cti_ref[0], 1, 0) + d_ref[...] * u
        yt_ref[...] = y
        y2_ref[...] = _gelu(y).astype(BF16)

    def mat(r, c_):
        return pl.BlockSpec((1, r, c_), lambda sb, b, c: (sb, 0, 0))

    ublk = pl.BlockSpec((ct, SSM_SB), lambda sb, b, c: (b * nc + c, sb))
    sblk = pl.BlockSpec((ct, ns), lambda sb, b, c: (b * nc + c, sb))
    return pl.pallas_call(
        body, name="ssm_fwd", grid=(nsb, cfg.b_loc, nc),
        in_specs=[ublk, mat(SSM_SB, ns), mat(SSM_SB, ns), mat(ns, SSM_SB), mat(ns, SSM_SB), mat(N_POW, ns),
                  mat(N_POW, ns), mat(ct, ns), mat(ct, ns), pl.BlockSpec((1, SSM_SB), lambda sb, b, c: (0, sb))],
        out_specs=[sblk, sblk, ublk, ublk],
        out_shape=[_sds((t, cfg.n_state), F32), _sds((t, cfg.n_state), F32), _sds((t, cfg.d_ssm), F32),
                   _sds((t, cfg.d_ssm), BF16)],
        scratch_shapes=[pltpu.VMEM((8, ns), F32), pltpu.VMEM((8, ns), F32)],
        compiler_params=_cp(("parallel", "arbitrary", "arbitrary")))(
            proj, bb_re, bb_im, ct_re, ct_im, pw_re, pw_im, tab_re, tab_im, dskip.reshape(1, cfg.d_ssm))


def ssm_bwd(cfg, dy2, ytot, proj, s_re, s_im, cf_re, cf_im, bbt_re, bbt_im, pw_re, pw_im, tabr_re, tabr_im, dskip):
    t = proj.shape[0]
    nsb = cfg.d_ssm // SSM_SB
    ns = SSM_SB // cfg.ssm_group * cfg.ssm_state
    ct = SSM_CHUNK
    nc = cfg.lp // ct

    def body(dy_ref, yt_ref, u_ref, sr_ref, si_ref, spr_ref, spi_ref, cfr_ref, cfi_ref, btr_ref, bti_ref,
             pr_ref, pi_ref, tr_ref, ti_ref, d_ref,
             du_ref, dctr_ref, dcti_ref, dbbr_ref, dbbi_ref, dar_ref, dai_ref, dd_ref,
             cr_scr, ci_scr, ar_scr, ai_scr):
        b, c = pl.program_id(1), pl.program_id(2)
        chunk = nc - 1 - c

        @pl.when((b == 0) & (c == 0))
        def _():
            for r in (dctr_ref, dcti_ref, dbbr_ref, dbbi_ref, dar_ref, dai_ref, dd_ref):
                r[...] = jnp.zeros_like(r)

        @pl.when(c == 0)
        def _():
            cr_scr[...] = jnp.zeros_like(cr_scr)
            ci_scr[...] = jnp.zeros_like(ci_scr)

        u = u_ref[...]
        dyt = dy_ref[...] * _gelu_grad(yt_ref[...])
        dytb = dyt.astype(BF16)
        xr, xi = _scan(_dot(dytb, cfr_ref[0], 1, 0), -_dot(dytb, cfi_ref[0], 1, 0), pr_ref, pi_ref, True)
        cr, ci = cr_scr[0:1, :], ci_scr[0:1, :]
        tr, ti = tr_ref[0], -ti_ref[0]
        a_re = xr + tr * cr - ti * ci
        a_im = xi + tr * ci + ti * cr
        ar_scr[...] = a_re
        ai_scr[...] = a_im
        cr_scr[0:1, :] = ar_scr[0:1, :]
        ci_scr[0:1, :] = ai_scr[0:1, :]

        sr, si = sr_ref[...], si_ref[...]
        rows = lax.broadcasted_iota(jnp.int32, (ct, 1), 0)
        keep_prev = chunk > 0
        pr_last = jnp.where(keep_prev, spr_ref[7:8, :], 0.0)
        pi_last = jnp.where(keep_prev, spi_ref[7:8, :], 0.0)
        sp_re = jnp.where(rows == 0, pr_last, pltpu.roll(sr, 1, 0))
        sp_im = jnp.where(rows == 0, pi_last, pltpu.roll(si, 1, 0))
        dar_ref[0] += jnp.sum(a_re * sp_re + a_im * sp_im, axis=0, keepdims=True)
        dai_ref[0] += jnp.sum(a_im * sp_re - a_re * sp_im, axis=0, keepdims=True)
        dctr_ref[0] += _dot(sr.astype(BF16), dytb, 0, 0)
        dcti_ref[0] += -_dot(si.astype(BF16), dytb, 0, 0)
        ub = u.astype(BF16)
        arb, aib = a_re.astype(BF16), a_im.astype(BF16)
        dbbr_ref[0] += _dot(ub, arb, 0, 0)
        dbbi_ref[0] += _dot(ub, aib, 0, 0)
        du = dyt * d_ref[...] + _dot(arb, btr_ref[0], 1, 0) + _dot(aib, bti_ref[0], 1, 0)
        valid = (chunk * ct + rows) >= cfg.pad
        du_ref[...] = jnp.where(valid, du, 0.0).astype(BF16)
        dd_ref[...] += jnp.sum(dyt * u, axis=0, keepdims=True)

    def mat(r, c_):
        return pl.BlockSpec((1, r, c_), lambda sb, b, c: (sb, 0, 0))

    lp8 = cfg.lp // 8
    ublk = pl.BlockSpec((ct, SSM_SB), lambda sb, b, c: (b * nc + nc - 1 - c, sb))
    sblk = pl.BlockSpec((ct, ns), lambda sb, b, c: (b * nc + nc - 1 - c, sb))
    sprev = pl.BlockSpec((8, ns), lambda sb, b, c: (jnp.maximum(b * lp8 + (nc - 1 - c) * (ct // 8) - 1, 0), sb))
    dvec = pl.BlockSpec((1, SSM_SB), lambda sb, b, c: (0, sb))
    return pl.pallas_call(
        body, name="ssm_bwd", grid=(nsb, cfg.b_loc, nc),
        in_specs=[ublk, ublk, ublk, sblk, sblk, sprev, sprev, mat(SSM_SB, ns), mat(SSM_SB, ns), mat(ns, SSM_SB),
                  mat(ns, SSM_SB), mat(N_POW, ns), mat(N_POW, ns), mat(ct, ns), mat(ct, ns), dvec],
        out_specs=[ublk, mat(ns, SSM_SB), mat(ns, SSM_SB), mat(SSM_SB, ns), mat(SSM_SB, ns), mat(1, ns), mat(1, ns),
                   dvec],
        out_shape=[_sds((t, cfg.d_ssm), BF16), _sds((nsb, ns, SSM_SB), F32), _sds((nsb, ns, SSM_SB), F32),
                   _sds((nsb, SSM_SB, ns), F32), _sds((nsb, SSM_SB, ns), F32), _sds((nsb, 1, ns), F32),
                   _sds((nsb, 1, ns), F32), _sds((1, cfg.d_ssm), F32)],
        scratch_shapes=[pltpu.VMEM((8, ns), F32), pltpu.VMEM((8, ns), F32), pltpu.VMEM((ct, ns), F32),
                        pltpu.VMEM((ct, ns), F32)],
        compiler_params=_cp(("arbitrary", "arbitrary", "arbitrary")))(
            dy2, ytot, proj, s_re, s_im, s_re, s_im, cf_re, cf_im, bbt_re, bbt_im, pw_re, pw_im, tabr_re, tabr_im,
            dskip.reshape(1, cfg.d_ssm))


def _ssm_tables(cfg, lam_re, lam_im, log_dt, b_re, b_im, c_re, c_im):
    g, n, c = cfg.groups, cfg.ssm_state, cfg.ssm_group
    gsb = SSM_SB // c
    nsb = cfg.d_ssm // SSM_SB
    ns = gsb * n
    br_t, bi_t = jnp.transpose(b_re, (2, 0, 1)), jnp.transpose(b_im, (2, 0, 1))
    prm = (lam_re, lam_im, log_dt.reshape(g, 1), br_t, bi_t)
    _, _, bbr, bbi, pr, pi, tr, ti = ssm_param_fwd(cfg, *prm)
    eye = jnp.eye(gsb, dtype=F32)

    def bdiag_b(x):
        return jnp.einsum('csgn,gh->sgchn', x.reshape(c, nsb, gsb, n), eye).reshape(nsb, SSM_SB, ns)

    def bdiag_c(x):
        return jnp.einsum('sgcn,gh->sgchn', x.reshape(nsb, gsb, c, n), eye).reshape(nsb, SSM_SB, ns)

    def lanes(x):
        return jnp.transpose(x.reshape(x.shape[0], nsb, ns), (1, 0, 2))

    bb_re, bb_im = bdiag_b(bbr), bdiag_b(bbi)
    cf_re, cf_im = bdiag_c(c_re), bdiag_c(c_im)
    sw = lambda x: jnp.swapaxes(x, 1, 2)
    return dict(prm=prm, bb_re=bb_re.astype(BF16), bb_im=bb_im.astype(BF16), bbt_re=sw(bb_re).astype(BF16),
                bbt_im=sw(bb_im).astype(BF16), cf_re=cf_re.astype(BF16), cf_im=cf_im.astype(BF16),
                ct_re=sw(cf_re).astype(BF16), ct_im=sw(cf_im).astype(BF16), pw_re=lanes(pr), pw_im=lanes(pi),
                tab_re=lanes(tr), tab_im=lanes(ti), tabr_re=lanes(tr[::-1]), tabr_im=lanes(ti[::-1]))


def _ssm_param_grads(cfg, tabs, dct_re, dct_im, dbb_re, dbb_im, dab_re, dab_im):
    g, n, c = cfg.groups, cfg.ssm_state, cfg.ssm_group
    gsb = SSM_SB // c
    nsb = cfg.d_ssm // SSM_SB

    def diag_b(x):
        d = jnp.einsum('sgcgn->sgcn', x.reshape(nsb, gsb, c, gsb, n))
        return jnp.transpose(d.reshape(g, c, n), (1, 0, 2))

    def diag_c(x):
        d = jnp.einsum('sgngc->sgcn', x.reshape(nsb, gsb, n, gsb, c))
        return d.reshape(g, c, n)

    dlr, dli, ddt, dbr_t, dbi_t = ssm_param_bwd(cfg, *tabs['prm'], dab_re.reshape(g, n), dab_im.reshape(g, n),
                                                diag_b(dbb_re), diag_b(dbb_im))
    return dict(ssm_lam_re=dlr, ssm_lam_im=dli, ssm_log_dt=ddt.reshape(g),
                ssm_b_re=jnp.transpose(dbr_t, (1, 2, 0)), ssm_b_im=jnp.transpose(dbi_t, (1, 2, 0)),
                ssm_c_re=diag_c(dct_re), ssm_c_im=diag_c(dct_im))


def _mm_tiles(t, n, k):
    tm = _tile(t, 1056 if k <= 3072 else 528, 16)
    tn = _tile(n, 512, 128)
    return tm, tn


def _nt(a, wt, name, out_dtype=F32):
    tm, tn = _mm_tiles(a.shape[0], wt.shape[0], a.shape[1])
    return mm([(a, wt, 0)], nt=True, n=wt.shape[0], tm=tm, tn=tn, out_dtype=out_dtype, name=name)


def _nn(pairs, n, name, res=None, res_scale=1.0, out_dtype=F32):
    kmax = max(p[0].shape[1] for p in pairs) * len(pairs)
    tm, tn = _mm_tiles(pairs[0][0].shape[0], n, kmax)
    return mm(pairs, nt=False, n=n, tm=tm, tn=tn, out_dtype=out_dtype, name=name, res=res, res_scale=res_scale)


def local_step(cfg, x, target, rep, ex):
    d, lp, bsz, t = cfg.d_model, cfg.lp, cfg.b_loc, cfg.tokens
    meta = jnp.broadcast_to(ex.meta_tokens[None], (bsz, cfg.n_meta, d))
    h = jnp.concatenate([jnp.zeros((bsz, cfg.pad, d), F32), meta, x], axis=1).reshape(t, d)

    saved = []
    for l in range(cfg.depth):
        w = ex.weights(l)
        r = {k: v[l] for k, v in rep.items() if k != 'final_norm'}
        s = {'h0': h}
        s['xn1'] = rms_fwd(h, r['ffn1_norm'], "rms_fwd")
        s['a1'], s['b1'], s['act1'] = dual_mm_act(s['xn1'], w['w13t_1'], kind='swiglu', name="ffn_up")
        h = _nn([(s['act1'], w['w2_1'], 0)], d, "ffn_down", res=h, res_scale=0.5)
        s['h1'] = h
        s['xnm'] = rms_fwd(h, r['mix_norm'], "rms_fwd")
        proj = _nt(s['xnm'], w['w_int'], "proj_in")
        s['proj'] = proj
        tabs = _ssm_tables(cfg, r['ssm_lam_re'], r['ssm_lam_im'], r['ssm_log_dt'], r['ssm_b_re'], r['ssm_b_im'],
                           r['ssm_c_re'], r['ssm_c_im'])
        s['tabs'] = tabs
        s['s_re'], s['s_im'], s['ytot'], s['y2'] = ssm_fwd(
            cfg, proj, tabs['bb_re'], tabs['bb_im'], tabs['ct_re'], tabs['ct_im'], tabs['pw_re'], tabs['pw_im'],
            tabs['tab_re'], tabs['tab_im'], r['ssm_d'])
        s['ga'], s['gg'], o_ssm = dual_mm_act(s['y2'], w['w_glut'], kind='glu', name="ssm_glu")
        s['cpre'], s['hn'] = conv_fwd(cfg, proj, w['conv_w'], r['conv_b'], r['conv_ln_g'], r['conv_ln_b'])
        s['o_conv'] = _nt(s['hn'], w['conv_w_outt'], "branch_out")
        src = ex.gather_src(l)
        s['o'], s['lt'], *arrived = attn_fwd(cfg, proj, src)
        if src is not None:
            ex.gathered(l, arrived[0])
        s['o_attn'] = _nt(s['o'], w['attn_w_ot'], "branch_out")
        s['merged'] = merge_fwd(cfg, proj, o_ssm, s['o_conv'], s['o_attn'])
        h = _nn([(s['merged'], w['w_out'], 0)], d, "mix_out", res=h, res_scale=1.0)
        s['h2'] = h
        s['xn2'] = rms_fwd(h, r['ffn2_norm'], "rms_fwd")
        s['a2'], s['b2'], s['act2'] = dual_mm_act(s['xn2'], w['w13t_2'], kind='swiglu', name="ffn_up")
        h = _nn([(s['act2'], w['w2_2'], 0)], d, "ffn_down", res=h, res_scale=0.5)
        saved.append((w, r, s))

    dh3, dh3_16, dgf, loss_sq = final_fwd_bwd(cfg, h.reshape(bsz, lp, d), rep['final_norm'], target)
    dh, dh16 = dh3.reshape(t, d), dh3_16.reshape(t, d)
    loss = 0.5 * loss_sq[0, 0] / d

    gl = []
    for l in reversed(range(cfg.depth)):
        w, r, s = saved[l]
        g = {}

        def ffn_bwd(dh, dh16, tag, a, b, act, xn, hin, norm):
            da, db = ffn_down_bwd(dh16, w['w2_' + tag], a, b, "ffn_down_bwd")
            g['w2_' + tag] = mm_tn(act, dh16, name="ffn_dw2", scale=0.5)
            dxn = _nn([(da, w['w13t_' + tag], 0), (db, w['w13t_' + tag], 1)], d, "ffn_dxn")
            g['w13t_' + tag] = jnp.concatenate([mm_tn(da, xn, name="ffn_dw13"), mm_tn(db, xn, name="ffn_dw13")], 0)
            return rms_bwd(dxn, hin, norm, dh, "rms_bwd")

        dh, dh16, dg2 = ffn_bwd(dh, dh16, '2', s['a2'], s['b2'], s['act2'], s['xn2'], s['h2'], r['ffn2_norm'])
        g['ffn2_norm'] = dg2.reshape(d)

        proj, tabs = s['proj'], s['tabs']
        dmerged = _nt(dh16, w['w_out'], "mix_dmerged")
        g['w_out'] = mm_tn(s['merged'], dh16, name="mix_dwout")
        dg0, dg1, dg2_, dga, dgg, do_conv, do_attn = merge_bwd(cfg, dmerged, proj, s['ga'], s['gg'], s['o_conv'],
                                                               s['o_attn'])
        dy2 = _nn([(dga, w['w_glut'], 0), (dgg, w['w_glut'], 1)], cfg.d_ssm, "ssm_dy2")
        g['w_glut'] = jnp.concatenate([mm_tn(dga, s['y2'], name="ssm_dwglu"), mm_tn(dgg, s['y2'], name="ssm_dwglu")], 0)
        du, dct_re, dct_im, dbb_re, dbb_im, dab_re, dab_im, dd = ssm_bwd(
            cfg, dy2, s['ytot'], proj, s['s_re'], s['s_im'], tabs['cf_re'], tabs['cf_im'], tabs['bbt_re'],
            tabs['bbt_im'], tabs['pw_re'], tabs['pw_im'], tabs['tabr_re'], tabs['tabr_im'], r['ssm_d'])
        g.update(_ssm_param_grads(cfg, tabs, dct_re, dct_im, dbb_re, dbb_im, dab_re, dab_im))
        g['ssm_d'] = dd.reshape(cfg.d_ssm)

        dhn = _nn([(do_conv, w['conv_w_outt'], 0)], cfg.d_conv, "branch_din")
        g['conv_w_outt'] = mm_tn(do_conv, s['hn'], name="branch_dw")
        dxa, dxg, dcw, dcb, dlg, dlb = conv_bwd(cfg, dhn, s['cpre'], proj, w['conv_w'], r['conv_ln_g'],
                                                r['conv_ln_b'])
        g['conv_w'], g['conv_b'] = dcw, dcb.reshape(cfg.d_conv)
        g['conv_ln_g'], g['conv_ln_b'] = dlg.reshape(cfg.d_conv), dlb.reshape(cfg.d_conv)

        do = _nn([(do_attn, w['attn_w_ot'], 0)], cfg.d_attn, "branch_din")
        g['attn_w_ot'] = mm_tn(do_attn, s['o'], name="branch_dw")
        send = ex.send_src(l)
        dq, dk, dv, *arrived = attn_bwd(cfg, proj, s['lt'], do, send)
        if send is not None:
            ex.received(l, arrived[0])

        dproj = jnp.concatenate([du, dxa, dxg, dq, dk, dv, dg0, dg1, dg2_],
                                axis=1)
        dxn = _nn([(dproj, w['w_int'], 0)], d, "proj_dxn")
        g['w_int'] = mm_tn(dproj, s['xnm'], name="proj_dw")
        dh, dh16, dgm = rms_bwd(dxn, s['h1'], r['mix_norm'], dh, "rms_bwd")
        g['mix_norm'] = dgm.reshape(d)

        dh, dh16, dg1n = ffn_bwd(dh, dh16, '1', s['a1'], s['b1'], s['act1'], s['xn1'], s['h0'], r['ffn1_norm'])
        g['ffn1_norm'] = dg1n.reshape(d)
        ex.matrix_grads(l, {name: g.pop(name) for name, _, _ in PACKED})
        gl.append(g)

    gl = gl[::-1]
    dh0 = dh.reshape(bsz, lp, d)
    grad_x = dh0[:, Q_BLOCK:]
    grads = {k: jnp.stack([g[k] for g in gl]) for k in gl[0]}
    grads['meta_tokens'] = jnp.sum(dh0[:, cfg.pad:Q_BLOCK], axis=0)
    grads['final_norm'] = dgf.reshape(d)
    return loss, grad_x, grads


def _me():
    return lax.axis_index("x"), lax.axis_index("y"), lax.axis_index("c")


def _flat(px, py, pc):
    return 4 * px + 2 * py + pc


def all_gather_blocks(block, name):
    r, c_ = block.shape

    def body(x_ref, out_ref, send_sems, recv_sems, local_sem):
        gather_start(x_ref, out_ref, send_sems, recv_sems, local_sem)
        gather_finish(x_ref, out_ref, send_sems, recv_sems, local_sem)

    return pl.pallas_call(
        body, name=name, out_shape=_sds((8, r, c_), block.dtype),
        in_specs=[pl.BlockSpec(memory_space=pl.ANY)], out_specs=pl.BlockSpec(memory_space=pl.ANY),
        scratch_shapes=exchange_semaphores())(block)


def exchange_semaphores():
    return [pltpu.SemaphoreType.DMA((7,)), pltpu.SemaphoreType.DMA((7,)), pltpu.SemaphoreType.DMA(())]


def _gather_copies(x_ref, out_ref, send_sems, recv_sems, local_sem):
    x, y, c = _me()
    me, sibling = (x, y, c), (x, y, 1 - c)
    chips = [(1 - x, y), (x, 1 - y), (1 - x, 1 - y)]

    def slot(px, py, pc):
        return out_ref.at[_flat(px, py, pc)]

    def copy(k, blk, to, src=None):
        return pltpu.make_async_remote_copy(
            src_ref=slot(*blk) if src is None else src, dst_ref=slot(*blk), send_sem=send_sems.at[k],
            recv_sem=recv_sems.at[k], device_id=to, device_id_type=pl.DeviceIdType.MESH)

    mine = pltpu.make_async_copy(x_ref, slot(*me), local_sem)
    first = [copy(0, me, sibling, src=x_ref)] + [copy(1 + j, me, (*chip, c), src=x_ref) for j, chip in enumerate(chips)]
    passed = [copy(4 + j, (*chip, c), sibling) for j, chip in enumerate(chips)]
    over_ici = [copy(1 + j, (*chip, c), me) for j, chip in enumerate(chips)]
    from_sibling = [copy(0, sibling, me)] + [copy(4 + j, (*chip, 1 - c), me) for j, chip in enumerate(chips)]
    return mine, first, passed, over_ici, from_sibling


def gather_start(*refs):
    mine, first, _, _, _ = _gather_copies(*refs)
    mine.start()
    for cp in first:
        cp.start()


def gather_finish(*refs):
    mine, first, passed, over_ici, from_sibling = _gather_copies(*refs)
    for arrival, forward in zip(over_ici, passed):
        arrival.wait_recv()
        forward.start()
    for arrival in from_sibling:
        arrival.wait_recv()
    for cp in first + passed:
        cp.wait_send()
    mine.wait()


def all_to_all_blocks(send, name):
    _, r, c_ = send.shape

    def body(s_ref, out_ref, send_sems, recv_sems, local_sem):
        all_to_all_start(s_ref, out_ref, send_sems, recv_sems, local_sem)
        all_to_all_finish(s_ref, out_ref, send_sems, recv_sems, local_sem)

    return pl.pallas_call(
        body, name=name, out_shape=_sds((8, r, c_), send.dtype),
        in_specs=[pl.BlockSpec(memory_space=pl.ANY)], out_specs=pl.BlockSpec(memory_space=pl.ANY),
        scratch_shapes=exchange_semaphores())(send)


def _all_to_all_copies(s_ref, out_ref, send_sems, recv_sems, local_sem):
    x, y, c = _me()
    me = _flat(x, y, c)
    mine = pltpu.make_async_copy(s_ref.at[me], out_ref.at[me], local_sem)
    copies = []
    for rel in range(1, 8):
        px = 1 - x if rel & 4 else x
        py = 1 - y if rel & 2 else y
        pc = 1 - c if rel & 1 else c
        copies.append(pltpu.make_async_remote_copy(
            src_ref=s_ref.at[_flat(px, py, pc)], dst_ref=out_ref.at[me], send_sem=send_sems.at[rel - 1],
            recv_sem=recv_sems.at[rel - 1], device_id=(px, py, pc), device_id_type=pl.DeviceIdType.MESH))
    return mine, copies


def all_to_all_start(*refs):
    mine, copies = _all_to_all_copies(*refs)
    mine.start()
    for cp in copies:
        cp.start()


def all_to_all_finish(*refs):
    mine, copies = _all_to_all_copies(*refs)
    for cp in copies:
        cp.wait_recv()
    for cp in copies:
        cp.wait_send()
    mine.wait()


def reduce_blocks(recv, name):
    nsrc, r, c_ = recv.shape
    tr = _tile(r, 128, 16)

    def body(x_ref, o_ref):
        acc = x_ref[0].astype(F32)
        for s in range(1, nsrc):
            acc = acc + x_ref[s].astype(F32)
        o_ref[...] = acc

    return pl.pallas_call(
        body, name=name, grid=(r // tr,),
        in_specs=[pl.BlockSpec((nsrc, tr, c_), lambda i: (0, i, 0))], out_specs=pl.BlockSpec((tr, c_), lambda i: (i, 0)),
        out_shape=_sds((r, c_), F32), compiler_params=_cp(("parallel",)))(recv)


def adamw(w, g, m, v):
    shape = w.shape
    size = math.prod(shape)
    if shape[-1] < V7X_LANES and size % 1024 == 0:
        view = (size // 1024, 1024)
    else:
        view = (size // shape[-1], shape[-1])
    rows, cols = view
    tr = _tile(rows, max(8, (1 << 19) // cols // 8 * 8), 8)
    c1 = 1.0 - ADAM_B1 ** ADAM_STEP
    c2 = 1.0 - ADAM_B2 ** ADAM_STEP

    def body(w_ref, g_ref, m_ref, v_ref, d_ref, nm_ref, nv_ref):
        gv = g_ref[...]
        nm = ADAM_B1 * m_ref[...] + (1.0 - ADAM_B1) * gv
        nv = ADAM_B2 * v_ref[...] + (1.0 - ADAM_B2) * (gv * gv)
        nm_ref[...] = nm
        nv_ref[...] = nv
        d_ref[...] = -ADAM_LR * ((nm / c1) / (jnp.sqrt(nv / c2) + ADAM_EPS) + ADAM_WD * w_ref[...])

    blk = pl.BlockSpec((tr, cols), lambda i: (i, 0))
    out = _sds(view, F32)
    res = pl.pallas_call(
        body, name="adamw", grid=(rows // tr,), in_specs=[blk] * 4, out_specs=[blk] * 3, out_shape=[out] * 3,
        compiler_params=_cp(("parallel",)))(w.reshape(view), g.reshape(view), m.reshape(view), v.reshape(view))
    return tuple(a.reshape(shape) for a in res)


PACKED = [('w13t_1', 'ffn1_w13', True), ('w2_1', 'ffn1_w2', False), ('w_int', 'w_in', True), ('w_out', 'w_out', False),
          ('w13t_2', 'ffn2_w13', True), ('w2_2', 'ffn2_w2', False), ('w_glut', 'ssm_w_glu', True),
          ('conv_w_outt', 'conv_w_out', True), ('attn_w_ot', 'attn_w_o', True)]


def _rows_of(a, d):
    return a.reshape(a.shape[0] * a.shape[1] // d, d)


def _pad_rows(flat, d, mult):
    n = flat.shape[-1]
    rows = -(-n // d)
    rows = -(-rows // mult) * mult
    flat = jnp.pad(flat, [(0, 0)] * (flat.ndim - 1) + [(0, rows * d - n)])
    return flat.reshape(flat.shape[:-1] + (rows, d))


def pack_weight_shards(cfg, wts):
    d = cfg.d_model
    blocks, layout = [], []
    for l in range(cfg.depth):
        parts = []
        for name, src, tr in PACKED:
            a = wts[src][l]
            a = (a.T if tr else a).astype(BF16)
            if l == 0:
                layout.append((name, a.shape))
            parts.append(_rows_of(a, d))
        blocks.append(jnp.concatenate(parts, axis=0))
    small = jnp.concatenate([wts['conv_w'].reshape(-1), wts['meta_tokens'].reshape(-1)])
    return blocks, _pad_rows(small, d, 8), layout


def unpack_gathered_layer(cfg, gathered, layout):
    d = cfg.d_model
    out = {}
    off = 0
    for name, shape in layout:
        rows = shape[0] * shape[1] // d
        out[name] = gathered[:, off:off + rows].reshape(8 * shape[0], shape[1])
        off += rows
    return out


def unpack_gathered_small(gathered_small, wts):
    n_cw, n_mt = wts['conv_w'].size, wts['meta_tokens'].size
    small = gathered_small.reshape(8, -1)
    cw = small[:, :n_cw].reshape((8,) + wts['conv_w'].shape)
    conv_w = jnp.transpose(cw, (1, 2, 0, 3)).reshape(cw.shape[1], cw.shape[2], -1)
    mt = small[:, n_cw:n_cw + n_mt].reshape((8,) + wts['meta_tokens'].shape)
    return conv_w, jnp.transpose(mt, (1, 0, 2)).reshape(mt.shape[1], -1)


def pack_layer_grads(cfg, g, layout):
    d = cfg.d_model
    parts = [g[name].reshape(8, shape[0] * shape[1] // d, d).astype(BF16) for name, shape in layout]
    return jnp.concatenate(parts, axis=1)


def pack_small_grads(cfg, grads, rep_names):
    cw = grads['conv_w']
    cw = jnp.transpose(cw.reshape(cw.shape[0], cw.shape[1], 8, -1), (2, 0, 1, 3)).reshape(8, -1)
    mt = grads['meta_tokens']
    mt = jnp.transpose(mt.reshape(mt.shape[0], 8, -1), (1, 0, 2)).reshape(8, -1)
    repl = jnp.concatenate([grads[k].reshape(-1) for k in rep_names])
    small = jnp.concatenate([cw, mt, jnp.broadcast_to(repl[None], (8, repl.shape[0]))], axis=1)
    return _pad_rows(small, cfg.d_model, 128)


def unpack_grads(cfg, gsums, gsmall, layout, wts, rep_names):
    d = cfg.d_model
    acc = {}
    for gsum in gsums:
        off = 0
        for name, shape in layout:
            rows = shape[0] * shape[1] // d
            acc.setdefault(name, []).append(gsum[off:off + rows].reshape(shape))
            off += rows
    out = {}
    for name, src, tr in PACKED:
        a = jnp.stack(acc[name])
        out[src] = jnp.swapaxes(a, 1, 2) if tr else a
    flat = gsmall.reshape(-1)
    pos = 0
    for k in ['conv_w', 'meta_tokens'] + rep_names:
        n = wts[k].size
        out[k] = flat[pos:pos + n].reshape(wts[k].shape)
        pos += n
    return out


class WeightExchange:
    def __init__(self, cfg, wts):
        self.cfg = cfg
        self.blocks, small, self.layout = pack_weight_shards(cfg, wts)
        self.conv_w, self.meta_tokens = unpack_gathered_small(all_gather_blocks(small, "all_gather_small_weights"), wts)
        self.full = {0: unpack_gathered_layer(cfg, all_gather_blocks(self.blocks[0], "all_gather_weights"), self.layout)}
        self.to_send, self.arrived = {}, {}

    def weights(self, l):
        return dict(self.full[l], conv_w=self.conv_w[l])

    def gather_src(self, l):
        return self.blocks[l + 1] if l + 1 < self.cfg.depth else None

    def gathered(self, l, blocks):
        self.full[l + 1] = unpack_gathered_layer(self.cfg, blocks, self.layout)

    def matrix_grads(self, l, g):
        self.to_send[l] = pack_layer_grads(self.cfg, g, self.layout)

    def send_src(self, l):
        return self.to_send.get(l + 1)

    def received(self, l, blocks):
        self.arrived[l + 1] = blocks

    def finish(self, grads, wts):
        self.arrived[0] = all_to_all_blocks(self.to_send[0], "all_to_all_grads")
        gsums = [reduce_blocks(self.arrived[l], "reduce_grads") for l in range(self.cfg.depth)]
        small = all_to_all_blocks(pack_small_grads(self.cfg, grads, REPLICATED), "all_to_all_small_grads")
        return unpack_grads(self.cfg, gsums, reduce_blocks(small, "reduce_small_grads"), self.layout, wts, REPLICATED)


def train_step(cfg, x, target, wts, ms, vs):
    ex = WeightExchange(cfg, wts)
    rep = {k: wts[k] for k in REPLICATED}
    loss, grad_x, grads = local_step(cfg, x, target, rep, ex)
    gw = ex.finish(grads, wts)
    loss = lax.psum(loss, ("x", "y", "c"))
    deltas, new_m, new_v = {}, {}, {}
    for k in WEIGHT_NAMES:
        deltas[k], new_m[k], new_v[k] = adamw(wts[k], gw[k], ms[k], vs[k])
    return (loss, grad_x, *[gw[k] for k in WEIGHT_NAMES], *[deltas[k] for k in WEIGHT_NAMES],
            *[new_m[k] for k in WEIGHT_NAMES], *[new_v[k] for k in WEIGHT_NAMES])


def kernel(x, meta_tokens, ffn1_norm, ffn1_w13, ffn1_w2, mix_norm, w_in, ssm_lam_re, ssm_lam_im, ssm_log_dt, ssm_b_re, ssm_b_im, ssm_c_re, ssm_c_im, ssm_d, ssm_w_glu, conv_w, conv_b, conv_ln_g, conv_ln_b, conv_w_out, attn_w_o, w_out, ffn2_norm, ffn2_w13, ffn2_w2, final_norm, loss_target, m_meta_tokens, m_ffn1_norm, m_ffn1_w13, m_ffn1_w2, m_mix_norm, m_w_in, m_ssm_lam_re, m_ssm_lam_im, m_ssm_log_dt, m_ssm_b_re, m_ssm_b_im, m_ssm_c_re, m_ssm_c_im, m_ssm_d, m_ssm_w_glu, m_conv_w, m_conv_b, m_conv_ln_g, m_conv_ln_b, m_conv_w_out, m_attn_w_o, m_w_out, m_ffn2_norm, m_ffn2_w13, m_ffn2_w2, m_final_norm, v_meta_tokens, v_ffn1_norm, v_ffn1_w13, v_ffn1_w2, v_mix_norm, v_w_in, v_ssm_lam_re, v_ssm_lam_im, v_ssm_log_dt, v_ssm_b_re, v_ssm_b_im, v_ssm_c_re, v_ssm_c_im, v_ssm_d, v_ssm_w_glu, v_conv_w, v_conv_b, v_conv_ln_g, v_conv_ln_b, v_conv_w_out, v_attn_w_o, v_w_out, v_ffn2_norm, v_ffn2_w13, v_ffn2_w2, v_final_norm):
    given = dict(locals())
    wts = {k: given[k] for k in WEIGHT_NAMES}
    ms = {k: given["m_" + k] for k in WEIGHT_NAMES}
    vs = {k: given["v_" + k] for k in WEIGHT_NAMES}
    return train_step(FULL, x, loss_target, wts, ms, vs)
```

```python
import functools
import math
from typing import NamedTuple

import jax
import jax.numpy as jnp
from jax import lax
from jax.experimental import pallas as pl
from jax.experimental.pallas import tpu as pltpu

F32 = jnp.float32
BF16 = jnp.bfloat16
RMS_EPS = 1e-6
LN_EPS = 1e-5
ADAM_LR = 0.001
ADAM_B1 = 0.9
ADAM_B2 = 0.999
ADAM_EPS = 1e-08
ADAM_WD = 0.01
ADAM_STEP = 10
V7X_LANES = 128
V7X_VMEM_LIMIT = 52 * 1024 * 1024
Q_BLOCK = 128
SSM_CHUNK = 128
SSM_SB = 128
CONV_HALO = 32

WEIGHT_NAMES = ['meta_tokens', 'ffn1_norm', 'ffn1_w13', 'ffn1_w2', 'mix_norm', 'w_in', 'ssm_lam_re', 'ssm_lam_im',
                'ssm_log_dt', 'ssm_b_re', 'ssm_b_im', 'ssm_c_re', 'ssm_c_im', 'ssm_d', 'ssm_w_glu', 'conv_w', 'conv_b',
                'conv_ln_g', 'conv_ln_b', 'conv_w_out', 'attn_w_o', 'w_out', 'ffn2_norm', 'ffn2_w13', 'ffn2_w2',
                'final_norm']
REPLICATED = ['ffn1_norm', 'mix_norm', 'ssm_lam_re', 'ssm_lam_im', 'ssm_log_dt', 'ssm_b_re', 'ssm_b_im', 'ssm_c_re',
              'ssm_c_im', 'ssm_d', 'conv_b', 'conv_ln_g', 'conv_ln_b', 'ffn2_norm', 'final_norm']


class Cfg(NamedTuple):
    d_model: int
    seq: int
    depth: int
    d_ff: int
    b_loc: int
    ndev: int = 8
    n_meta: int = 16
    n_heads: int = 8
    head_dim: int = 64
    ssm_group: int = 16
    ssm_state: int = 64
    conv_width: int = 31

    @property
    def d_ssm(self): return self.d_model // 2
    @property
    def groups(self): return self.d_ssm // self.ssm_group
    @property
    def n_state(self): return self.groups * self.ssm_state
    @property
    def d_conv(self): return self.d_model // 2
    @property
    def d_attn(self): return self.n_heads * self.head_dim
    @property
    def off_xa(self): return self.d_ssm
    @property
    def off_xg(self): return self.d_ssm + self.d_conv
    @property
    def off_q(self): return self.d_ssm + 2 * self.d_conv
    @property
    def off_k(self): return self.off_q + self.d_attn
    @property
    def off_v(self): return self.off_k + self.d_attn
    @property
    def off_gate(self): return self.off_v + self.d_attn
    @property
    def d_in(self): return self.off_gate + 3 * self.d_model
    @property
    def pad(self): return Q_BLOCK - self.n_meta
    @property
    def lp(self): return Q_BLOCK + self.seq
    @property
    def tokens(self): return self.b_loc * self.lp


FULL = Cfg(d_model=1024, seq=4096, depth=4, d_ff=2816, b_loc=2)


def _tile(n, pref, align):
    for t in range(min(pref, n), 0, -1):
        if n % t == 0 and t % align == 0:
            return t
    return n


def _cp(sem):
    return pltpu.CompilerParams(dimension_semantics=sem, vmem_limit_bytes=V7X_VMEM_LIMIT)


def _sds(shape, dtype):
    return jax.ShapeDtypeStruct(shape, dtype)


def _sigmoid(x):
    return jax.nn.sigmoid(x)


def _dot(a, b, ca, cb):
    return lax.dot_general(a, b, (((ca,), (cb,)), ((), ())), preferred_element_type=F32)


def rms_fwd(h, g, name):
    t, d = h.shape
    tm = _tile(t, 1056, 16)

    def body(h_ref, g_ref, o_ref):
        x = h_ref[...]
        r = lax.rsqrt(jnp.mean(x * x, axis=-1, keepdims=True) + RMS_EPS)
        o_ref[...] = (x * r * g_ref[...]).astype(BF16)

    return pl.pallas_call(
        body, name=name, grid=(t // tm,),
        in_specs=[pl.BlockSpec((tm, d), lambda i: (i, 0)), pl.BlockSpec((1, d), lambda i: (0, 0))],
        out_specs=pl.BlockSpec((tm, d), lambda i: (i, 0)), out_shape=_sds((t, d), BF16),
        compiler_params=_cp(("parallel",)))(h, g.reshape(1, d))


def rms_bwd(dxn, h, g, dh_res, name):
    t, d = h.shape
    tm = _tile(t, 528, 16)

    def body(dxn_ref, h_ref, g_ref, r_ref, dh_ref, dh16_ref, dg_ref):
        x = h_ref[...]
        r = lax.rsqrt(jnp.mean(x * x, axis=-1, keepdims=True) + RMS_EPS)
        xhat = x * r
        dy = dxn_ref[...]
        dyg = dy * g_ref[...]
        dh = r_ref[...] + r * (dyg - xhat * jnp.mean(dyg * xhat, axis=-1, keepdims=True))
        dh_ref[...] = dh
        dh16_ref[...] = dh.astype(BF16)

        @pl.when(pl.program_id(0) == 0)
        def _():
            dg_ref[...] = jnp.zeros_like(dg_ref)

        dg_ref[...] += jnp.sum(dy * xhat, axis=0, keepdims=True)

    row = pl.BlockSpec((tm, d), lambda i: (i, 0))
    vec = pl.BlockSpec((1, d), lambda i: (0, 0))
    return pl.pallas_call(
        body, name=name, grid=(t // tm,), in_specs=[row, row, vec, row], out_specs=[row, row, vec],
        out_shape=[_sds((t, d), F32), _sds((t, d), BF16), _sds((1, d), F32)],
        compiler_params=_cp(("arbitrary",)))(dxn, h, g.reshape(1, d), dh_res)


def mm(pairs, *, nt, n, tm, tn, out_dtype, name, res=None, res_scale=1.0):
    m = pairs[0][0].shape[0]
    np_ = len(pairs)

    def body(*refs):
        o_ref = refs[-1]
        acc = None
        for p in range(np_):
            a = refs[2 * p][...].astype(BF16)
            b = refs[2 * p + 1][...].astype(BF16)
            d = _dot(a, b, 1, 1 if nt else 0)
            acc = d if acc is None else acc + d
        if res is not None:
            acc = refs[2 * np_][...] + res_scale * acc
        o_ref[...] = acc.astype(out_dtype)

    in_specs, args = [], []
    for a, b, kblk in pairs:
        k = a.shape[1]
        in_specs.append(pl.BlockSpec((tm, k), lambda i, j: (i, 0)))
        if nt:
            nb = n // tn
            in_specs.append(pl.BlockSpec((tn, k), functools.partial(lambda i, j, o: (j + o, 0), o=kblk * nb)))
        else:
            in_specs.append(pl.BlockSpec((k, tn), functools.partial(lambda i, j, o: (o, j), o=kblk)))
        args += [a, b]
    if res is not None:
        in_specs.append(pl.BlockSpec((tm, tn), lambda i, j: (i, j)))
        args.append(res)
    return pl.pallas_call(
        body, name=name, grid=(m // tm, n // tn), in_specs=in_specs,
        out_specs=pl.BlockSpec((tm, tn), lambda i, j: (i, j)), out_shape=_sds((m, n), out_dtype),
        compiler_params=_cp(("parallel", "arbitrary")))(*args)


def mm_tn(a, b, *, name, scale=1.0):
    t, m = a.shape
    n = b.shape[1]
    tm = _tile(m, 1536, 128)
    tn = _tile(n, 1024, 128)
    tk = _tile(t, 1056, 16)
    nk = t // tk

    def body(a_ref, b_ref, o_ref):
        k = pl.program_id(2)

        @pl.when(k == 0)
        def _():
            o_ref[...] = jnp.zeros_like(o_ref)

        o_ref[...] += _dot(a_ref[...].astype(BF16), b_ref[...].astype(BF16), 0, 0)
        if scale != 1.0:
            @pl.when(k == nk - 1)
            def _():
                o_ref[...] = o_ref[...] * scale

    return pl.pallas_call(
        body, name=name, grid=(m // tm, n // tn, nk),
        in_specs=[pl.BlockSpec((tk, tm), lambda i, j, k: (k, i)), pl.BlockSpec((tk, tn), lambda i, j, k: (k, j))],
        out_specs=pl.BlockSpec((tm, tn), lambda i, j, k: (i, j)), out_shape=_sds((m, n), F32),
        compiler_params=_cp(("parallel", "parallel", "arbitrary")))(a, b)


def dual_mm_act(x, wt, *, kind, name):
    t, k = x.shape
    hdim = wt.shape[0] // 2
    tm = _tile(t, 528, 16)
    tn = _tile(hdim, 1408, 128)
    nb = hdim // tn
    act_dtype = BF16 if kind == 'swiglu' else F32

    def body(x_ref, wa_ref, wb_ref, a_ref, b_ref, act_ref):
        xv = x_ref[...]
        a = _dot(xv, wa_ref[...], 1, 1)
        b = _dot(xv, wb_ref[...], 1, 1)
        a_ref[...] = a.astype(act_dtype)
        b_ref[...] = b.astype(act_dtype)
        if kind == 'swiglu':
            act_ref[...] = (a * _sigmoid(a) * b).astype(act_dtype)
        else:
            act_ref[...] = (a * _sigmoid(b)).astype(act_dtype)

    ob = pl.BlockSpec((tm, tn), lambda j, i: (i, j))
    return pl.pallas_call(
        body, name=name, grid=(nb, t // tm),
        in_specs=[pl.BlockSpec((tm, k), lambda j, i: (i, 0)), pl.BlockSpec((tn, k), lambda j, i: (j, 0)),
                  pl.BlockSpec((tn, k), lambda j, i: (j + nb, 0))],
        out_specs=[ob, ob, ob], out_shape=[_sds((t, hdim), act_dtype)] * 3,
        compiler_params=_cp(("parallel", "arbitrary")))(x, wt, wt)


def ffn_down_bwd(dh, w2, a, b, name):
    t, d = dh.shape
    hdim = w2.shape[0]
    tm = _tile(t, 528, 16)
    tn = _tile(hdim, 1408, 128)

    def body(dh_ref, w_ref, a_ref, b_ref, da_ref, db_ref):
        dact = 0.5 * _dot(dh_ref[...].astype(BF16), w_ref[...], 1, 1)
        av = a_ref[...].astype(F32)
        sg = _sigmoid(av)
        da_ref[...] = (dact * b_ref[...].astype(F32) * sg * (1.0 + av * (1.0 - sg))).astype(BF16)
        db_ref[...] = (dact * av * sg).astype(BF16)

    ob = pl.BlockSpec((tm, tn), lambda j, i: (i, j))
    return pl.pallas_call(
        body, name=name, grid=(hdim // tn, t // tm),
        in_specs=[pl.BlockSpec((tm, d), lambda j, i: (i, 0)), pl.BlockSpec((tn, d), lambda j, i: (j, 0)), ob, ob],
        out_specs=[ob, ob], out_shape=[_sds((t, hdim), BF16), _sds((t, hdim), BF16)],
        compiler_params=_cp(("parallel", "arbitrary")))(dh, w2, a, b)


def final_fwd_bwd(cfg, h, gf, target):
    bsz, lp, d = h.shape
    nq = lp // Q_BLOCK

    def body(h_ref, g_ref, t_ref, dh_ref, dh16_ref, dg_ref, loss_ref):
        b, j = pl.program_id(0), pl.program_id(1)
        x = h_ref[0]
        r = lax.rsqrt(jnp.mean(x * x, axis=-1, keepdims=True) + RMS_EPS)
        xhat = x * r
        gv = g_ref[...]
        diff = jnp.where(j > 0, xhat * gv - t_ref[0], 0.0)
        dy = diff * (1.0 / d)
        dyg = dy * gv
        dh = r * (dyg - xhat * jnp.mean(dyg * xhat, axis=-1, keepdims=True))
        dh_ref[0] = dh
        dh16_ref[0] = dh.astype(BF16)

        @pl.when((b == 0) & (j == 0))
        def _():
            dg_ref[...] = jnp.zeros_like(dg_ref)
            loss_ref[...] = jnp.zeros_like(loss_ref)

        dg_ref[...] += jnp.sum(dy * xhat, axis=0, keepdims=True)
        loss_ref[...] += jnp.sum(jnp.sum(diff * diff, axis=1, keepdims=True), axis=0, keepdims=True)

    blk = pl.BlockSpec((1, Q_BLOCK, d), lambda b, j: (b, j, 0))
    return pl.pallas_call(
        body, name="final_loss", grid=(bsz, nq),
        in_specs=[blk, pl.BlockSpec((1, d), lambda b, j: (0, 0)),
                  pl.BlockSpec((1, Q_BLOCK, d), lambda b, j: (b, jnp.maximum(j - 1, 0), 0))],
        out_specs=[blk, blk, pl.BlockSpec((1, d), lambda b, j: (0, 0)), pl.BlockSpec((1, 1), lambda b, j: (0, 0))],
        out_shape=[_sds((bsz, lp, d), F32), _sds((bsz, lp, d), BF16), _sds((1, d), F32), _sds((1, 1), F32)],
        compiler_params=_cp(("arbitrary", "arbitrary")))(h, gf.reshape(1, d), target)


def _row_valid(cfg, tm, i):
    pos = (i * tm) % cfg.lp + lax.broadcasted_iota(jnp.int32, (tm, 1), 0)
    return pos >= cfg.pad


def merge_fwd(cfg, proj, o_ssm, o_conv, o_attn):
    t, d = o_ssm.shape
    tc = math.gcd(math.gcd(d, cfg.off_gate), 512)
    tm = _tile(cfg.lp, 528, 16)
    g0 = cfg.off_gate // tc
    nc = d // tc

    def body(g0_ref, g1_ref, g2_ref, s_ref, c_ref, a_ref, o_ref):
        valid = _row_valid(cfg, tm, pl.program_id(0))
        m = (_sigmoid(g0_ref[...]) * s_ref[...] + _sigmoid(g1_ref[...]) * c_ref[...]
             + _sigmoid(g2_ref[...]) * a_ref[...])
        o_ref[...] = jnp.where(valid, m, 0.0).astype(BF16)

    gate = [pl.BlockSpec((tm, tc), functools.partial(lambda i, j, o: (i, o + j), o=g0 + k * nc)) for k in range(3)]
    blk = pl.BlockSpec((tm, tc), lambda i, j: (i, j))
    return pl.pallas_call(
        body, name="merge_fwd", grid=(t // tm, nc), in_specs=gate + [blk, blk, blk], out_specs=blk,
        out_shape=_sds((t, d), BF16), compiler_params=_cp(("parallel", "parallel")))(
            proj, proj, proj, o_ssm, o_conv, o_attn)


def merge_bwd(cfg, dmerged, proj, ga, gg, o_conv, o_attn):
    t, d = dmerged.shape
    tc = math.gcd(math.gcd(d, cfg.off_gate), 512)
    tm = _tile(cfg.lp, 528, 16)
    g0 = cfg.off_gate // tc
    nc = d // tc

    def body(dm_ref, g0_ref, g1_ref, g2_ref, ga_ref, gg_ref, c_ref, a_ref,
             dg0_ref, dg1_ref, dg2_ref, dga_ref, dgg_ref, dc_ref, da_ref):
        dm = dm_ref[...]
        s0, s1, s2 = _sigmoid(g0_ref[...]), _sigmoid(g1_ref[...]), _sigmoid(g2_ref[...])
        sg = _sigmoid(gg_ref[...])
        gav = ga_ref[...]
        o_s = gav * sg
        dg0_ref[...] = (dm * o_s * s0 * (1.0 - s0)).astype(BF16)
        dg1_ref[...] = (dm * c_ref[...] * s1 * (1.0 - s1)).astype(BF16)
        dg2_ref[...] = (dm * a_ref[...] * s2 * (1.0 - s2)).astype(BF16)
        dos = dm * s0
        dga_ref[...] = (dos * sg).astype(BF16)
        dgg_ref[...] = (dos * gav * sg * (1.0 - sg)).astype(BF16)
        dc_ref[...] = (dm * s1).astype(BF16)
        da_ref[...] = (dm * s2).astype(BF16)

    gate = [pl.BlockSpec((tm, tc), functools.partial(lambda i, j, o: (i, o + j), o=g0 + k * nc)) for k in range(3)]
    blk = pl.BlockSpec((tm, tc), lambda i, j: (i, j))
    out = _sds((t, d), BF16)
    return pl.pallas_call(
        body, name="merge_bwd", grid=(t // tm, nc), in_specs=[blk] + gate + [blk, blk, blk, blk],
        out_specs=[blk] * 7, out_shape=[out] * 7, compiler_params=_cp(("parallel", "parallel")))(
            dmerged, proj, proj, proj, ga, gg, o_conv, o_attn)


def _conv_tile(cfg):
    return _tile(cfg.lp, 384, CONV_HALO)


def _shift_up(x, off, rows):
    if off == 0:
        return x[:rows]
    return pltpu.roll(x, x.shape[0] - off, 0)[:rows]


def conv_fwd(cfg, proj, w, bias, ln_g, ln_b):
    t = proj.shape[0]
    wc = cfg.d_conv
    tt = _conv_tile(cfg)
    nc = cfg.lp // tt
    kw = cfg.conv_width
    first = CONV_HALO - (kw - 1)

    def body(a_ref, g_ref, ap_ref, gp_ref, w_ref, b_ref, lg_ref, lb_ref, cpre_ref, hn_ref):
        c = pl.program_id(1)
        hc = a_ref[...] * _sigmoid(g_ref[...])
        hp = jnp.where(c > 0, ap_ref[...] * _sigmoid(gp_ref[...]), 0.0)
        hcat = jnp.concatenate([hp, hc], axis=0)
        acc = jnp.zeros((tt, wc), F32) + b_ref[...]
        for k in range(kw):
            acc = acc + w_ref[k:k + 1, :] * _shift_up(hcat, first + k, tt)
        cpre_ref[...] = acc
        mu = jnp.mean(acc, axis=-1, keepdims=True)
        xc = acc - mu
        y = xc * lax.rsqrt(jnp.mean(xc * xc, axis=-1, keepdims=True) + LN_EPS) * lg_ref[...] + lb_ref[...]
        hn_ref[...] = (y * _sigmoid(y)).astype(BF16)

    ca, cg = cfg.off_xa // wc, cfg.off_xg // wc
    hb = tt // CONV_HALO
    lph = cfg.lp // CONV_HALO

    def halo(col):
        return pl.BlockSpec((CONV_HALO, wc), lambda b, c: (jnp.maximum(b * lph + c * hb - 1, 0), col))

    vec = pl.BlockSpec((1, wc), lambda b, c: (0, 0))
    blk = pl.BlockSpec((tt, wc), lambda b, c: (b * nc + c, 0))
    return pl.pallas_call(
        body, name="conv_fwd", grid=(cfg.b_loc, nc),
        in_specs=[pl.BlockSpec((tt, wc), lambda b, c: (b * nc + c, ca)),
                  pl.BlockSpec((tt, wc), lambda b, c: (b * nc + c, cg)), halo(ca), halo(cg),
                  pl.BlockSpec((kw, wc), lambda b, c: (0, 0)), vec, vec, vec],
        out_specs=[blk, blk], out_shape=[_sds((t, wc), F32), _sds((t, wc), BF16)],
        compiler_params=_cp(("parallel", "arbitrary")))(
            proj, proj, proj, proj, w, bias.reshape(1, wc), ln_g.reshape(1, wc), ln_b.reshape(1, wc))


def conv_bwd(cfg, dhn, cpre, proj, w, ln_g, ln_b):
    t = proj.shape[0]
    wc = cfg.d_conv
    tt = _conv_tile(cfg)
    nc = cfg.lp // tt
    kw = cfg.conv_width
    first = CONV_HALO - (kw - 1)

    def body(dhn_ref, dhn_nx_ref, cp_ref, cp_nx_ref, a_ref, g_ref, ap_ref, gp_ref, w_ref, lg_ref, lb_ref,
             dxa_ref, dxg_ref, dw_ref, db_ref, dlg_ref, dlb_ref):
        b, c = pl.program_id(0), pl.program_id(1)
        lg, lb = lg_ref[...], lb_ref[...]

        def ln_silu_bwd(dh, cp):
            mu = jnp.mean(cp, axis=-1, keepdims=True)
            xc = cp - mu
            rstd = lax.rsqrt(jnp.mean(xc * xc, axis=-1, keepdims=True) + LN_EPS)
            xhat = xc * rstd
            y = xhat * lg + lb
            sg = _sigmoid(y)
            dy = dh * sg * (1.0 + y * (1.0 - sg))
            dxh = dy * lg
            dc = rstd * (dxh - jnp.mean(dxh, axis=-1, keepdims=True)
                         - xhat * jnp.mean(dxh * xhat, axis=-1, keepdims=True))
            return dc, dy, xhat

        dc, dy, xhat = ln_silu_bwd(dhn_ref[...], cp_ref[...])
        dc_nx, _, _ = ln_silu_bwd(dhn_nx_ref[...], cp_nx_ref[...])
        dc_nx = jnp.where(c < nc - 1, dc_nx, 0.0)
        dcat = jnp.concatenate([dc, dc_nx], axis=0)
        dhc = jnp.zeros((tt, wc), F32)
        for k in range(kw):
            dhc = dhc + w_ref[k:k + 1, :] * _shift_up(dcat, kw - 1 - k, tt)
        av, gv = a_ref[...], g_ref[...]
        sg = _sigmoid(gv)
        valid = (c * tt + lax.broadcasted_iota(jnp.int32, (tt, 1), 0)) >= cfg.pad
        dhc = jnp.where(valid, dhc, 0.0)
        dxa_ref[...] = (dhc * sg).astype(BF16)
        dxg_ref[...] = (dhc * av * sg * (1.0 - sg)).astype(BF16)

        @pl.when((b == 0) & (c == 0))
        def _():
            dw_ref[...] = jnp.zeros_like(dw_ref)
            db_ref[...] = jnp.zeros_like(db_ref)
            dlg_ref[...] = jnp.zeros_like(dlg_ref)
            dlb_ref[...] = jnp.zeros_like(dlb_ref)

        hp = jnp.where(c > 0, ap_ref[...] * _sigmoid(gp_ref[...]), 0.0)
        hcat = jnp.concatenate([hp, av * sg], axis=0)
        for k in range(kw):
            dw_ref[k:k + 1, :] += jnp.sum(dc * _shift_up(hcat, first + k, tt), axis=0, keepdims=True)
        db_ref[...] += jnp.sum(dc, axis=0, keepdims=True)
        dlg_ref[...] += jnp.sum(dy * xhat, axis=0, keepdims=True)
        dlb_ref[...] += jnp.sum(dy, axis=0, keepdims=True)

    ca, cg = cfg.off_xa // wc, cfg.off_xg // wc
    hb = tt // CONV_HALO
    lph = cfg.lp // CONV_HALO
    last = cfg.b_loc * lph - 1

    def prev(col):
        return pl.BlockSpec((CONV_HALO, wc), lambda b, c: (jnp.maximum(b * lph + c * hb - 1, 0), col))

    nxt = pl.BlockSpec((CONV_HALO, wc), lambda b, c: (jnp.minimum(b * lph + (c + 1) * hb, last), 0))
    vec = pl.BlockSpec((1, wc), lambda b, c: (0, 0))
    blk = pl.BlockSpec((tt, wc), lambda b, c: (b * nc + c, 0))
    wspec = pl.BlockSpec((kw, wc), lambda b, c: (0, 0))
    return pl.pallas_call(
        body, name="conv_bwd", grid=(cfg.b_loc, nc),
        in_specs=[blk, nxt, blk, nxt, pl.BlockSpec((tt, wc), lambda b, c: (b * nc + c, ca)),
                  pl.BlockSpec((tt, wc), lambda b, c: (b * nc + c, cg)), prev(ca), prev(cg), wspec, vec, vec],
        out_specs=[blk, blk, wspec, vec, vec, vec],
        out_shape=[_sds((t, wc), BF16), _sds((t, wc), BF16), _sds((kw, wc), F32), _sds((1, wc), F32),
                   _sds((1, wc), F32), _sds((1, wc), F32)],
        compiler_params=_cp(("arbitrary", "arbitrary")))(
            dhn, dhn, cpre, cpre, proj, proj, proj, proj, w, ln_g.reshape(1, wc), ln_b.reshape(1, wc))


def _split_bf16(x):
    hi = x.astype(BF16)
    return hi, (x - hi.astype(F32)).astype(BF16)


K_CHUNK = 2 * Q_BLOCK
NEG_BIG = -1e30


def _sb_logits(cfg, qh, kblk, col0, row_t, masked):
    z = _dot(qh, kblk, 1, 1)
    lk = -(jnp.maximum(z, 0.0) + jnp.log(1.0 + jnp.exp(-jnp.abs(z))))
    if masked:
        col_s = col0 + lax.broadcasted_iota(jnp.int32, (1, K_CHUNK), 1)
        mask = (col_s < row_t) & (col_s >= cfg.pad)
        z, lk = jnp.where(mask, z, NEG_BIG), jnp.where(mask, lk, 0.0)
    return z, lk


def _tri_sum(x, tri):
    hi, lo = _split_bf16(x)
    return _dot(hi, tri, 1, 0) + _dot(lo, tri, 1, 0)


def _grid_ends(dims):
    ids = [pl.program_id(a) for a in range(len(dims))]
    first = functools.reduce(lambda p, q: p & q, [i == 0 for i in ids])
    last = functools.reduce(lambda p, q: p & q, [i == n - 1 for i, n in zip(ids, dims)])
    return first, last


def attn_fwd(cfg, proj, gather_src=None):
    t = proj.shape[0]
    lp = cfg.lp
    nq = lp // Q_BLOCK
    nhp = cfg.d_attn // V7X_LANES
    cq, ck, cv = cfg.off_q // V7X_LANES, cfg.off_k // V7X_LANES, cfg.off_v // V7X_LANES
    scale = 1.0 / math.sqrt(cfg.head_dim)
    grid = (cfg.b_loc, nhp, nq)

    def body(*refs):
        if gather_src is None:
            q_ref, k_ref, v_ref, o_ref, lt_ref, kbf, vbf, tri_from = refs
        else:
            q_ref, k_ref, v_ref, src_ref, o_ref, lt_ref, dst_ref, kbf, vbf, tri_from = refs[:10]
            exchange = (src_ref, dst_ref) + tuple(refs[10:])
            first_step, last_step = _grid_ends(grid)
            pl.when(first_step)(lambda: gather_start(*exchange))
        qb = pl.program_id(2)

        @pl.when(qb == 0)
        def _():
            kbf[pl.ds(0, lp), :] = k_ref[...].astype(BF16)
            vbf[pl.ds(0, lp), :] = v_ref[...].astype(BF16)
            kbf[pl.ds(lp, Q_BLOCK), :] = jnp.zeros((Q_BLOCK, V7X_LANES), BF16)
            vbf[pl.ds(lp, Q_BLOCK), :] = jnp.zeros((Q_BLOCK, V7X_LANES), BF16)
            ri = lax.broadcasted_iota(jnp.int32, (K_CHUNK, K_CHUNK), 0)
            ci = lax.broadcasted_iota(jnp.int32, (K_CHUNK, K_CHUNK), 1)
            tri_from[...] = (ri >= ci).astype(BF16)

        q = q_ref[...] * scale
        lane = lax.broadcasted_iota(jnp.int32, (1, V7X_LANES), 1)
        head0 = lane < cfg.head_dim
        qs = jnp.concatenate([jnp.where(head0, q, 0.0), jnp.where(head0, 0.0, q)], axis=0).astype(BF16)
        rows2 = lax.broadcasted_iota(jnp.int32, (2 * Q_BLOCK, 1), 0)
        row_t = qb * Q_BLOCK + jnp.where(rows2 >= Q_BLOCK, rows2 - Q_BLOCK, rows2)
        last = qb // (K_CHUNK // Q_BLOCK)

        def stage1(k, masked):
            c = jnp.clip(last - k, 0, last)
            off = pl.multiple_of(c * K_CHUNK, K_CHUNK)
            col0 = jnp.where(k <= last, c * K_CHUNK, cfg.lp + K_CHUNK)
            return _sb_logits(cfg, qs, kbf[pl.ds(off, K_CHUNK), :], col0, row_t, masked)

        def stage2(z, lk):
            return z + _tri_sum(lk, tri_from[...]), jnp.sum(lk, axis=1, keepdims=True)

        def stage3(pre, rows, acc, r_carry, k):
            off = pl.multiple_of(jnp.clip(last - k, 0, last) * K_CHUNK, K_CHUNK)
            w = jnp.exp(pre + r_carry)
            return acc + _dot(w.astype(BF16), vbf[pl.ds(off, K_CHUNK), :], 1, 0), r_carry + rows

        def step(masked):
            def body(k, carry):
                z_p, lk_p, pre_pp, rows_pp, acc, r_carry = carry
                acc, r_carry = stage3(pre_pp, rows_pp, acc, r_carry, k - 2)
                pre_p, rows_p = stage2(z_p, lk_p)
                z, lk = stage1(k, masked)
                return z, lk, pre_p, rows_p, acc, r_carry
            return body

        zcol = jnp.where(rows2 < 0, 1.0, 0.0)
        first = stage1(0, True)
        carry = stage1(1, True) + stage2(*first) + (zcol + jnp.where(lane < 0, 1.0, 0.0), zcol)
        carry = lax.fori_loop(2, jnp.maximum(last, 2), step(False), carry)
        carry = lax.fori_loop(jnp.maximum(last, 2), jnp.maximum(last + 1, 2), step(True), carry)
        z_p, lk_p, pre_pp, rows_pp, acc, r_tot = carry
        k_end = jnp.maximum(last + 1, 2)
        acc, r_tot = stage3(pre_pp, rows_pp, acc, r_tot, k_end - 2)
        acc, r_tot = stage3(*stage2(z_p, lk_p), acc, r_tot, k_end - 1)
        o_ref[...] = jnp.where(head0, acc[:Q_BLOCK], acc[Q_BLOCK:]).astype(BF16)
        lt_ref[...] = (jnp.where(lane == 0, r_tot[:Q_BLOCK], 0.0)
                       + jnp.where(lane == cfg.head_dim, r_tot[Q_BLOCK:], 0.0))
        if gather_src is not None:
            pl.when(last_step)(lambda: gather_finish(*exchange))

    oblk = pl.BlockSpec((Q_BLOCK, V7X_LANES), lambda b, hp, qb: (b * nq + qb, hp))
    hbm = pl.BlockSpec(memory_space=pl.ANY)
    comm = gather_src is not None
    return pl.pallas_call(
        body, name="attn_fwd_gather" if comm else "attn_fwd", grid=grid,
        in_specs=[pl.BlockSpec((Q_BLOCK, V7X_LANES), lambda b, hp, qb: (b * nq + qb, cq + hp)),
                  pl.BlockSpec((lp, V7X_LANES), lambda b, hp, qb: (b, ck + hp)),
                  pl.BlockSpec((lp, V7X_LANES), lambda b, hp, qb: (b, cv + hp))] + ([hbm] if comm else []),
        out_specs=[oblk, oblk] + ([hbm] if comm else []),
        out_shape=[_sds((t, cfg.d_attn), BF16), _sds((t, cfg.d_attn), F32)]
        + ([_sds((8,) + gather_src.shape, gather_src.dtype)] if comm else []),
        scratch_shapes=[pltpu.VMEM((lp + Q_BLOCK, V7X_LANES), BF16), pltpu.VMEM((lp + Q_BLOCK, V7X_LANES), BF16),
                        pltpu.VMEM((K_CHUNK, K_CHUNK), BF16)] + (exchange_semaphores() if comm else []),
        compiler_params=_cp(("arbitrary", "arbitrary", "arbitrary")))(*((proj, proj, proj) + ((gather_src,) if comm else ())))


def attn_bwd(cfg, proj, lt, do, send=None):
    t = proj.shape[0]
    lp = cfg.lp
    nq = lp // Q_BLOCK
    nhp = cfg.d_attn // V7X_LANES
    cq, ck, cv = cfg.off_q // V7X_LANES, cfg.off_k // V7X_LANES, cfg.off_v // V7X_LANES
    scale = 1.0 / math.sqrt(cfg.head_dim)
    grid = (cfg.b_loc, nhp, nq)

    def body(*refs):
        if send is None:
            (q_ref, k_ref, v_ref, lt_ref, do_ref, dq_ref, dk_ref, dv_ref, kbf, vbf, dk_acc, dv_acc,
             tri_before, tri_upto) = refs
        else:
            (q_ref, k_ref, v_ref, lt_ref, do_ref, src_ref, dq_ref, dk_ref, dv_ref, dst_ref,
             kbf, vbf, dk_acc, dv_acc, tri_before, tri_upto) = refs[:16]
            exchange = (src_ref, dst_ref) + tuple(refs[16:])
            first_step, last_step = _grid_ends(grid)
            pl.when(first_step)(lambda: all_to_all_start(*exchange))
        qb = pl.program_id(2)

        @pl.when(qb == 0)
        def _():
            kbf[pl.ds(0, lp), :] = k_ref[...].astype(BF16)
            vbf[pl.ds(0, lp), :] = v_ref[...].astype(BF16)
            kbf[pl.ds(lp, Q_BLOCK), :] = jnp.zeros((Q_BLOCK, V7X_LANES), BF16)
            vbf[pl.ds(lp, Q_BLOCK), :] = jnp.zeros((Q_BLOCK, V7X_LANES), BF16)
            dk_acc[...] = jnp.zeros_like(dk_acc)
            dv_acc[...] = jnp.zeros_like(dv_acc)
            ri = lax.broadcasted_iota(jnp.int32, (K_CHUNK, K_CHUNK), 0)
            ci = lax.broadcasted_iota(jnp.int32, (K_CHUNK, K_CHUNK), 1)
            tri_before[...] = (ri < ci).astype(BF16)
            tri_upto[...] = (ri <= ci).astype(BF16)

        q = q_ref[...] * scale
        ltv = lt_ref[...]
        dov = do_ref[...]
        lane = lax.broadcasted_iota(jnp.int32, (1, V7X_LANES), 1)
        head0 = lane < cfg.head_dim
        qs = jnp.concatenate([jnp.where(head0, q, 0.0), jnp.where(head0, 0.0, q)], axis=0).astype(BF16)
        dos = jnp.concatenate([jnp.where(head0, dov, 0.0), jnp.where(head0, 0.0, dov)], axis=0).astype(BF16)
        lk_tot = jnp.concatenate(
            [jnp.sum(jnp.where(lane == 0, ltv, 0.0), axis=1, keepdims=True),
             jnp.sum(jnp.where(lane == cfg.head_dim, ltv, 0.0), axis=1, keepdims=True)], axis=0)
        rows2 = lax.broadcasted_iota(jnp.int32, (2 * Q_BLOCK, 1), 0)
        row_t = qb * Q_BLOCK + jnp.where(rows2 >= Q_BLOCK, rows2 - Q_BLOCK, rows2)
        n_chunks = qb // (K_CHUNK // Q_BLOCK) + 1

        def stage1(k, masked, lk_before):
            off = pl.multiple_of(k * K_CHUNK, K_CHUNK)
            z, lk = _sb_logits(cfg, qs, kbf[pl.ds(off, K_CHUNK), :], k * K_CHUNK, row_t, masked)
            dw = _dot(dos, vbf[pl.ds(off, K_CHUNK), :], 1, 1)
            w = jnp.exp(z + (lk_tot - lk_before - _tri_sum(lk, tri_before[...])))
            sg = 1.0 - jnp.exp(lk)
            return w * dw, w.astype(BF16), sg, lk_before + jnp.sum(lk, axis=1, keepdims=True)

        def stage2(e, w16, sg, k, dq_acc, e_before):
            off = pl.multiple_of(k * K_CHUNK, K_CHUNK)
            dz = (e - sg * (e_before + _tri_sum(e, tri_upto[...]))).astype(BF16)
            dk_acc[pl.ds(off, K_CHUNK), :] += _dot(dz, qs, 0, 0)
            dv_acc[pl.ds(off, K_CHUNK), :] += _dot(w16, dos, 0, 0)
            return dq_acc + _dot(dz, kbf[pl.ds(off, K_CHUNK), :], 1, 0), e_before + jnp.sum(e, axis=1, keepdims=True)

        def step(masked):
            def body(k, carry):
                e_p, w_p, sg_p, dq_acc, lk_before, e_before = carry
                dq_acc, e_before = stage2(e_p, w_p, sg_p, k - 1, dq_acc, e_before)
                e, w16, sg, lk_before = stage1(k, masked, lk_before)
                return e, w16, sg, dq_acc, lk_before, e_before
            return body

        zcol = jnp.where(rows2 < 0, 1.0, 0.0)
        e, w16, sg, lk_before = stage1(0, True, zcol)
        carry = (e, w16, sg, zcol + jnp.where(lane < 0, 1.0, 0.0), lk_before, zcol)
        carry = lax.fori_loop(1, jnp.maximum(n_chunks - 1, 1), step(False), carry)
        carry = lax.fori_loop(jnp.maximum(n_chunks - 1, 1), n_chunks, step(True), carry)
        dq, _ = stage2(carry[0], carry[1], carry[2], n_chunks - 1, carry[3], carry[5])
        dq_ref[...] = (jnp.where(head0, dq[:Q_BLOCK], dq[Q_BLOCK:]) * scale).astype(BF16)

        @pl.when(qb == nq - 1)
        def _():
            dk_ref[...] = dk_acc[pl.ds(0, lp), :].astype(BF16)
            dv_ref[...] = dv_acc[pl.ds(0, lp), :].astype(BF16)

        if send is not None:
            pl.when(last_step)(lambda: all_to_all_finish(*exchange))

    qblk = pl.BlockSpec((Q_BLOCK, V7X_LANES), lambda b, hp, qb: (b * nq + qb, hp))
    seq = pl.BlockSpec((lp, V7X_LANES), lambda b, hp, qb: (b, hp))
    hbm = pl.BlockSpec(memory_space=pl.ANY)
    out = _sds((t, cfg.d_attn), BF16)
    comm = send is not None
    return pl.pallas_call(
        body, name="attn_bwd_exchange" if comm else "attn_bwd", grid=grid,
        in_specs=[pl.BlockSpec((Q_BLOCK, V7X_LANES), lambda b, hp, qb: (b * nq + qb, cq + hp)),
                  pl.BlockSpec((lp, V7X_LANES), lambda b, hp, qb: (b, ck + hp)),
                  pl.BlockSpec((lp, V7X_LANES), lambda b, hp, qb: (b, cv + hp)), qblk, qblk] + ([hbm] if comm else []),
        out_specs=[qblk, seq, seq] + ([hbm] if comm else []),
        out_shape=[out, out, out] + ([_sds(send.shape, send.dtype)] if comm else []),
        scratch_shapes=[pltpu.VMEM((lp + Q_BLOCK, V7X_LANES), BF16), pltpu.VMEM((lp + Q_BLOCK, V7X_LANES), BF16),
                        pltpu.VMEM((lp + Q_BLOCK, V7X_LANES), F32), pltpu.VMEM((lp + Q_BLOCK, V7X_LANES), F32),
                        pltpu.VMEM((K_CHUNK, K_CHUNK), BF16), pltpu.VMEM((K_CHUNK, K_CHUNK), BF16)]
        + (exchange_semaphores() if comm else []),
        compiler_params=_cp(("arbitrary", "arbitrary", "arbitrary")))(
            *((proj, proj, proj, lt, do) + ((send,) if comm else ())))


N_POW = 8


def _discretize(lr, li, logdt, br_t, bi_t):
    dt = jnp.exp(logdt)
    mag = jnp.exp(lr * dt)
    ab_re = mag * jnp.cos(li * dt)
    ab_im = mag * jnp.sin(li * dt)
    den = lr * lr + li * li
    nr = ab_re - 1.0
    ni = ab_im
    coef_re = (nr * lr + ni * li) / den
    coef_im = (ni * lr - nr * li) / den
    bb_re = coef_re[None] * br_t - coef_im[None] * bi_t
    bb_im = coef_re[None] * bi_t + coef_im[None] * br_t
    return ab_re, ab_im, bb_re, bb_im


def ssm_param_fwd(cfg, lr, li, logdt, br_t, bi_t):
    g, n, c = cfg.groups, cfg.ssm_state, cfg.ssm_group

    def body(lr_ref, li_ref, dt_ref, br_ref, bi_ref, ar_ref, ai_ref, bbr_ref, bbi_ref, pr_ref, pi_ref,
             tr_ref, ti_ref):
        ab_re, ab_im, bb_re, bb_im = _discretize(lr_ref[...], li_ref[...], dt_ref[...], br_ref[...], bi_ref[...])
        ar_ref[...] = ab_re
        ai_ref[...] = ab_im
        bbr_ref[...] = bb_re
        bbi_ref[...] = bb_im
        pr, pi = ab_re, ab_im
        for k in range(N_POW):
            pr_ref[k] = pr
            pi_ref[k] = pi
            pr, pi = pr * pr - pi * pi, 2.0 * pr * pi
        cr, ci = ab_re, ab_im
        for j in range(SSM_CHUNK):
            tr_ref[j] = cr
            ti_ref[j] = ci
            cr, ci = cr * ab_re - ci * ab_im, cr * ab_im + ci * ab_re

    gn, cgn = _sds((g, n), F32), _sds((c, g, n), F32)
    return pl.pallas_call(
        body, name="ssm_param_fwd",
        out_shape=[gn, gn, cgn, cgn, _sds((N_POW, g, n), F32), _sds((N_POW, g, n), F32),
                   _sds((SSM_CHUNK, g, n), F32), _sds((SSM_CHUNK, g, n), F32)])(lr, li, logdt, br_t, bi_t)


def ssm_param_bwd(cfg, lr, li, logdt, br_t, bi_t, dar, dai, dbbr, dbbi):
    g, n, c = cfg.groups, cfg.ssm_state, cfg.ssm_group

    def body(lr_ref, li_ref, dt_ref, br_ref, bi_ref, dar_ref, dai_ref, dbr_ref, dbi_ref,
             olr_ref, oli_ref, odt_ref, obr_ref, obi_ref):
        _, vjp = jax.vjp(_discretize, lr_ref[...], li_ref[...], dt_ref[...], br_ref[...], bi_ref[...])
        dlr, dli, ddt, dbr, dbi = vjp((dar_ref[...], dai_ref[...], dbr_ref[...], dbi_ref[...]))
        olr_ref[...] = dlr
        oli_ref[...] = dli
        odt_ref[...] = ddt
        obr_ref[...] = dbr
        obi_ref[...] = dbi

    gn, cgn = _sds((g, n), F32), _sds((c, g, n), F32)
    return pl.pallas_call(body, name="ssm_param_bwd", out_shape=[gn, gn, _sds((g, 1), F32), cgn, cgn])(
        lr, li, logdt, br_t, bi_t, dar, dai, dbbr, dbbi)


def _scan(xr, xi, pr_ref, pi_ref, reverse):
    ct = xr.shape[0]
    rows = lax.broadcasted_iota(jnp.int32, (ct, 1), 0)
    d, k = 1, 0
    while d < ct:
        if reverse:
            sr, si = pltpu.roll(xr, ct - d, 0), pltpu.roll(xi, ct - d, 0)
            keep = rows < ct - d
        else:
            sr, si = pltpu.roll(xr, d, 0), pltpu.roll(xi, d, 0)
            keep = rows >= d
        sr = jnp.where(keep, sr, 0.0)
        si = jnp.where(keep, si, 0.0)
        ar = pr_ref[0, k:k + 1, :]
        ai = -pi_ref[0, k:k + 1, :] if reverse else pi_ref[0, k:k + 1, :]
        xr, xi = xr + ar * sr - ai * si, xi + ar * si + ai * sr
        d *= 2
        k += 1
    return xr, xi


def _gelu(x):
    k = math.sqrt(2.0 / math.pi)
    return 0.5 * x * (1.0 + jnp.tanh(k * (x + 0.044715 * x * x * x)))


def _gelu_grad(x):
    k = math.sqrt(2.0 / math.pi)
    th = jnp.tanh(k * (x + 0.044715 * x * x * x))
    return 0.5 * (1.0 + th) + 0.5 * x * (1.0 - th * th) * k * (1.0 + 3.0 * 0.044715 * x * x)


def ssm_fwd(cfg, proj, bb_re, bb_im, ct_re, ct_im, pw_re, pw_im, tab_re, tab_im, dskip):
    t = proj.shape[0]
    nsb = cfg.d_ssm // SSM_SB
    ns = SSM_SB // cfg.ssm_group * cfg.ssm_state
    ct = SSM_CHUNK
    nc = cfg.lp // ct

    def body(u_ref, bbr_ref, bbi_ref, ctr_ref, cti_ref, pr_ref, pi_ref, tr_ref, ti_ref, d_ref,
             sr_ref, si_ref, yt_ref, y2_ref, cr_scr, ci_scr):
        c = pl.program_id(2)

        @pl.when(c == 0)
        def _():
            cr_scr[...] = jnp.zeros_like(cr_scr)
            ci_scr[...] = jnp.zeros_like(ci_scr)

        u = u_ref[...]
        ub = u.astype(BF16)
        xr, xi = _scan(_dot(ub, bbr_ref[0], 1, 0), _dot(ub, bbi_ref[0], 1, 0), pr_ref, pi_ref, False)
        cr, ci = cr_scr[0:1, :], ci_scr[0:1, :]
        tr, ti = tr_ref[0], ti_ref[0]
        sr = xr + tr * cr - ti * ci
        si = xi + tr * ci + ti * cr
        sr_ref[...] = sr
        si_ref[...] = si
        cr_scr[0:1, :] = sr_ref[ct - 1:ct, :]
        ci_scr[0:1, :] = si_ref[ct - 1:ct, :]
        y = _dot(sr.astype(BF16), ctr_ref[0], 1, 0) - _dot(si.astype(BF16), cti_ref[0], 1, 0) + d_ref[...] * u
        yt_ref[...] = y
        y2_ref[...] = _gelu(y).astype(BF16)

    def mat(r, c_):
        return pl.BlockSpec((1, r, c_), lambda sb, b, c: (sb, 0, 0))

    ublk = pl.BlockSpec((ct, SSM_SB), lambda sb, b, c: (b * nc + c, sb))
    sblk = pl.BlockSpec((ct, ns), lambda sb, b, c: (b * nc + c, sb))
    return pl.pallas_call(
        body, name="ssm_fwd", grid=(nsb, cfg.b_loc, nc),
        in_specs=[ublk, mat(SSM_SB, ns), mat(SSM_SB, ns), mat(ns, SSM_SB), mat(ns, SSM_SB), mat(N_POW, ns),
                  mat(N_POW, ns), mat(ct, ns), mat(ct, ns), pl.BlockSpec((1, SSM_SB), lambda sb, b, c: (0, sb))],
        out_specs=[sblk, sblk, ublk, ublk],
        out_shape=[_sds((t, cfg.n_state), F32), _sds((t, cfg.n_state), F32), _sds((t, cfg.d_ssm), F32),
                   _sds((t, cfg.d_ssm), BF16)],
        scratch_shapes=[pltpu.VMEM((8, ns), F32), pltpu.VMEM((8, ns), F32)],
        compiler_params=_cp(("parallel", "arbitrary", "arbitrary")))(
            proj, bb_re, bb_im, ct_re, ct_im, pw_re, pw_im, tab_re, tab_im, dskip.reshape(1, cfg.d_ssm))


def ssm_bwd(cfg, dy2, ytot, proj, s_re, s_im, cf_re, cf_im, bbt_re, bbt_im, pw_re, pw_im, tabr_re, tabr_im, dskip):
    t = proj.shape[0]
    nsb = cfg.d_ssm // SSM_SB
    ns = SSM_SB // cfg.ssm_group * cfg.ssm_state
    ct = SSM_CHUNK
    nc = cfg.lp // ct

    def body(dy_ref, yt_ref, u_ref, sr_ref, si_ref, spr_ref, spi_ref, cfr_ref, cfi_ref, btr_ref, bti_ref,
             pr_ref, pi_ref, tr_ref, ti_ref, d_ref,
             du_ref, dctr_ref, dcti_ref, dbbr_ref, dbbi_ref, dar_ref, dai_ref, dd_ref,
             cr_scr, ci_scr, ar_scr, ai_scr):
        b, c = pl.program_id(1), pl.program_id(2)
        chunk = nc - 1 - c

        @pl.when((b == 0) & (c == 0))
        def _():
            for r in (dctr_ref, dcti_ref, dbbr_ref, dbbi_ref, dar_ref, dai_ref, dd_ref):
                r[...] = jnp.zeros_like(r)

        @pl.when(c == 0)
        def _():
            cr_scr[...] = jnp.zeros_like(cr_scr)
            ci_scr[...] = jnp.zeros_like(ci_scr)

        u = u_ref[...]
        dyt = dy_ref[...] * _gelu_grad(yt_ref[...])
        dytb = dyt.astype(BF16)
        xr, xi = _scan(_dot(dytb, cfr_ref[0], 1, 0), -_dot(dytb, cfi_ref[0], 1, 0), pr_ref, pi_ref, True)
        cr, ci = cr_scr[0:1, :], ci_scr[0:1, :]
        tr, ti = tr_ref[0], -ti_ref[0]
        a_re = xr + tr * cr - ti * ci
        a_im = xi + tr * ci + ti * cr
        ar_scr[...] = a_re
        ai_scr[...] = a_im
        cr_scr[0:1, :] = ar_scr[0:1, :]
        ci_scr[0:1, :] = ai_scr[0:1, :]

        sr, si = sr_ref[...], si_ref[...]
        rows = lax.broadcasted_iota(jnp.int32, (ct, 1), 0)
        keep_prev = chunk > 0
        pr_last = jnp.where(keep_prev, spr_ref[7:8, :], 0.0)
        pi_last = jnp.where(keep_prev, spi_ref[7:8, :], 0.0)
        sp_re = jnp.where(rows == 0, pr_last, pltpu.roll(sr, 1, 0))
        sp_im = jnp.where(rows == 0, pi_last, pltpu.roll(si, 1, 0))
        dar_ref[0] += jnp.sum(a_re * sp_re + a_im * sp_im, axis=0, keepdims=True)
        dai_ref[0] += jnp.sum(a_im * sp_re - a_re * sp_im, axis=0, keepdims=True)
        dctr_ref[0] += _dot(sr.astype(BF16), dytb, 0, 0)
        dcti_ref[0] += -_dot(si.astype(BF16), dytb, 0, 0)
        ub = u.astype(BF16)
        arb, aib = a_re.astype(BF16), a_im.astype(BF16)
        dbbr_ref[0] += _dot(ub, arb, 0, 0)
        dbbi_ref[0] += _dot(ub, aib, 0, 0)
        du = dyt * d_ref[...] + _dot(arb, btr_ref[0], 1, 0) + _dot(aib, bti_ref[0], 1, 0)
        valid = (chunk * ct + rows) >= cfg.pad
        du_ref[...] = jnp.where(valid, du, 0.0).astype(BF16)
        dd_ref[...] += jnp.sum(dyt * u, axis=0, keepdims=True)

    def mat(r, c_):
        return pl.BlockSpec((1, r, c_), lambda sb, b, c: (sb, 0, 0))

    lp8 = cfg.lp // 8
    ublk = pl.BlockSpec((ct, SSM_SB), lambda sb, b, c: (b * nc + nc - 1 - c, sb))
    sblk = pl.BlockSpec((ct, ns), lambda sb, b, c: (b * nc + nc - 1 - c, sb))
    sprev = pl.BlockSpec((8, ns), lambda sb, b, c: (jnp.maximum(b * lp8 + (nc - 1 - c) * (ct // 8) - 1, 0), sb))
    dvec = pl.BlockSpec((1, SSM_SB), lambda sb, b, c: (0, sb))
    return pl.pallas_call(
        body, name="ssm_bwd", grid=(nsb, cfg.b_loc, nc),
        in_specs=[ublk, ublk, ublk, sblk, sblk, sprev, sprev, mat(SSM_SB, ns), mat(SSM_SB, ns), mat(ns, SSM_SB),
                  mat(ns, SSM_SB), mat(N_POW, ns), mat(N_POW, ns), mat(ct, ns), mat(ct, ns), dvec],
        out_specs=[ublk, mat(ns, SSM_SB), mat(ns, SSM_SB), mat(SSM_SB, ns), mat(SSM_SB, ns), mat(1, ns), mat(1, ns),
                   dvec],
        out_shape=[_sds((t, cfg.d_ssm), BF16), _sds((nsb, ns, SSM_SB), F32), _sds((nsb, ns, SSM_SB), F32),
                   _sds((nsb, SSM_SB, ns), F32), _sds((nsb, SSM_SB, ns), F32), _sds((nsb, 1, ns), F32),
                   _sds((nsb, 1, ns), F32), _sds((1, cfg.d_ssm), F32)],
        scratch_shapes=[pltpu.VMEM((8, ns), F32), pltpu.VMEM((8, ns), F32), pltpu.VMEM((ct, ns), F32),
                        pltpu.VMEM((ct, ns), F32)],
        compiler_params=_cp(("arbitrary", "arbitrary", "arbitrary")))(
            dy2, ytot, proj, s_re, s_im, s_re, s_im, cf_re, cf_im, bbt_re, bbt_im, pw_re, pw_im, tabr_re, tabr_im,
            dskip.reshape(1, cfg.d_ssm))


def _ssm_tables(cfg, lam_re, lam_im, log_dt, b_re, b_im, c_re, c_im):
    g, n, c = cfg.groups, cfg.ssm_state, cfg.ssm_group
    gsb = SSM_SB // c
    nsb = cfg.d_ssm // SSM_SB
    ns = gsb * n
    br_t, bi_t = jnp.transpose(b_re, (2, 0, 1)), jnp.transpose(b_im, (2, 0, 1))
    prm = (lam_re, lam_im, log_dt.reshape(g, 1), br_t, bi_t)
    _, _, bbr, bbi, pr, pi, tr, ti = ssm_param_fwd(cfg, *prm)
    eye = jnp.eye(gsb, dtype=F32)

    def bdiag_b(x):
        return jnp.einsum('csgn,gh->sgchn', x.reshape(c, nsb, gsb, n), eye).reshape(nsb, SSM_SB, ns)

    def bdiag_c(x):
        return jnp.einsum('sgcn,gh->sgchn', x.reshape(nsb, gsb, c, n), eye).reshape(nsb, SSM_SB, ns)

    def lanes(x):
        return jnp.transpose(x.reshape(x.shape[0], nsb, ns), (1, 0, 2))

    bb_re, bb_im = bdiag_b(bbr), bdiag_b(bbi)
    cf_re, cf_im = bdiag_c(c_re), bdiag_c(c_im)
    sw = lambda x: jnp.swapaxes(x, 1, 2)
    return dict(prm=prm, bb_re=bb_re.astype(BF16), bb_im=bb_im.astype(BF16), bbt_re=sw(bb_re).astype(BF16),
                bbt_im=sw(bb_im).astype(BF16), cf_re=cf_re.astype(BF16), cf_im=cf_im.astype(BF16),
                ct_re=sw(cf_re).astype(BF16), ct_im=sw(cf_im).astype(BF16), pw_re=lanes(pr), pw_im=lanes(pi),
                tab_re=lanes(tr), tab_im=lanes(ti), tabr_re=lanes(tr[::-1]), tabr_im=lanes(ti[::-1]))


def _ssm_param_grads(cfg, tabs, dct_re, dct_im, dbb_re, dbb_im, dab_re, dab_im):
    g, n, c = cfg.groups, cfg.ssm_state, cfg.ssm_group
    gsb = SSM_SB // c
    nsb = cfg.d_ssm // SSM_SB

    def diag_b(x):
        d = jnp.einsum('sgcgn->sgcn', x.reshape(nsb, gsb, c, gsb, n))
        return jnp.transpose(d.reshape(g, c, n), (1, 0, 2))

    def diag_c(x):
        d = jnp.einsum('sgngc->sgcn', x.reshape(nsb, gsb, n, gsb, c))
        return d.reshape(g, c, n)

    dlr, dli, ddt, dbr_t, dbi_t = ssm_param_bwd(cfg, *tabs['prm'], dab_re.reshape(g, n), dab_im.reshape(g, n),
                                                diag_b(dbb_re), diag_b(dbb_im))
    return dict(ssm_lam_re=dlr, ssm_lam_im=dli, ssm_log_dt=ddt.reshape(g),
                ssm_b_re=jnp.transpose(dbr_t, (1, 2, 0)), ssm_b_im=jnp.transpose(dbi_t, (1, 2, 0)),
                ssm_c_re=diag_c(dct_re), ssm_c_im=diag_c(dct_im))


def _mm_tiles(t, n, k):
    tm = _tile(t, 1056 if k <= 3072 else 528, 16)
    tn = _tile(n, 512, 128)
    return tm, tn


def _nt(a, wt, name, out_dtype=F32):
    tm, tn = _mm_tiles(a.shape[0], wt.shape[0], a.shape[1])
    return mm([(a, wt, 0)], nt=True, n=wt.shape[0], tm=tm, tn=tn, out_dtype=out_dtype, name=name)


def _nn(pairs, n, name, res=None, res_scale=1.0, out_dtype=F32):
    ktot = sum(p[0].shape[1] for p in pairs)
    tm, tn = _tile(pairs[0][0].shape[0], 528 if ktot <= 3072 else 352, 16), n
    return mm(pairs, nt=False, n=n, tm=tm, tn=tn, out_dtype=out_dtype, name=name, res=res, res_scale=res_scale)


def local_step(cfg, x, target, rep, ex):
    d, lp, bsz, t = cfg.d_model, cfg.lp, cfg.b_loc, cfg.tokens
    meta = jnp.broadcast_to(ex.meta_tokens[None], (bsz, cfg.n_meta, d))
    h = jnp.concatenate([jnp.zeros((bsz, cfg.pad, d), F32), meta, x], axis=1).reshape(t, d)

    saved = []
    for l in range(cfg.depth):
        w = ex.weights(l)
        r = {k: v[l] for k, v in rep.items() if k != 'final_norm'}
        s = {'h0': h}
        s['xn1'] = rms_fwd(h, r['ffn1_norm'], "rms_fwd")
        s['a1'], s['b1'], s['act1'] = dual_mm_act(s['xn1'], w['w13t_1'], kind='swiglu', name="ffn_up")
        h = _nn([(s['act1'], w['w2_1'], 0)], d, "ffn_down", res=h, res_scale=0.5)
        s['h1'] = h
        s['xnm'] = rms_fwd(h, r['mix_norm'], "rms_fwd")
        proj = _nt(s['xnm'], w['w_int'], "proj_in")
        s['proj'] = proj
        tabs = _ssm_tables(cfg, r['ssm_lam_re'], r['ssm_lam_im'], r['ssm_log_dt'], r['ssm_b_re'], r['ssm_b_im'],
                           r['ssm_c_re'], r['ssm_c_im'])
        s['tabs'] = tabs
        s['s_re'], s['s_im'], s['ytot'], s['y2'] = ssm_fwd(
            cfg, proj, tabs['bb_re'], tabs['bb_im'], tabs['ct_re'], tabs['ct_im'], tabs['pw_re'], tabs['pw_im'],
            tabs['tab_re'], tabs['tab_im'], r['ssm_d'])
        s['ga'], s['gg'], o_ssm = dual_mm_act(s['y2'], w['w_glut'], kind='glu', name="ssm_glu")
        s['cpre'], s['hn'] = conv_fwd(cfg, proj, w['conv_w'], r['conv_b'], r['conv_ln_g'], r['conv_ln_b'])
        s['o_conv'] = _nt(s['hn'], w['conv_w_outt'], "branch_out")
        src = ex.gather_src(l)
        s['o'], s['lt'], *arrived = attn_fwd(cfg, proj, src)
        if src is not None:
            ex.gathered(l, arrived[0])
        s['o_attn'] = _nt(s['o'], w['attn_w_ot'], "branch_out")
        s['merged'] = merge_fwd(cfg, proj, o_ssm, s['o_conv'], s['o_attn'])
        h = _nn([(s['merged'], w['w_out'], 0)], d, "mix_out", res=h, res_scale=1.0)
        s['h2'] = h
        s['xn2'] = rms_fwd(h, r['ffn2_norm'], "rms_fwd")
        s['a2'], s['b2'], s['act2'] = dual_mm_act(s['xn2'], w['w13t_2'], kind='swiglu', name="ffn_up")
        h = _nn([(s['act2'], w['w2_2'], 0)], d, "ffn_down", res=h, res_scale=0.5)
        saved.append((w, r, s))

    dh3, dh3_16, dgf, loss_sq = final_fwd_bwd(cfg, h.reshape(bsz, lp, d), rep['final_norm'], target)
    dh, dh16 = dh3.reshape(t, d), dh3_16.reshape(t, d)
    loss = 0.5 * loss_sq[0, 0] / d

    gl = []
    for l in reversed(range(cfg.depth)):
        w, r, s = saved[l]
        g = {}

        def ffn_bwd(dh, dh16, tag, a, b, act, xn, hin, norm):
            da, db = ffn_down_bwd(dh16, w['w2_' + tag], a, b, "ffn_down_bwd")
            g['w2_' + tag] = mm_tn(act, dh16, name="ffn_dw2", scale=0.5)
            dxn = _nn([(da, w['w13t_' + tag], 0), (db, w['w13t_' + tag], 1)], d, "ffn_dxn")
            g['w13t_' + tag] = jnp.concatenate([mm_tn(da, xn, name="ffn_dw13"), mm_tn(db, xn, name="ffn_dw13")], 0)
            return rms_bwd(dxn, hin, norm, dh, "rms_bwd")

        dh, dh16, dg2 = ffn_bwd(dh, dh16, '2', s['a2'], s['b2'], s['act2'], s['xn2'], s['h2'], r['ffn2_norm'])
        g['ffn2_norm'] = dg2.reshape(d)

        proj, tabs = s['proj'], s['tabs']
        dmerged = _nt(dh16, w['w_out'], "mix_dmerged")
        g['w_out'] = mm_tn(s['merged'], dh16, name="mix_dwout")
        dg0, dg1, dg2_, dga, dgg, do_conv, do_attn = merge_bwd(cfg, dmerged, proj, s['ga'], s['gg'], s['o_conv'],
                                                               s['o_attn'])
        dy2 = _nn([(dga, w['w_glut'], 0), (dgg, w['w_glut'], 1)], cfg.d_ssm, "ssm_dy2")
        g['w_glut'] = jnp.concatenate([mm_tn(dga, s['y2'], name="ssm_dwglu"), mm_tn(dgg, s['y2'], name="ssm_dwglu")], 0)
        du, dct_re, dct_im, dbb_re, dbb_im, dab_re, dab_im, dd = ssm_bwd(
            cfg, dy2, s['ytot'], proj, s['s_re'], s['s_im'], tabs['cf_re'], tabs['cf_im'], tabs['bbt_re'],
            tabs['bbt_im'], tabs['pw_re'], tabs['pw_im'], tabs['tabr_re'], tabs['tabr_im'], r['ssm_d'])
        g.update(_ssm_param_grads(cfg, tabs, dct_re, dct_im, dbb_re, dbb_im, dab_re, dab_im))
        g['ssm_d'] = dd.reshape(cfg.d_ssm)

        dhn = _nn([(do_conv, w['conv_w_outt'], 0)], cfg.d_conv, "branch_din")
        g['conv_w_outt'] = mm_tn(do_conv, s['hn'], name="branch_dw")
        dxa, dxg, dcw, dcb, dlg, dlb = conv_bwd(cfg, dhn, s['cpre'], proj, w['conv_w'], r['conv_ln_g'],
                                                r['conv_ln_b'])
        g['conv_w'], g['conv_b'] = dcw, dcb.reshape(cfg.d_conv)
        g['conv_ln_g'], g['conv_ln_b'] = dlg.reshape(cfg.d_conv), dlb.reshape(cfg.d_conv)

        do = _nn([(do_attn, w['attn_w_ot'], 0)], cfg.d_attn, "branch_din")
        g['attn_w_ot'] = mm_tn(do_attn, s['o'], name="branch_dw")
        send = ex.send_src(l)
        dq, dk, dv, *arrived = attn_bwd(cfg, proj, s['lt'], do, send)
        if send is not None:
            ex.received(l, arrived[0])

        dproj = jnp.concatenate([du, dxa, dxg, dq, dk, dv, dg0, dg1, dg2_],
                                axis=1)
        dxn = _nn([(dproj, w['w_int'], 0)], d, "proj_dxn")
        g['w_int'] = mm_tn(dproj, s['xnm'], name="proj_dw")
        dh, dh16, dgm = rms_bwd(dxn, s['h1'], r['mix_norm'], dh, "rms_bwd")
        g['mix_norm'] = dgm.reshape(d)

        dh, dh16, dg1n = ffn_bwd(dh, dh16, '1', s['a1'], s['b1'], s['act1'], s['xn1'], s['h0'], r['ffn1_norm'])
        g['ffn1_norm'] = dg1n.reshape(d)
        ex.matrix_grads(l, {name: g.pop(name) for name, _, _ in PACKED})
        gl.append(g)

    gl = gl[::-1]
    dh0 = dh.reshape(bsz, lp, d)
    grad_x = dh0[:, Q_BLOCK:]
    grads = {k: jnp.stack([g[k] for g in gl]) for k in gl[0]}
    grads['meta_tokens'] = jnp.sum(dh0[:, cfg.pad:Q_BLOCK], axis=0)
    grads['final_norm'] = dgf.reshape(d)
    return loss, grad_x, grads


def _me():
    return lax.axis_index("x"), lax.axis_index("y"), lax.axis_index("c")


def _flat(px, py, pc):
    return 4 * px + 2 * py + pc


def all_gather_blocks(block, name):
    r, c_ = block.shape

    def body(x_ref, out_ref, send_sems, recv_sems, local_sem):
        gather_start(x_ref, out_ref, send_sems, recv_sems, local_sem)
        gather_finish(x_ref, out_ref, send_sems, recv_sems, local_sem)

    return pl.pallas_call(
        body, name=name, out_shape=_sds((8, r, c_), block.dtype),
        in_specs=[pl.BlockSpec(memory_space=pl.ANY)], out_specs=pl.BlockSpec(memory_space=pl.ANY),
        scratch_shapes=exchange_semaphores())(block)


def exchange_semaphores():
    return [pltpu.SemaphoreType.DMA((7,)), pltpu.SemaphoreType.DMA((7,)), pltpu.SemaphoreType.DMA(())]


def _gather_copies(x_ref, out_ref, send_sems, recv_sems, local_sem):
    x, y, c = _me()
    me, sibling = (x, y, c), (x, y, 1 - c)
    chips = [(1 - x, y), (x, 1 - y), (1 - x, 1 - y)]

    def slot(px, py, pc):
        return out_ref.at[_flat(px, py, pc)]

    def copy(k, blk, to, src=None):
        return pltpu.make_async_remote_copy(
            src_ref=slot(*blk) if src is None else src, dst_ref=slot(*blk), send_sem=send_sems.at[k],
            recv_sem=recv_sems.at[k], device_id=to, device_id_type=pl.DeviceIdType.MESH)

    mine = pltpu.make_async_copy(x_ref, slot(*me), local_sem)
    first = [copy(0, me, sibling, src=x_ref)] + [copy(1 + j, me, (*chip, c), src=x_ref) for j, chip in enumerate(chips)]
    passed = [copy(4 + j, (*chip, c), sibling) for j, chip in enumerate(chips)]
    over_ici = [copy(1 + j, (*chip, c), me) for j, chip in enumerate(chips)]
    from_sibling = [copy(0, sibling, me)] + [copy(4 + j, (*chip, 1 - c), me) for j, chip in enumerate(chips)]
    return mine, first, passed, over_ici, from_sibling


def gather_start(*refs):
    mine, first, _, _, _ = _gather_copies(*refs)
    mine.start()
    for cp in first:
        cp.start()


def gather_finish(*refs):
    mine, first, passed, over_ici, from_sibling = _gather_copies(*refs)
    for arrival, forward in zip(over_ici, passed):
        arrival.wait_recv()
        forward.start()
    for arrival in from_sibling:
        arrival.wait_recv()
    for cp in first + passed:
        cp.wait_send()
    mine.wait()


def all_to_all_blocks(send, name):
    _, r, c_ = send.shape

    def body(s_ref, out_ref, send_sems, recv_sems, local_sem):
        all_to_all_start(s_ref, out_ref, send_sems, recv_sems, local_sem)
        all_to_all_finish(s_ref, out_ref, send_sems, recv_sems, local_sem)

    return pl.pallas_call(
        body, name=name, out_shape=_sds((8, r, c_), send.dtype),
        in_specs=[pl.BlockSpec(memory_space=pl.ANY)], out_specs=pl.BlockSpec(memory_space=pl.ANY),
        scratch_shapes=exchange_semaphores())(send)


def _all_to_all_copies(s_ref, out_ref, send_sems, recv_sems, local_sem):
    x, y, c = _me()
    me = _flat(x, y, c)
    mine = pltpu.make_async_copy(s_ref.at[me], out_ref.at[me], local_sem)
    copies = []
    for rel in range(1, 8):
        px = 1 - x if rel & 4 else x
        py = 1 - y if rel & 2 else y
        pc = 1 - c if rel & 1 else c
        copies.append(pltpu.make_async_remote_copy(
            src_ref=s_ref.at[_flat(px, py, pc)], dst_ref=out_ref.at[me], send_sem=send_sems.at[rel - 1],
            recv_sem=recv_sems.at[rel - 1], device_id=(px, py, pc), device_id_type=pl.DeviceIdType.MESH))
    return mine, copies


def all_to_all_start(*refs):
    mine, copies = _all_to_all_copies(*refs)
    mine.start()
    for cp in copies:
        cp.start()


def all_to_all_finish(*refs):
    mine, copies = _all_to_all_copies(*refs)
    for cp in copies:
        cp.wait_recv()
    for cp in copies:
        cp.wait_send()
    mine.wait()


def reduce_blocks(recv, name):
    nsrc, r, c_ = recv.shape
    tr = _tile(r, 128, 16)

    def body(x_ref, o_ref):
        acc = x_ref[0].astype(F32)
        for s in range(1, nsrc):
            acc = acc + x_ref[s].astype(F32)
        o_ref[...] = acc

    return pl.pallas_call(
        body, name=name, grid=(r // tr,),
        in_specs=[pl.BlockSpec((nsrc, tr, c_), lambda i: (0, i, 0))], out_specs=pl.BlockSpec((tr, c_), lambda i: (i, 0)),
        out_shape=_sds((r, c_), F32), compiler_params=_cp(("parallel",)))(recv)


def adamw(w, g, m, v):
    shape = w.shape
    size = math.prod(shape)
    if shape[-1] < V7X_LANES and size % 1024 == 0:
        view = (size // 1024, 1024)
    else:
        view = (size // shape[-1], shape[-1])
    rows, cols = view
    tr = _tile(rows, max(8, (1 << 19) // cols // 8 * 8), 8)
    c1 = 1.0 - ADAM_B1 ** ADAM_STEP
    c2 = 1.0 - ADAM_B2 ** ADAM_STEP

    def body(w_ref, g_ref, m_ref, v_ref, d_ref, nm_ref, nv_ref):
        gv = g_ref[...]
        nm = ADAM_B1 * m_ref[...] + (1.0 - ADAM_B1) * gv
        nv = ADAM_B2 * v_ref[...] + (1.0 - ADAM_B2) * (gv * gv)
        nm_ref[...] = nm
        nv_ref[...] = nv
        d_ref[...] = -ADAM_LR * ((nm / c1) / (jnp.sqrt(nv / c2) + ADAM_EPS) + ADAM_WD * w_ref[...])

    blk = pl.BlockSpec((tr, cols), lambda i: (i, 0))
    out = _sds(view, F32)
    res = pl.pallas_call(
        body, name="adamw", grid=(rows // tr,), in_specs=[blk] * 4, out_specs=[blk] * 3, out_shape=[out] * 3,
        compiler_params=_cp(("parallel",)))(w.reshape(view), g.reshape(view), m.reshape(view), v.reshape(view))
    return tuple(a.reshape(shape) for a in res)


PACKED = [('w13t_1', 'ffn1_w13', True), ('w2_1', 'ffn1_w2', False), ('w_int', 'w_in', True), ('w_out', 'w_out', False),
          ('w13t_2', 'ffn2_w13', True), ('w2_2', 'ffn2_w2', False), ('w_glut', 'ssm_w_glu', True),
          ('conv_w_outt', 'conv_w_out', True), ('attn_w_ot', 'attn_w_o', True)]


def _rows_of(a, d):
    return a.reshape(a.shape[0] * a.shape[1] // d, d)


def _pad_rows(flat, d, mult):
    n = flat.shape[-1]
    rows = -(-n // d)
    rows = -(-rows // mult) * mult
    flat = jnp.pad(flat, [(0, 0)] * (flat.ndim - 1) + [(0, rows * d - n)])
    return flat.reshape(flat.shape[:-1] + (rows, d))


def pack_weight_shards(cfg, wts):
    d = cfg.d_model
    blocks, layout = [], []
    for l in range(cfg.depth):
        parts = []
        for name, src, tr in PACKED:
            a = wts[src][l]
            a = (a.T if tr else a).astype(BF16)
            if l == 0:
                layout.append((name, a.shape))
            parts.append(_rows_of(a, d))
        blocks.append(jnp.concatenate(parts, axis=0))
    small = jnp.concatenate([wts['conv_w'].reshape(-1), wts['meta_tokens'].reshape(-1)])
    return blocks, _pad_rows(small, d, 8), layout


def unpack_gathered_layer(cfg, gathered, layout):
    d = cfg.d_model
    out = {}
    off = 0
    for name, shape in layout:
        rows = shape[0] * shape[1] // d
        out[name] = gathered[:, off:off + rows].reshape(8 * shape[0], shape[1])
        off += rows
    return out


def unpack_gathered_small(gathered_small, wts):
    n_cw, n_mt = wts['conv_w'].size, wts['meta_tokens'].size
    small = gathered_small.reshape(8, -1)
    cw = small[:, :n_cw].reshape((8,) + wts['conv_w'].shape)
    conv_w = jnp.transpose(cw, (1, 2, 0, 3)).reshape(cw.shape[1], cw.shape[2], -1)
    mt = small[:, n_cw:n_cw + n_mt].reshape((8,) + wts['meta_tokens'].shape)
    return conv_w, jnp.transpose(mt, (1, 0, 2)).reshape(mt.shape[1], -1)


def pack_layer_grads(cfg, g, layout):
    d = cfg.d_model
    parts = [g[name].reshape(8, shape[0] * shape[1] // d, d).astype(BF16) for name, shape in layout]
    return jnp.concatenate(parts, axis=1)


def pack_small_grads(cfg, grads, rep_names):
    cw = grads['conv_w']
    cw = jnp.transpose(cw.reshape(cw.shape[0], cw.shape[1], 8, -1), (2, 0, 1, 3)).reshape(8, -1)
    mt = grads['meta_tokens']
    mt = jnp.transpose(mt.reshape(mt.shape[0], 8, -1), (1, 0, 2)).reshape(8, -1)
    repl = jnp.concatenate([grads[k].reshape(-1) for k in rep_names])
    small = jnp.concatenate([cw, mt, jnp.broadcast_to(repl[None], (8, repl.shape[0]))], axis=1)
    return _pad_rows(small, cfg.d_model, 128)


def unpack_grads(cfg, gsums, gsmall, layout, wts, rep_names):
    d = cfg.d_model
    acc = {}
    for gsum in gsums:
        off = 0
        for name, shape in layout:
            rows = shape[0] * shape[1] // d
            acc.setdefault(name, []).append(gsum[off:off + rows].reshape(shape))
            off += rows
    out = {}
    for name, src, tr in PACKED:
        a = jnp.stack(acc[name])
        out[src] = jnp.swapaxes(a, 1, 2) if tr else a
    flat = gsmall.reshape(-1)
    pos = 0
    for k in ['conv_w', 'meta_tokens'] + rep_names:
        n = wts[k].size
        out[k] = flat[pos:pos + n].reshape(wts[k].shape)
        pos += n
    return out


class WeightExchange:
    def __init__(self, cfg, wts):
        self.cfg = cfg
        self.blocks, small, self.layout = pack_weight_shards(cfg, wts)
        self.conv_w, self.meta_tokens = unpack_gathered_small(all_gather_blocks(small, "all_gather_small_weights"), wts)
        self.full = {0: unpack_gathered_layer(cfg, all_gather_blocks(self.blocks[0], "all_gather_weights"), self.layout)}
        self.to_send, self.arrived = {}, {}

    def weights(self, l):
        return dict(self.full[l], conv_w=self.conv_w[l])

    def gather_src(self, l):
        return self.blocks[l + 1] if l + 1 < self.cfg.depth else None

    def gathered(self, l, blocks):
        self.full[l + 1] = unpack_gathered_layer(self.cfg, blocks, self.layout)

    def matrix_grads(self, l, g):
        self.to_send[l] = pack_layer_grads(self.cfg, g, self.layout)

    def send_src(self, l):
        return self.to_send.get(l + 1)

    def received(self, l, blocks):
        self.arrived[l + 1] = blocks

    def finish(self, grads, wts):
        self.arrived[0] = all_to_all_blocks(self.to_send[0], "all_to_all_grads")
        gsums = [reduce_blocks(self.arrived[l], "reduce_grads") for l in range(self.cfg.depth)]
        small = all_to_all_blocks(pack_small_grads(self.cfg, grads, REPLICATED), "all_to_all_small_grads")
        return unpack_grads(self.cfg, gsums, reduce_blocks(small, "reduce_small_grads"), self.layout, wts, REPLICATED)


def train_step(cfg, x, target, wts, ms, vs):
    ex = WeightExchange(cfg, wts)
    rep = {k: wts[k] for k in REPLICATED}
    loss, grad_x, grads = local_step(cfg, x, target, rep, ex)
    gw = ex.finish(grads, wts)
    loss = lax.psum(loss, ("x", "y", "c"))
    deltas, new_m, new_v = {}, {}, {}
    for k in WEIGHT_NAMES:
        deltas[k], new_m[k], new_v[k] = adamw(wts[k], gw[k], ms[k], vs[k])
    return (loss, grad_x, *[gw[k] for k in WEIGHT_NAMES], *[deltas[k] for k in WEIGHT_NAMES],
            *[new_m[k] for k in WEIGHT_NAMES], *[new_v[k] for k in WEIGHT_NAMES])


def kernel(x, meta_tokens, ffn1_norm, ffn1_w13, ffn1_w2, mix_norm, w_in, ssm_lam_re, ssm_lam_im, ssm_log_dt, ssm_b_re, ssm_b_im, ssm_c_re, ssm_c_im, ssm_d, ssm_w_glu, conv_w, conv_b, conv_ln_g, conv_ln_b, conv_w_out, attn_w_o, w_out, ffn2_norm, ffn2_w13, ffn2_w2, final_norm, loss_target, m_meta_tokens, m_ffn1_norm, m_ffn1_w13, m_ffn1_w2, m_mix_norm, m_w_in, m_ssm_lam_re, m_ssm_lam_im, m_ssm_log_dt, m_ssm_b_re, m_ssm_b_im, m_ssm_c_re, m_ssm_c_im, m_ssm_d, m_ssm_w_glu, m_conv_w, m_conv_b, m_conv_ln_g, m_conv_ln_b, m_conv_w_out, m_attn_w_o, m_w_out, m_ffn2_norm, m_ffn2_w13, m_ffn2_w2, m_final_norm, v_meta_tokens, v_ffn1_norm, v_ffn1_w13, v_ffn1_w2, v_mix_norm, v_w_in, v_ssm_lam_re, v_ssm_lam_im, v_ssm_log_dt, v_ssm_b_re, v_ssm_b_im, v_ssm_c_re, v_ssm_c_im, v_ssm_d, v_ssm_w_glu, v_conv_w, v_conv_b, v_conv_ln_g, v_conv_ln_b, v_conv_w_out, v_attn_w_o, v_w_out, v_ffn2_norm, v_ffn2_w13, v_ffn2_w2, v_final_norm):
    given = dict(locals())
    wts = {k: given[k] for k in WEIGHT_NAMES}
    ms = {k: given["m_" + k] for k in WEIGHT_NAMES}
    vs = {k: given["v_" + k] for k in WEIGHT_NAMES}
    return train_step(FULL, x, loss_target, wts, ms, vs)
```
